```python
import jax, jax.numpy as jnp
from jax import lax
import numpy as np


D_MODEL = 1024
BATCH = 8
SEQ = 2048
DEPTH = 1

HG_HEADS = 4
HG_DK = 128
HG_DV = 128
HG_WIDTH = HG_HEADS * HG_DV
HG_CHUNK = 64

MLA_HEADS = 4
MLA_Q_LORA = 384
MLA_KV_LORA = 256
MLA_NOPE = 128
MLA_ROPE = 64
MLA_QK = MLA_NOPE + MLA_ROPE
MLA_V = 128
MLA_WIDTH = MLA_HEADS * MLA_V
ATTN_BLOCK = 128
ROPE_THETA = 10000.0

MIX_WIDTH = HG_WIDTH + MLA_WIDTH
IN_SPLITS = (HG_HEADS * HG_DK, HG_HEADS * HG_DK, HG_HEADS * HG_DK, HG_WIDTH, HG_WIDTH,
             MLA_Q_LORA, MLA_KV_LORA, MLA_ROPE)
IN_WIDTH = 3 * HG_HEADS * HG_DK + 2 * HG_WIDTH + MLA_Q_LORA + MLA_KV_LORA + MLA_ROPE

PEER_HEADS = 8
PEER_NKEYS = 128
PEER_EXPERTS = PEER_NKEYS * PEER_NKEYS
PEER_TOPK = 16
PEER_DHALF = 128
PEER_DKEY = 2 * PEER_DHALF
PEER_BLOCK = 128

EPS = 1e-6

kernel_name = 'hybrid_hgrn2_mla_peer_encoder'


def rms_norm(x, gain):
    x32 = x.astype(jnp.float32)
    y = x32 * lax.rsqrt(jnp.mean(x32 * x32, axis=-1, keepdims=True) + EPS)
    return (y * gain.astype(jnp.float32)).astype(x.dtype)


def rope_tables(positions):
    inv_freq = 1.0 / (ROPE_THETA ** (jnp.arange(0, MLA_ROPE, 2, dtype=jnp.float32) / MLA_ROPE))
    ang = positions.astype(jnp.float32)[..., None] * inv_freq
    return jnp.cos(ang)[:, :, None, :], jnp.sin(ang)[:, :, None, :]


def apply_rope(t, cos, sin):
    half = MLA_ROPE // 2
    c = cos.astype(t.dtype)
    s = sin.astype(t.dtype)
    t1, t2 = t[..., :half], t[..., half:]
    return jnp.concatenate([t1 * c - t2 * s, t2 * c + t1 * s], axis=-1)


def hgrn2_chunk_scan(q, k, v, log_f):
    B, S, H, DK = q.shape
    DV = v.shape[-1]
    n = S // HG_CHUNK

    def to_chunks(t):
        return t.reshape(B, n, HG_CHUNK, H, t.shape[-1]).transpose(1, 0, 3, 2, 4)

    mask = jnp.tril(jnp.ones((HG_CHUNK, HG_CHUNK), dtype=bool))[:, :, None]

    def step(state, inp):
        qb, kb, vb, gb = inp
        b = jnp.cumsum(gb, axis=2)
        diff = b[:, :, :, None, :] - b[:, :, None, :, :]
        decay = jnp.exp(jnp.where(mask, diff, -jnp.inf))
        scores = jnp.einsum('bhtk,bhsk,bhtsk->bhts', qb, kb, decay)
        o = jnp.einsum('bhts,bhsv->bhtv', scores, vb) \
            + jnp.einsum('bhtk,bhkv->bhtv', qb * jnp.exp(b), state)
        b_last = b[:, :, -1:, :]
        new_state = jnp.exp(b_last[:, :, 0, :])[..., None] * state \
            + jnp.einsum('bhsk,bhsv->bhkv', kb * jnp.exp(b_last - b), vb)
        return new_state, o

    state0 = jnp.zeros((B, H, DK, DV), jnp.float32)
    _, o = lax.scan(step, state0, (to_chunks(q), to_chunks(k), to_chunks(v), to_chunks(log_f)))
    return o.transpose(1, 0, 3, 2, 4).reshape(B, S, H, DV)


def hgrn2_mixer(q_raw, f_fwd_raw, f_bwd_raw, i_raw, g_raw, lb, o_gain):
    B, S, _ = q_raw.shape
    q = jax.nn.silu(q_raw.reshape(B, S, HG_HEADS, HG_DK).astype(jnp.float32))
    v = i_raw.reshape(B, S, HG_HEADS, HG_DV).astype(jnp.float32)

    def direction(f_raw, lb_dir, reverse):
        lbh = lb_dir.reshape(HG_HEADS, HG_DK)
        f = lbh + (1.0 - lbh) * jax.nn.sigmoid(f_raw.reshape(B, S, HG_HEADS, HG_DK).astype(jnp.float32))
        k = 1.0 - f
        log_f = jnp.log(f)
        if reverse:
            o = hgrn2_chunk_scan(q[:, ::-1], k[:, ::-1], v[:, ::-1], log_f[:, ::-1])
            return o[:, ::-1]
        return hgrn2_chunk_scan(q, k, v, log_f)

    o = direction(f_fwd_raw, lb[0], False) + direction(f_bwd_raw, lb[1], True)
    gate = jax.nn.silu(g_raw.reshape(B, S, HG_HEADS, HG_DV).astype(jnp.float32))
    o = rms_norm(o, o_gain).astype(jnp.float32) * gate
    return o.reshape(B, S, HG_WIDTH).astype(q_raw.dtype)


def bidirectional_attention(q, k, v):
    B, S, H, DQ = q.shape
    nb = S // ATTN_BLOCK
    scale = DQ ** -0.5
    qb = q.reshape(B, nb, ATTN_BLOCK, H, DQ).transpose(1, 0, 2, 3, 4)

    def one_block(q_blk):
        s = jnp.einsum('bqhd,bkhd->bhqk', q_blk, k).astype(jnp.float32) * scale
        p = jax.nn.softmax(s, axis=-1).astype(v.dtype)
        return jnp.einsum('bhqk,bkhd->bqhd', p, v)

    o = lax.map(one_block, qb)
    return o.transpose(1, 0, 2, 3, 4).reshape(B, S, H, v.shape[-1])


def mla_mixer(cq_raw, ckv_raw, kr_raw, cos, sin, q_a_gain, w_q_up, kv_a_gain, w_kv_up,
              q_gain, k_gain, o_gain):
    B, S, _ = cq_raw.shape
    q = (rms_norm(cq_raw, q_a_gain) @ w_q_up).reshape(B, S, MLA_HEADS, MLA_QK)
    kv = (rms_norm(ckv_raw, kv_a_gain) @ w_kv_up).reshape(B, S, MLA_HEADS, MLA_NOPE + MLA_V)
    k_nope, v = kv[..., :MLA_NOPE], kv[..., MLA_NOPE:]
    q_nope = rms_norm(q[..., :MLA_NOPE], q_gain[:MLA_NOPE])
    q_rope = apply_rope(rms_norm(q[..., MLA_NOPE:], q_gain[MLA_NOPE:]), cos, sin)
    k_nope = rms_norm(k_nope, k_gain[:MLA_NOPE])
    k_rope = apply_rope(rms_norm(kr_raw.reshape(B, S, 1, MLA_ROPE), k_gain[MLA_NOPE:]), cos, sin)
    k_rope = jnp.broadcast_to(k_rope, (B, S, MLA_HEADS, MLA_ROPE))
    qf = jnp.concatenate([q_nope, q_rope], axis=-1)
    kf = jnp.concatenate([k_nope, k_rope], axis=-1)
    o = bidirectional_attention(qf, kf, v)
    o = rms_norm(o, o_gain)
    return o.reshape(B, S, MLA_WIDTH)


def peer_ffn(h, w_q, sub_keys, u_tab, v_tab):
    B, S, D = h.shape
    q = (h @ w_q).reshape(B, S, PEER_HEADS, 2, PEER_DHALF)
    scores = jnp.einsum('bspcd,pcnd->bspcn', q, sub_keys).astype(jnp.float32)
    s_top, i_top = lax.top_k(scores, PEER_TOPK)
    cand_s = (s_top[..., 0, :, None] + s_top[..., 1, None, :]).reshape(B, S, PEER_HEADS, PEER_TOPK * PEER_TOPK)
    cand_i = (i_top[..., 0, :, None] * PEER_NKEYS + i_top[..., 1, None, :]).reshape(B, S, PEER_HEADS, PEER_TOPK * PEER_TOPK)
    best_s, best_pos = lax.top_k(cand_s, PEER_TOPK)
    idx = jnp.take_along_axis(cand_i, best_pos, axis=-1)
    gates = jax.nn.softmax(best_s, axis=-1).astype(h.dtype)

    nb = (B * S) // PEER_BLOCK
    hb = h.reshape(nb, PEER_BLOCK, D)
    ib = idx.reshape(nb, PEER_BLOCK, PEER_HEADS, PEER_TOPK)
    gb = gates.reshape(nb, PEER_BLOCK, PEER_HEADS, PEER_TOPK)

    def one_block(args):
        hx, ix, gx = args
        act = jax.nn.gelu(jnp.einsum('td,tpkd->tpk', hx, u_tab[ix]), approximate=False)
        return jnp.einsum('tpk,tpkd->td', gx * act, v_tab[ix])

    y = lax.map(one_block, (hb, ib, gb))
    return y.reshape(B, S, D)


def setup_inputs(seed: int = 0) -> dict:
    key = jax.random.key(seed)
    ks = jax.random.split(key, 24)
    f32 = jnp.float32
    L = DEPTH

    def nrm(k, shape, scale):
        return jax.random.normal(k, shape, f32) * scale

    def gain(k, shape):
        return 1.0 + 0.02 * jax.random.normal(k, shape, f32)

    x = nrm(ks[0], (BATCH, SEQ, D_MODEL), 1.0)
    positions = (jnp.arange(SEQ, dtype=jnp.int32)[None, :]
                 + jax.random.randint(ks[1], (BATCH, 1), 0, 1024, dtype=jnp.int32))
    return {
        'x': x,
        'positions': positions,
        'attn_norm': gain(ks[2], (L, D_MODEL)),
        'w_in': nrm(ks[3], (L, D_MODEL, IN_WIDTH), D_MODEL ** -0.5),
        'hg_lb_logits': nrm(ks[4], (DEPTH + 1, 2, HG_HEADS * HG_DK), 0.5),
        'hg_o_norm': gain(ks[5], (L, HG_HEADS, HG_DV)),
        'q_a_norm': gain(ks[6], (L, MLA_Q_LORA)),
        'w_q_up': nrm(ks[7], (L, MLA_Q_LORA, MLA_HEADS * MLA_QK), MLA_Q_LORA ** -0.5),
        'kv_a_norm': gain(ks[8], (L, MLA_KV_LORA)),
        'w_kv_up': nrm(ks[9], (L, MLA_KV_LORA, MLA_HEADS * (MLA_NOPE + MLA_V)), MLA_KV_LORA ** -0.5),
        'q_norm': gain(ks[10], (L, MLA_QK)),
        'k_norm': gain(ks[11], (L, MLA_QK)),
        'mla_o_norm': gain(ks[12], (L, MLA_HEADS, MLA_V)),
        'w_out': nrm(ks[13], (L, MIX_WIDTH, D_MODEL), MIX_WIDTH ** -0.5),
        'ffn_norm': gain(ks[14], (L, D_MODEL)),
        'peer_w_q': nrm(ks[15], (L, D_MODEL, PEER_HEADS * PEER_DKEY), D_MODEL ** -0.5),
        'peer_sub_keys': nrm(ks[16], (L, PEER_HEADS, 2, PEER_NKEYS, PEER_DHALF), PEER_DHALF ** -0.5),
        'peer_u': nrm(ks[17], (L, PEER_EXPERTS, D_MODEL), D_MODEL ** -0.5),
        'peer_v': nrm(ks[18], (L, PEER_EXPERTS, D_MODEL), 0.3),
    }


def reference(x, positions, attn_norm, w_in, hg_lb_logits, hg_o_norm, q_a_norm, w_q_up,
              kv_a_norm, w_kv_up, q_norm, k_norm, mla_o_norm, w_out, ffn_norm,
              peer_w_q, peer_sub_keys, peer_u, peer_v):
    lower_bounds = jnp.cumsum(jax.nn.softmax(hg_lb_logits.astype(jnp.float32), axis=0), axis=0)
    cos, sin = rope_tables(positions)
    offsets = []
    acc = 0
    for w in IN_SPLITS[:-1]:
        acc += w
        offsets.append(acc)
    for l in range(DEPTH):
        h = rms_norm(x, attn_norm[l])
        proj = h @ w_in[l]
        hq, hf_fwd, hf_bwd, hi, hg, cq, ckv, kr = jnp.split(proj, offsets, axis=-1)
        y_hg = hgrn2_mixer(hq, hf_fwd, hf_bwd, hi, hg, lower_bounds[l], hg_o_norm[l])
        y_mla = mla_mixer(cq, ckv, kr, cos, sin, q_a_norm[l], w_q_up[l], kv_a_norm[l],
                          w_kv_up[l], q_norm[l], k_norm[l], mla_o_norm[l])
        x = x + jnp.concatenate([y_hg, y_mla], axis=-1) @ w_out[l]
        x = x + peer_ffn(rms_norm(x, ffn_norm[l]), peer_w_q[l], peer_sub_keys[l],
                         peer_u[l], peer_v[l])
    return x
```

```python
import functools
import math

import jax
import jax.numpy as jnp
from jax import lax
from jax.experimental import pallas as pl
from jax.experimental.pallas import tpu as pltpu

F32 = jnp.float32
BF16 = jnp.bfloat16
EPS = 1e-6
LANES = 128
SUBLANES = 8
NEG_INF = float("-inf")

HG_HEADS = 4
HG_DK = 128
HG_CHUNK = 64
HG_HALF = HG_CHUNK // 2
MLA_HEADS = 4
MLA_Q_LORA = 384
MLA_KV_LORA = 256
MLA_NOPE = 128
MLA_ROPE = 64
MLA_V = 128
MLA_QK = MLA_NOPE + MLA_ROPE
MLA_QPAD = 2 * LANES
ROPE_THETA = 10000.0
PEER_HEADS = 8
PEER_NKEYS = 128
PEER_TOPK = 16
PEER_DHALF = 128

VMEM_LIMIT = 56 * 1024 * 1024


def _cparams(sem):
    return pltpu.CompilerParams(dimension_semantics=sem, vmem_limit_bytes=VMEM_LIMIT)


def _rms(x, gain):
    ms = jnp.mean(x * x, axis=-1, keepdims=True)
    return x * lax.rsqrt(ms + EPS) * gain


def _sigmoid(x):
    return 1.0 / (1.0 + jnp.exp(-x))


def _dot_nt(a, b):
    return lax.dot_general(a, b, (((1,), (1,)), ((), ())), preferred_element_type=F32)


def _dot_tn(a, b):
    return lax.dot_general(a, b, (((0,), (0,)), ((), ())), preferred_element_type=F32)


def _in_proj_kernel(x_ref, g_ref, whg_ref, wmla_ref, hg_ref, mla_ref):
    h = _rms(x_ref[...], g_ref[...]).astype(BF16)
    hg_ref[...] = jnp.dot(h, whg_ref[...], preferred_element_type=F32)
    mla_ref[...] = jnp.dot(h, wmla_ref[...], preferred_element_type=F32)


def _in_proj(x2, gain, w_hg, w_mla, tm):
    T, D = x2.shape
    n_hg, n_mla = w_hg.shape[1], w_mla.shape[1]
    return pl.pallas_call(
        _in_proj_kernel,
        grid=(T // tm,),
        in_specs=[
            pl.BlockSpec((tm, D), lambda i: (i, 0)),
            pl.BlockSpec((1, D), lambda i: (0, 0)),
            pl.BlockSpec((D, n_hg), lambda i: (0, 0)),
            pl.BlockSpec((D, n_mla), lambda i: (0, 0)),
        ],
        out_specs=[
            pl.BlockSpec((tm, n_hg), lambda i: (i, 0)),
            pl.BlockSpec((tm, n_mla), lambda i: (i, 0)),
        ],
        out_shape=[jax.ShapeDtypeStruct((T, n_hg), F32), jax.ShapeDtypeStruct((T, n_mla), F32)],
        compiler_params=_cparams(("parallel",)),
        name="in_proj",
    )(x2, gain, w_hg, w_mla)


def _hgrn_kernel(q_ref, f_ref, v_ref, lb_ref, o_ref, st_ref, b_scr, k_scr, p_scr, *, reverse, n_chunks):
    C, Hf = HG_CHUNK, HG_HALF

    @pl.when(pl.program_id(2) == 0)
    def _():
        st_ref[...] = jnp.zeros_like(st_ref)

    logits = lb_ref[...]
    ex = jnp.exp(logits - jnp.max(logits, axis=0, keepdims=True))
    lb = ex[0:1, :] / jnp.sum(ex, axis=0, keepdims=True)

    row = lax.broadcasted_iota(jnp.int32, (C, C), 0)
    col = lax.broadcasted_iota(jnp.int32, (C, C), 1)
    tri = (col >= row) if reverse else (col <= row)
    cum_mat = tri.astype(F32)
    same_half = (row >= Hf) == (col >= Hf)
    row_c = lax.broadcasted_iota(jnp.int32, (C, LANES), 0)
    q_side = (row_c < Hf) if reverse else (row_c >= Hf)
    lane_h = lax.broadcasted_iota(jnp.int32, (Hf, LANES), 1)
    ones = jnp.ones((LANES, LANES), BF16)
    mid_row = Hf if reverse else Hf - 1
    end_row = 0 if reverse else C - 1

    def chunk_body(ci, carry):
        c = (n_chunks - 1 - ci) if reverse else ci
        r0 = pl.multiple_of(c * C, C)
        q_raw = q_ref[pl.ds(r0, C), :]
        q = q_raw * _sigmoid(q_raw)
        f = lb + (1.0 - lb) * _sigmoid(f_ref[pl.ds(r0, C), :])
        k = 1.0 - f
        vb = v_ref[pl.ds(r0, C), :].astype(BF16)
        b = jnp.dot(cum_mat, jnp.log(f), precision=lax.Precision.HIGHEST,
                    preferred_element_type=F32)
        b_scr[...] = b
        k_scr[...] = k
        b_mid = b_scr[pl.ds(mid_row, 1), :]
        b_end = b_scr[pl.ds(end_row, 1), :]

        st = st_ref[...]
        o = _dot_nt((q * jnp.exp(b)).astype(BF16), st.astype(BF16))
        kd = (k * jnp.exp(b_end - b)).astype(BF16)
        st_ref[...] = st * jnp.exp(b_end) + _dot_tn(vb, kd)

        qt = jnp.where(q_side, q * jnp.exp(jnp.minimum(b - b_mid, 0.0)), 0.0).astype(BF16)
        kt = jnp.where(q_side, 0.0, k * jnp.exp(jnp.minimum(b_mid - b, 0.0))).astype(BF16)
        sc_off = _dot_nt(qt, kt)

        for hb in range(2):
            base = hb * Hf
            qb = q[base:base + Hf, :]
            bb = b[base:base + Hf, :]
            for s in range(Hf):
                bs = b_scr[pl.ds(base + s, 1), :]
                ks = k_scr[pl.ds(base + s, 1), :]
                p = (qb * ks) * jnp.exp(jnp.minimum(bb - bs, 0.0))
                p_scr[pl.ds((base + s) * Hf, Hf), :] = p.astype(BF16)
        rs = jnp.dot(p_scr[...], ones, preferred_element_type=F32)
        halves = []
        for hb in range(2):
            base = hb * Hf
            acc = jnp.zeros((Hf, LANES), F32)
            for s in range(Hf):
                r1 = (base + s) * Hf
                acc = jnp.where(lane_h == base + s, rs[r1:r1 + Hf, :], acc)
            halves.append(acc)
        sc_diag = jnp.concatenate(halves, axis=0)[:, :C]
        scores = jnp.where(same_half, jnp.where(tri, sc_diag, 0.0), sc_off)
        o = o + jnp.dot(scores.astype(BF16), vb, preferred_element_type=F32)
        o_ref[pl.ds(r0, C), :] = o
        return carry

    lax.fori_loop(0, n_chunks, chunk_body, 0)


def _hgrn_scan(hg, lb_logits_dir, B, S, reverse, tb):
    T = hg.shape[0]
    nblk = S // tb
    f_col0 = (2 if reverse else 1) * HG_HEADS
    v_col0 = 3 * HG_HEADS

    def tmap(b, i):
        return b * nblk + ((nblk - 1 - i) if reverse else i)

    kern = functools.partial(_hgrn_kernel, reverse=reverse, n_chunks=tb // HG_CHUNK)
    return pl.pallas_call(
        kern,
        grid=(B, HG_HEADS, nblk),
        in_specs=[
            pl.BlockSpec((tb, HG_DK), lambda b, h, i: (tmap(b, i), h)),
            pl.BlockSpec((tb, HG_DK), lambda b, h, i: (tmap(b, i), f_col0 + h)),
            pl.BlockSpec((tb, HG_DK), lambda b, h, i: (tmap(b, i), v_col0 + h)),
            pl.BlockSpec((lb_logits_dir.shape[0], HG_DK), lambda b, h, i: (0, h)),
        ],
        out_specs=pl.BlockSpec((tb, HG_DK), lambda b, h, i: (tmap(b, i), h)),
        out_shape=jax.ShapeDtypeStruct((T, HG_HEADS * HG_DK), F32),
        scratch_shapes=[
            pltpu.VMEM((HG_DK, HG_DK), F32),
            pltpu.VMEM((HG_CHUNK, HG_DK), F32),
            pltpu.VMEM((HG_CHUNK, HG_DK), F32),
            pltpu.VMEM((HG_CHUNK * HG_HALF, HG_DK), BF16),
        ],
        compiler_params=_cparams(("parallel", "parallel", "arbitrary")),
        name="hgrn_rev" if reverse else "hgrn_fwd",
    )(hg, hg, hg, lb_logits_dir)


def _mla_prep_kernel(mla_ref, pos_ref, invf_ref, qa_ref, kva_ref, wq_ref, wkv_ref,
                     qgn_ref, qgr_ref, kgn_ref, kgr_ref, q_ref, k_ref, v_ref):
    tm = mla_ref.shape[0]
    lane = lax.broadcasted_iota(jnp.int32, (tm, LANES), 1)
    ang = pos_ref[...].astype(F32) * invf_ref[...]
    cos_t = jnp.where(lane < MLA_ROPE, jnp.cos(ang), 0.0)
    sin_a = jnp.sin(ang)
    sin_t = jnp.where(lane < MLA_ROPE // 2, -sin_a, jnp.where(lane < MLA_ROPE, sin_a, 0.0))

    def rope(t):
        swapped = jnp.where(lane < MLA_ROPE // 2,
                            pltpu.roll(t, LANES - MLA_ROPE // 2, 1),
                            pltpu.roll(t, MLA_ROPE // 2, 1))
        return t * cos_t + swapped * sin_t

    def norm_rope_part(t, gain):
        ms = jnp.sum(t * t, axis=-1, keepdims=True) * (1.0 / MLA_ROPE)
        return t * lax.rsqrt(ms + EPS) * gain

    cq = mla_ref[:, 0:MLA_Q_LORA]
    ckv = mla_ref[:, MLA_Q_LORA:MLA_Q_LORA + MLA_KV_LORA]
    kr = mla_ref[:, MLA_Q_LORA + MLA_KV_LORA:]
    q_up = jnp.dot(_rms(cq, qa_ref[...]).astype(BF16), wq_ref[...], preferred_element_type=F32)
    kv_up = jnp.dot(_rms(ckv, kva_ref[...]).astype(BF16), wkv_ref[...], preferred_element_type=F32)
    k_rope = rope(norm_rope_part(kr, kgr_ref[...])).astype(BF16)
    for h in range(MLA_HEADS):
        c0 = h * MLA_QPAD
        q_ref[:, c0:c0 + LANES] = _rms(q_up[:, c0:c0 + LANES], qgn_ref[...]).astype(BF16)
        q_ref[:, c0 + LANES:c0 + 2 * LANES] = rope(
            norm_rope_part(q_up[:, c0 + LANES:c0 + 2 * LANES], qgr_ref[...])).astype(BF16)
        k_ref[:, c0:c0 + LANES] = _rms(kv_up[:, c0:c0 + LANES], kgn_ref[...]).astype(BF16)
        k_ref[:, c0 + LANES:c0 + 2 * LANES] = k_rope
        v_ref[:, h * MLA_V:(h + 1) * MLA_V] = kv_up[:, c0 + LANES:c0 + 2 * LANES].astype(BF16)


def _mla_prep(mla, pos, invf, qa, kva, wq, wkv, qgn, qgr, kgn, kgr, tm):
    T, n_mla = mla.shape
    full = lambda a: pl.BlockSpec(a.shape, lambda i: (0, 0))
    return pl.pallas_call(
        _mla_prep_kernel,
        grid=(T // tm,),
        in_specs=[
            pl.BlockSpec((tm, n_mla), lambda i: (i, 0)),
            pl.BlockSpec((tm, 1), lambda i: (i, 0)),
            full(invf), full(qa), full(kva), full(wq), full(wkv),
            full(qgn), full(qgr), full(kgn), full(kgr),
        ],
        out_specs=[
            pl.BlockSpec((tm, MLA_HEADS * MLA_QPAD), lambda i: (i, 0)),
            pl.BlockSpec((tm, MLA_HEADS * MLA_QPAD), lambda i: (i, 0)),
            pl.BlockSpec((tm, MLA_HEADS * MLA_V), lambda i: (i, 0)),
        ],
        out_shape=[
            jax.ShapeDtypeStruct((T, MLA_HEADS * MLA_QPAD), BF16),
            jax.ShapeDtypeStruct((T, MLA_HEADS * MLA_QPAD), BF16),
            jax.ShapeDtypeStruct((T, MLA_HEADS * MLA_V), BF16),
        ],
        compiler_params=_cparams(("parallel",)),
        name="mla_prep",
    )(mla, pos, invf, qa, kva, wq, wkv, qgn, qgr, kgn, kgr)


def _attn_kernel(q_ref, k_ref, v_ref, g_ref, o_ref):
    scale = MLA_QK ** -0.5
    s = _dot_nt(q_ref[...], k_ref[...])
    m = jnp.max(s, axis=-1, keepdims=True)
    p = jnp.exp((s - m) * scale)
    l = jnp.sum(p, axis=-1, keepdims=True)
    o = jnp.dot(p.astype(BF16), v_ref[...], preferred_element_type=F32) / l
    o_ref[...] = _rms(o, g_ref[...]).astype(o_ref.dtype)


def _attention(q, k, v, o_gain, B, S, tq):
    T = q.shape[0]
    nq = S // tq
    return pl.pallas_call(
        _attn_kernel,
        grid=(B, MLA_HEADS, nq),
        in_specs=[
            pl.BlockSpec((tq, MLA_QPAD), lambda b, h, i: (b * nq + i, h)),
            pl.BlockSpec((S, MLA_QPAD), lambda b, h, i: (b, h)),
            pl.BlockSpec((S, MLA_V), lambda b, h, i: (b, h)),
            pl.BlockSpec((1, MLA_V), lambda b, h, i: (0, h)),
        ],
        out_specs=pl.BlockSpec((tq, MLA_V), lambda b, h, i: (b * nq + i, h)),
        out_shape=jax.ShapeDtypeStruct((T, MLA_HEADS * MLA_V), BF16),
        compiler_params=_cparams(("parallel", "parallel", "arbitrary")),
        name="mla_attention",
    )(q, k, v, o_gain)


def _out_proj_kernel(of_ref, ob_ref, g_ref, ym_ref, x_ref, hgg_ref, wo_ref, fg_ref, wq_ref, keys_ref,
                     x1_ref, h2_ref, st_ref):
    o = of_ref[...] + ob_ref[...]
    gate_raw = g_ref[...]
    gate = gate_raw * _sigmoid(gate_raw)
    parts = []
    for h in range(HG_HEADS):
        sl = slice(h * HG_DK, (h + 1) * HG_DK)
        parts.append(_rms(o[:, sl], hgg_ref[:, sl]) * gate[:, sl])
    y_hg = jnp.concatenate(parts, axis=-1).astype(BF16)
    w_hg_rows = y_hg.shape[1]
    mix = jnp.dot(y_hg, wo_ref[0:w_hg_rows, :], preferred_element_type=F32)
    mix = mix + jnp.dot(ym_ref[...], wo_ref[w_hg_rows:, :], preferred_element_type=F32)
    x1 = x_ref[...] + mix
    x1_ref[...] = x1
    h2 = _rms(x1, fg_ref[...]).astype(BF16)
    h2_ref[...] = h2
    pq = jnp.dot(h2, wq_ref[...], preferred_element_type=F32).astype(BF16)
    for pc in range(keys_ref.shape[0]):
        st_ref[pc] = _dot_nt(keys_ref[pc], pq[:, pc * PEER_DHALF:(pc + 1) * PEER_DHALF])


def _out_proj(o_f, o_b, hg, y_mla, x2, hg_gain, w_out, ffn_gain, w_pq, keys, tm):
    T, D = x2.shape
    W = o_f.shape[1]
    npc = keys.shape[0]
    full2 = lambda a: pl.BlockSpec(a.shape, lambda i: (0, 0))
    return pl.pallas_call(
        _out_proj_kernel,
        grid=(T // tm,),
        in_specs=[
            pl.BlockSpec((tm, W), lambda i: (i, 0)),
            pl.BlockSpec((tm, W), lambda i: (i, 0)),
            pl.BlockSpec((tm, W), lambda i: (i, 4)),
            pl.BlockSpec((tm, y_mla.shape[1]), lambda i: (i, 0)),
            pl.BlockSpec((tm, D), lambda i: (i, 0)),
            full2(hg_gain), full2(w_out), full2(ffn_gain), full2(w_pq),
            pl.BlockSpec(keys.shape, lambda i: (0, 0, 0)),
        ],
        out_specs=[
            pl.BlockSpec((tm, D), lambda i: (i, 0)),
            pl.BlockSpec((tm, D), lambda i: (i, 0)),
            pl.BlockSpec((npc, PEER_NKEYS, tm), lambda i: (0, 0, i)),
        ],
        out_shape=[
            jax.ShapeDtypeStruct((T, D), F32),
            jax.ShapeDtypeStruct((T, D), BF16),
            jax.ShapeDtypeStruct((npc, PEER_NKEYS, T), F32),
        ],
        compiler_params=_cparams(("parallel",)),
        name="out_proj_peer_query",
    )(o_f, o_b, hg, y_mla, x2, hg_gain, w_out, ffn_gain, w_pq, keys)


def _tree_max(xs):
    xs = list(xs)
    while len(xs) > 1:
        nxt = [jnp.maximum(xs[i], xs[i + 1]) for i in range(0, len(xs) - 1, 2)]
        if len(xs) % 2:
            nxt.append(xs[-1])
        xs = nxt
    return xs[0]


def _peer_select_kernel(s_ref, tau_ref, m1_ref, m2_ref, iz_ref, top_scr):
    n_keep = PEER_TOPK + 1
    for p in range(PEER_HEADS):
        for c in range(2):
            cur = s_ref[2 * p + c]
            for r in range(n_keep):
                m = jnp.max(cur, axis=0, keepdims=True)
                top_scr[c, r, pl.ds(p, 1), :] = m
                if r + 1 < n_keep:
                    cur = jnp.where(cur == m, NEG_INF, cur)
    a = [top_scr[0, r] for r in range(n_keep)]
    b = [top_scr[1, r] for r in range(n_keep)]
    cands = [a[i] + b[j] for i in range(n_keep) for j in range(n_keep) if (i + 1) * (j + 1) <= n_keep]
    top = a[0] + b[0]
    z = jnp.zeros_like(top)
    kth = top
    for r in range(PEER_TOPK):
        kth = _tree_max(cands)
        z = z + jnp.exp(kth - top)
        cands = [jnp.where(cv == kth, NEG_INF, cv) for cv in cands]
    nxt = _tree_max(cands)
    tau_ref[...] = 0.5 * (kth + nxt)
    m1_ref[...] = a[0]
    m2_ref[...] = b[0]
    iz_ref[...] = 1.0 / z


def _peer_select(scores_t, tt):
    npc, nk, T = scores_t.shape
    out = jax.ShapeDtypeStruct((PEER_HEADS, T), F32)
    ospec = pl.BlockSpec((PEER_HEADS, tt), lambda i: (0, i))
    return pl.pallas_call(
        _peer_select_kernel,
        grid=(T // tt,),
        in_specs=[pl.BlockSpec((npc, nk, tt), lambda i: (0, 0, i))],
        out_specs=[ospec] * 4,
        out_shape=[out] * 4,
        scratch_shapes=[pltpu.VMEM((2, PEER_TOPK + 1, PEER_HEADS, tt), F32)],
        compiler_params=_cparams(("parallel",)),
        name="peer_select",
    )(scores_t)


def _gelu_exact(x):
    return 0.5 * x * (1.0 + lax.erf(x * (1.0 / math.sqrt(2.0))))


def _peer_dense_kernel(h_ref, u_ref, vt_ref, s_ref, tau_ref, m1_ref, m2_ref, iz_ref, yt_ref,
                       xs_scr, e1_scr, e2_scr, a_scr, p_scr):
    jt = pl.program_id(1)
    te, tt = a_scr.shape
    nb = te // PEER_NKEYS
    n_grp = nb // SUBLANES

    @pl.when(jt == 0)
    def _():
        yt_ref[...] = jnp.zeros_like(yt_ref)
        grouped = (PEER_NKEYS // SUBLANES, SUBLANES, tt)
        for p in range(PEER_HEADS):
            s1 = s_ref[2 * p]
            s2 = s_ref[2 * p + 1]
            xs_scr[p] = (tau_ref[pl.ds(p, 1), :] - s1).reshape(grouped)
            e1_scr[p] = jnp.exp(s1 - m1_ref[pl.ds(p, 1), :]).reshape(grouped)
            e2_scr[p] = jnp.exp(s2 - m2_ref[pl.ds(p, 1), :]) * iz_ref[pl.ds(p, 1), :]

    a_scr[...] = _dot_nt(u_ref[...], h_ref[...])
    for ib in range(nb):
        grp = jt * n_grp + ib // SUBLANES
        sub = pl.ds(ib % SUBLANES, 1)
        for st in range(tt // LANES):
            sl = pl.ds(st * LANES, LANES)
            gate = jnp.zeros((PEER_NKEYS, LANES), F32)
            for p in range(PEER_HEADS):
                x = xs_scr[p, grp, sub, sl]
                e1 = e1_scr[p, grp, sub, sl]
                gate = gate + jnp.where(s_ref[2 * p + 1, :, sl] >= x, e2_scr[p, :, sl], 0.0) * e1
            rows = pl.ds(ib * PEER_NKEYS, PEER_NKEYS)
            p_scr[rows, sl] = (_gelu_exact(a_scr[rows, sl]) * gate).astype(BF16)
    yt_ref[...] += jnp.dot(vt_ref[...], p_scr[...], preferred_element_type=F32)


def _peer_dense(h2, u_bf, vt_bf, scores_t, tau, m1, m2, iz, tt, te):
    T, D = h2.shape
    E = u_bf.shape[0]
    npc, nk, _ = scores_t.shape
    assert te % (PEER_NKEYS * SUBLANES) == 0 and E % te == 0
    row = pl.BlockSpec((PEER_HEADS, tt), lambda t, j: (0, t))
    return pl.pallas_call(
        _peer_dense_kernel,
        grid=(T // tt, E // te),
        in_specs=[
            pl.BlockSpec((tt, D), lambda t, j: (t, 0)),
            pl.BlockSpec((te, D), lambda t, j: (j, 0)),
            pl.BlockSpec((D, te), lambda t, j: (0, j)),
            pl.BlockSpec((npc, nk, tt), lambda t, j: (0, 0, t)),
            row, row, row, row,
        ],
        out_specs=pl.BlockSpec((D, tt), lambda t, j: (0, t)),
        out_shape=jax.ShapeDtypeStruct((D, T), F32),
        scratch_shapes=[
            pltpu.VMEM((PEER_HEADS, nk // SUBLANES, SUBLANES, tt), F32),
            pltpu.VMEM((PEER_HEADS, nk // SUBLANES, SUBLANES, tt), F32),
            pltpu.VMEM((PEER_HEADS, nk, tt), F32),
            pltpu.VMEM((te, tt), F32),
            pltpu.VMEM((te, tt), BF16),
        ],
        compiler_params=_cparams(("parallel", "arbitrary")),
        name="peer_dense",
    )(h2, u_bf, vt_bf, scores_t, tau, m1, m2, iz)


def _finish_kernel(x1_ref, yt_ref, o_ref):
    o_ref[...] = x1_ref[...] + yt_ref[...].T


def _finish(x1, yt, tm):
    T, D = x1.shape
    return pl.pallas_call(
        _finish_kernel,
        grid=(T // tm,),
        in_specs=[pl.BlockSpec((tm, D), lambda i: (i, 0)), pl.BlockSpec((D, tm), lambda i: (0, i))],
        out_specs=pl.BlockSpec((tm, D), lambda i: (i, 0)),
        out_shape=jax.ShapeDtypeStruct((T, D), F32),
        compiler_params=_cparams(("parallel",)),
        name="finish",
    )(x1, yt)


def _pick(n, pref):
    t = min(n, pref)
    assert n % t == 0, (n, t)
    return t


def _layer(x2, pos, B, S, attn_norm, w_in, lb_logits, hg_o_norm, q_a_norm, w_q_up, kv_a_norm,
           w_kv_up, q_norm, k_norm, mla_o_norm, w_out, ffn_norm, peer_w_q, peer_sub_keys, peer_u, peer_v):
    T, D = x2.shape
    n_hg = 5 * HG_HEADS * HG_DK
    row = lambda a: a.reshape(1, -1).astype(F32)

    w_hg = w_in[:, :n_hg].astype(BF16)
    n_mla = w_in.shape[1] - n_hg
    w_mla = jnp.pad(w_in[:, n_hg:], ((0, 0), (0, -n_mla % LANES))).astype(BF16)
    pad_r = LANES - MLA_ROPE
    wq = w_q_up.reshape(MLA_Q_LORA, MLA_HEADS, MLA_QK)
    wq = jnp.pad(wq, ((0, 0), (0, 0), (0, MLA_QPAD - MLA_QK))).reshape(MLA_Q_LORA, MLA_HEADS * MLA_QPAD)
    wq = wq.astype(BF16)
    wkv = w_kv_up.astype(BF16)
    qgn, qgr = row(q_norm[:MLA_NOPE]), row(jnp.pad(q_norm[MLA_NOPE:], (0, pad_r)))
    kgn, kgr = row(k_norm[:MLA_NOPE]), row(jnp.pad(k_norm[MLA_NOPE:], (0, pad_r)))
    inv_freq = 1.0 / (ROPE_THETA ** (jnp.arange(0, MLA_ROPE, 2, dtype=F32) / MLA_ROPE))
    invf = row(jnp.concatenate([inv_freq, inv_freq, jnp.zeros((pad_r,), F32)]))
    keys = peer_sub_keys.reshape(2 * PEER_HEADS, PEER_NKEYS, PEER_DHALF).astype(BF16)
    u_bf = peer_u.astype(BF16)
    vt_bf = peer_v.astype(BF16).T

    hg, mla = _in_proj(x2, row(attn_norm), w_hg, w_mla, _pick(T, 256))
    tb = _pick(S, 512)
    o_f = _hgrn_scan(hg, lb_logits[:, 0, :], B, S, False, tb)
    o_b = _hgrn_scan(hg, lb_logits[:, 1, :], B, S, True, tb)
    q, k, v = _mla_prep(mla, pos, invf, row(q_a_norm), row(kv_a_norm), wq, wkv, qgn, qgr, kgn, kgr,
                        _pick(T, 256))
    y_mla = _attention(q, k, v, mla_o_norm.reshape(1, -1).astype(F32), B, S, _pick(S, 512))
    x1, h2, scores_t = _out_proj(o_f, o_b, hg, y_mla, x2, row(hg_o_norm), w_out.astype(BF16),
                                 row(ffn_norm), peer_w_q.astype(BF16), keys, _pick(T, 256))
    tau, m1, m2, iz = _peer_select(scores_t, _pick(T, 256))
    yt = _peer_dense(h2, u_bf, vt_bf, scores_t, tau, m1, m2, iz, _pick(T, 512), 1024)
    return _finish(x1, yt, _pick(T, 256))


def kernel(x, positions, attn_norm, w_in, hg_lb_logits, hg_o_norm, q_a_norm, w_q_up, kv_a_norm, w_kv_up,
           q_norm, k_norm, mla_o_norm, w_out, ffn_norm, peer_w_q, peer_sub_keys, peer_u, peer_v):
    B, S, D = x.shape
    x2 = x.reshape(B * S, D)
    pos = positions.reshape(B * S, 1)
    for l in range(attn_norm.shape[0]):
        assert l == 0
        x2 = _layer(x2, pos, B, S, attn_norm[l], w_in[l], hg_lb_logits, hg_o_norm[l], q_a_norm[l],
                    w_q_up[l], kv_a_norm[l], w_kv_up[l], q_norm[l], k_norm[l], mla_o_norm[l], w_out[l],
                    ffn_norm[l], peer_w_q[l], peer_sub_keys[l], peer_u[l], peer_v[l])
    return x2.reshape(B, S, D)
```

```python
import functools
import math

import jax
import jax.numpy as jnp
from jax import lax
from jax.experimental import pallas as pl
from jax.experimental.pallas import tpu as pltpu

F32 = jnp.float32
BF16 = jnp.bfloat16
EPS = 1e-6
LANES = 128
SUBLANES = 8
NEG_INF = float("-inf")

HG_HEADS = 4
HG_DK = 128
HG_CHUNK = 64
HG_HALF = HG_CHUNK // 2
MLA_HEADS = 4
MLA_Q_LORA = 384
MLA_KV_LORA = 256
MLA_NOPE = 128
MLA_ROPE = 64
MLA_V = 128
MLA_QK = MLA_NOPE + MLA_ROPE
MLA_QPAD = 2 * LANES
ROPE_THETA = 10000.0
PEER_HEADS = 8
PEER_NKEYS = 128
PEER_TOPK = 16
PEER_DHALF = 128

VMEM_LIMIT = 56 * 1024 * 1024


def _cparams(sem):
    return pltpu.CompilerParams(dimension_semantics=sem, vmem_limit_bytes=VMEM_LIMIT)


def _rms(x, gain):
    ms = jnp.mean(x * x, axis=-1, keepdims=True)
    return x * lax.rsqrt(ms + EPS) * gain


def _sigmoid(x):
    return 1.0 / (1.0 + jnp.exp(-x))


def _dot_nt(a, b):
    return lax.dot_general(a, b, (((1,), (1,)), ((), ())), preferred_element_type=F32)


def _dot_tn(a, b):
    return lax.dot_general(a, b, (((0,), (0,)), ((), ())), preferred_element_type=F32)


def _in_proj_kernel(x_ref, g_ref, whg_ref, wmla_ref, hg_ref, mla_ref):
    h = _rms(x_ref[...], g_ref[...]).astype(BF16)
    hg_ref[...] = jnp.dot(h, whg_ref[...], preferred_element_type=F32)
    mla_ref[...] = jnp.dot(h, wmla_ref[...], preferred_element_type=F32)


def _in_proj(x2, gain, w_hg, w_mla, tm):
    T, D = x2.shape
    n_hg, n_mla = w_hg.shape[1], w_mla.shape[1]
    return pl.pallas_call(
        _in_proj_kernel,
        grid=(T // tm,),
        in_specs=[
            pl.BlockSpec((tm, D), lambda i: (i, 0)),
            pl.BlockSpec((1, D), lambda i: (0, 0)),
            pl.BlockSpec((D, n_hg), lambda i: (0, 0)),
            pl.BlockSpec((D, n_mla), lambda i: (0, 0)),
        ],
        out_specs=[
            pl.BlockSpec((tm, n_hg), lambda i: (i, 0)),
            pl.BlockSpec((tm, n_mla), lambda i: (i, 0)),
        ],
        out_shape=[jax.ShapeDtypeStruct((T, n_hg), F32), jax.ShapeDtypeStruct((T, n_mla), F32)],
        compiler_params=_cparams(("parallel",)),
        name="in_proj",
    )(x2, gain, w_hg, w_mla)


def _hgrn_kernel(q_ref, f_ref, v_ref, lb_ref, o_ref, st_ref, b_scr, k_scr, p_scr, *, reverse, n_chunks):
    C, Hf, G = HG_CHUNK, HG_HALF, SUBLANES
    n_heads = st_ref.shape[0]

    @pl.when(pl.program_id(1) == 0)
    def _():
        st_ref[...] = jnp.zeros_like(st_ref)

    logits = lb_ref[...]
    ex = jnp.exp(logits - jnp.max(logits, axis=0, keepdims=True))
    lb_all = ex[0:1, :] / jnp.sum(ex, axis=0, keepdims=True)

    row = lax.broadcasted_iota(jnp.int32, (C, C), 0)
    col = lax.broadcasted_iota(jnp.int32, (C, C), 1)
    tri = (col >= row) if reverse else (col <= row)
    cum_mat = tri.astype(F32)
    same_half = (row >= Hf) == (col >= Hf)
    row_c = lax.broadcasted_iota(jnp.int32, (C, LANES), 0)
    q_side = (row_c < Hf) if reverse else (row_c >= Hf)
    lane_g = lax.broadcasted_iota(jnp.int32, (G, LANES), 1)
    ones = jnp.ones((LANES, LANES), BF16)
    mid_row = Hf if reverse else Hf - 1
    end_row = 0 if reverse else C - 1

    def live_groups(s):
        return range(0, s // G + 1) if reverse else range(s // G, Hf // G)

    def gates(hh, r0):
        cols = slice(hh * HG_DK, (hh + 1) * HG_DK)
        lb = lb_all[:, cols]
        q_raw = q_ref[pl.ds(r0, C), cols]
        q = q_raw * _sigmoid(q_raw)
        f = lb + (1.0 - lb) * _sigmoid(f_ref[pl.ds(r0, C), cols])
        k = 1.0 - f
        vb = v_ref[pl.ds(r0, C), cols].astype(BF16)
        b = jnp.dot(cum_mat, jnp.log2(f), precision=lax.Precision.HIGHEST,
                    preferred_element_type=F32)
        b_scr[hh] = b
        k_scr[hh] = k
        return q, k, vb, b

    def state_and_offdiag(hh, q, k, vb, b):
        b_mid = b_scr[hh, pl.ds(mid_row, 1), :]
        b_end = b_scr[hh, pl.ds(end_row, 1), :]
        st = st_ref[hh]
        o = _dot_nt((q * jnp.exp2(b)).astype(BF16), st.astype(BF16))
        kd = (k * jnp.exp2(b_end - b)).astype(BF16)
        st_ref[hh] = st * jnp.exp2(b_end) + _dot_tn(vb, kd)
        qt = jnp.where(q_side, q * jnp.exp2(b - b_mid), 0.0).astype(BF16)
        kt = jnp.where(q_side, 0.0, k * jnp.exp2(b_mid - b)).astype(BF16)
        return o, _dot_nt(qt, kt)

    def diag_products(hh, q, b):
        n_rows = 0
        for hb in range(2):
            base = hb * Hf
            for s in range(Hf):
                g0, g1 = live_groups(s)[0], live_groups(s)[-1] + 1
                rows = slice(base + g0 * G, base + g1 * G)
                bs = b_scr[hh, pl.ds(base + s, 1), :]
                ks = k_scr[hh, pl.ds(base + s, 1), :]
                p_scr[hh, pl.ds(n_rows, (g1 - g0) * G), :] = (q[rows, :] * ks) * jnp.exp2(b[rows, :] - bs)
                n_rows += (g1 - g0) * G

    def diag_scores(rs):
        n_rows = 0
        groups = []
        for hb in range(2):
            base = hb * Hf
            acc = [jnp.zeros((G, LANES), F32) for _ in range(Hf // G)]
            for s in range(Hf):
                for g in live_groups(s):
                    acc[g] = jnp.where(lane_g == base + s, rs[n_rows:n_rows + G, :], acc[g])
                    n_rows += G
            groups.extend(acc)
        return jnp.concatenate(groups, axis=0)[:, :C]

    def chunk_body(ci, carry):
        c = (n_chunks - 1 - ci) if reverse else ci
        r0 = pl.multiple_of(c * C, C)
        heads = range(n_heads)
        qkvb = [gates(hh, r0) for hh in heads]
        o_sc = [state_and_offdiag(hh, *qkvb[hh]) for hh in heads]
        for hh in heads:
            diag_products(hh, qkvb[hh][0], qkvb[hh][3])
        rs = [jnp.dot(p_scr[hh].astype(BF16), ones, preferred_element_type=F32) for hh in heads]
        for hh in heads:
            o, sc_off = o_sc[hh]
            scores = jnp.where(same_half, jnp.where(tri, diag_scores(rs[hh]), 0.0), sc_off)
            o = o + jnp.dot(scores.astype(BF16), qkvb[hh][2], preferred_element_type=F32)
            o_ref[pl.ds(r0, C), hh * HG_DK:(hh + 1) * HG_DK] = o
        return carry

    lax.fori_loop(0, n_chunks, chunk_body, 0)


def _hgrn_diag_rows():
    return 2 * sum(HG_HALF - (s // SUBLANES) * SUBLANES for s in range(HG_HALF))


def _hgrn_scan(hg, lb_logits_dir, B, S, reverse, tb):
    T = hg.shape[0]
    W = HG_HEADS * HG_DK
    nblk = S // tb
    f_blk = 2 if reverse else 1
    v_blk = 3

    def tmap(b, i):
        return b * nblk + ((nblk - 1 - i) if reverse else i)

    kern = functools.partial(_hgrn_kernel, reverse=reverse, n_chunks=tb // HG_CHUNK)
    return pl.pallas_call(
        kern,
        grid=(B, nblk),
        in_specs=[
            pl.BlockSpec((tb, W), lambda b, i: (tmap(b, i), 0)),
            pl.BlockSpec((tb, W), lambda b, i: (tmap(b, i), f_blk)),
            pl.BlockSpec((tb, W), lambda b, i: (tmap(b, i), v_blk)),
            pl.BlockSpec((lb_logits_dir.shape[0], W), lambda b, i: (0, 0)),
        ],
        out_specs=pl.BlockSpec((tb, W), lambda b, i: (tmap(b, i), 0)),
        out_shape=jax.ShapeDtypeStruct((T, W), F32),
        scratch_shapes=[
            pltpu.VMEM((HG_HEADS, HG_DK, HG_DK), F32),
            pltpu.VMEM((HG_HEADS, HG_CHUNK, HG_DK), F32),
            pltpu.VMEM((HG_HEADS, HG_CHUNK, HG_DK), F32),
            pltpu.VMEM((HG_HEADS, _hgrn_diag_rows(), HG_DK), F32),
        ],
        compiler_params=_cparams(("parallel", "arbitrary")),
        name="hgrn_rev" if reverse else "hgrn_fwd",
    )(hg, hg, hg, lb_logits_dir)


def _mla_prep_kernel(mla_ref, pos_ref, invf_ref, qa_ref, kva_ref, wq_ref, wkv_ref,
                     qgn_ref, qgr_ref, kgn_ref, kgr_ref, q_ref, k_ref, v_ref):
    tm = mla_ref.shape[0]
    lane = lax.broadcasted_iota(jnp.int32, (tm, LANES), 1)
    ang = pos_ref[...].astype(F32) * invf_ref[...]
    cos_t = jnp.where(lane < MLA_ROPE, jnp.cos(ang), 0.0)
    sin_a = jnp.sin(ang)
    sin_t = jnp.where(lane < MLA_ROPE // 2, -sin_a, jnp.where(lane < MLA_ROPE, sin_a, 0.0))

    def rope(t):
        swapped = jnp.where(lane < MLA_ROPE // 2,
                            pltpu.roll(t, LANES - MLA_ROPE // 2, 1),
                            pltpu.roll(t, MLA_ROPE // 2, 1))
        return t * cos_t + swapped * sin_t

    def norm_rope_part(t, gain):
        ms = jnp.sum(t * t, axis=-1, keepdims=True) * (1.0 / MLA_ROPE)
        return t * lax.rsqrt(ms + EPS) * gain

    cq = mla_ref[:, 0:MLA_Q_LORA]
    ckv = mla_ref[:, MLA_Q_LORA:MLA_Q_LORA + MLA_KV_LORA]
    kr = mla_ref[:, MLA_Q_LORA + MLA_KV_LORA:]
    q_up = jnp.dot(_rms(cq, qa_ref[...]).astype(BF16), wq_ref[...], preferred_element_type=F32)
    kv_up = jnp.dot(_rms(ckv, kva_ref[...]).astype(BF16), wkv_ref[...], preferred_element_type=F32)
    k_rope = rope(norm_rope_part(kr, kgr_ref[...])).astype(BF16)
    for h in range(MLA_HEADS):
        c0 = h * MLA_QPAD
        q_ref[:, c0:c0 + LANES] = _rms(q_up[:, c0:c0 + LANES], qgn_ref[...]).astype(BF16)
        q_ref[:, c0 + LANES:c0 + 2 * LANES] = rope(
            norm_rope_part(q_up[:, c0 + LANES:c0 + 2 * LANES], qgr_ref[...])).astype(BF16)
        k_ref[:, c0:c0 + LANES] = _rms(kv_up[:, c0:c0 + LANES], kgn_ref[...]).astype(BF16)
        k_ref[:, c0 + LANES:c0 + 2 * LANES] = k_rope
        v_ref[:, h * MLA_V:(h + 1) * MLA_V] = kv_up[:, c0 + LANES:c0 + 2 * LANES].astype(BF16)


def _mla_prep(mla, pos, invf, qa, kva, wq, wkv, qgn, qgr, kgn, kgr, tm):
    T, n_mla = mla.shape
    full = lambda a: pl.BlockSpec(a.shape, lambda i: (0, 0))
    return pl.pallas_call(
        _mla_prep_kernel,
        grid=(T // tm,),
        in_specs=[
            pl.BlockSpec((tm, n_mla), lambda i: (i, 0)),
            pl.BlockSpec((tm, 1), lambda i: (i, 0)),
            full(invf), full(qa), full(kva), full(wq), full(wkv),
            full(qgn), full(qgr), full(kgn), full(kgr),
        ],
        out_specs=[
            pl.BlockSpec((tm, MLA_HEADS * MLA_QPAD), lambda i: (i, 0)),
            pl.BlockSpec((tm, MLA_HEADS * MLA_QPAD), lambda i: (i, 0)),
            pl.BlockSpec((tm, MLA_HEADS * MLA_V), lambda i: (i, 0)),
        ],
        out_shape=[
            jax.ShapeDtypeStruct((T, MLA_HEADS * MLA_QPAD), BF16),
            jax.ShapeDtypeStruct((T, MLA_HEADS * MLA_QPAD), BF16),
            jax.ShapeDtypeStruct((T, MLA_HEADS * MLA_V), BF16),
        ],
        compiler_params=_cparams(("parallel",)),
        name="mla_prep",
    )(mla, pos, invf, qa, kva, wq, wkv, qgn, qgr, kgn, kgr)


def _attn_kernel(q_ref, k_ref, v_ref, g_ref, o_ref):
    scale = MLA_QK ** -0.5
    s = _dot_nt(q_ref[...], k_ref[...])
    m = jnp.max(s, axis=-1, keepdims=True)
    p = jnp.exp((s - m) * scale)
    l = jnp.sum(p, axis=-1, keepdims=True)
    o = jnp.dot(p.astype(BF16), v_ref[...], preferred_element_type=F32) / l
    o_ref[...] = _rms(o, g_ref[...]).astype(o_ref.dtype)


def _attention(q, k, v, o_gain, B, S, tq):
    T = q.shape[0]
    nq = S // tq
    return pl.pallas_call(
        _attn_kernel,
        grid=(B, MLA_HEADS, nq),
        in_specs=[
            pl.BlockSpec((tq, MLA_QPAD), lambda b, h, i: (b * nq + i, h)),
            pl.BlockSpec((S, MLA_QPAD), lambda b, h, i: (b, h)),
            pl.BlockSpec((S, MLA_V), lambda b, h, i: (b, h)),
            pl.BlockSpec((1, MLA_V), lambda b, h, i: (0, h)),
        ],
        out_specs=pl.BlockSpec((tq, MLA_V), lambda b, h, i: (b * nq + i, h)),
        out_shape=jax.ShapeDtypeStruct((T, MLA_HEADS * MLA_V), BF16),
        compiler_params=_cparams(("parallel", "parallel", "arbitrary")),
        name="mla_attention",
    )(q, k, v, o_gain)


def _out_proj_kernel(of_ref, ob_ref, g_ref, ym_ref, x_ref, hgg_ref, wo_ref, fg_ref, wq_ref, keys_ref,
                     x1_ref, h2_ref, st_ref):
    o = of_ref[...] + ob_ref[...]
    gate_raw = g_ref[...]
    gate = gate_raw * _sigmoid(gate_raw)
    parts = []
    for h in range(HG_HEADS):
        sl = slice(h * HG_DK, (h + 1) * HG_DK)
        parts.append(_rms(o[:, sl], hgg_ref[:, sl]) * gate[:, sl])
    y_hg = jnp.concatenate(parts, axis=-1).astype(BF16)
    w_hg_rows = y_hg.shape[1]
    mix = jnp.dot(y_hg, wo_ref[0:w_hg_rows, :], preferred_element_type=F32)
    mix = mix + jnp.dot(ym_ref[...], wo_ref[w_hg_rows:, :], preferred_element_type=F32)
    x1 = x_ref[...] + mix
    x1_ref[...] = x1
    h2 = _rms(x1, fg_ref[...]).astype(BF16)
    h2_ref[...] = h2
    pq = jnp.dot(h2, wq_ref[...], preferred_element_type=F32).astype(BF16)
    for pc in range(keys_ref.shape[0]):
        st_ref[pc] = _dot_nt(keys_ref[pc], pq[:, pc * PEER_DHALF:(pc + 1) * PEER_DHALF])


def _out_proj(o_f, o_b, hg, y_mla, x2, hg_gain, w_out, ffn_gain, w_pq, keys, tm):
    T, D = x2.shape
    W = o_f.shape[1]
    npc = keys.shape[0]
    full2 = lambda a: pl.BlockSpec(a.shape, lambda i: (0, 0))
    return pl.pallas_call(
        _out_proj_kernel,
        grid=(T // tm,),
        in_specs=[
            pl.BlockSpec((tm, W), lambda i: (i, 0)),
            pl.BlockSpec((tm, W), lambda i: (i, 0)),
            pl.BlockSpec((tm, W), lambda i: (i, 4)),
            pl.BlockSpec((tm, y_mla.shape[1]), lambda i: (i, 0)),
            pl.BlockSpec((tm, D), lambda i: (i, 0)),
            full2(hg_gain), full2(w_out), full2(ffn_gain), full2(w_pq),
            pl.BlockSpec(keys.shape, lambda i: (0, 0, 0)),
        ],
        out_specs=[
            pl.BlockSpec((tm, D), lambda i: (i, 0)),
            pl.BlockSpec((tm, D), lambda i: (i, 0)),
            pl.BlockSpec((npc, PEER_NKEYS, tm), lambda i: (0, 0, i)),
        ],
        out_shape=[
            jax.ShapeDtypeStruct((T, D), F32),
            jax.ShapeDtypeStruct((T, D), BF16),
            jax.ShapeDtypeStruct((npc, PEER_NKEYS, T), F32),
        ],
        compiler_params=_cparams(("parallel",)),
        name="out_proj_peer_query",
    )(o_f, o_b, hg, y_mla, x2, hg_gain, w_out, ffn_gain, w_pq, keys)


def _tree_max(xs):
    xs = list(xs)
    while len(xs) > 1:
        nxt = [jnp.maximum(xs[i], xs[i + 1]) for i in range(0, len(xs) - 1, 2)]
        if len(xs) % 2:
            nxt.append(xs[-1])
        xs = nxt
    return xs[0]


def _peer_select_kernel(s_ref, tau_ref, m1_ref, iz_ref, btop_ref, top_scr):
    n_keep = PEER_TOPK + 1
    for p in range(PEER_HEADS):
        for c in range(2):
            cur = s_ref[2 * p + c]
            for r in range(n_keep):
                m = jnp.max(cur, axis=0, keepdims=True)
                top_scr[c, r, pl.ds(p, 1), :] = m
                if r + 1 < n_keep:
                    cur = jnp.where(cur == m, NEG_INF, cur)
    a = [top_scr[0, r] for r in range(n_keep)]
    b = [top_scr[1, r] for r in range(n_keep)]
    cands = [a[i] + b[j] for i in range(n_keep) for j in range(n_keep) if (i + 1) * (j + 1) <= n_keep]
    top = a[0] + b[0]
    z = jnp.zeros_like(top)
    kth = top
    for r in range(PEER_TOPK):
        kth = _tree_max(cands)
        z = z + jnp.exp(kth - top)
        cands = [jnp.where(cv == kth, NEG_INF, cv) for cv in cands]
    nxt = _tree_max(cands)
    tau_ref[...] = 0.5 * (kth + nxt)
    m1_ref[...] = a[0]
    iz_ref[...] = 1.0 / z
    for r in range(PEER_TOPK):
        btop_ref[r] = b[r]


def _peer_select(scores_t, tt):
    npc, nk, T = scores_t.shape
    out = jax.ShapeDtypeStruct((PEER_HEADS, T), F32)
    ospec = pl.BlockSpec((PEER_HEADS, tt), lambda i: (0, i))
    return pl.pallas_call(
        _peer_select_kernel,
        grid=(T // tt,),
        in_specs=[pl.BlockSpec((npc, nk, tt), lambda i: (0, 0, i))],
        out_specs=[ospec] * 3 + [pl.BlockSpec((PEER_TOPK, PEER_HEADS, tt), lambda i: (0, 0, i))],
        out_shape=[out] * 3 + [jax.ShapeDtypeStruct((PEER_TOPK, PEER_HEADS, T), F32)],
        scratch_shapes=[pltpu.VMEM((2, PEER_TOPK + 1, PEER_HEADS, tt), F32)],
        compiler_params=_cparams(("parallel",)),
        name="peer_select",
    )(scores_t)


def _gelu_exact(x):
    return 0.5 * x * (1.0 + lax.erf(x * (1.0 / math.sqrt(2.0))))


def _peer_dense_kernel(h_ref, u_ref, vt_ref, s_ref, tau_ref, m1_ref, iz_ref, btop_ref, yt_ref,
                       n_scr, e1_scr, rank_scr, e2_scr, a_scr, p_scr):
    jt = pl.program_id(1)
    te, tt = a_scr.shape
    nb = te // PEER_NKEYS
    n_grp = nb // SUBLANES
    IBQ = 4
    PK = 2 * SUBLANES

    @pl.when(jt == 0)
    def _():
        yt_ref[...] = jnp.zeros_like(yt_ref)
        grouped = (PEER_NKEYS // SUBLANES, SUBLANES, LANES)
        for p in range(PEER_HEADS):
            for st in range(tt // LANES):
                sl = pl.ds(st * LANES, LANES)
                s1 = s_ref[2 * p, :, sl]
                s2 = s_ref[2 * p + 1, :, sl]
                x = tau_ref[pl.ds(p, 1), sl] - s1
                n = jnp.zeros_like(s1)
                rank = jnp.zeros_like(s2)
                for r in range(PEER_TOPK):
                    b_r = btop_ref[r, pl.ds(p, 1), sl]
                    n = jnp.where(b_r >= x, r + 1.0, n)
                    rank = jnp.where(b_r > s2, r + 1.0, rank)
                n_scr[p, :, :, sl] = n.reshape(grouped)
                e1_scr[p, :, :, sl] = jnp.exp(s1 - m1_ref[pl.ds(p, 1), sl]).reshape(grouped)
                rank_scr[p, :, sl] = rank.astype(BF16)
                m2 = btop_ref[0, pl.ds(p, 1), sl]
                e2_scr[p, :, sl] = (jnp.exp(s2 - m2) * iz_ref[pl.ds(p, 1), sl]).astype(BF16)

    a_scr[...] = _dot_nt(u_ref[...], h_ref[...])
    zero = jnp.zeros((PK, LANES), BF16)
    for ibg in range(nb // IBQ):
        for st in range(tt // LANES):
            sl = pl.ds(st * LANES, LANES)
            gates = [[zero for _ in range(PEER_NKEYS // PK)] for _ in range(IBQ)]
            for p in range(PEER_HEADS):
                rank = rank_scr[p, :, sl]
                e2 = e2_scr[p, :, sl]
                for q in range(IBQ):
                    ib = ibg * IBQ + q
                    grp = jt * n_grp + ib // SUBLANES
                    sub = pl.ds(ib % SUBLANES, 1)
                    n = jnp.broadcast_to(n_scr[p, grp, sub, sl], (PK, LANES)).astype(BF16)
                    e1 = jnp.broadcast_to(e1_scr[p, grp, sub, sl], (PK, LANES)).astype(BF16)
                    for r in range(PEER_NKEYS // PK):
                        rr = slice(r * PK, (r + 1) * PK)
                        gates[q][r] = gates[q][r] + jnp.where(rank[rr, :] < n, e2[rr, :], zero) * e1
            for q in range(IBQ):
                rows = pl.ds((ibg * IBQ + q) * PEER_NKEYS, PEER_NKEYS)
                gate = jnp.concatenate(gates[q], axis=0)
                p_scr[rows, sl] = _gelu_exact(a_scr[rows, sl]).astype(BF16) * gate
    yt_ref[...] += jnp.dot(vt_ref[...], p_scr[...], preferred_element_type=F32)


def _peer_dense(h2, u_bf, vt_bf, scores_t, tau, m1, iz, btop, tt, te):
    T, D = h2.shape
    E = u_bf.shape[0]
    npc, nk, _ = scores_t.shape
    assert te % (PEER_NKEYS * SUBLANES) == 0 and E % te == 0
    row = pl.BlockSpec((PEER_HEADS, tt), lambda t, j: (0, t))
    return pl.pallas_call(
        _peer_dense_kernel,
        grid=(T // tt, E // te),
        in_specs=[
            pl.BlockSpec((tt, D), lambda t, j: (t, 0)),
            pl.BlockSpec((te, D), lambda t, j: (j, 0)),
            pl.BlockSpec((D, te), lambda t, j: (0, j)),
            pl.BlockSpec((npc, nk, tt), lambda t, j: (0, 0, t)),
            row, row, row,
            pl.BlockSpec((PEER_TOPK, PEER_HEADS, tt), lambda t, j: (0, 0, t)),
        ],
        out_specs=pl.BlockSpec((D, tt), lambda t, j: (0, t)),
        out_shape=jax.ShapeDtypeStruct((D, T), F32),
        scratch_shapes=[
            pltpu.VMEM((PEER_HEADS, nk // SUBLANES, SUBLANES, tt), F32),
            pltpu.VMEM((PEER_HEADS, nk // SUBLANES, SUBLANES, tt), F32),
            pltpu.VMEM((PEER_HEADS, nk, tt), BF16),
            pltpu.VMEM((PEER_HEADS, nk, tt), BF16),
            pltpu.VMEM((te, tt), F32),
            pltpu.VMEM((te, tt), BF16),
        ],
        compiler_params=_cparams(("parallel", "arbitrary")),
        name="peer_dense",
    )(h2, u_bf, vt_bf, scores_t, tau, m1, iz, btop)


def _finish_kernel(x1_ref, yt_ref, o_ref):
    o_ref[...] = x1_ref[...] + yt_ref[...].T


def _finish(x1, yt, tm):
    T, D = x1.shape
    return pl.pallas_call(
        _finish_kernel,
        grid=(T // tm,),
        in_specs=[pl.BlockSpec((tm, D), lambda i: (i, 0)), pl.BlockSpec((D, tm), lambda i: (0, i))],
        out_specs=pl.BlockSpec((tm, D), lambda i: (i, 0)),
        out_shape=jax.ShapeDtypeStruct((T, D), F32),
        compiler_params=_cparams(("parallel",)),
        name="finish",
    )(x1, yt)


def _pick(n, pref):
    t = min(n, pref)
    assert n % t == 0, (n, t)
    return t


def _layer(x2, pos, B, S, attn_norm, w_in, lb_logits, hg_o_norm, q_a_norm, w_q_up, kv_a_norm,
           w_kv_up, q_norm, k_norm, mla_o_norm, w_out, ffn_norm, peer_w_q, peer_sub_keys, peer_u, peer_v):
    T, D = x2.shape
    n_hg = 5 * HG_HEADS * HG_DK
    row = lambda a: a.reshape(1, -1).astype(F32)

    w_hg = w_in[:, :n_hg].astype(BF16)
    n_mla = w_in.shape[1] - n_hg
    w_mla = jnp.pad(w_in[:, n_hg:], ((0, 0), (0, -n_mla % LANES))).astype(BF16)
    pad_r = LANES - MLA_ROPE
    wq = w_q_up.reshape(MLA_Q_LORA, MLA_HEADS, MLA_QK)
    wq = jnp.pad(wq, ((0, 0), (0, 0), (0, MLA_QPAD - MLA_QK))).reshape(MLA_Q_LORA, MLA_HEADS * MLA_QPAD)
    wq = wq.astype(BF16)
    wkv = w_kv_up.astype(BF16)
    qgn, qgr = row(q_norm[:MLA_NOPE]), row(jnp.pad(q_norm[MLA_NOPE:], (0, pad_r)))
    kgn, kgr = row(k_norm[:MLA_NOPE]), row(jnp.pad(k_norm[MLA_NOPE:], (0, pad_r)))
    inv_freq = 1.0 / (ROPE_THETA ** (jnp.arange(0, MLA_ROPE, 2, dtype=F32) / MLA_ROPE))
    invf = row(jnp.concatenate([inv_freq, inv_freq, jnp.zeros((pad_r,), F32)]))
    keys = peer_sub_keys.reshape(2 * PEER_HEADS, PEER_NKEYS, PEER_DHALF).astype(BF16)
    u_bf = peer_u.astype(BF16)
    vt_bf = peer_v.astype(BF16).T

    hg, mla = _in_proj(x2, row(attn_norm), w_hg, w_mla, _pick(T, 256))
    tb = _pick(S, 512)
    o_f = _hgrn_scan(hg, lb_logits[:, 0, :], B, S, False, tb)
    o_b = _hgrn_scan(hg, lb_logits[:, 1, :], B, S, True, tb)
    q, k, v = _mla_prep(mla, pos, invf, row(q_a_norm), row(kv_a_norm), wq, wkv, qgn, qgr, kgn, kgr,
                        _pick(T, 256))
    y_mla = _attention(q, k, v, mla_o_norm.reshape(1, -1).astype(F32), B, S, _pick(S, 512))
    x1, h2, scores_t = _out_proj(o_f, o_b, hg, y_mla, x2, row(hg_o_norm), w_out.astype(BF16),
                                 row(ffn_norm), peer_w_q.astype(BF16), keys, _pick(T, 256))
    tau, m1, iz, btop = _peer_select(scores_t, _pick(T, 256))
    yt = _peer_dense(h2, u_bf, vt_bf, scores_t, tau, m1, iz, btop, _pick(T, 512), 1024)
    return _finish(x1, yt, _pick(T, 256))


def kernel(x, positions, attn_norm, w_in, hg_lb_logits, hg_o_norm, q_a_norm, w_q_up, kv_a_norm, w_kv_up,
           q_norm, k_norm, mla_o_norm, w_out, ffn_norm, peer_w_q, peer_sub_keys, peer_u, peer_v):
    B, S, D = x.shape
    x2 = x.reshape(B * S, D)
    pos = positions.reshape(B * S, 1)
    for l in range(attn_norm.shape[0]):
        assert l == 0
        x2 = _layer(x2, pos, B, S, attn_norm[l], w_in[l], hg_lb_logits, hg_o_norm[l], q_a_norm[l],
                    w_q_up[l], kv_a_norm[l], w_kv_up[l], q_norm[l], k_norm[l], mla_o_norm[l], w_out[l],
                    ffn_norm[l], peer_w_q[l], peer_sub_keys[l], peer_u[l], peer_v[l])
    return x2.reshape(B, S, D)
```

```python
import functools
import math

import jax
import jax.numpy as jnp
from jax import lax
from jax.experimental import pallas as pl
from jax.experimental.pallas import tpu as pltpu

F32 = jnp.float32
BF16 = jnp.bfloat16
EPS = 1e-6
LANES = 128
SUBLANES = 8
NEG_INF = float("-inf")

HG_HEADS = 4
HG_DK = 128
HG_CHUNK = 64
HG_HALF = HG_CHUNK // 2
MLA_HEADS = 4
MLA_Q_LORA = 384
MLA_KV_LORA = 256
MLA_NOPE = 128
MLA_ROPE = 64
MLA_V = 128
MLA_QK = MLA_NOPE + MLA_ROPE
MLA_QPAD = 2 * LANES
ROPE_THETA = 10000.0
PEER_HEADS = 8
PEER_NKEYS = 128
PEER_TOPK = 16
PEER_DHALF = 128

VMEM_LIMIT = 56 * 1024 * 1024


def _cparams(sem):
    return pltpu.CompilerParams(dimension_semantics=sem, vmem_limit_bytes=VMEM_LIMIT)


def _rms(x, gain):
    ms = jnp.mean(x * x, axis=-1, keepdims=True)
    return x * lax.rsqrt(ms + EPS) * gain


def _sigmoid(x):
    return 1.0 / (1.0 + jnp.exp(-x))


def _dot_nt(a, b):
    return lax.dot_general(a, b, (((1,), (1,)), ((), ())), preferred_element_type=F32)


def _dot_tn(a, b):
    return lax.dot_general(a, b, (((0,), (0,)), ((), ())), preferred_element_type=F32)


def _in_proj_kernel(x_ref, g_ref, whg_ref, wmla_ref, hg_ref, mla_ref):
    h = _rms(x_ref[...], g_ref[...]).astype(BF16)
    hg_ref[...] = jnp.dot(h, whg_ref[...], preferred_element_type=F32)
    mla_ref[...] = jnp.dot(h, wmla_ref[...], preferred_element_type=F32)


def _in_proj(x2, gain, w_hg, w_mla, tm):
    T, D = x2.shape
    n_hg, n_mla = w_hg.shape[1], w_mla.shape[1]
    return pl.pallas_call(
        _in_proj_kernel,
        grid=(T // tm,),
        in_specs=[
            pl.BlockSpec((tm, D), lambda i: (i, 0)),
            pl.BlockSpec((1, D), lambda i: (0, 0)),
            pl.BlockSpec((D, n_hg), lambda i: (0, 0)),
            pl.BlockSpec((D, n_mla), lambda i: (0, 0)),
        ],
        out_specs=[
            pl.BlockSpec((tm, n_hg), lambda i: (i, 0)),
            pl.BlockSpec((tm, n_mla), lambda i: (i, 0)),
        ],
        out_shape=[jax.ShapeDtypeStruct((T, n_hg), F32), jax.ShapeDtypeStruct((T, n_mla), F32)],
        compiler_params=_cparams(("parallel",)),
        name="in_proj",
    )(x2, gain, w_hg, w_mla)


def _hgrn_kernel(q_ref, f_ref, v_ref, lb_ref, o_ref, st_ref, b_scr, k_scr, p_scr, *, reverse, n_chunks):
    C, Hf, G = HG_CHUNK, HG_HALF, SUBLANES
    n_heads = st_ref.shape[0]

    @pl.when(pl.program_id(1) == 0)
    def _():
        st_ref[...] = jnp.zeros_like(st_ref)

    logits = lb_ref[...]
    ex = jnp.exp(logits - jnp.max(logits, axis=0, keepdims=True))
    lb_all = ex[0:1, :] / jnp.sum(ex, axis=0, keepdims=True)

    row = lax.broadcasted_iota(jnp.int32, (C, C), 0)
    col = lax.broadcasted_iota(jnp.int32, (C, C), 1)
    tri = (col >= row) if reverse else (col <= row)
    cum_mat = tri.astype(F32)
    same_half = (row >= Hf) == (col >= Hf)
    row_c = lax.broadcasted_iota(jnp.int32, (C, LANES), 0)
    q_side = (row_c < Hf) if reverse else (row_c >= Hf)
    lane_g = lax.broadcasted_iota(jnp.int32, (G, LANES), 1)
    ones = jnp.ones((LANES, LANES), BF16)
    mid_row = Hf if reverse else Hf - 1
    end_row = 0 if reverse else C - 1

    def live_groups(s):
        return range(0, s // G + 1) if reverse else range(s // G, Hf // G)

    def gates(hh, r0):
        cols = slice(hh * HG_DK, (hh + 1) * HG_DK)
        lb = lb_all[:, cols]
        q_raw = q_ref[pl.ds(r0, C), cols]
        q = q_raw * _sigmoid(q_raw)
        f = lb + (1.0 - lb) * _sigmoid(f_ref[pl.ds(r0, C), cols])
        k = 1.0 - f
        vb = v_ref[pl.ds(r0, C), cols].astype(BF16)
        b = jnp.dot(cum_mat, jnp.log2(f), precision=lax.Precision.HIGHEST,
                    preferred_element_type=F32)
        b_scr[hh] = b
        k_scr[hh] = k
        return q, k, vb, b

    def state_and_offdiag(hh, q, k, vb, b):
        b_mid = b_scr[hh, pl.ds(mid_row, 1), :]
        b_end = b_scr[hh, pl.ds(end_row, 1), :]
        st = st_ref[hh]
        o = _dot_nt((q * jnp.exp2(b)).astype(BF16), st.astype(BF16))
        kd = (k * jnp.exp2(b_end - b)).astype(BF16)
        st_ref[hh] = st * jnp.exp2(b_end) + _dot_tn(vb, kd)
        qt = jnp.where(q_side, q * jnp.exp2(b - b_mid), 0.0).astype(BF16)
        kt = jnp.where(q_side, 0.0, k * jnp.exp2(b_mid - b)).astype(BF16)
        return o, _dot_nt(qt, kt)

    def diag_products(hh, q, b):
        n_rows = 0
        for hb in range(2):
            base = hb * Hf
            for s in range(Hf):
                g0, g1 = live_groups(s)[0], live_groups(s)[-1] + 1
                rows = slice(base + g0 * G, base + g1 * G)
                bs = b_scr[hh, pl.ds(base + s, 1), :]
                ks = k_scr[hh, pl.ds(base + s, 1), :]
                p_scr[hh, pl.ds(n_rows, (g1 - g0) * G), :] = (q[rows, :] * ks) * jnp.exp2(b[rows, :] - bs)
                n_rows += (g1 - g0) * G

    def diag_scores(rs):
        n_rows = 0
        groups = []
        for hb in range(2):
            base = hb * Hf
            acc = [jnp.zeros((G, LANES), F32) for _ in range(Hf // G)]
            for s in range(Hf):
                for g in live_groups(s):
                    acc[g] = jnp.where(lane_g == base + s, rs[n_rows:n_rows + G, :], acc[g])
                    n_rows += G
            groups.extend(acc)
        return jnp.concatenate(groups, axis=0)[:, :C]

    def chunk_body(ci, carry):
        c = (n_chunks - 1 - ci) if reverse else ci
        r0 = pl.multiple_of(c * C, C)
        heads = range(n_heads)
        qkvb = [gates(hh, r0) for hh in heads]
        o_sc = [state_and_offdiag(hh, *qkvb[hh]) for hh in heads]
        for hh in heads:
            diag_products(hh, qkvb[hh][0], qkvb[hh][3])
        rs = [jnp.dot(p_scr[hh].astype(BF16), ones, preferred_element_type=F32) for hh in heads]
        for hh in heads:
            o, sc_off = o_sc[hh]
            scores = jnp.where(same_half, jnp.where(tri, diag_scores(rs[hh]), 0.0), sc_off)
            o = o + jnp.dot(scores.astype(BF16), qkvb[hh][2], preferred_element_type=F32)
            o_ref[pl.ds(r0, C), hh * HG_DK:(hh + 1) * HG_DK] = o
        return carry

    lax.fori_loop(0, n_chunks, chunk_body, 0)


def _hgrn_diag_rows():
    return 2 * sum(HG_HALF - (s // SUBLANES) * SUBLANES for s in range(HG_HALF))


def _hgrn_scan(hg, lb_logits_dir, B, S, reverse, tb):
    T = hg.shape[0]
    W = HG_HEADS * HG_DK
    nblk = S // tb
    f_blk = 2 if reverse else 1
    v_blk = 3

    def tmap(b, i):
        return b * nblk + ((nblk - 1 - i) if reverse else i)

    kern = functools.partial(_hgrn_kernel, reverse=reverse, n_chunks=tb // HG_CHUNK)
    return pl.pallas_call(
        kern,
        grid=(B, nblk),
        in_specs=[
            pl.BlockSpec((tb, W), lambda b, i: (tmap(b, i), 0)),
            pl.BlockSpec((tb, W), lambda b, i: (tmap(b, i), f_blk)),
            pl.BlockSpec((tb, W), lambda b, i: (tmap(b, i), v_blk)),
            pl.BlockSpec((lb_logits_dir.shape[0], W), lambda b, i: (0, 0)),
        ],
        out_specs=pl.BlockSpec((tb, W), lambda b, i: (tmap(b, i), 0)),
        out_shape=jax.ShapeDtypeStruct((T, W), F32),
        scratch_shapes=[
            pltpu.VMEM((HG_HEADS, HG_DK, HG_DK), F32),
            pltpu.VMEM((HG_HEADS, HG_CHUNK, HG_DK), F32),
            pltpu.VMEM((HG_HEADS, HG_CHUNK, HG_DK), F32),
            pltpu.VMEM((HG_HEADS, _hgrn_diag_rows(), HG_DK), F32),
        ],
        compiler_params=_cparams(("parallel", "arbitrary")),
        name="hgrn_rev" if reverse else "hgrn_fwd",
    )(hg, hg, hg, lb_logits_dir)


def _mla_prep_kernel(mla_ref, pos_ref, invf_ref, qa_ref, kva_ref, wq_ref, wkv_ref,
                     qgn_ref, qgr_ref, kgn_ref, kgr_ref, q_ref, k_ref, v_ref):
    tm = mla_ref.shape[0]
    lane = lax.broadcasted_iota(jnp.int32, (tm, LANES), 1)
    ang = pos_ref[...].astype(F32) * invf_ref[...]
    cos_t = jnp.where(lane < MLA_ROPE, jnp.cos(ang), 0.0)
    sin_a = jnp.sin(ang)
    sin_t = jnp.where(lane < MLA_ROPE // 2, -sin_a, jnp.where(lane < MLA_ROPE, sin_a, 0.0))

    def rope(t):
        swapped = jnp.where(lane < MLA_ROPE // 2,
                            pltpu.roll(t, LANES - MLA_ROPE // 2, 1),
                            pltpu.roll(t, MLA_ROPE // 2, 1))
        return t * cos_t + swapped * sin_t

    def norm_rope_part(t, gain):
        ms = jnp.sum(t * t, axis=-1, keepdims=True) * (1.0 / MLA_ROPE)
        return t * lax.rsqrt(ms + EPS) * gain

    cq = mla_ref[:, 0:MLA_Q_LORA]
    ckv = mla_ref[:, MLA_Q_LORA:MLA_Q_LORA + MLA_KV_LORA]
    kr = mla_ref[:, MLA_Q_LORA + MLA_KV_LORA:]
    q_up = jnp.dot(_rms(cq, qa_ref[...]).astype(BF16), wq_ref[...], preferred_element_type=F32)
    kv_up = jnp.dot(_rms(ckv, kva_ref[...]).astype(BF16), wkv_ref[...], preferred_element_type=F32)
    k_rope = rope(norm_rope_part(kr, kgr_ref[...])).astype(BF16)
    for h in range(MLA_HEADS):
        c0 = h * MLA_QPAD
        q_ref[:, c0:c0 + LANES] = _rms(q_up[:, c0:c0 + LANES], qgn_ref[...]).astype(BF16)
        q_ref[:, c0 + LANES:c0 + 2 * LANES] = rope(
            norm_rope_part(q_up[:, c0 + LANES:c0 + 2 * LANES], qgr_ref[...])).astype(BF16)
        k_ref[:, c0:c0 + LANES] = _rms(kv_up[:, c0:c0 + LANES], kgn_ref[...]).astype(BF16)
        k_ref[:, c0 + LANES:c0 + 2 * LANES] = k_rope
        v_ref[:, h * MLA_V:(h + 1) * MLA_V] = kv_up[:, c0 + LANES:c0 + 2 * LANES].astype(BF16)


def _mla_prep(mla, pos, invf, qa, kva, wq, wkv, qgn, qgr, kgn, kgr, tm):
    T, n_mla = mla.shape
    full = lambda a: pl.BlockSpec(a.shape, lambda i: (0, 0))
    return pl.pallas_call(
        _mla_prep_kernel,
        grid=(T // tm,),
        in_specs=[
            pl.BlockSpec((tm, n_mla), lambda i: (i, 0)),
            pl.BlockSpec((tm, 1), lambda i: (i, 0)),
            full(invf), full(qa), full(kva), full(wq), full(wkv),
            full(qgn), full(qgr), full(kgn), full(kgr),
        ],
        out_specs=[
            pl.BlockSpec((tm, MLA_HEADS * MLA_QPAD), lambda i: (i, 0)),
            pl.BlockSpec((tm, MLA_HEADS * MLA_QPAD), lambda i: (i, 0)),
            pl.BlockSpec((tm, MLA_HEADS * MLA_V), lambda i: (i, 0)),
        ],
        out_shape=[
            jax.ShapeDtypeStruct((T, MLA_HEADS * MLA_QPAD), BF16),
            jax.ShapeDtypeStruct((T, MLA_HEADS * MLA_QPAD), BF16),
            jax.ShapeDtypeStruct((T, MLA_HEADS * MLA_V), BF16),
        ],
        compiler_params=_cparams(("parallel",)),
        name="mla_prep",
    )(mla, pos, invf, qa, kva, wq, wkv, qgn, qgr, kgn, kgr)


def _attn_kernel(q_ref, k_ref, v_ref, g_ref, o_ref):
    scale = MLA_QK ** -0.5
    s = _dot_nt(q_ref[...], k_ref[...])
    m = jnp.max(s, axis=-1, keepdims=True)
    p = jnp.exp((s - m) * scale)
    l = jnp.sum(p, axis=-1, keepdims=True)
    o = jnp.dot(p.astype(BF16), v_ref[...], preferred_element_type=F32) / l
    o_ref[...] = _rms(o, g_ref[...]).astype(o_ref.dtype)


def _attention(q, k, v, o_gain, B, S, tq):
    T = q.shape[0]
    nq = S // tq
    return pl.pallas_call(
        _attn_kernel,
        grid=(B, MLA_HEADS, nq),
        in_specs=[
            pl.BlockSpec((tq, MLA_QPAD), lambda b, h, i: (b * nq + i, h)),
            pl.BlockSpec((S, MLA_QPAD), lambda b, h, i: (b, h)),
            pl.BlockSpec((S, MLA_V), lambda b, h, i: (b, h)),
            pl.BlockSpec((1, MLA_V), lambda b, h, i: (0, h)),
        ],
        out_specs=pl.BlockSpec((tq, MLA_V), lambda b, h, i: (b * nq + i, h)),
        out_shape=jax.ShapeDtypeStruct((T, MLA_HEADS * MLA_V), BF16),
        compiler_params=_cparams(("parallel", "parallel", "arbitrary")),
        name="mla_attention",
    )(q, k, v, o_gain)


def _out_proj_kernel(of_ref, ob_ref, g_ref, ym_ref, x_ref, hgg_ref, wo_ref, fg_ref, wq_ref, keys_ref,
                     x1_ref, h2_ref, st_ref):
    o = of_ref[...] + ob_ref[...]
    gate_raw = g_ref[...]
    gate = gate_raw * _sigmoid(gate_raw)
    parts = []
    for h in range(HG_HEADS):
        sl = slice(h * HG_DK, (h + 1) * HG_DK)
        parts.append(_rms(o[:, sl], hgg_ref[:, sl]) * gate[:, sl])
    y_hg = jnp.concatenate(parts, axis=-1).astype(BF16)
    w_hg_rows = y_hg.shape[1]
    mix = jnp.dot(y_hg, wo_ref[0:w_hg_rows, :], preferred_element_type=F32)
    mix = mix + jnp.dot(ym_ref[...], wo_ref[w_hg_rows:, :], preferred_element_type=F32)
    x1 = x_ref[...] + mix
    x1_ref[...] = x1
    h2 = _rms(x1, fg_ref[...]).astype(BF16)
    h2_ref[...] = h2
    pq = jnp.dot(h2, wq_ref[...], preferred_element_type=F32).astype(BF16)
    for pc in range(keys_ref.shape[0]):
        st_ref[pc] = _dot_nt(keys_ref[pc], pq[:, pc * PEER_DHALF:(pc + 1) * PEER_DHALF])


def _out_proj(o_f, o_b, hg, y_mla, x2, hg_gain, w_out, ffn_gain, w_pq, keys, tm):
    T, D = x2.shape
    W = o_f.shape[1]
    npc = keys.shape[0]
    full2 = lambda a: pl.BlockSpec(a.shape, lambda i: (0, 0))
    return pl.pallas_call(
        _out_proj_kernel,
        grid=(T // tm,),
        in_specs=[
            pl.BlockSpec((tm, W), lambda i: (i, 0)),
            pl.BlockSpec((tm, W), lambda i: (i, 0)),
            pl.BlockSpec((tm, W), lambda i: (i, 4)),
            pl.BlockSpec((tm, y_mla.shape[1]), lambda i: (i, 0)),
            pl.BlockSpec((tm, D), lambda i: (i, 0)),
            full2(hg_gain), full2(w_out), full2(ffn_gain), full2(w_pq),
            pl.BlockSpec(keys.shape, lambda i: (0, 0, 0)),
        ],
        out_specs=[
            pl.BlockSpec((tm, D), lambda i: (i, 0)),
            pl.BlockSpec((tm, D), lambda i: (i, 0)),
            pl.BlockSpec((npc, PEER_NKEYS, tm), lambda i: (0, 0, i)),
        ],
        out_shape=[
            jax.ShapeDtypeStruct((T, D), F32),
            jax.ShapeDtypeStruct((T, D), BF16),
            jax.ShapeDtypeStruct((npc, PEER_NKEYS, T), F32),
        ],
        compiler_params=_cparams(("parallel",)),
        name="out_proj_peer_query",
    )(o_f, o_b, hg, y_mla, x2, hg_gain, w_out, ffn_gain, w_pq, keys)


def _tree_max(xs):
    xs = list(xs)
    while len(xs) > 1:
        nxt = [jnp.maximum(xs[i], xs[i + 1]) for i in range(0, len(xs) - 1, 2)]
        if len(xs) % 2:
            nxt.append(xs[-1])
        xs = nxt
    return xs[0]


def _peer_select_kernel(s_ref, tau_ref, m1_ref, iz_ref, btop_ref, top_scr):
    n_keep = PEER_TOPK + 1
    for p in range(PEER_HEADS):
        for c in range(2):
            cur = s_ref[2 * p + c]
            for r in range(n_keep):
                m = jnp.max(cur, axis=0, keepdims=True)
                top_scr[c, r, pl.ds(p, 1), :] = m
                if r + 1 < n_keep:
                    cur = jnp.where(cur == m, NEG_INF, cur)
    a = [top_scr[0, r] for r in range(n_keep)]
    b = [top_scr[1, r] for r in range(n_keep)]
    cands = [a[i] + b[j] for i in range(n_keep) for j in range(n_keep) if (i + 1) * (j + 1) <= n_keep]
    top = a[0] + b[0]
    z = jnp.zeros_like(top)
    kth = top
    for r in range(PEER_TOPK):
        kth = _tree_max(cands)
        z = z + jnp.exp(kth - top)
        cands = [jnp.where(cv == kth, NEG_INF, cv) for cv in cands]
    nxt = _tree_max(cands)
    tau_ref[...] = 0.5 * (kth + nxt)
    m1_ref[...] = a[0]
    iz_ref[...] = 1.0 / z
    for r in range(PEER_TOPK):
        btop_ref[r] = b[r]


def _peer_select(scores_t, tt):
    npc, nk, T = scores_t.shape
    out = jax.ShapeDtypeStruct((PEER_HEADS, T), F32)
    ospec = pl.BlockSpec((PEER_HEADS, tt), lambda i: (0, i))
    return pl.pallas_call(
        _peer_select_kernel,
        grid=(T // tt,),
        in_specs=[pl.BlockSpec((npc, nk, tt), lambda i: (0, 0, i))],
        out_specs=[ospec] * 3 + [pl.BlockSpec((PEER_TOPK, PEER_HEADS, tt), lambda i: (0, 0, i))],
        out_shape=[out] * 3 + [jax.ShapeDtypeStruct((PEER_TOPK, PEER_HEADS, T), F32)],
        scratch_shapes=[pltpu.VMEM((2, PEER_TOPK + 1, PEER_HEADS, tt), F32)],
        compiler_params=_cparams(("parallel",)),
        name="peer_select",
    )(scores_t)


def _gelu_exact(x):
    return 0.5 * x * (1.0 + lax.erf(x * (1.0 / math.sqrt(2.0))))


def _peer_dense_kernel(h_ref, u_ref, vt_ref, s_ref, tau_ref, m1_ref, iz_ref, btop_ref, yt_ref,
                       n_scr, e1_scr, rank_scr, e2_scr, a_scr, p_scr):
    jt = pl.program_id(1)
    te, tt = a_scr.shape
    nb = te // PEER_NKEYS
    n_grp = nb // SUBLANES
    IBQ = 4
    PK = 2 * SUBLANES

    @pl.when(jt == 0)
    def _():
        yt_ref[...] = jnp.zeros_like(yt_ref)
        grouped = (PEER_NKEYS // SUBLANES, SUBLANES, LANES)
        for p in range(PEER_HEADS):
            for st in range(tt // LANES):
                sl = pl.ds(st * LANES, LANES)
                s1 = s_ref[2 * p, :, sl]
                s2 = s_ref[2 * p + 1, :, sl]
                x = tau_ref[pl.ds(p, 1), sl] - s1
                n = jnp.zeros_like(s1)
                rank = jnp.zeros_like(s2)
                for r in range(PEER_TOPK):
                    b_r = btop_ref[r, pl.ds(p, 1), sl]
                    n = jnp.where(b_r >= x, r + 1.0, n)
                    rank = jnp.where(b_r > s2, r + 1.0, rank)
                n_scr[p, :, :, sl] = n.reshape(grouped)
                e1_scr[p, :, :, sl] = jnp.exp(s1 - m1_ref[pl.ds(p, 1), sl]).reshape(grouped)
                rank_scr[p, :, sl] = rank.astype(BF16)
                m2 = btop_ref[0, pl.ds(p, 1), sl]
                e2_scr[p, :, sl] = (jnp.exp(s2 - m2) * iz_ref[pl.ds(p, 1), sl]).astype(BF16)

    a_scr[...] = _dot_nt(u_ref[...], h_ref[...])
    zero = jnp.zeros((PK, LANES), BF16)
    for ibg in range(nb // IBQ):
        for st in range(tt // LANES):
            sl = pl.ds(st * LANES, LANES)
            gates = [[zero for _ in range(PEER_NKEYS // PK)] for _ in range(IBQ)]
            for p in range(PEER_HEADS):
                rank = rank_scr[p, :, sl]
                e2 = e2_scr[p, :, sl]
                for q in range(IBQ):
                    ib = ibg * IBQ + q
                    grp = jt * n_grp + ib // SUBLANES
                    sub = pl.ds(ib % SUBLANES, 1)
                    n = jnp.broadcast_to(n_scr[p, grp, sub, sl], (PK, LANES)).astype(BF16)
                    e1 = jnp.broadcast_to(e1_scr[p, grp, sub, sl], (PK, LANES)).astype(BF16)
                    for r in range(PEER_NKEYS // PK):
                        rr = slice(r * PK, (r + 1) * PK)
                        gates[q][r] = gates[q][r] + jnp.where(rank[rr, :] < n, e2[rr, :], zero) * e1
            for q in range(IBQ):
                rows = pl.ds((ibg * IBQ + q) * PEER_NKEYS, PEER_NKEYS)
                gate = jnp.concatenate(gates[q], axis=0)
                p_scr[rows, sl] = _gelu_exact(a_scr[rows, sl]).astype(BF16) * gate
    yt_ref[...] += jnp.dot(vt_ref[...], p_scr[...], preferred_element_type=F32)


def _peer_dense(h2, u_bf, vt_bf, scores_t, tau, m1, iz, btop, tt, te):
    T, D = h2.shape
    E = u_bf.shape[0]
    npc, nk, _ = scores_t.shape
    assert te % (PEER_NKEYS * SUBLANES) == 0 and E % te == 0
    row = pl.BlockSpec((PEER_HEADS, tt), lambda t, j: (0, t))
    return pl.pallas_call(
        _peer_dense_kernel,
        grid=(T // tt, E // te),
        in_specs=[
            pl.BlockSpec((tt, D), lambda t, j: (t, 0)),
            pl.BlockSpec((te, D), lambda t, j: (j, 0)),
            pl.BlockSpec((D, te), lambda t, j: (0, j)),
            pl.BlockSpec((npc, nk, tt), lambda t, j: (0, 0, t), pipeline_mode=pl.Buffered(1)),
            row, row, row,
            pl.BlockSpec((PEER_TOPK, PEER_HEADS, tt), lambda t, j: (0, 0, t)),
        ],
        out_specs=pl.BlockSpec((D, tt), lambda t, j: (0, t)),
        out_shape=jax.ShapeDtypeStruct((D, T), F32),
        scratch_shapes=[
            pltpu.VMEM((PEER_HEADS, nk // SUBLANES, SUBLANES, tt), F32),
            pltpu.VMEM((PEER_HEADS, nk // SUBLANES, SUBLANES, tt), F32),
            pltpu.VMEM((PEER_HEADS, nk, tt), BF16),
            pltpu.VMEM((PEER_HEADS, nk, tt), BF16),
            pltpu.VMEM((te, tt), F32),
            pltpu.VMEM((te, tt), BF16),
        ],
        compiler_params=_cparams(("parallel", "arbitrary")),
        name="peer_dense",
    )(h2, u_bf, vt_bf, scores_t, tau, m1, iz, btop)


def _finish_kernel(x1_ref, yt_ref, o_ref):
    o_ref[...] = x1_ref[...] + yt_ref[...].T


def _finish(x1, yt, tm):
    T, D = x1.shape
    return pl.pallas_call(
        _finish_kernel,
        grid=(T // tm,),
        in_specs=[pl.BlockSpec((tm, D), lambda i: (i, 0)), pl.BlockSpec((D, tm), lambda i: (0, i))],
        out_specs=pl.BlockSpec((tm, D), lambda i: (i, 0)),
        out_shape=jax.ShapeDtypeStruct((T, D), F32),
        compiler_params=_cparams(("parallel",)),
        name="finish",
    )(x1, yt)


def _pick(n, pref):
    t = min(n, pref)
    assert n % t == 0, (n, t)
    return t


def _layer(x2, pos, B, S, attn_norm, w_in, lb_logits, hg_o_norm, q_a_norm, w_q_up, kv_a_norm,
           w_kv_up, q_norm, k_norm, mla_o_norm, w_out, ffn_norm, peer_w_q, peer_sub_keys, peer_u, peer_v):
    T, D = x2.shape
    n_hg = 5 * HG_HEADS * HG_DK
    row = lambda a: a.reshape(1, -1).astype(F32)

    w_hg = w_in[:, :n_hg].astype(BF16)
    n_mla = w_in.shape[1] - n_hg
    w_mla = jnp.pad(w_in[:, n_hg:], ((0, 0), (0, -n_mla % LANES))).astype(BF16)
    pad_r = LANES - MLA_ROPE
    wq = w_q_up.reshape(MLA_Q_LORA, MLA_HEADS, MLA_QK)
    wq = jnp.pad(wq, ((0, 0), (0, 0), (0, MLA_QPAD - MLA_QK))).reshape(MLA_Q_LORA, MLA_HEADS * MLA_QPAD)
    wq = wq.astype(BF16)
    wkv = w_kv_up.astype(BF16)
    qgn, qgr = row(q_norm[:MLA_NOPE]), row(jnp.pad(q_norm[MLA_NOPE:], (0, pad_r)))
    kgn, kgr = row(k_norm[:MLA_NOPE]), row(jnp.pad(k_norm[MLA_NOPE:], (0, pad_r)))
    inv_freq = 1.0 / (ROPE_THETA ** (jnp.arange(0, MLA_ROPE, 2, dtype=F32) / MLA_ROPE))
    invf = row(jnp.concatenate([inv_freq, inv_freq, jnp.zeros((pad_r,), F32)]))
    keys = peer_sub_keys.reshape(2 * PEER_HEADS, PEER_NKEYS, PEER_DHALF).astype(BF16)
    u_bf = peer_u.astype(BF16)
    vt_bf = peer_v.astype(BF16).T

    hg, mla = _in_proj(x2, row(attn_norm), w_hg, w_mla, _pick(T, 256))
    tb = _pick(S, 512)
    o_f = _hgrn_scan(hg, lb_logits[:, 0, :], B, S, False, tb)
    o_b = _hgrn_scan(hg, lb_logits[:, 1, :], B, S, True, tb)
    q, k, v = _mla_prep(mla, pos, invf, row(q_a_norm), row(kv_a_norm), wq, wkv, qgn, qgr, kgn, kgr,
                        _pick(T, 256))
    y_mla = _attention(q, k, v, mla_o_norm.reshape(1, -1).astype(F32), B, S, _pick(S, 512))
    x1, h2, scores_t = _out_proj(o_f, o_b, hg, y_mla, x2, row(hg_o_norm), w_out.astype(BF16),
                                 row(ffn_norm), peer_w_q.astype(BF16), keys, _pick(T, 256))
    tau, m1, iz, btop = _peer_select(scores_t, _pick(T, 256))
    yt = _peer_dense(h2, u_bf, vt_bf, scores_t, tau, m1, iz, btop, _pick(T, 1024), 1024)
    return _finish(x1, yt, _pick(T, 256))


def kernel(x, positions, attn_norm, w_in, hg_lb_logits, hg_o_norm, q_a_norm, w_q_up, kv_a_norm, w_kv_up,
           q_norm, k_norm, mla_o_norm, w_out, ffn_norm, peer_w_q, peer_sub_keys, peer_u, peer_v):
    B, S, D = x.shape
    x2 = x.reshape(B * S, D)
    pos = positions.reshape(B * S, 1)
    for l in range(attn_norm.shape[0]):
        assert l == 0
        x2 = _layer(x2, pos, B, S, attn_norm[l], w_in[l], hg_lb_logits, hg_o_norm[l], q_a_norm[l],
                    w_q_up[l], kv_a_norm[l], w_kv_up[l], q_norm[l], k_norm[l], mla_o_norm[l], w_out[l],
                    ffn_norm[l], peer_w_q[l], peer_sub_keys[l], peer_u[l], peer_v[l])
    return x2.reshape(B, S, D)
```

```python
import functools
import math

import jax
import jax.numpy as jnp
from jax import lax
from jax.experimental import pallas as pl
from jax.experimental.pallas import tpu as pltpu

F32 = jnp.float32
BF16 = jnp.bfloat16
EPS = 1e-6
LANES = 128
SUBLANES = 8
NEG_INF = float("-inf")

HG_HEADS = 4
HG_DK = 128
HG_CHUNK = 64
HG_HALF = HG_CHUNK // 2
MLA_HEADS = 4
MLA_Q_LORA = 384
MLA_KV_LORA = 256
MLA_NOPE = 128
MLA_ROPE = 64
MLA_V = 128
MLA_QK = MLA_NOPE + MLA_ROPE
MLA_QPAD = 2 * LANES
ROPE_THETA = 10000.0
PEER_HEADS = 8
PEER_NKEYS = 128
PEER_TOPK = 16
PEER_DHALF = 128
GATE_DTYPE = F32

VMEM_LIMIT = 56 * 1024 * 1024


def _cparams(sem):
    return pltpu.CompilerParams(dimension_semantics=sem, vmem_limit_bytes=VMEM_LIMIT)


def _rms(x, gain):
    ms = jnp.mean(x * x, axis=-1, keepdims=True)
    return x * lax.rsqrt(ms + EPS) * gain


def _sigmoid(x):
    return 1.0 / (1.0 + jnp.exp(-x))


def _dot_nt(a, b):
    return lax.dot_general(a, b, (((1,), (1,)), ((), ())), preferred_element_type=F32)


def _dot_tn(a, b):
    return lax.dot_general(a, b, (((0,), (0,)), ((), ())), preferred_element_type=F32)


def _in_proj_kernel(x_ref, g_ref, whg_ref, wmla_ref, hg_ref, mla_ref):
    h = _rms(x_ref[...], g_ref[...]).astype(BF16)
    hg_ref[...] = jnp.dot(h, whg_ref[...], preferred_element_type=F32)
    mla_ref[...] = jnp.dot(h, wmla_ref[...], preferred_element_type=F32)


def _in_proj(x2, gain, w_hg, w_mla, tm):
    T, D = x2.shape
    n_hg, n_mla = w_hg.shape[1], w_mla.shape[1]
    return pl.pallas_call(
        _in_proj_kernel,
        grid=(T // tm,),
        in_specs=[
            pl.BlockSpec((tm, D), lambda i: (i, 0)),
            pl.BlockSpec((1, D), lambda i: (0, 0)),
            pl.BlockSpec((D, n_hg), lambda i: (0, 0)),
            pl.BlockSpec((D, n_mla), lambda i: (0, 0)),
        ],
        out_specs=[
            pl.BlockSpec((tm, n_hg), lambda i: (i, 0)),
            pl.BlockSpec((tm, n_mla), lambda i: (i, 0)),
        ],
        out_shape=[jax.ShapeDtypeStruct((T, n_hg), F32), jax.ShapeDtypeStruct((T, n_mla), F32)],
        compiler_params=_cparams(("parallel",)),
        name="in_proj",
    )(x2, gain, w_hg, w_mla)


def _hgrn_kernel(q_ref, f_ref, v_ref, lb_ref, o_ref, st_ref, b_scr, k_scr, p_scr, *, reverse, n_chunks):
    C, Hf, G = HG_CHUNK, HG_HALF, SUBLANES
    n_heads = st_ref.shape[0]

    @pl.when(pl.program_id(1) == 0)
    def _():
        st_ref[...] = jnp.zeros_like(st_ref)

    logits = lb_ref[...]
    ex = jnp.exp(logits - jnp.max(logits, axis=0, keepdims=True))
    lb_all = ex[0:1, :] / jnp.sum(ex, axis=0, keepdims=True)

    row = lax.broadcasted_iota(jnp.int32, (C, C), 0)
    col = lax.broadcasted_iota(jnp.int32, (C, C), 1)
    tri = (col >= row) if reverse else (col <= row)
    cum_mat = tri.astype(F32)
    same_half = (row >= Hf) == (col >= Hf)
    row_c = lax.broadcasted_iota(jnp.int32, (C, LANES), 0)
    q_side = (row_c < Hf) if reverse else (row_c >= Hf)
    lane_g = lax.broadcasted_iota(jnp.int32, (G, LANES), 1)
    ones = jnp.ones((LANES, LANES), BF16)
    mid_row = Hf if reverse else Hf - 1
    end_row = 0 if reverse else C - 1

    def live_groups(s):
        return range(0, s // G + 1) if reverse else range(s // G, Hf // G)

    def gates(hh, r0):
        cols = slice(hh * HG_DK, (hh + 1) * HG_DK)
        lb = lb_all[:, cols]
        q_raw = q_ref[pl.ds(r0, C), cols]
        q = q_raw * _sigmoid(q_raw)
        f = lb + (1.0 - lb) * _sigmoid(f_ref[pl.ds(r0, C), cols])
        k = 1.0 - f
        vb = v_ref[pl.ds(r0, C), cols].astype(BF16)
        b = jnp.dot(cum_mat, jnp.log2(f), precision=lax.Precision.HIGHEST,
                    preferred_element_type=F32)
        b_scr[hh] = b
        k_scr[hh] = k
        return q, k, vb, b

    def state_and_offdiag(hh, q, k, vb, b):
        b_mid = b_scr[hh, pl.ds(mid_row, 1), :]
        b_end = b_scr[hh, pl.ds(end_row, 1), :]
        st = st_ref[hh]
        o = _dot_nt((q * jnp.exp2(b)).astype(BF16), st.astype(BF16))
        kd = (k * jnp.exp2(b_end - b)).astype(BF16)
        st_ref[hh] = st * jnp.exp2(b_end) + _dot_tn(vb, kd)
        qt = jnp.where(q_side, q * jnp.exp2(b - b_mid), 0.0).astype(BF16)
        kt = jnp.where(q_side, 0.0, k * jnp.exp2(b_mid - b)).astype(BF16)
        return o, _dot_nt(qt, kt)

    def diag_products(hh, q, b):
        n_rows = 0
        for hb in range(2):
            base = hb * Hf
            for s in range(Hf):
                g0, g1 = live_groups(s)[0], live_groups(s)[-1] + 1
                rows = slice(base + g0 * G, base + g1 * G)
                bs = b_scr[hh, pl.ds(base + s, 1), :]
                ks = k_scr[hh, pl.ds(base + s, 1), :]
                p_scr[hh, pl.ds(n_rows, (g1 - g0) * G), :] = (q[rows, :] * ks) * jnp.exp2(b[rows, :] - bs)
                n_rows += (g1 - g0) * G

    def diag_scores(rs):
        n_rows = 0
        groups = []
        for hb in range(2):
            base = hb * Hf
            acc = [jnp.zeros((G, LANES), F32) for _ in range(Hf // G)]
            for s in range(Hf):
                for g in live_groups(s):
                    acc[g] = jnp.where(lane_g == base + s, rs[n_rows:n_rows + G, :], acc[g])
                    n_rows += G
            groups.extend(acc)
        return jnp.concatenate(groups, axis=0)[:, :C]

    def chunk_body(ci, carry):
        c = (n_chunks - 1 - ci) if reverse else ci
        r0 = pl.multiple_of(c * C, C)
        heads = range(n_heads)
        qkvb = [gates(hh, r0) for hh in heads]
        o_sc = [state_and_offdiag(hh, *qkvb[hh]) for hh in heads]
        for hh in heads:
            diag_products(hh, qkvb[hh][0], qkvb[hh][3])
        rs = [jnp.dot(p_scr[hh].astype(BF16), ones, preferred_element_type=F32) for hh in heads]
        for hh in heads:
            o, sc_off = o_sc[hh]
            scores = jnp.where(same_half, jnp.where(tri, diag_scores(rs[hh]), 0.0), sc_off)
            o = o + jnp.dot(scores.astype(BF16), qkvb[hh][2], preferred_element_type=F32)
            o_ref[pl.ds(r0, C), hh * HG_DK:(hh + 1) * HG_DK] = o
        return carry

    lax.fori_loop(0, n_chunks, chunk_body, 0)


def _hgrn_diag_rows():
    return 2 * sum(HG_HALF - (s // SUBLANES) * SUBLANES for s in range(HG_HALF))


def _hgrn_scan(hg, lb_logits_dir, B, S, reverse, tb):
    T = hg.shape[0]
    W = HG_HEADS * HG_DK
    nblk = S // tb
    f_blk = 2 if reverse else 1
    v_blk = 3

    def tmap(b, i):
        return b * nblk + ((nblk - 1 - i) if reverse else i)

    kern = functools.partial(_hgrn_kernel, reverse=reverse, n_chunks=tb // HG_CHUNK)
    return pl.pallas_call(
        kern,
        grid=(B, nblk),
        in_specs=[
            pl.BlockSpec((tb, W), lambda b, i: (tmap(b, i), 0)),
            pl.BlockSpec((tb, W), lambda b, i: (tmap(b, i), f_blk)),
            pl.BlockSpec((tb, W), lambda b, i: (tmap(b, i), v_blk)),
            pl.BlockSpec((lb_logits_dir.shape[0], W), lambda b, i: (0, 0)),
        ],
        out_specs=pl.BlockSpec((tb, W), lambda b, i: (tmap(b, i), 0)),
        out_shape=jax.ShapeDtypeStruct((T, W), F32),
        scratch_shapes=[
            pltpu.VMEM((HG_HEADS, HG_DK, HG_DK), F32),
            pltpu.VMEM((HG_HEADS, HG_CHUNK, HG_DK), F32),
            pltpu.VMEM((HG_HEADS, HG_CHUNK, HG_DK), F32),
            pltpu.VMEM((HG_HEADS, _hgrn_diag_rows(), HG_DK), F32),
        ],
        compiler_params=_cparams(("parallel", "arbitrary")),
        name="hgrn_rev" if reverse else "hgrn_fwd",
    )(hg, hg, hg, lb_logits_dir)


def _mla_prep_kernel(mla_ref, pos_ref, invf_ref, qa_ref, kva_ref, wq_ref, wkv_ref,
                     qgn_ref, qgr_ref, kgn_ref, kgr_ref, q_ref, k_ref, v_ref):
    tm = mla_ref.shape[0]
    lane = lax.broadcasted_iota(jnp.int32, (tm, LANES), 1)
    ang = pos_ref[...].astype(F32) * invf_ref[...]
    cos_t = jnp.where(lane < MLA_ROPE, jnp.cos(ang), 0.0)
    sin_a = jnp.sin(ang)
    sin_t = jnp.where(lane < MLA_ROPE // 2, -sin_a, jnp.where(lane < MLA_ROPE, sin_a, 0.0))

    def rope(t):
        swapped = jnp.where(lane < MLA_ROPE // 2,
                            pltpu.roll(t, LANES - MLA_ROPE // 2, 1),
                            pltpu.roll(t, MLA_ROPE // 2, 1))
        return t * cos_t + swapped * sin_t

    def norm_rope_part(t, gain):
        ms = jnp.sum(t * t, axis=-1, keepdims=True) * (1.0 / MLA_ROPE)
        return t * lax.rsqrt(ms + EPS) * gain

    cq = mla_ref[:, 0:MLA_Q_LORA]
    ckv = mla_ref[:, MLA_Q_LORA:MLA_Q_LORA + MLA_KV_LORA]
    kr = mla_ref[:, MLA_Q_LORA + MLA_KV_LORA:]
    q_up = jnp.dot(_rms(cq, qa_ref[...]).astype(BF16), wq_ref[...], preferred_element_type=F32)
    kv_up = jnp.dot(_rms(ckv, kva_ref[...]).astype(BF16), wkv_ref[...], preferred_element_type=F32)
    k_rope = rope(norm_rope_part(kr, kgr_ref[...])).astype(BF16)
    for h in range(MLA_HEADS):
        c0 = h * MLA_QPAD
        q_ref[:, c0:c0 + LANES] = _rms(q_up[:, c0:c0 + LANES], qgn_ref[...]).astype(BF16)
        q_ref[:, c0 + LANES:c0 + 2 * LANES] = rope(
            norm_rope_part(q_up[:, c0 + LANES:c0 + 2 * LANES], qgr_ref[...])).astype(BF16)
        k_ref[:, c0:c0 + LANES] = _rms(kv_up[:, c0:c0 + LANES], kgn_ref[...]).astype(BF16)
        k_ref[:, c0 + LANES:c0 + 2 * LANES] = k_rope
        v_ref[:, h * MLA_V:(h + 1) * MLA_V] = kv_up[:, c0 + LANES:c0 + 2 * LANES].astype(BF16)


def _mla_prep(mla, pos, invf, qa, kva, wq, wkv, qgn, qgr, kgn, kgr, tm):
    T, n_mla = mla.shape
    full = lambda a: pl.BlockSpec(a.shape, lambda i: (0, 0))
    return pl.pallas_call(
        _mla_prep_kernel,
        grid=(T // tm,),
        in_specs=[
            pl.BlockSpec((tm, n_mla), lambda i: (i, 0)),
            pl.BlockSpec((tm, 1), lambda i: (i, 0)),
            full(invf), full(qa), full(kva), full(wq), full(wkv),
            full(qgn), full(qgr), full(kgn), full(kgr),
        ],
        out_specs=[
            pl.BlockSpec((tm, MLA_HEADS * MLA_QPAD), lambda i: (i, 0)),
            pl.BlockSpec((tm, MLA_HEADS * MLA_QPAD), lambda i: (i, 0)),
            pl.BlockSpec((tm, MLA_HEADS * MLA_V), lambda i: (i, 0)),
        ],
        out_shape=[
            jax.ShapeDtypeStruct((T, MLA_HEADS * MLA_QPAD), BF16),
            jax.ShapeDtypeStruct((T, MLA_HEADS * MLA_QPAD), BF16),
            jax.ShapeDtypeStruct((T, MLA_HEADS * MLA_V), BF16),
        ],
        compiler_params=_cparams(("parallel",)),
        name="mla_prep",
    )(mla, pos, invf, qa, kva, wq, wkv, qgn, qgr, kgn, kgr)


def _attn_kernel(q_ref, k_ref, v_ref, g_ref, o_ref):
    scale = MLA_QK ** -0.5
    s = _dot_nt(q_ref[...], k_ref[...])
    m = jnp.max(s, axis=-1, keepdims=True)
    p = jnp.exp((s - m) * scale)
    l = jnp.sum(p, axis=-1, keepdims=True)
    o = jnp.dot(p.astype(BF16), v_ref[...], preferred_element_type=F32) / l
    o_ref[...] = _rms(o, g_ref[...]).astype(o_ref.dtype)


def _attention(q, k, v, o_gain, B, S, tq):
    T = q.shape[0]
    nq = S // tq
    return pl.pallas_call(
        _attn_kernel,
        grid=(B, MLA_HEADS, nq),
        in_specs=[
            pl.BlockSpec((tq, MLA_QPAD), lambda b, h, i: (b * nq + i, h)),
            pl.BlockSpec((S, MLA_QPAD), lambda b, h, i: (b, h)),
            pl.BlockSpec((S, MLA_V), lambda b, h, i: (b, h)),
            pl.BlockSpec((1, MLA_V), lambda b, h, i: (0, h)),
        ],
        out_specs=pl.BlockSpec((tq, MLA_V), lambda b, h, i: (b * nq + i, h)),
        out_shape=jax.ShapeDtypeStruct((T, MLA_HEADS * MLA_V), BF16),
        compiler_params=_cparams(("parallel", "parallel", "arbitrary")),
        name="mla_attention",
    )(q, k, v, o_gain)


def _out_proj_kernel(of_ref, ob_ref, g_ref, ym_ref, x_ref, hgg_ref, wo_ref, fg_ref, wq_ref, keys_ref,
                     x1_ref, h2_ref, st_ref):
    o = of_ref[...] + ob_ref[...]
    gate_raw = g_ref[...]
    gate = gate_raw * _sigmoid(gate_raw)
    parts = []
    for h in range(HG_HEADS):
        sl = slice(h * HG_DK, (h + 1) * HG_DK)
        parts.append(_rms(o[:, sl], hgg_ref[:, sl]) * gate[:, sl])
    y_hg = jnp.concatenate(parts, axis=-1).astype(BF16)
    w_hg_rows = y_hg.shape[1]
    mix = jnp.dot(y_hg, wo_ref[0:w_hg_rows, :], preferred_element_type=F32)
    mix = mix + jnp.dot(ym_ref[...], wo_ref[w_hg_rows:, :], preferred_element_type=F32)
    x1 = x_ref[...] + mix
    x1_ref[...] = x1
    h2 = _rms(x1, fg_ref[...]).astype(BF16)
    h2_ref[...] = h2
    pq = jnp.dot(h2, wq_ref[...], preferred_element_type=F32).astype(BF16)
    for pc in range(keys_ref.shape[0]):
        st_ref[pc] = _dot_nt(keys_ref[pc], pq[:, pc * PEER_DHALF:(pc + 1) * PEER_DHALF])


def _out_proj(o_f, o_b, hg, y_mla, x2, hg_gain, w_out, ffn_gain, w_pq, keys, tm):
    T, D = x2.shape
    W = o_f.shape[1]
    npc = keys.shape[0]
    full2 = lambda a: pl.BlockSpec(a.shape, lambda i: (0, 0))
    return pl.pallas_call(
        _out_proj_kernel,
        grid=(T // tm,),
        in_specs=[
            pl.BlockSpec((tm, W), lambda i: (i, 0)),
            pl.BlockSpec((tm, W), lambda i: (i, 0)),
            pl.BlockSpec((tm, W), lambda i: (i, 4)),
            pl.BlockSpec((tm, y_mla.shape[1]), lambda i: (i, 0)),
            pl.BlockSpec((tm, D), lambda i: (i, 0)),
            full2(hg_gain), full2(w_out), full2(ffn_gain), full2(w_pq),
            pl.BlockSpec(keys.shape, lambda i: (0, 0, 0)),
        ],
        out_specs=[
            pl.BlockSpec((tm, D), lambda i: (i, 0)),
            pl.BlockSpec((tm, D), lambda i: (i, 0)),
            pl.BlockSpec((npc, PEER_NKEYS, tm), lambda i: (0, 0, i)),
        ],
        out_shape=[
            jax.ShapeDtypeStruct((T, D), F32),
            jax.ShapeDtypeStruct((T, D), BF16),
            jax.ShapeDtypeStruct((npc, PEER_NKEYS, T), F32),
        ],
        compiler_params=_cparams(("parallel",)),
        name="out_proj_peer_query",
    )(o_f, o_b, hg, y_mla, x2, hg_gain, w_out, ffn_gain, w_pq, keys)


def _tree_max(xs):
    xs = list(xs)
    while len(xs) > 1:
        nxt = [jnp.maximum(xs[i], xs[i + 1]) for i in range(0, len(xs) - 1, 2)]
        if len(xs) % 2:
            nxt.append(xs[-1])
        xs = nxt
    return xs[0]


def _peer_select_kernel(s_ref, tau_ref, m1_ref, iz_ref, btop_ref, top_scr):
    n_keep = PEER_TOPK + 1
    for p in range(PEER_HEADS):
        for c in range(2):
            cur = s_ref[2 * p + c]
            for r in range(n_keep):
                m = jnp.max(cur, axis=0, keepdims=True)
                top_scr[c, r, pl.ds(p, 1), :] = m
                if r + 1 < n_keep:
                    cur = jnp.where(cur == m, NEG_INF, cur)
    a = [top_scr[0, r] for r in range(n_keep)]
    b = [top_scr[1, r] for r in range(n_keep)]
    cands = [a[i] + b[j] for i in range(n_keep) for j in range(n_keep) if (i + 1) * (j + 1) <= n_keep]
    top = a[0] + b[0]
    z = jnp.zeros_like(top)
    kth = top
    for r in range(PEER_TOPK):
        kth = _tree_max(cands)
        z = z + jnp.exp(kth - top)
        cands = [jnp.where(cv == kth, NEG_INF, cv) for cv in cands]
    nxt = _tree_max(cands)
    tau_ref[...] = 0.5 * (kth + nxt)
    m1_ref[...] = a[0]
    iz_ref[...] = 1.0 / z
    for r in range(PEER_TOPK):
        btop_ref[r] = b[r]


def _peer_select(scores_t, tt):
    npc, nk, T = scores_t.shape
    out = jax.ShapeDtypeStruct((PEER_HEADS, T), F32)
    ospec = pl.BlockSpec((PEER_HEADS, tt), lambda i: (0, i))
    return pl.pallas_call(
        _peer_select_kernel,
        grid=(T // tt,),
        in_specs=[pl.BlockSpec((npc, nk, tt), lambda i: (0, 0, i))],
        out_specs=[ospec] * 3 + [pl.BlockSpec((PEER_TOPK, PEER_HEADS, tt), lambda i: (0, 0, i))],
        out_shape=[out] * 3 + [jax.ShapeDtypeStruct((PEER_TOPK, PEER_HEADS, T), F32)],
        scratch_shapes=[pltpu.VMEM((2, PEER_TOPK + 1, PEER_HEADS, tt), F32)],
        compiler_params=_cparams(("parallel",)),
        name="peer_select",
    )(scores_t)


def _gelu_exact(x):
    return 0.5 * x * (1.0 + lax.erf(x * (1.0 / math.sqrt(2.0))))


def _peer_dense_kernel(h_ref, u_ref, vt_ref, s_ref, tau_ref, m1_ref, iz_ref, btop_ref, yt_ref,
                       n_scr, e1_scr, rank_scr, e2_scr, a_scr, p_scr):
    jt = pl.program_id(1)
    te, tt = a_scr.shape
    nb = te // PEER_NKEYS
    n_grp = nb // SUBLANES
    PK = SUBLANES * (4 // jnp.dtype(GATE_DTYPE).itemsize)
    IBQ = 32 * PK // PEER_NKEYS

    @pl.when(jt == 0)
    def _():
        yt_ref[...] = jnp.zeros_like(yt_ref)
        grouped = (PEER_NKEYS // SUBLANES, SUBLANES, LANES)
        for p in range(PEER_HEADS):
            for st in range(tt // LANES):
                sl = pl.ds(st * LANES, LANES)
                s1 = s_ref[2 * p, :, sl]
                s2 = s_ref[2 * p + 1, :, sl]
                x = tau_ref[pl.ds(p, 1), sl] - s1
                n = jnp.zeros_like(s1)
                rank = jnp.zeros_like(s2)
                for r in range(PEER_TOPK):
                    b_r = btop_ref[r, pl.ds(p, 1), sl]
                    n = jnp.where(b_r >= x, r + 1.0, n)
                    rank = jnp.where(b_r > s2, r + 1.0, rank)
                n_scr[p, :, :, sl] = n.reshape(grouped)
                e1_scr[p, :, :, sl] = jnp.exp(s1 - m1_ref[pl.ds(p, 1), sl]).reshape(grouped)
                rank_scr[p, :, sl] = rank.astype(GATE_DTYPE)
                m2 = btop_ref[0, pl.ds(p, 1), sl]
                e2_scr[p, :, sl] = (jnp.exp(s2 - m2) * iz_ref[pl.ds(p, 1), sl]).astype(GATE_DTYPE)

    a_scr[...] = _dot_nt(u_ref[...], h_ref[...])
    zero = jnp.zeros((PK, LANES), GATE_DTYPE)
    for ibg in range(nb // IBQ):
        for st in range(tt // LANES):
            sl = pl.ds(st * LANES, LANES)
            gates = [[zero for _ in range(PEER_NKEYS // PK)] for _ in range(IBQ)]
            for p in range(PEER_HEADS):
                rank = rank_scr[p, :, sl]
                e2 = e2_scr[p, :, sl]
                for q in range(IBQ):
                    ib = ibg * IBQ + q
                    grp = jt * n_grp + ib // SUBLANES
                    sub = pl.ds(ib % SUBLANES, 1)
                    n = jnp.broadcast_to(n_scr[p, grp, sub, sl], (PK, LANES)).astype(GATE_DTYPE)
                    e1 = jnp.broadcast_to(e1_scr[p, grp, sub, sl], (PK, LANES)).astype(GATE_DTYPE)
                    for r in range(PEER_NKEYS // PK):
                        rr = slice(r * PK, (r + 1) * PK)
                        gates[q][r] = gates[q][r] + jnp.where(rank[rr, :] < n, e2[rr, :], zero) * e1
            for q in range(IBQ):
                rows = pl.ds((ibg * IBQ + q) * PEER_NKEYS, PEER_NKEYS)
                gate = jnp.concatenate(gates[q], axis=0)
                p_scr[rows, sl] = (_gelu_exact(a_scr[rows, sl]).astype(GATE_DTYPE) * gate).astype(BF16)
    yt_ref[...] += jnp.dot(vt_ref[...], p_scr[...], preferred_element_type=F32)


def _peer_dense(h2, u_bf, vt_bf, scores_t, tau, m1, iz, btop, tt, te):
    T, D = h2.shape
    E = u_bf.shape[0]
    npc, nk, _ = scores_t.shape
    assert te % (PEER_NKEYS * SUBLANES) == 0 and E % te == 0
    row = pl.BlockSpec((PEER_HEADS, tt), lambda t, j: (0, t))
    return pl.pallas_call(
        _peer_dense_kernel,
        grid=(T // tt, E // te),
        in_specs=[
            pl.BlockSpec((tt, D), lambda t, j: (t, 0)),
            pl.BlockSpec((te, D), lambda t, j: (j, 0)),
            pl.BlockSpec((D, te), lambda t, j: (0, j)),
            pl.BlockSpec((npc, nk, tt), lambda t, j: (0, 0, t)),
            row, row, row,
            pl.BlockSpec((PEER_TOPK, PEER_HEADS, tt), lambda t, j: (0, 0, t)),
        ],
        out_specs=pl.BlockSpec((D, tt), lambda t, j: (0, t)),
        out_shape=jax.ShapeDtypeStruct((D, T), F32),
        scratch_shapes=[
            pltpu.VMEM((PEER_HEADS, nk // SUBLANES, SUBLANES, tt), F32),
            pltpu.VMEM((PEER_HEADS, nk // SUBLANES, SUBLANES, tt), F32),
            pltpu.VMEM((PEER_HEADS, nk, tt), GATE_DTYPE),
            pltpu.VMEM((PEER_HEADS, nk, tt), GATE_DTYPE),
            pltpu.VMEM((te, tt), F32),
            pltpu.VMEM((te, tt), BF16),
        ],
        compiler_params=_cparams(("parallel", "arbitrary")),
        name="peer_dense",
    )(h2, u_bf, vt_bf, scores_t, tau, m1, iz, btop)


def _finish_kernel(x1_ref, yt_ref, o_ref):
    o_ref[...] = x1_ref[...] + yt_ref[...].T


def _finish(x1, yt, tm):
    T, D = x1.shape
    return pl.pallas_call(
        _finish_kernel,
        grid=(T // tm,),
        in_specs=[pl.BlockSpec((tm, D), lambda i: (i, 0)), pl.BlockSpec((D, tm), lambda i: (0, i))],
        out_specs=pl.BlockSpec((tm, D), lambda i: (i, 0)),
        out_shape=jax.ShapeDtypeStruct((T, D), F32),
        compiler_params=_cparams(("parallel",)),
        name="finish",
    )(x1, yt)


def _pick(n, pref):
    t = min(n, pref)
    assert n % t == 0, (n, t)
    return t


def _layer(x2, pos, B, S, attn_norm, w_in, lb_logits, hg_o_norm, q_a_norm, w_q_up, kv_a_norm,
           w_kv_up, q_norm, k_norm, mla_o_norm, w_out, ffn_norm, peer_w_q, peer_sub_keys, peer_u, peer_v):
    T, D = x2.shape
    n_hg = 5 * HG_HEADS * HG_DK
    row = lambda a: a.reshape(1, -1).astype(F32)

    w_hg = w_in[:, :n_hg].astype(BF16)
    n_mla = w_in.shape[1] - n_hg
    w_mla = jnp.pad(w_in[:, n_hg:], ((0, 0), (0, -n_mla % LANES))).astype(BF16)
    pad_r = LANES - MLA_ROPE
    wq = w_q_up.reshape(MLA_Q_LORA, MLA_HEADS, MLA_QK)
    wq = jnp.pad(wq, ((0, 0), (0, 0), (0, MLA_QPAD - MLA_QK))).reshape(MLA_Q_LORA, MLA_HEADS * MLA_QPAD)
    wq = wq.astype(BF16)
    wkv = w_kv_up.astype(BF16)
    qgn, qgr = row(q_norm[:MLA_NOPE]), row(jnp.pad(q_norm[MLA_NOPE:], (0, pad_r)))
    kgn, kgr = row(k_norm[:MLA_NOPE]), row(jnp.pad(k_norm[MLA_NOPE:], (0, pad_r)))
    inv_freq = 1.0 / (ROPE_THETA ** (jnp.arange(0, MLA_ROPE, 2, dtype=F32) / MLA_ROPE))
    invf = row(jnp.concatenate([inv_freq, inv_freq, jnp.zeros((pad_r,), F32)]))
    keys = peer_sub_keys.reshape(2 * PEER_HEADS, PEER_NKEYS, PEER_DHALF).astype(BF16)
    u_bf = peer_u.astype(BF16)
    vt_bf = peer_v.astype(BF16).T

    hg, mla = _in_proj(x2, row(attn_norm), w_hg, w_mla, _pick(T, 256))
    tb = _pick(S, 512)
    o_f = _hgrn_scan(hg, lb_logits[:, 0, :], B, S, False, tb)
    o_b = _hgrn_scan(hg, lb_logits[:, 1, :], B, S, True, tb)
    q, k, v = _mla_prep(mla, pos, invf, row(q_a_norm), row(kv_a_norm), wq, wkv, qgn, qgr, kgn, kgr,
                        _pick(T, 256))
    y_mla = _attention(q, k, v, mla_o_norm.reshape(1, -1).astype(F32), B, S, _pick(S, 512))
    x1, h2, scores_t = _out_proj(o_f, o_b, hg, y_mla, x2, row(hg_o_norm), w_out.astype(BF16),
                                 row(ffn_norm), peer_w_q.astype(BF16), keys, _pick(T, 256))
    tau, m1, iz, btop = _peer_select(scores_t, _pick(T, 256))
    yt = _peer_dense(h2, u_bf, vt_bf, scores_t, tau, m1, iz, btop, _pick(T, 512), 1024)
    return _finish(x1, yt, _pick(T, 256))


def kernel(x, positions, attn_norm, w_in, hg_lb_logits, hg_o_norm, q_a_norm, w_q_up, kv_a_norm, w_kv_up,
           q_norm, k_norm, mla_o_norm, w_out, ffn_norm, peer_w_q, peer_sub_keys, peer_u, peer_v):
    B, S, D = x.shape
    x2 = x.reshape(B * S, D)
    pos = positions.reshape(B * S, 1)
    for l in range(attn_norm.shape[0]):
        assert l == 0
        x2 = _layer(x2, pos, B, S, attn_norm[l], w_in[l], hg_lb_logits, hg_o_norm[l], q_a_norm[l],
                    w_q_up[l], kv_a_norm[l], w_kv_up[l], q_norm[l], k_norm[l], mla_o_norm[l], w_out[l],
                    ffn_norm[l], peer_w_q[l], peer_sub_keys[l], peer_u[l], peer_v[l])
    return x2.reshape(B, S, D)
```

```python
import functools
import math

import jax
import jax.numpy as jnp
from jax import lax
from jax.experimental import pallas as pl
from jax.experimental.pallas import tpu as pltpu

F32 = jnp.float32
BF16 = jnp.bfloat16
EPS = 1e-6
LANES = 128
SUBLANES = 8
NEG_INF = float("-inf")

HG_HEADS = 4
HG_DK = 128
HG_CHUNK = 64
HG_HALF = HG_CHUNK // 2
MLA_HEADS = 4
MLA_Q_LORA = 384
MLA_KV_LORA = 256
MLA_NOPE = 128
MLA_ROPE = 64
MLA_V = 128
MLA_QK = MLA_NOPE + MLA_ROPE
MLA_QPAD = 2 * LANES
ROPE_THETA = 10000.0
ATTN_CHUNKS = 4
PEER_HEADS = 8
PEER_NKEYS = 128
PEER_TOPK = 16
PEER_DHALF = 128
GATE_DTYPE = BF16

VMEM_LIMIT = 56 * 1024 * 1024


def _cparams(sem):
    return pltpu.CompilerParams(dimension_semantics=sem, vmem_limit_bytes=VMEM_LIMIT)


def _rms(x, gain):
    ms = jnp.mean(x * x, axis=-1, keepdims=True)
    return x * lax.rsqrt(ms + EPS) * gain


def _sigmoid(x):
    return 1.0 / (1.0 + jnp.exp(-x))


def _dot_nt(a, b):
    return lax.dot_general(a, b, (((1,), (1,)), ((), ())), preferred_element_type=F32)


def _dot_tn(a, b):
    return lax.dot_general(a, b, (((0,), (0,)), ((), ())), preferred_element_type=F32)


def _in_proj_kernel(x_ref, g_ref, whg_ref, wmla_ref, hg_ref, mla_ref):
    h = _rms(x_ref[...], g_ref[...]).astype(BF16)
    hg_ref[...] = jnp.dot(h, whg_ref[...], preferred_element_type=F32)
    mla_ref[...] = jnp.dot(h, wmla_ref[...], preferred_element_type=F32)


def _in_proj(x2, gain, w_hg, w_mla, tm):
    T, D = x2.shape
    n_hg, n_mla = w_hg.shape[1], w_mla.shape[1]
    return pl.pallas_call(
        _in_proj_kernel,
        grid=(T // tm,),
        in_specs=[
            pl.BlockSpec((tm, D), lambda i: (i, 0)),
            pl.BlockSpec((1, D), lambda i: (0, 0)),
            pl.BlockSpec((D, n_hg), lambda i: (0, 0)),
            pl.BlockSpec((D, n_mla), lambda i: (0, 0)),
        ],
        out_specs=[
            pl.BlockSpec((tm, n_hg), lambda i: (i, 0)),
            pl.BlockSpec((tm, n_mla), lambda i: (i, 0)),
        ],
        out_shape=[jax.ShapeDtypeStruct((T, n_hg), F32), jax.ShapeDtypeStruct((T, n_mla), F32)],
        compiler_params=_cparams(("parallel",)),
        name="in_proj",
    )(x2, gain, w_hg, w_mla)


def _hgrn_kernel(q_ref, f_ref, v_ref, lb_ref, o_ref, st_ref, b_scr, k_scr, p_scr, *, reverse, n_chunks):
    C, Hf, G = HG_CHUNK, HG_HALF, SUBLANES
    n_heads = st_ref.shape[0]

    @pl.when(pl.program_id(1) == 0)
    def _():
        st_ref[...] = jnp.zeros_like(st_ref)

    logits = lb_ref[...]
    ex = jnp.exp(logits - jnp.max(logits, axis=0, keepdims=True))
    lb_all = ex[0:1, :] / jnp.sum(ex, axis=0, keepdims=True)

    row = lax.broadcasted_iota(jnp.int32, (C, C), 0)
    col = lax.broadcasted_iota(jnp.int32, (C, C), 1)
    tri = (col >= row) if reverse else (col <= row)
    cum_mat = tri.astype(F32)
    same_half = (row >= Hf) == (col >= Hf)
    row_c = lax.broadcasted_iota(jnp.int32, (C, LANES), 0)
    q_side = (row_c < Hf) if reverse else (row_c >= Hf)
    lane_g = lax.broadcasted_iota(jnp.int32, (G, LANES), 1)
    ones = jnp.ones((LANES, LANES), BF16)
    mid_row = Hf if reverse else Hf - 1
    end_row = 0 if reverse else C - 1

    def live_groups(s):
        return range(0, s // G + 1) if reverse else range(s // G, Hf // G)

    def gates(hh, r0):
        cols = slice(hh * HG_DK, (hh + 1) * HG_DK)
        lb = lb_all[:, cols]
        q_raw = q_ref[pl.ds(r0, C), cols]
        q = q_raw * _sigmoid(q_raw)
        f = lb + (1.0 - lb) * _sigmoid(f_ref[pl.ds(r0, C), cols])
        k = 1.0 - f
        vb = v_ref[pl.ds(r0, C), cols].astype(BF16)
        b = jnp.dot(cum_mat, jnp.log2(f), precision=lax.Precision.HIGHEST,
                    preferred_element_type=F32)
        b_scr[hh] = b
        k_scr[hh] = k
        return q, k, vb, b

    def state_and_offdiag(hh, q, k, vb, b):
        b_mid = b_scr[hh, pl.ds(mid_row, 1), :]
        b_end = b_scr[hh, pl.ds(end_row, 1), :]
        st = st_ref[hh]
        o = _dot_nt((q * jnp.exp2(b)).astype(BF16), st.astype(BF16))
        kd = (k * jnp.exp2(b_end - b)).astype(BF16)
        st_ref[hh] = st * jnp.exp2(b_end) + _dot_tn(vb, kd)
        qt = jnp.where(q_side, q * jnp.exp2(b - b_mid), 0.0).astype(BF16)
        kt = jnp.where(q_side, 0.0, k * jnp.exp2(b_mid - b)).astype(BF16)
        return o, _dot_nt(qt, kt)

    def diag_products(hh, q, b):
        n_rows = 0
        for hb in range(2):
            base = hb * Hf
            for s in range(Hf):
                g0, g1 = live_groups(s)[0], live_groups(s)[-1] + 1
                rows = slice(base + g0 * G, base + g1 * G)
                bs = b_scr[hh, pl.ds(base + s, 1), :]
                ks = k_scr[hh, pl.ds(base + s, 1), :]
                p_scr[hh, pl.ds(n_rows, (g1 - g0) * G), :] = (q[rows, :] * ks) * jnp.exp2(b[rows, :] - bs)
                n_rows += (g1 - g0) * G

    def diag_scores(rs):
        n_rows = 0
        groups = []
        for hb in range(2):
            base = hb * Hf
            acc = [jnp.zeros((G, LANES), F32) for _ in range(Hf // G)]
            for s in range(Hf):
                for g in live_groups(s):
                    acc[g] = jnp.where(lane_g == base + s, rs[n_rows:n_rows + G, :], acc[g])
                    n_rows += G
            groups.extend(acc)
        return jnp.concatenate(groups, axis=0)[:, :C]

    def chunk_body(ci, carry):
        c = (n_chunks - 1 - ci) if reverse else ci
        r0 = pl.multiple_of(c * C, C)
        heads = range(n_heads)
        qkvb = [gates(hh, r0) for hh in heads]
        o_sc = [state_and_offdiag(hh, *qkvb[hh]) for hh in heads]
        for hh in heads:
            diag_products(hh, qkvb[hh][0], qkvb[hh][3])
        rs = [jnp.dot(p_scr[hh].astype(BF16), ones, preferred_element_type=F32) for hh in heads]
        for hh in heads:
            o, sc_off = o_sc[hh]
            scores = jnp.where(same_half, jnp.where(tri, diag_scores(rs[hh]), 0.0), sc_off)
            o = o + jnp.dot(scores.astype(BF16), qkvb[hh][2], preferred_element_type=F32)
            o_ref[pl.ds(r0, C), hh * HG_DK:(hh + 1) * HG_DK] = o
        return carry

    lax.fori_loop(0, n_chunks, chunk_body, 0)


def _hgrn_diag_rows():
    return 2 * sum(HG_HALF - (s // SUBLANES) * SUBLANES for s in range(HG_HALF))


def _hgrn_scan(hg, lb_logits_dir, B, S, reverse, tb):
    T = hg.shape[0]
    W = HG_HEADS * HG_DK
    nblk = S // tb
    f_blk = 2 if reverse else 1
    v_blk = 3

    def tmap(b, i):
        return b * nblk + ((nblk - 1 - i) if reverse else i)

    kern = functools.partial(_hgrn_kernel, reverse=reverse, n_chunks=tb // HG_CHUNK)
    return pl.pallas_call(
        kern,
        grid=(B, nblk),
        in_specs=[
            pl.BlockSpec((tb, W), lambda b, i: (tmap(b, i), 0)),
            pl.BlockSpec((tb, W), lambda b, i: (tmap(b, i), f_blk)),
            pl.BlockSpec((tb, W), lambda b, i: (tmap(b, i), v_blk)),
            pl.BlockSpec((lb_logits_dir.shape[0], W), lambda b, i: (0, 0)),
        ],
        out_specs=pl.BlockSpec((tb, W), lambda b, i: (tmap(b, i), 0)),
        out_shape=jax.ShapeDtypeStruct((T, W), F32),
        scratch_shapes=[
            pltpu.VMEM((HG_HEADS, HG_DK, HG_DK), F32),
            pltpu.VMEM((HG_HEADS, HG_CHUNK, HG_DK), F32),
            pltpu.VMEM((HG_HEADS, HG_CHUNK, HG_DK), F32),
            pltpu.VMEM((HG_HEADS, _hgrn_diag_rows(), HG_DK), F32),
        ],
        compiler_params=_cparams(("parallel", "arbitrary")),
        name="hgrn_rev" if reverse else "hgrn_fwd",
    )(hg, hg, hg, lb_logits_dir)


def _mla_prep_kernel(mla_ref, pos_ref, invf_ref, qa_ref, kva_ref, wq_ref, wkv_ref,
                     qgn_ref, qgr_ref, kgn_ref, kgr_ref, q_ref, k_ref, v_ref):
    tm = mla_ref.shape[0]
    lane = lax.broadcasted_iota(jnp.int32, (tm, LANES), 1)
    ang = pos_ref[...].astype(F32) * invf_ref[...]
    cos_t = jnp.where(lane < MLA_ROPE, jnp.cos(ang), 0.0)
    sin_a = jnp.sin(ang)
    sin_t = jnp.where(lane < MLA_ROPE // 2, -sin_a, jnp.where(lane < MLA_ROPE, sin_a, 0.0))

    def rope(t):
        swapped = jnp.where(lane < MLA_ROPE // 2,
                            pltpu.roll(t, LANES - MLA_ROPE // 2, 1),
                            pltpu.roll(t, MLA_ROPE // 2, 1))
        return t * cos_t + swapped * sin_t

    def norm_rope_part(t, gain):
        ms = jnp.sum(t * t, axis=-1, keepdims=True) * (1.0 / MLA_ROPE)
        return t * lax.rsqrt(ms + EPS) * gain

    cq = mla_ref[:, 0:MLA_Q_LORA]
    ckv = mla_ref[:, MLA_Q_LORA:MLA_Q_LORA + MLA_KV_LORA]
    kr = mla_ref[:, MLA_Q_LORA + MLA_KV_LORA:]
    q_up = jnp.dot(_rms(cq, qa_ref[...]).astype(BF16), wq_ref[...], preferred_element_type=F32)
    kv_up = jnp.dot(_rms(ckv, kva_ref[...]).astype(BF16), wkv_ref[...], preferred_element_type=F32)
    k_rope = rope(norm_rope_part(kr, kgr_ref[...])).astype(BF16)
    for h in range(MLA_HEADS):
        c0 = h * MLA_QPAD
        q_ref[:, c0:c0 + LANES] = _rms(q_up[:, c0:c0 + LANES], qgn_ref[...]).astype(BF16)
        q_ref[:, c0 + LANES:c0 + 2 * LANES] = rope(
            norm_rope_part(q_up[:, c0 + LANES:c0 + 2 * LANES], qgr_ref[...])).astype(BF16)
        k_ref[:, c0:c0 + LANES] = _rms(kv_up[:, c0:c0 + LANES], kgn_ref[...]).astype(BF16)
        k_ref[:, c0 + LANES:c0 + 2 * LANES] = k_rope
        v_ref[:, h * MLA_V:(h + 1) * MLA_V] = kv_up[:, c0 + LANES:c0 + 2 * LANES].astype(BF16)


def _mla_prep(mla, pos, invf, qa, kva, wq, wkv, qgn, qgr, kgn, kgr, tm):
    T, n_mla = mla.shape
    full = lambda a: pl.BlockSpec(a.shape, lambda i: (0, 0))
    return pl.pallas_call(
        _mla_prep_kernel,
        grid=(T // tm,),
        in_specs=[
            pl.BlockSpec((tm, n_mla), lambda i: (i, 0)),
            pl.BlockSpec((tm, 1), lambda i: (i, 0)),
            full(invf), full(qa), full(kva), full(wq), full(wkv),
            full(qgn), full(qgr), full(kgn), full(kgr),
        ],
        out_specs=[
            pl.BlockSpec((tm, MLA_HEADS * MLA_QPAD), lambda i: (i, 0)),
            pl.BlockSpec((tm, MLA_HEADS * MLA_QPAD), lambda i: (i, 0)),
            pl.BlockSpec((tm, MLA_HEADS * MLA_V), lambda i: (i, 0)),
        ],
        out_shape=[
            jax.ShapeDtypeStruct((T, MLA_HEADS * MLA_QPAD), BF16),
            jax.ShapeDtypeStruct((T, MLA_HEADS * MLA_QPAD), BF16),
            jax.ShapeDtypeStruct((T, MLA_HEADS * MLA_V), BF16),
        ],
        compiler_params=_cparams(("parallel",)),
        name="mla_prep",
    )(mla, pos, invf, qa, kva, wq, wkv, qgn, qgr, kgn, kgr)


def _attn_kernel(q_ref, k_ref, v_ref, g_ref, o_ref):
    c = MLA_QK ** -0.5 * math.log2(math.e)
    tq = q_ref.shape[0]
    rc = tq // ATTN_CHUNKS
    rows = [pl.ds(i * rc, rc) for i in range(ATTN_CHUNKS)]
    s = [_dot_nt(q_ref[r, :], k_ref[...]) for r in rows]
    p = [jnp.exp2((si - jnp.max(si, axis=-1, keepdims=True)) * c) for si in s]
    l = [jnp.sum(pi, axis=-1, keepdims=True) for pi in p]
    o = [jnp.dot(pi.astype(BF16), v_ref[...], preferred_element_type=F32) for pi in p]
    for r, oi, li in zip(rows, o, l):
        o_ref[r, :] = _rms(oi / li, g_ref[...]).astype(o_ref.dtype)


def _attention(q, k, v, o_gain, B, S, tq):
    T = q.shape[0]
    nq = S // tq
    return pl.pallas_call(
        _attn_kernel,
        grid=(B, MLA_HEADS, nq),
        in_specs=[
            pl.BlockSpec((tq, MLA_QPAD), lambda b, h, i: (b * nq + i, h)),
            pl.BlockSpec((S, MLA_QPAD), lambda b, h, i: (b, h)),
            pl.BlockSpec((S, MLA_V), lambda b, h, i: (b, h)),
            pl.BlockSpec((1, MLA_V), lambda b, h, i: (0, h)),
        ],
        out_specs=pl.BlockSpec((tq, MLA_V), lambda b, h, i: (b * nq + i, h)),
        out_shape=jax.ShapeDtypeStruct((T, MLA_HEADS * MLA_V), BF16),
        compiler_params=_cparams(("parallel", "parallel", "arbitrary")),
        name="mla_attention",
    )(q, k, v, o_gain)


def _out_proj_kernel(of_ref, ob_ref, g_ref, ym_ref, x_ref, hgg_ref, wo_ref, fg_ref, wq_ref, keys_ref,
                     x1_ref, h2_ref, st_ref):
    o = of_ref[...] + ob_ref[...]
    gate_raw = g_ref[...]
    gate = gate_raw * _sigmoid(gate_raw)
    parts = []
    for h in range(HG_HEADS):
        sl = slice(h * HG_DK, (h + 1) * HG_DK)
        parts.append(_rms(o[:, sl], hgg_ref[:, sl]) * gate[:, sl])
    y_hg = jnp.concatenate(parts, axis=-1).astype(BF16)
    w_hg_rows = y_hg.shape[1]
    mix = jnp.dot(y_hg, wo_ref[0:w_hg_rows, :], preferred_element_type=F32)
    mix = mix + jnp.dot(ym_ref[...], wo_ref[w_hg_rows:, :], preferred_element_type=F32)
    x1 = x_ref[...] + mix
    x1_ref[...] = x1
    h2 = _rms(x1, fg_ref[...]).astype(BF16)
    h2_ref[...] = h2
    pq = jnp.dot(h2, wq_ref[...], preferred_element_type=F32).astype(BF16)
    for pc in range(keys_ref.shape[0]):
        st_ref[pc] = _dot_nt(keys_ref[pc], pq[:, pc * PEER_DHALF:(pc + 1) * PEER_DHALF])


def _out_proj(o_f, o_b, hg, y_mla, x2, hg_gain, w_out, ffn_gain, w_pq, keys, tm):
    T, D = x2.shape
    W = o_f.shape[1]
    npc = keys.shape[0]
    full2 = lambda a: pl.BlockSpec(a.shape, lambda i: (0, 0))
    return pl.pallas_call(
        _out_proj_kernel,
        grid=(T // tm,),
        in_specs=[
            pl.BlockSpec((tm, W), lambda i: (i, 0)),
            pl.BlockSpec((tm, W), lambda i: (i, 0)),
            pl.BlockSpec((tm, W), lambda i: (i, 4)),
            pl.BlockSpec((tm, y_mla.shape[1]), lambda i: (i, 0)),
            pl.BlockSpec((tm, D), lambda i: (i, 0)),
            full2(hg_gain), full2(w_out), full2(ffn_gain), full2(w_pq),
            pl.BlockSpec(keys.shape, lambda i: (0, 0, 0)),
        ],
        out_specs=[
            pl.BlockSpec((tm, D), lambda i: (i, 0)),
            pl.BlockSpec((tm, D), lambda i: (i, 0)),
            pl.BlockSpec((npc, PEER_NKEYS, tm), lambda i: (0, 0, i)),
        ],
        out_shape=[
            jax.ShapeDtypeStruct((T, D), F32),
            jax.ShapeDtypeStruct((T, D), BF16),
            jax.ShapeDtypeStruct((npc, PEER_NKEYS, T), F32),
        ],
        compiler_params=_cparams(("parallel",)),
        name="out_proj_peer_query",
    )(o_f, o_b, hg, y_mla, x2, hg_gain, w_out, ffn_gain, w_pq, keys)


def _tree_max(xs):
    xs = list(xs)
    while len(xs) > 1:
        nxt = [jnp.maximum(xs[i], xs[i + 1]) for i in range(0, len(xs) - 1, 2)]
        if len(xs) % 2:
            nxt.append(xs[-1])
        xs = nxt
    return xs[0]


def _peer_select_kernel(s_ref, tau_ref, m1_ref, iz_ref, btop_ref, top_scr):
    n_keep = PEER_TOPK + 1
    for p in range(PEER_HEADS):
        for c in range(2):
            cur = s_ref[2 * p + c]
            for r in range(n_keep):
                m = jnp.max(cur, axis=0, keepdims=True)
                top_scr[c, r, pl.ds(p, 1), :] = m
                if r + 1 < n_keep:
                    cur = jnp.where(cur == m, NEG_INF, cur)
    a = [top_scr[0, r] for r in range(n_keep)]
    b = [top_scr[1, r] for r in range(n_keep)]
    cands = [a[i] + b[j] for i in range(n_keep) for j in range(n_keep) if (i + 1) * (j + 1) <= n_keep]
    top = a[0] + b[0]
    z = jnp.zeros_like(top)
    kth = top
    for r in range(PEER_TOPK):
        kth = _tree_max(cands)
        z = z + jnp.exp(kth - top)
        cands = [jnp.where(cv == kth, NEG_INF, cv) for cv in cands]
    nxt = _tree_max(cands)
    tau_ref[...] = 0.5 * (kth + nxt)
    m1_ref[...] = a[0]
    iz_ref[...] = 1.0 / z
    for r in range(PEER_TOPK):
        btop_ref[r] = b[r]


def _peer_select(scores_t, tt):
    npc, nk, T = scores_t.shape
    out = jax.ShapeDtypeStruct((PEER_HEADS, T), F32)
    ospec = pl.BlockSpec((PEER_HEADS, tt), lambda i: (0, i))
    return pl.pallas_call(
        _peer_select_kernel,
        grid=(T // tt,),
        in_specs=[pl.BlockSpec((npc, nk, tt), lambda i: (0, 0, i))],
        out_specs=[ospec] * 3 + [pl.BlockSpec((PEER_TOPK, PEER_HEADS, tt), lambda i: (0, 0, i))],
        out_shape=[out] * 3 + [jax.ShapeDtypeStruct((PEER_TOPK, PEER_HEADS, T), F32)],
        scratch_shapes=[pltpu.VMEM((2, PEER_TOPK + 1, PEER_HEADS, tt), F32)],
        compiler_params=_cparams(("parallel",)),
        name="peer_select",
    )(scores_t)


def _gelu_exact(x):
    return 0.5 * x * (1.0 + lax.erf(x * (1.0 / math.sqrt(2.0))))


def _peer_dense_kernel(h_ref, u_ref, vt_ref, s_ref, tau_ref, m1_ref, iz_ref, btop_ref, yt_ref,
                       n_scr, e1_scr, rank_scr, e2_scr, a_scr, p_scr):
    jt = pl.program_id(1)
    te, tt = a_scr.shape
    nb = te // PEER_NKEYS
    n_grp = nb // SUBLANES
    PK = SUBLANES * (4 // jnp.dtype(GATE_DTYPE).itemsize)
    IBQ = 32 * PK // PEER_NKEYS

    @pl.when(jt == 0)
    def _():
        yt_ref[...] = jnp.zeros_like(yt_ref)
        grouped = (PEER_NKEYS // SUBLANES, SUBLANES, LANES)
        for p in range(PEER_HEADS):
            for st in range(tt // LANES):
                sl = pl.ds(st * LANES, LANES)
                s1 = s_ref[2 * p, :, sl]
                s2 = s_ref[2 * p + 1, :, sl]
                x = tau_ref[pl.ds(p, 1), sl] - s1
                n = jnp.zeros_like(s1)
                rank = jnp.zeros_like(s2)
                for r in range(PEER_TOPK):
                    b_r = btop_ref[r, pl.ds(p, 1), sl]
                    n = jnp.where(b_r >= x, r + 1.0, n)
                    rank = jnp.where(b_r > s2, r + 1.0, rank)
                n_scr[p, :, :, sl] = n.reshape(grouped)
                e1_scr[p, :, :, sl] = jnp.exp(s1 - m1_ref[pl.ds(p, 1), sl]).reshape(grouped)
                rank_scr[p, :, sl] = rank.astype(GATE_DTYPE)
                m2 = btop_ref[0, pl.ds(p, 1), sl]
                e2_scr[p, :, sl] = (jnp.exp(s2 - m2) * iz_ref[pl.ds(p, 1), sl]).astype(GATE_DTYPE)

    a_scr[...] = _dot_nt(u_ref[...], h_ref[...])
    zero = jnp.zeros((PK, LANES), GATE_DTYPE)
    for ibg in range(nb // IBQ):
        for st in range(tt // LANES):
            sl = pl.ds(st * LANES, LANES)
            gates = [[zero for _ in range(PEER_NKEYS // PK)] for _ in range(IBQ)]
            for p in range(PEER_HEADS):
                rank = rank_scr[p, :, sl]
                e2 = e2_scr[p, :, sl]
                for q in range(IBQ):
                    ib = ibg * IBQ + q
                    grp = jt * n_grp + ib // SUBLANES
                    sub = pl.ds(ib % SUBLANES, 1)
                    n = jnp.broadcast_to(n_scr[p, grp, sub, sl], (PK, LANES)).astype(GATE_DTYPE)
                    e1 = jnp.broadcast_to(e1_scr[p, grp, sub, sl], (PK, LANES)).astype(GATE_DTYPE)
                    for r in range(PEER_NKEYS // PK):
                        rr = slice(r * PK, (r + 1) * PK)
                        gates[q][r] = gates[q][r] + jnp.where(rank[rr, :] < n, e2[rr, :], zero) * e1
            for q in range(IBQ):
                rows = pl.ds((ibg * IBQ + q) * PEER_NKEYS, PEER_NKEYS)
                gate = jnp.concatenate(gates[q], axis=0)
                p_scr[rows, sl] = (_gelu_exact(a_scr[rows, sl]).astype(GATE_DTYPE) * gate).astype(BF16)
    yt_ref[...] += jnp.dot(vt_ref[...], p_scr[...], preferred_element_type=F32)


def _peer_dense(h2, u_bf, vt_bf, scores_t, tau, m1, iz, btop, tt, te):
    T, D = h2.shape
    E = u_bf.shape[0]
    npc, nk, _ = scores_t.shape
    assert te % (PEER_NKEYS * SUBLANES) == 0 and E % te == 0
    row = pl.BlockSpec((PEER_HEADS, tt), lambda t, j: (0, t))
    return pl.pallas_call(
        _peer_dense_kernel,
        grid=(T // tt, E // te),
        in_specs=[
            pl.BlockSpec((tt, D), lambda t, j: (t, 0)),
            pl.BlockSpec((te, D), lambda t, j: (j, 0)),
            pl.BlockSpec((D, te), lambda t, j: (0, j)),
            pl.BlockSpec((npc, nk, tt), lambda t, j: (0, 0, t)),
            row, row, row,
            pl.BlockSpec((PEER_TOPK, PEER_HEADS, tt), lambda t, j: (0, 0, t)),
        ],
        out_specs=pl.BlockSpec((D, tt), lambda t, j: (0, t)),
        out_shape=jax.ShapeDtypeStruct((D, T), F32),
        scratch_shapes=[
            pltpu.VMEM((PEER_HEADS, nk // SUBLANES, SUBLANES, tt), F32),
            pltpu.VMEM((PEER_HEADS, nk // SUBLANES, SUBLANES, tt), F32),
            pltpu.VMEM((PEER_HEADS, nk, tt), GATE_DTYPE),
            pltpu.VMEM((PEER_HEADS, nk, tt), GATE_DTYPE),
            pltpu.VMEM((te, tt), F32),
            pltpu.VMEM((te, tt), BF16),
        ],
        compiler_params=_cparams(("parallel", "arbitrary")),
        name="peer_dense",
    )(h2, u_bf, vt_bf, scores_t, tau, m1, iz, btop)


def _finish_kernel(x1_ref, yt_ref, o_ref):
    o_ref[...] = x1_ref[...] + yt_ref[...].T


def _finish(x1, yt, tm):
    T, D = x1.shape
    return pl.pallas_call(
        _finish_kernel,
        grid=(T // tm,),
        in_specs=[pl.BlockSpec((tm, D), lambda i: (i, 0)), pl.BlockSpec((D, tm), lambda i: (0, i))],
        out_specs=pl.BlockSpec((tm, D), lambda i: (i, 0)),
        out_shape=jax.ShapeDtypeStruct((T, D), F32),
        compiler_params=_cparams(("parallel",)),
        name="finish",
    )(x1, yt)


def _pick(n, pref):
    t = min(n, pref)
    assert n % t == 0, (n, t)
    return t


def _layer(x2, pos, B, S, attn_norm, w_in, lb_logits, hg_o_norm, q_a_norm, w_q_up, kv_a_norm,
           w_kv_up, q_norm, k_norm, mla_o_norm, w_out, ffn_norm, peer_w_q, peer_sub_keys, peer_u, peer_v):
    T, D = x2.shape
    n_hg = 5 * HG_HEADS * HG_DK
    row = lambda a: a.reshape(1, -1).astype(F32)

    w_hg = w_in[:, :n_hg].astype(BF16)
    n_mla = w_in.shape[1] - n_hg
    w_mla = jnp.pad(w_in[:, n_hg:], ((0, 0), (0, -n_mla % LANES))).astype(BF16)
    pad_r = LANES - MLA_ROPE
    wq = w_q_up.reshape(MLA_Q_LORA, MLA_HEADS, MLA_QK)
    wq = jnp.pad(wq, ((0, 0), (0, 0), (0, MLA_QPAD - MLA_QK))).reshape(MLA_Q_LORA, MLA_HEADS * MLA_QPAD)
    wq = wq.astype(BF16)
    wkv = w_kv_up.astype(BF16)
    qgn, qgr = row(q_norm[:MLA_NOPE]), row(jnp.pad(q_norm[MLA_NOPE:], (0, pad_r)))
    kgn, kgr = row(k_norm[:MLA_NOPE]), row(jnp.pad(k_norm[MLA_NOPE:], (0, pad_r)))
    inv_freq = 1.0 / (ROPE_THETA ** (jnp.arange(0, MLA_ROPE, 2, dtype=F32) / MLA_ROPE))
    invf = row(jnp.concatenate([inv_freq, inv_freq, jnp.zeros((pad_r,), F32)]))
    keys = peer_sub_keys.reshape(2 * PEER_HEADS, PEER_NKEYS, PEER_DHALF).astype(BF16)
    u_bf = peer_u.astype(BF16)
    vt_bf = peer_v.astype(BF16).T

    hg, mla = _in_proj(x2, row(attn_norm), w_hg, w_mla, _pick(T, 256))
    tb = _pick(S, 512)
    o_f = _hgrn_scan(hg, lb_logits[:, 0, :], B, S, False, tb)
    o_b = _hgrn_scan(hg, lb_logits[:, 1, :], B, S, True, tb)
    q, k, v = _mla_prep(mla, pos, invf, row(q_a_norm), row(kv_a_norm), wq, wkv, qgn, qgr, kgn, kgr,
                        _pick(T, 256))
    y_mla = _attention(q, k, v, mla_o_norm.reshape(1, -1).astype(F32), B, S, _pick(S, 1024))
    x1, h2, scores_t = _out_proj(o_f, o_b, hg, y_mla, x2, row(hg_o_norm), w_out.astype(BF16),
                                 row(ffn_norm), peer_w_q.astype(BF16), keys, _pick(T, 256))
    tau, m1, iz, btop = _peer_select(scores_t, _pick(T, 256))
    yt = _peer_dense(h2, u_bf, vt_bf, scores_t, tau, m1, iz, btop, _pick(T, 512), 1024)
    return _finish(x1, yt, _pick(T, 256))


def kernel(x, positions, attn_norm, w_in, hg_lb_logits, hg_o_norm, q_a_norm, w_q_up, kv_a_norm, w_kv_up,
           q_norm, k_norm, mla_o_norm, w_out, ffn_norm, peer_w_q, peer_sub_keys, peer_u, peer_v):
    B, S, D = x.shape
    x2 = x.reshape(B * S, D)
    pos = positions.reshape(B * S, 1)
    for l in range(attn_norm.shape[0]):
        assert l == 0
        x2 = _layer(x2, pos, B, S, attn_norm[l], w_in[l], hg_lb_logits, hg_o_norm[l], q_a_norm[l],
                    w_q_up[l], kv_a_norm[l], w_kv_up[l], q_norm[l], k_norm[l], mla_o_norm[l], w_out[l],
                    ffn_norm[l], peer_w_q[l], peer_sub_keys[l], peer_u[l], peer_v[l])
    return x2.reshape(B, S, D)
```

```python
import functools
import math

import jax
import jax.numpy as jnp
from jax import lax
from jax.experimental import pallas as pl
from jax.experimental.pallas import tpu as pltpu

F32 = jnp.float32
BF16 = jnp.bfloat16
EPS = 1e-6
LANES = 128
SUBLANES = 8
NEG_INF = float("-inf")

HG_HEADS = 4
HG_DK = 128
HG_CHUNK = 64
HG_HALF = HG_CHUNK // 2
HG_QUARTER = HG_CHUNK // 4
MLA_HEADS = 4
MLA_Q_LORA = 384
MLA_KV_LORA = 256
MLA_NOPE = 128
MLA_ROPE = 64
MLA_V = 128
MLA_QK = MLA_NOPE + MLA_ROPE
MLA_QPAD = 2 * LANES
ROPE_THETA = 10000.0
ATTN_CHUNKS = 4
PEER_HEADS = 8
PEER_NKEYS = 128
PEER_TOPK = 16
PEER_DHALF = 128
GATE_DTYPE = BF16

VMEM_LIMIT = 56 * 1024 * 1024


def _cparams(sem):
    return pltpu.CompilerParams(dimension_semantics=sem, vmem_limit_bytes=VMEM_LIMIT)


def _rms(x, gain):
    ms = jnp.mean(x * x, axis=-1, keepdims=True)
    return x * lax.rsqrt(ms + EPS) * gain


def _sigmoid(x):
    return 1.0 / (1.0 + jnp.exp(-x))


def _dot_nt(a, b):
    return lax.dot_general(a, b, (((1,), (1,)), ((), ())), preferred_element_type=F32)


def _dot_tn(a, b):
    return lax.dot_general(a, b, (((0,), (0,)), ((), ())), preferred_element_type=F32)


def _in_proj_kernel(x_ref, g_ref, whg_ref, wmla_ref, hg_ref, mla_ref):
    h = _rms(x_ref[...], g_ref[...]).astype(BF16)
    hg_ref[...] = jnp.dot(h, whg_ref[...], preferred_element_type=F32)
    mla_ref[...] = jnp.dot(h, wmla_ref[...], preferred_element_type=F32)


def _in_proj(x2, gain, w_hg, w_mla, tm):
    T, D = x2.shape
    n_hg, n_mla = w_hg.shape[1], w_mla.shape[1]
    return pl.pallas_call(
        _in_proj_kernel,
        grid=(T // tm,),
        in_specs=[
            pl.BlockSpec((tm, D), lambda i: (i, 0)),
            pl.BlockSpec((1, D), lambda i: (0, 0)),
            pl.BlockSpec((D, n_hg), lambda i: (0, 0)),
            pl.BlockSpec((D, n_mla), lambda i: (0, 0)),
        ],
        out_specs=[
            pl.BlockSpec((tm, n_hg), lambda i: (i, 0)),
            pl.BlockSpec((tm, n_mla), lambda i: (i, 0)),
        ],
        out_shape=[jax.ShapeDtypeStruct((T, n_hg), F32), jax.ShapeDtypeStruct((T, n_mla), F32)],
        compiler_params=_cparams(("parallel",)),
        name="in_proj",
    )(x2, gain, w_hg, w_mla)


def _hgrn_kernel(q_ref, f_ref, v_ref, lb_ref, o_ref, st_ref, b_scr, k_scr, p_scr, *, reverse, n_chunks):
    C, Hf, Qt, G = HG_CHUNK, HG_HALF, HG_QUARTER, SUBLANES
    n_heads = st_ref.shape[0]

    @pl.when(pl.program_id(1) == 0)
    def _():
        st_ref[...] = jnp.zeros_like(st_ref)

    logits = lb_ref[...]
    ex = jnp.exp(logits - jnp.max(logits, axis=0, keepdims=True))
    lb_all = ex[0:1, :] / jnp.sum(ex, axis=0, keepdims=True)

    row = lax.broadcasted_iota(jnp.int32, (C, C), 0)
    col = lax.broadcasted_iota(jnp.int32, (C, C), 1)
    tri = (col >= row) if reverse else (col <= row)
    cum_mat = tri.astype(F32)
    same_half = (row >= Hf) == (col >= Hf)
    row_c = lax.broadcasted_iota(jnp.int32, (C, LANES), 0)
    q_side = (row_c < Hf) if reverse else (row_c >= Hf)
    odd_quarter = (row_c // Qt) % 2 == 1
    q_side4 = jnp.logical_not(odd_quarter) if reverse else odd_quarter
    low_half = row_c < Hf
    lane_g = lax.broadcasted_iota(jnp.int32, (G, LANES), 1)
    ones = jnp.ones((LANES, LANES), BF16)
    mid_row = Hf if reverse else Hf - 1
    qmid_rows = (Qt, Hf + Qt) if reverse else (Qt - 1, Hf + Qt - 1)
    end_row = 0 if reverse else C - 1

    def live_groups(s):
        return range(0, s // G + 1) if reverse else range(s // G, Qt // G)

    def gates(hh, r0):
        cols = slice(hh * HG_DK, (hh + 1) * HG_DK)
        lb = lb_all[:, cols]
        q_raw = q_ref[pl.ds(r0, C), cols]
        q = q_raw * _sigmoid(q_raw)
        f = lb + (1.0 - lb) * _sigmoid(f_ref[pl.ds(r0, C), cols])
        k = 1.0 - f
        vb = v_ref[pl.ds(r0, C), cols].astype(BF16)
        b = jnp.dot(cum_mat, jnp.log2(f), precision=lax.Precision.HIGHEST,
                    preferred_element_type=F32)
        b_scr[hh] = b
        k_scr[hh] = k
        return q, k, vb, b

    def state_and_offdiag(hh, q, k, vb, b):
        b_mid = b_scr[hh, pl.ds(mid_row, 1), :]
        b_end = b_scr[hh, pl.ds(end_row, 1), :]
        st = st_ref[hh]
        o = _dot_nt((q * jnp.exp2(b)).astype(BF16), st.astype(BF16))
        kd = (k * jnp.exp2(b_end - b)).astype(BF16)
        st_ref[hh] = st * jnp.exp2(b_end) + _dot_tn(vb, kd)
        qt = jnp.where(q_side, q * jnp.exp2(b - b_mid), 0.0).astype(BF16)
        kt = jnp.where(q_side, 0.0, k * jnp.exp2(b_mid - b)).astype(BF16)
        b_q = jnp.where(low_half, b_scr[hh, pl.ds(qmid_rows[0], 1), :], b_scr[hh, pl.ds(qmid_rows[1], 1), :])
        qt4 = jnp.where(q_side4, q * jnp.exp2(b - b_q), 0.0).astype(BF16)
        kt4 = jnp.where(q_side4, 0.0, k * jnp.exp2(b_q - b)).astype(BF16)
        return o, _dot_nt(qt, kt), _dot_nt(qt4, kt4)

    def diag_products(hh, q, b):
        n_rows = 0
        for qb in range(C // Qt):
            base = qb * Qt
            for s in range(Qt):
                g0, g1 = live_groups(s)[0], live_groups(s)[-1] + 1
                rows = slice(base + g0 * G, base + g1 * G)
                bs = b_scr[hh, pl.ds(base + s, 1), :]
                ks = k_scr[hh, pl.ds(base + s, 1), :]
                p_scr[hh, pl.ds(n_rows, (g1 - g0) * G), :] = (q[rows, :] * ks) * jnp.exp2(b[rows, :] - bs)
                n_rows += (g1 - g0) * G

    def diag_scores(rs):
        n_rows = 0
        groups = []
        for qb in range(C // Qt):
            base = qb * Qt
            acc = [jnp.zeros((G, LANES), F32) for _ in range(Qt // G)]
            for s in range(Qt):
                for g in live_groups(s):
                    acc[g] = jnp.where(lane_g == base + s, rs[n_rows:n_rows + G, :], acc[g])
                    n_rows += G
            groups.extend(acc)
        return jnp.concatenate(groups, axis=0)[:, :C]

    def chunk_body(ci, carry):
        c = (n_chunks - 1 - ci) if reverse else ci
        r0 = pl.multiple_of(c * C, C)
        heads = range(n_heads)
        qkvb = [gates(hh, r0) for hh in heads]
        o_sc = [state_and_offdiag(hh, *qkvb[hh]) for hh in heads]
        for hh in heads:
            diag_products(hh, qkvb[hh][0], qkvb[hh][3])
        rs = [jnp.dot(p_scr[hh].astype(BF16), ones, preferred_element_type=F32) for hh in heads]
        for hh in heads:
            o, sc_half, sc_quarter = o_sc[hh]
            scores = jnp.where(same_half, jnp.where(tri, diag_scores(rs[hh]) + sc_quarter, 0.0), sc_half)
            o = o + jnp.dot(scores.astype(BF16), qkvb[hh][2], preferred_element_type=F32)
            o_ref[pl.ds(r0, C), hh * HG_DK:(hh + 1) * HG_DK] = o
        return carry

    lax.fori_loop(0, n_chunks, chunk_body, 0)


def _hgrn_diag_rows():
    return (HG_CHUNK // HG_QUARTER) * sum(HG_QUARTER - (s // SUBLANES) * SUBLANES for s in range(HG_QUARTER))


def _hgrn_scan(hg, lb_logits_dir, B, S, reverse, tb):
    T = hg.shape[0]
    W = HG_HEADS * HG_DK
    nblk = S // tb
    f_blk = 2 if reverse else 1
    v_blk = 3

    def tmap(b, i):
        return b * nblk + ((nblk - 1 - i) if reverse else i)

    kern = functools.partial(_hgrn_kernel, reverse=reverse, n_chunks=tb // HG_CHUNK)
    return pl.pallas_call(
        kern,
        grid=(B, nblk),
        in_specs=[
            pl.BlockSpec((tb, W), lambda b, i: (tmap(b, i), 0)),
            pl.BlockSpec((tb, W), lambda b, i: (tmap(b, i), f_blk)),
            pl.BlockSpec((tb, W), lambda b, i: (tmap(b, i), v_blk)),
            pl.BlockSpec((lb_logits_dir.shape[0], W), lambda b, i: (0, 0)),
        ],
        out_specs=pl.BlockSpec((tb, W), lambda b, i: (tmap(b, i), 0)),
        out_shape=jax.ShapeDtypeStruct((T, W), F32),
        scratch_shapes=[
            pltpu.VMEM((HG_HEADS, HG_DK, HG_DK), F32),
            pltpu.VMEM((HG_HEADS, HG_CHUNK, HG_DK), F32),
            pltpu.VMEM((HG_HEADS, HG_CHUNK, HG_DK), F32),
            pltpu.VMEM((HG_HEADS, _hgrn_diag_rows(), HG_DK), F32),
        ],
        compiler_params=_cparams(("parallel", "arbitrary")),
        name="hgrn_rev" if reverse else "hgrn_fwd",
    )(hg, hg, hg, lb_logits_dir)


def _mla_prep_kernel(mla_ref, pos_ref, invf_ref, qa_ref, kva_ref, wq_ref, wkv_ref,
                     qgn_ref, qgr_ref, kgn_ref, kgr_ref, q_ref, k_ref, v_ref):
    tm = mla_ref.shape[0]
    lane = lax.broadcasted_iota(jnp.int32, (tm, LANES), 1)
    ang = pos_ref[...].astype(F32) * invf_ref[...]
    cos_t = jnp.where(lane < MLA_ROPE, jnp.cos(ang), 0.0)
    sin_a = jnp.sin(ang)
    sin_t = jnp.where(lane < MLA_ROPE // 2, -sin_a, jnp.where(lane < MLA_ROPE, sin_a, 0.0))

    def rope(t):
        swapped = jnp.where(lane < MLA_ROPE // 2,
                            pltpu.roll(t, LANES - MLA_ROPE // 2, 1),
                            pltpu.roll(t, MLA_ROPE // 2, 1))
        return t * cos_t + swapped * sin_t

    def norm_rope_part(t, gain):
        ms = jnp.sum(t * t, axis=-1, keepdims=True) * (1.0 / MLA_ROPE)
        return t * lax.rsqrt(ms + EPS) * gain

    cq = mla_ref[:, 0:MLA_Q_LORA]
    ckv = mla_ref[:, MLA_Q_LORA:MLA_Q_LORA + MLA_KV_LORA]
    kr = mla_ref[:, MLA_Q_LORA + MLA_KV_LORA:]
    q_up = jnp.dot(_rms(cq, qa_ref[...]).astype(BF16), wq_ref[...], preferred_element_type=F32)
    kv_up = jnp.dot(_rms(ckv, kva_ref[...]).astype(BF16), wkv_ref[...], preferred_element_type=F32)
    k_rope = rope(norm_rope_part(kr, kgr_ref[...])).astype(BF16)
    for h in range(MLA_HEADS):
        c0 = h * MLA_QPAD
        q_ref[:, c0:c0 + LANES] = _rms(q_up[:, c0:c0 + LANES], qgn_ref[...]).astype(BF16)
        q_ref[:, c0 + LANES:c0 + 2 * LANES] = rope(
            norm_rope_part(q_up[:, c0 + LANES:c0 + 2 * LANES], qgr_ref[...])).astype(BF16)
        k_ref[:, c0:c0 + LANES] = _rms(kv_up[:, c0:c0 + LANES], kgn_ref[...]).astype(BF16)
        k_ref[:, c0 + LANES:c0 + 2 * LANES] = k_rope
        v_ref[:, h * MLA_V:(h + 1) * MLA_V] = kv_up[:, c0 + LANES:c0 + 2 * LANES].astype(BF16)


def _mla_prep(mla, pos, invf, qa, kva, wq, wkv, qgn, qgr, kgn, kgr, tm):
    T, n_mla = mla.shape
    full = lambda a: pl.BlockSpec(a.shape, lambda i: (0, 0))
    return pl.pallas_call(
        _mla_prep_kernel,
        grid=(T // tm,),
        in_specs=[
            pl.BlockSpec((tm, n_mla), lambda i: (i, 0)),
            pl.BlockSpec((tm, 1), lambda i: (i, 0)),
            full(invf), full(qa), full(kva), full(wq), full(wkv),
            full(qgn), full(qgr), full(kgn), full(kgr),
        ],
        out_specs=[
            pl.BlockSpec((tm, MLA_HEADS * MLA_QPAD), lambda i: (i, 0)),
            pl.BlockSpec((tm, MLA_HEADS * MLA_QPAD), lambda i: (i, 0)),
            pl.BlockSpec((tm, MLA_HEADS * MLA_V), lambda i: (i, 0)),
        ],
        out_shape=[
            jax.ShapeDtypeStruct((T, MLA_HEADS * MLA_QPAD), BF16),
            jax.ShapeDtypeStruct((T, MLA_HEADS * MLA_QPAD), BF16),
            jax.ShapeDtypeStruct((T, MLA_HEADS * MLA_V), BF16),
        ],
        compiler_params=_cparams(("parallel",)),
        name="mla_prep",
    )(mla, pos, invf, qa, kva, wq, wkv, qgn, qgr, kgn, kgr)


def _attn_kernel(q_ref, k_ref, v_ref, g_ref, o_ref):
    c = MLA_QK ** -0.5 * math.log2(math.e)
    tq = q_ref.shape[0]
    rc = tq // ATTN_CHUNKS
    rows = [pl.ds(i * rc, rc) for i in range(ATTN_CHUNKS)]
    s = [_dot_nt(q_ref[r, :], k_ref[...]) for r in rows]
    p = [jnp.exp2((si - jnp.max(si, axis=-1, keepdims=True)) * c) for si in s]
    l = [jnp.sum(pi, axis=-1, keepdims=True) for pi in p]
    o = [jnp.dot(pi.astype(BF16), v_ref[...], preferred_element_type=F32) for pi in p]
    for r, oi, li in zip(rows, o, l):
        o_ref[r, :] = _rms(oi / li, g_ref[...]).astype(o_ref.dtype)


def _attention(q, k, v, o_gain, B, S, tq):
    T = q.shape[0]
    nq = S // tq
    return pl.pallas_call(
        _attn_kernel,
        grid=(B, MLA_HEADS, nq),
        in_specs=[
            pl.BlockSpec((tq, MLA_QPAD), lambda b, h, i: (b * nq + i, h)),
            pl.BlockSpec((S, MLA_QPAD), lambda b, h, i: (b, h)),
            pl.BlockSpec((S, MLA_V), lambda b, h, i: (b, h)),
            pl.BlockSpec((1, MLA_V), lambda b, h, i: (0, h)),
        ],
        out_specs=pl.BlockSpec((tq, MLA_V), lambda b, h, i: (b * nq + i, h)),
        out_shape=jax.ShapeDtypeStruct((T, MLA_HEADS * MLA_V), BF16),
        compiler_params=_cparams(("parallel", "parallel", "arbitrary")),
        name="mla_attention",
    )(q, k, v, o_gain)


def _out_proj_kernel(of_ref, ob_ref, g_ref, ym_ref, x_ref, hgg_ref, wo_ref, fg_ref, wq_ref, keys_ref,
                     x1_ref, h2_ref, st_ref):
    o = of_ref[...] + ob_ref[...]
    gate_raw = g_ref[...]
    gate = gate_raw * _sigmoid(gate_raw)
    parts = []
    for h in range(HG_HEADS):
        sl = slice(h * HG_DK, (h + 1) * HG_DK)
        parts.append(_rms(o[:, sl], hgg_ref[:, sl]) * gate[:, sl])
    y_hg = jnp.concatenate(parts, axis=-1).astype(BF16)
    w_hg_rows = y_hg.shape[1]
    mix = jnp.dot(y_hg, wo_ref[0:w_hg_rows, :], preferred_element_type=F32)
    mix = mix + jnp.dot(ym_ref[...], wo_ref[w_hg_rows:, :], preferred_element_type=F32)
    x1 = x_ref[...] + mix
    x1_ref[...] = x1
    h2 = _rms(x1, fg_ref[...]).astype(BF16)
    h2_ref[...] = h2
    pq = jnp.dot(h2, wq_ref[...], preferred_element_type=F32).astype(BF16)
    for pc in range(keys_ref.shape[0]):
        st_ref[pc] = _dot_nt(keys_ref[pc], pq[:, pc * PEER_DHALF:(pc + 1) * PEER_DHALF])


def _out_proj(o_f, o_b, hg, y_mla, x2, hg_gain, w_out, ffn_gain, w_pq, keys, tm):
    T, D = x2.shape
    W = o_f.shape[1]
    npc = keys.shape[0]
    full2 = lambda a: pl.BlockSpec(a.shape, lambda i: (0, 0))
    return pl.pallas_call(
        _out_proj_kernel,
        grid=(T // tm,),
        in_specs=[
            pl.BlockSpec((tm, W), lambda i: (i, 0)),
            pl.BlockSpec((tm, W), lambda i: (i, 0)),
            pl.BlockSpec((tm, W), lambda i: (i, 4)),
            pl.BlockSpec((tm, y_mla.shape[1]), lambda i: (i, 0)),
            pl.BlockSpec((tm, D), lambda i: (i, 0)),
            full2(hg_gain), full2(w_out), full2(ffn_gain), full2(w_pq),
            pl.BlockSpec(keys.shape, lambda i: (0, 0, 0)),
        ],
        out_specs=[
            pl.BlockSpec((tm, D), lambda i: (i, 0)),
            pl.BlockSpec((tm, D), lambda i: (i, 0)),
            pl.BlockSpec((npc, PEER_NKEYS, tm), lambda i: (0, 0, i)),
        ],
        out_shape=[
            jax.ShapeDtypeStruct((T, D), F32),
            jax.ShapeDtypeStruct((T, D), BF16),
            jax.ShapeDtypeStruct((npc, PEER_NKEYS, T), F32),
        ],
        compiler_params=_cparams(("parallel",)),
        name="out_proj_peer_query",
    )(o_f, o_b, hg, y_mla, x2, hg_gain, w_out, ffn_gain, w_pq, keys)


def _tree_max(xs):
    xs = list(xs)
    while len(xs) > 1:
        nxt = [jnp.maximum(xs[i], xs[i + 1]) for i in range(0, len(xs) - 1, 2)]
        if len(xs) % 2:
            nxt.append(xs[-1])
        xs = nxt
    return xs[0]


def _peer_select_kernel(s_ref, tau_ref, m1_ref, iz_ref, btop_ref, top_scr):
    n_keep = PEER_TOPK + 1
    for p in range(PEER_HEADS):
        for c in range(2):
            cur = s_ref[2 * p + c]
            for r in range(n_keep):
                m = jnp.max(cur, axis=0, keepdims=True)
                top_scr[c, r, pl.ds(p, 1), :] = m
                if r + 1 < n_keep:
                    cur = jnp.where(cur == m, NEG_INF, cur)
    a = [top_scr[0, r] for r in range(n_keep)]
    b = [top_scr[1, r] for r in range(n_keep)]
    cands = [a[i] + b[j] for i in range(n_keep) for j in range(n_keep) if (i + 1) * (j + 1) <= n_keep]
    top = a[0] + b[0]
    z = jnp.zeros_like(top)
    kth = top
    for r in range(PEER_TOPK):
        kth = _tree_max(cands)
        z = z + jnp.exp(kth - top)
        cands = [jnp.where(cv == kth, NEG_INF, cv) for cv in cands]
    nxt = _tree_max(cands)
    tau_ref[...] = 0.5 * (kth + nxt)
    m1_ref[...] = a[0]
    iz_ref[...] = 1.0 / z
    for r in range(PEER_TOPK):
        btop_ref[r] = b[r]


def _peer_select(scores_t, tt):
    npc, nk, T = scores_t.shape
    out = jax.ShapeDtypeStruct((PEER_HEADS, T), F32)
    ospec = pl.BlockSpec((PEER_HEADS, tt), lambda i: (0, i))
    return pl.pallas_call(
        _peer_select_kernel,
        grid=(T // tt,),
        in_specs=[pl.BlockSpec((npc, nk, tt), lambda i: (0, 0, i))],
        out_specs=[ospec] * 3 + [pl.BlockSpec((PEER_TOPK, PEER_HEADS, tt), lambda i: (0, 0, i))],
        out_shape=[out] * 3 + [jax.ShapeDtypeStruct((PEER_TOPK, PEER_HEADS, T), F32)],
        scratch_shapes=[pltpu.VMEM((2, PEER_TOPK + 1, PEER_HEADS, tt), F32)],
        compiler_params=_cparams(("parallel",)),
        name="peer_select",
    )(scores_t)


def _gelu_exact(x):
    return 0.5 * x * (1.0 + lax.erf(x * (1.0 / math.sqrt(2.0))))


def _peer_dense_kernel(h_ref, u_ref, vt_ref, s_ref, tau_ref, m1_ref, iz_ref, btop_ref, x1_ref, o_ref,
                       n_scr, e1_scr, rank_scr, e2_scr, a_scr, p_scr, yt_scr):
    jt = pl.program_id(1)
    te, tt = a_scr.shape
    nb = te // PEER_NKEYS
    n_grp = nb // SUBLANES
    PK = SUBLANES * (4 // jnp.dtype(GATE_DTYPE).itemsize)
    IBQ = 32 * PK // PEER_NKEYS

    @pl.when(jt == 0)
    def _():
        yt_scr[...] = jnp.zeros_like(yt_scr)
        grouped = (PEER_NKEYS // SUBLANES, SUBLANES, LANES)
        for p in range(PEER_HEADS):
            for st in range(tt // LANES):
                sl = pl.ds(st * LANES, LANES)
                s1 = s_ref[2 * p, :, sl]
                s2 = s_ref[2 * p + 1, :, sl]
                x = tau_ref[pl.ds(p, 1), sl] - s1
                n = jnp.zeros_like(s1)
                rank = jnp.zeros_like(s2)
                for r in range(PEER_TOPK):
                    b_r = btop_ref[r, pl.ds(p, 1), sl]
                    n = jnp.where(b_r >= x, r + 1.0, n)
                    rank = jnp.where(b_r > s2, r + 1.0, rank)
                n_scr[p, :, :, sl] = n.reshape(grouped)
                e1_scr[p, :, :, sl] = jnp.exp(s1 - m1_ref[pl.ds(p, 1), sl]).reshape(grouped)
                rank_scr[p, :, sl] = rank.astype(GATE_DTYPE)
                m2 = btop_ref[0, pl.ds(p, 1), sl]
                e2_scr[p, :, sl] = (jnp.exp(s2 - m2) * iz_ref[pl.ds(p, 1), sl]).astype(GATE_DTYPE)

    a_scr[...] = _dot_nt(u_ref[...], h_ref[...])
    zero = jnp.zeros((PK, LANES), GATE_DTYPE)
    for ibg in range(nb // IBQ):
        for st in range(tt // LANES):
            sl = pl.ds(st * LANES, LANES)
            gates = [[zero for _ in range(PEER_NKEYS // PK)] for _ in range(IBQ)]
            for p in range(PEER_HEADS):
                rank = rank_scr[p, :, sl]
                e2 = e2_scr[p, :, sl]
                for q in range(IBQ):
                    ib = ibg * IBQ + q
                    grp = jt * n_grp + ib // SUBLANES
                    sub = pl.ds(ib % SUBLANES, 1)
                    n = jnp.broadcast_to(n_scr[p, grp, sub, sl], (PK, LANES)).astype(GATE_DTYPE)
                    e1 = jnp.broadcast_to(e1_scr[p, grp, sub, sl], (PK, LANES)).astype(GATE_DTYPE)
                    for r in range(PEER_NKEYS // PK):
                        rr = slice(r * PK, (r + 1) * PK)
                        gates[q][r] = gates[q][r] + jnp.where(rank[rr, :] < n, e2[rr, :], zero) * e1
            for q in range(IBQ):
                rows = pl.ds((ibg * IBQ + q) * PEER_NKEYS, PEER_NKEYS)
                gate = jnp.concatenate(gates[q], axis=0)
                p_scr[rows, sl] = (_gelu_exact(a_scr[rows, sl]).astype(GATE_DTYPE) * gate).astype(BF16)
    yt_scr[...] += jnp.dot(vt_ref[...], p_scr[...], preferred_element_type=F32)

    @pl.when(jt == pl.num_programs(1) - 1)
    def _():
        o_ref[...] = x1_ref[...] + yt_scr[...].T


def _peer_dense(h2, u_bf, vt_bf, scores_t, tau, m1, iz, btop, x1, tt, te):
    T, D = h2.shape
    E = u_bf.shape[0]
    npc, nk, _ = scores_t.shape
    assert te % (PEER_NKEYS * SUBLANES) == 0 and E % te == 0
    row = pl.BlockSpec((PEER_HEADS, tt), lambda t, j: (0, t))
    return pl.pallas_call(
        _peer_dense_kernel,
        grid=(T // tt, E // te),
        in_specs=[
            pl.BlockSpec((tt, D), lambda t, j: (t, 0)),
            pl.BlockSpec((te, D), lambda t, j: (j, 0)),
            pl.BlockSpec((D, te), lambda t, j: (0, j)),
            pl.BlockSpec((npc, nk, tt), lambda t, j: (0, 0, t)),
            row, row, row,
            pl.BlockSpec((PEER_TOPK, PEER_HEADS, tt), lambda t, j: (0, 0, t)),
            pl.BlockSpec((tt, D), lambda t, j: (t, 0)),
        ],
        out_specs=pl.BlockSpec((tt, D), lambda t, j: (t, 0)),
        out_shape=jax.ShapeDtypeStruct((T, D), F32),
        scratch_shapes=[
            pltpu.VMEM((PEER_HEADS, nk // SUBLANES, SUBLANES, tt), F32),
            pltpu.VMEM((PEER_HEADS, nk // SUBLANES, SUBLANES, tt), F32),
            pltpu.VMEM((PEER_HEADS, nk, tt), GATE_DTYPE),
            pltpu.VMEM((PEER_HEADS, nk, tt), GATE_DTYPE),
            pltpu.VMEM((te, tt), F32),
            pltpu.VMEM((te, tt), BF16),
            pltpu.VMEM((D, tt), F32),
        ],
        compiler_params=_cparams(("parallel", "arbitrary")),
        name="peer_dense",
    )(h2, u_bf, vt_bf, scores_t, tau, m1, iz, btop, x1)


def _pick(n, pref):
    t = min(n, pref)
    assert n % t == 0, (n, t)
    return t


def _layer(x2, pos, B, S, attn_norm, w_in, lb_logits, hg_o_norm, q_a_norm, w_q_up, kv_a_norm,
           w_kv_up, q_norm, k_norm, mla_o_norm, w_out, ffn_norm, peer_w_q, peer_sub_keys, peer_u, peer_v):
    T, D = x2.shape
    n_hg = 5 * HG_HEADS * HG_DK
    row = lambda a: a.reshape(1, -1).astype(F32)

    w_hg = w_in[:, :n_hg].astype(BF16)
    n_mla = w_in.shape[1] - n_hg
    w_mla = jnp.pad(w_in[:, n_hg:], ((0, 0), (0, -n_mla % LANES))).astype(BF16)
    pad_r = LANES - MLA_ROPE
    wq = w_q_up.reshape(MLA_Q_LORA, MLA_HEADS, MLA_QK)
    wq = jnp.pad(wq, ((0, 0), (0, 0), (0, MLA_QPAD - MLA_QK))).reshape(MLA_Q_LORA, MLA_HEADS * MLA_QPAD)
    wq = wq.astype(BF16)
    wkv = w_kv_up.astype(BF16)
    qgn, qgr = row(q_norm[:MLA_NOPE]), row(jnp.pad(q_norm[MLA_NOPE:], (0, pad_r)))
    kgn, kgr = row(k_norm[:MLA_NOPE]), row(jnp.pad(k_norm[MLA_NOPE:], (0, pad_r)))
    inv_freq = 1.0 / (ROPE_THETA ** (jnp.arange(0, MLA_ROPE, 2, dtype=F32) / MLA_ROPE))
    invf = row(jnp.concatenate([inv_freq, inv_freq, jnp.zeros((pad_r,), F32)]))
    keys = peer_sub_keys.reshape(2 * PEER_HEADS, PEER_NKEYS, PEER_DHALF).astype(BF16)
    u_bf = peer_u.astype(BF16)
    vt_bf = peer_v.astype(BF16).T

    hg, mla = _in_proj(x2, row(attn_norm), w_hg, w_mla, _pick(T, 256))
    tb = _pick(S, 512)
    o_f = _hgrn_scan(hg, lb_logits[:, 0, :], B, S, False, tb)
    o_b = _hgrn_scan(hg, lb_logits[:, 1, :], B, S, True, tb)
    q, k, v = _mla_prep(mla, pos, invf, row(q_a_norm), row(kv_a_norm), wq, wkv, qgn, qgr, kgn, kgr,
                        _pick(T, 256))
    y_mla = _attention(q, k, v, mla_o_norm.reshape(1, -1).astype(F32), B, S, _pick(S, 1024))
    x1, h2, scores_t = _out_proj(o_f, o_b, hg, y_mla, x2, row(hg_o_norm), w_out.astype(BF16),
                                 row(ffn_norm), peer_w_q.astype(BF16), keys, _pick(T, 256))
    tau, m1, iz, btop = _peer_select(scores_t, _pick(T, 256))
    return _peer_dense(h2, u_bf, vt_bf, scores_t, tau, m1, iz, btop, x1, _pick(T, 512), 1024)


def kernel(x, positions, attn_norm, w_in, hg_lb_logits, hg_o_norm, q_a_norm, w_q_up, kv_a_norm, w_kv_up,
           q_norm, k_norm, mla_o_norm, w_out, ffn_norm, peer_w_q, peer_sub_keys, peer_u, peer_v):
    B, S, D = x.shape
    x2 = x.reshape(B * S, D)
    pos = positions.reshape(B * S, 1)
    for l in range(attn_norm.shape[0]):
        assert l == 0
        x2 = _layer(x2, pos, B, S, attn_norm[l], w_in[l], hg_lb_logits, hg_o_norm[l], q_a_norm[l],
                    w_q_up[l], kv_a_norm[l], w_kv_up[l], q_norm[l], k_norm[l], mla_o_norm[l], w_out[l],
                    ffn_norm[l], peer_w_q[l], peer_sub_keys[l], peer_u[l], peer_v[l])
    return x2.reshape(B, S, D)
```

```python
import functools
import math

import jax
import jax.numpy as jnp
from jax import lax
from jax.experimental import pallas as pl
from jax.experimental.pallas import tpu as pltpu

F32 = jnp.float32
BF16 = jnp.bfloat16
EPS = 1e-6
LANES = 128
SUBLANES = 8
NEG_INF = float("-inf")

HG_HEADS = 4
HG_DK = 128
HG_CHUNK = 64
HG_HALF = HG_CHUNK // 2
HG_QUARTER = HG_CHUNK // 4
HG_BATCH_PER_STEP = 4
MLA_HEADS = 4
MLA_Q_LORA = 384
MLA_KV_LORA = 256
MLA_NOPE = 128
MLA_ROPE = 64
MLA_V = 128
MLA_QK = MLA_NOPE + MLA_ROPE
MLA_QPAD = 2 * LANES
ROPE_THETA = 10000.0
ATTN_CHUNKS = 4
PEER_HEADS = 8
PEER_NKEYS = 128
PEER_TOPK = 16
PEER_DHALF = 128
GATE_DTYPE = BF16

VMEM_LIMIT = 56 * 1024 * 1024


def _cparams(sem):
    return pltpu.CompilerParams(dimension_semantics=sem, vmem_limit_bytes=VMEM_LIMIT)


def _rms(x, gain):
    ms = jnp.mean(x * x, axis=-1, keepdims=True)
    return x * lax.rsqrt(ms + EPS) * gain


def _sigmoid(x):
    return 1.0 / (1.0 + jnp.exp(-x))


def _dot_nt(a, b):
    return lax.dot_general(a, b, (((1,), (1,)), ((), ())), preferred_element_type=F32)


def _dot_tn(a, b):
    return lax.dot_general(a, b, (((0,), (0,)), ((), ())), preferred_element_type=F32)


def _in_proj_kernel(x_ref, g_ref, whg_ref, wmla_ref, hg_ref, mla_ref):
    h = _rms(x_ref[...], g_ref[...]).astype(BF16)
    hg_ref[...] = jnp.dot(h, whg_ref[...], preferred_element_type=F32)
    mla_ref[...] = jnp.dot(h, wmla_ref[...], preferred_element_type=F32)


def _in_proj(x2, gain, w_hg, w_mla, tm):
    T, D = x2.shape
    n_hg, n_mla = w_hg.shape[1], w_mla.shape[1]
    return pl.pallas_call(
        _in_proj_kernel,
        grid=(T // tm,),
        in_specs=[
            pl.BlockSpec((tm, D), lambda i: (i, 0)),
            pl.BlockSpec((1, D), lambda i: (0, 0)),
            pl.BlockSpec((D, n_hg), lambda i: (0, 0)),
            pl.BlockSpec((D, n_mla), lambda i: (0, 0)),
        ],
        out_specs=[
            pl.BlockSpec((tm, n_hg), lambda i: (i, 0)),
            pl.BlockSpec((tm, n_mla), lambda i: (i, 0)),
        ],
        out_shape=[jax.ShapeDtypeStruct((T, n_hg), F32), jax.ShapeDtypeStruct((T, n_mla), F32)],
        compiler_params=_cparams(("parallel",)),
        name="in_proj",
    )(x2, gain, w_hg, w_mla)


def _hgrn_kernel(q_ref, f_ref, v_ref, lb_ref, o_ref, st_ref, b_scr, k_scr, p_scr, *, reverse, n_chunks):
    C, Hf, Qt, G = HG_CHUNK, HG_HALF, HG_QUARTER, SUBLANES
    n_batch = q_ref.shape[0]
    n_heads = st_ref.shape[0] // n_batch
    chains = [(bi, hh) for bi in range(n_batch) for hh in range(n_heads)]

    @pl.when(pl.program_id(1) == 0)
    def _():
        st_ref[...] = jnp.zeros_like(st_ref)

    logits = lb_ref[...]
    ex = jnp.exp(logits - jnp.max(logits, axis=0, keepdims=True))
    lb_all = ex[0:1, :] / jnp.sum(ex, axis=0, keepdims=True)

    row = lax.broadcasted_iota(jnp.int32, (C, C), 0)
    col = lax.broadcasted_iota(jnp.int32, (C, C), 1)
    tri = (col >= row) if reverse else (col <= row)
    cum_mat = tri.astype(F32)
    same_half = (row >= Hf) == (col >= Hf)
    row_c = lax.broadcasted_iota(jnp.int32, (C, LANES), 0)
    q_side = (row_c < Hf) if reverse else (row_c >= Hf)
    odd_quarter = (row_c // Qt) % 2 == 1
    q_side4 = jnp.logical_not(odd_quarter) if reverse else odd_quarter
    low_half = row_c < Hf
    lane_g = lax.broadcasted_iota(jnp.int32, (G, LANES), 1)
    ones = jnp.ones((LANES, LANES), BF16)
    mid_row = Hf if reverse else Hf - 1
    qmid_rows = (Qt, Hf + Qt) if reverse else (Qt - 1, Hf + Qt - 1)
    end_row = 0 if reverse else C - 1

    def live_groups(s):
        return range(0, s // G + 1) if reverse else range(s // G, Qt // G)

    def gates(ci, r0):
        bi, hh = chains[ci]
        cols = slice(hh * HG_DK, (hh + 1) * HG_DK)
        lb = lb_all[:, cols]
        q_raw = q_ref[bi, pl.ds(r0, C), cols]
        q = q_raw * _sigmoid(q_raw)
        f = lb + (1.0 - lb) * _sigmoid(f_ref[bi, pl.ds(r0, C), cols])
        k = 1.0 - f
        vb = v_ref[bi, pl.ds(r0, C), cols].astype(BF16)
        b = jnp.dot(cum_mat, jnp.log2(f), precision=lax.Precision.HIGHEST,
                    preferred_element_type=F32)
        b_scr[ci] = b
        k_scr[ci] = k
        return q, k, vb, b

    def state_and_offdiag(ci, q, k, vb, b):
        b_mid = b_scr[ci, pl.ds(mid_row, 1), :]
        b_end = b_scr[ci, pl.ds(end_row, 1), :]
        st = st_ref[ci]
        o = _dot_nt((q * jnp.exp2(b)).astype(BF16), st.astype(BF16))
        kd = (k * jnp.exp2(b_end - b)).astype(BF16)
        st_ref[ci] = st * jnp.exp2(b_end) + _dot_tn(vb, kd)
        qt = jnp.where(q_side, q * jnp.exp2(b - b_mid), 0.0).astype(BF16)
        kt = jnp.where(q_side, 0.0, k * jnp.exp2(b_mid - b)).astype(BF16)
        b_q = jnp.where(low_half, b_scr[ci, pl.ds(qmid_rows[0], 1), :], b_scr[ci, pl.ds(qmid_rows[1], 1), :])
        qt4 = jnp.where(q_side4, q * jnp.exp2(b - b_q), 0.0).astype(BF16)
        kt4 = jnp.where(q_side4, 0.0, k * jnp.exp2(b_q - b)).astype(BF16)
        return o, _dot_nt(qt, kt), _dot_nt(qt4, kt4)

    def diag_products(ci, q, b):
        n_rows = 0
        for qb in range(C // Qt):
            base = qb * Qt
            for s in range(Qt):
                g0, g1 = live_groups(s)[0], live_groups(s)[-1] + 1
                rows = slice(base + g0 * G, base + g1 * G)
                bs = b_scr[ci, pl.ds(base + s, 1), :]
                ks = k_scr[ci, pl.ds(base + s, 1), :]
                p_scr[ci, pl.ds(n_rows, (g1 - g0) * G), :] = (q[rows, :] * ks) * jnp.exp2(b[rows, :] - bs)
                n_rows += (g1 - g0) * G

    def diag_scores(rs):
        n_rows = 0
        groups = []
        for qb in range(C // Qt):
            base = qb * Qt
            acc = [jnp.zeros((G, LANES), F32) for _ in range(Qt // G)]
            for s in range(Qt):
                for g in live_groups(s):
                    acc[g] = jnp.where(lane_g == base + s, rs[n_rows:n_rows + G, :], acc[g])
                    n_rows += G
            groups.extend(acc)
        return jnp.concatenate(groups, axis=0)[:, :C]

    def chunk_body(ci, carry):
        c = (n_chunks - 1 - ci) if reverse else ci
        r0 = pl.multiple_of(c * C, C)
        ids = range(len(chains))
        qkvb = [gates(ci, r0) for ci in ids]
        o_sc = [state_and_offdiag(ci, *qkvb[ci]) for ci in ids]
        for ci in ids:
            diag_products(ci, qkvb[ci][0], qkvb[ci][3])
        rs = [jnp.dot(p_scr[ci].astype(BF16), ones, preferred_element_type=F32) for ci in ids]
        for ci in ids:
            bi, hh = chains[ci]
            o, sc_half, sc_quarter = o_sc[ci]
            scores = jnp.where(same_half, jnp.where(tri, diag_scores(rs[ci]) + sc_quarter, 0.0), sc_half)
            o = o + jnp.dot(scores.astype(BF16), qkvb[ci][2], preferred_element_type=F32)
            o_ref[bi, pl.ds(r0, C), hh * HG_DK:(hh + 1) * HG_DK] = o
        return carry

    lax.fori_loop(0, n_chunks, chunk_body, 0)


def _hgrn_diag_rows():
    return (HG_CHUNK // HG_QUARTER) * sum(HG_QUARTER - (s // SUBLANES) * SUBLANES for s in range(HG_QUARTER))


def _hgrn_scan(hg, lb_logits_dir, B, S, reverse, tb):
    T = hg.shape[0]
    W = HG_HEADS * HG_DK
    nblk = S // tb
    f_blk = 2 if reverse else 1
    v_blk = 3
    nbat = HG_BATCH_PER_STEP if B % HG_BATCH_PER_STEP == 0 else 1
    n_chains = nbat * HG_HEADS
    hg4 = hg.reshape(B // nbat, nbat, S, hg.shape[1])

    def spec(col_blk):
        return pl.BlockSpec((None, nbat, tb, W),
                            lambda g, i: (g, 0, (nblk - 1 - i) if reverse else i, col_blk))

    kern = functools.partial(_hgrn_kernel, reverse=reverse, n_chunks=tb // HG_CHUNK)
    out = pl.pallas_call(
        kern,
        grid=(B // nbat, nblk),
        in_specs=[spec(0), spec(f_blk), spec(v_blk),
                  pl.BlockSpec((lb_logits_dir.shape[0], W), lambda g, i: (0, 0))],
        out_specs=spec(0),
        out_shape=jax.ShapeDtypeStruct((B // nbat, nbat, S, W), F32),
        scratch_shapes=[
            pltpu.VMEM((n_chains, HG_DK, HG_DK), F32),
            pltpu.VMEM((n_chains, HG_CHUNK, HG_DK), F32),
            pltpu.VMEM((n_chains, HG_CHUNK, HG_DK), F32),
            pltpu.VMEM((n_chains, _hgrn_diag_rows(), HG_DK), F32),
        ],
        compiler_params=_cparams(("parallel", "arbitrary")),
        name="hgrn_rev" if reverse else "hgrn_fwd",
    )(hg4, hg4, hg4, lb_logits_dir)
    return out.reshape(T, W)


def _rope_table_kernel(invf_ref, pos_ref, cos_ref, sin_ref):
    pos = pos_ref[...].astype(F32)
    for f in range(cos_ref.shape[0]):
        ang = pos * invf_ref[f]
        cos_ref[f] = jnp.cos(ang)
        sin_ref[f] = jnp.sin(ang)


def _rope_tables(pos, inv_freq):
    T = pos.shape[0]
    nf = inv_freq.shape[0]
    rows = T // LANES
    tr = _pick(rows, SUBLANES)
    plane = jax.ShapeDtypeStruct((nf, rows, LANES), F32)
    cos, sin = pl.pallas_call(
        _rope_table_kernel,
        grid=(rows // tr,),
        in_specs=[pl.BlockSpec(memory_space=pltpu.SMEM), pl.BlockSpec((tr, LANES), lambda i: (i, 0))],
        out_specs=[pl.BlockSpec((nf, tr, LANES), lambda i: (0, i, 0))] * 2,
        out_shape=[plane, plane],
        compiler_params=_cparams(("parallel",)),
        name="rope_tables",
    )(inv_freq, pos.reshape(rows, LANES))

    def per_token(t):
        t = t.reshape(nf, T).T
        return jnp.concatenate([t, t, jnp.zeros((T, LANES - 2 * nf), F32)], axis=1)

    return per_token(cos), per_token(sin)


def _mla_prep_kernel(mla_ref, cos_ref, sin_ref, qa_ref, kva_ref, wq_ref, wkv_ref,
                     qgn_ref, qgr_ref, kgn_ref, kgr_ref, q_ref, k_ref, v_ref):
    tm = mla_ref.shape[0]
    lane = lax.broadcasted_iota(jnp.int32, (tm, LANES), 1)
    cos_t = cos_ref[...]
    sin_t = jnp.where(lane < MLA_ROPE // 2, -sin_ref[...], sin_ref[...])

    def rope(t):
        swapped = jnp.where(lane < MLA_ROPE // 2,
                            pltpu.roll(t, LANES - MLA_ROPE // 2, 1),
                            pltpu.roll(t, MLA_ROPE // 2, 1))
        return t * cos_t + swapped * sin_t

    def norm_rope_part(t, gain):
        ms = jnp.sum(t * t, axis=-1, keepdims=True) * (1.0 / MLA_ROPE)
        return t * lax.rsqrt(ms + EPS) * gain

    cq = mla_ref[:, 0:MLA_Q_LORA]
    ckv = mla_ref[:, MLA_Q_LORA:MLA_Q_LORA + MLA_KV_LORA]
    kr = mla_ref[:, MLA_Q_LORA + MLA_KV_LORA:]
    q_up = jnp.dot(_rms(cq, qa_ref[...]).astype(BF16), wq_ref[...], preferred_element_type=F32)
    kv_up = jnp.dot(_rms(ckv, kva_ref[...]).astype(BF16), wkv_ref[...], preferred_element_type=F32)
    k_rope = rope(norm_rope_part(kr, kgr_ref[...])).astype(BF16)
    for h in range(MLA_HEADS):
        c0 = h * MLA_QPAD
        q_ref[:, c0:c0 + LANES] = _rms(q_up[:, c0:c0 + LANES], qgn_ref[...]).astype(BF16)
        q_ref[:, c0 + LANES:c0 + 2 * LANES] = rope(
            norm_rope_part(q_up[:, c0 + LANES:c0 + 2 * LANES], qgr_ref[...])).astype(BF16)
        k_ref[:, c0:c0 + LANES] = _rms(kv_up[:, c0:c0 + LANES], kgn_ref[...]).astype(BF16)
        k_ref[:, c0 + LANES:c0 + 2 * LANES] = k_rope
        v_ref[:, h * MLA_V:(h + 1) * MLA_V] = kv_up[:, c0 + LANES:c0 + 2 * LANES].astype(BF16)


def _mla_prep(mla, cos, sin, qa, kva, wq, wkv, qgn, qgr, kgn, kgr, tm):
    T, n_mla = mla.shape
    full = lambda a: pl.BlockSpec(a.shape, lambda i: (0, 0))
    return pl.pallas_call(
        _mla_prep_kernel,
        grid=(T // tm,),
        in_specs=[
            pl.BlockSpec((tm, n_mla), lambda i: (i, 0)),
            pl.BlockSpec((tm, LANES), lambda i: (i, 0)),
            pl.BlockSpec((tm, LANES), lambda i: (i, 0)),
            full(qa), full(kva), full(wq), full(wkv),
            full(qgn), full(qgr), full(kgn), full(kgr),
        ],
        out_specs=[
            pl.BlockSpec((tm, MLA_HEADS * MLA_QPAD), lambda i: (i, 0)),
            pl.BlockSpec((tm, MLA_HEADS * MLA_QPAD), lambda i: (i, 0)),
            pl.BlockSpec((tm, MLA_HEADS * MLA_V), lambda i: (i, 0)),
        ],
        out_shape=[
            jax.ShapeDtypeStruct((T, MLA_HEADS * MLA_QPAD), BF16),
            jax.ShapeDtypeStruct((T, MLA_HEADS * MLA_QPAD), BF16),
            jax.ShapeDtypeStruct((T, MLA_HEADS * MLA_V), BF16),
        ],
        compiler_params=_cparams(("parallel",)),
        name="mla_prep",
    )(mla, cos, sin, qa, kva, wq, wkv, qgn, qgr, kgn, kgr)


def _attn_kernel(q_ref, k_ref, v_ref, g_ref, o_ref):
    c = MLA_QK ** -0.5 * math.log2(math.e)
    tq = q_ref.shape[0]
    rc = tq // ATTN_CHUNKS
    rows = [pl.ds(i * rc, rc) for i in range(ATTN_CHUNKS)]
    s = [_dot_nt(q_ref[r, :], k_ref[...]) for r in rows]
    p = [jnp.exp2((si - jnp.max(si, axis=-1, keepdims=True)) * c) for si in s]
    l = [jnp.sum(pi, axis=-1, keepdims=True) for pi in p]
    o = [jnp.dot(pi.astype(BF16), v_ref[...], preferred_element_type=F32) for pi in p]
    for r, oi, li in zip(rows, o, l):
        o_ref[r, :] = _rms(oi / li, g_ref[...]).astype(o_ref.dtype)


def _attention(q, k, v, o_gain, B, S, tq):
    T = q.shape[0]
    nq = S // tq
    return pl.pallas_call(
        _attn_kernel,
        grid=(B, MLA_HEADS, nq),
        in_specs=[
            pl.BlockSpec((tq, MLA_QPAD), lambda b, h, i: (b * nq + i, h)),
            pl.BlockSpec((S, MLA_QPAD), lambda b, h, i: (b, h)),
            pl.BlockSpec((S, MLA_V), lambda b, h, i: (b, h)),
            pl.BlockSpec((1, MLA_V), lambda b, h, i: (0, h)),
        ],
        out_specs=pl.BlockSpec((tq, MLA_V), lambda b, h, i: (b * nq + i, h)),
        out_shape=jax.ShapeDtypeStruct((T, MLA_HEADS * MLA_V), BF16),
        compiler_params=_cparams(("parallel", "parallel", "arbitrary")),
        name="mla_attention",
    )(q, k, v, o_gain)


def _out_proj_kernel(of_ref, ob_ref, g_ref, ym_ref, x_ref, hgg_ref, wo_ref, fg_ref, wq_ref, keys_ref,
                     x1_ref, h2_ref, st_ref):
    o = of_ref[...] + ob_ref[...]
    gate_raw = g_ref[...]
    gate = gate_raw * _sigmoid(gate_raw)
    parts = []
    for h in range(HG_HEADS):
        sl = slice(h * HG_DK, (h + 1) * HG_DK)
        parts.append(_rms(o[:, sl], hgg_ref[:, sl]) * gate[:, sl])
    y_hg = jnp.concatenate(parts, axis=-1).astype(BF16)
    w_hg_rows = y_hg.shape[1]
    mix = jnp.dot(y_hg, wo_ref[0:w_hg_rows, :], preferred_element_type=F32)
    mix = mix + jnp.dot(ym_ref[...], wo_ref[w_hg_rows:, :], preferred_element_type=F32)
    x1 = x_ref[...] + mix
    x1_ref[...] = x1
    h2 = _rms(x1, fg_ref[...]).astype(BF16)
    h2_ref[...] = h2
    pq = jnp.dot(h2, wq_ref[...], preferred_element_type=F32).astype(BF16)
    for pc in range(keys_ref.shape[0]):
        st_ref[pc] = _dot_nt(keys_ref[pc], pq[:, pc * PEER_DHALF:(pc + 1) * PEER_DHALF])


def _out_proj(o_f, o_b, hg, y_mla, x2, hg_gain, w_out, ffn_gain, w_pq, keys, tm):
    T, D = x2.shape
    W = o_f.shape[1]
    npc = keys.shape[0]
    full2 = lambda a: pl.BlockSpec(a.shape, lambda i: (0, 0))
    return pl.pallas_call(
        _out_proj_kernel,
        grid=(T // tm,),
        in_specs=[
            pl.BlockSpec((tm, W), lambda i: (i, 0)),
            pl.BlockSpec((tm, W), lambda i: (i, 0)),
            pl.BlockSpec((tm, W), lambda i: (i, 4)),
            pl.BlockSpec((tm, y_mla.shape[1]), lambda i: (i, 0)),
            pl.BlockSpec((tm, D), lambda i: (i, 0)),
            full2(hg_gain), full2(w_out), full2(ffn_gain), full2(w_pq),
            pl.BlockSpec(keys.shape, lambda i: (0, 0, 0)),
        ],
        out_specs=[
            pl.BlockSpec((tm, D), lambda i: (i, 0)),
            pl.BlockSpec((tm, D), lambda i: (i, 0)),
            pl.BlockSpec((npc, PEER_NKEYS, tm), lambda i: (0, 0, i)),
        ],
        out_shape=[
            jax.ShapeDtypeStruct((T, D), F32),
            jax.ShapeDtypeStruct((T, D), BF16),
            jax.ShapeDtypeStruct((npc, PEER_NKEYS, T), F32),
        ],
        compiler_params=_cparams(("parallel",)),
        name="out_proj_peer_query",
    )(o_f, o_b, hg, y_mla, x2, hg_gain, w_out, ffn_gain, w_pq, keys)


def _tree_max(xs):
    xs = list(xs)
    while len(xs) > 1:
        nxt = [jnp.maximum(xs[i], xs[i + 1]) for i in range(0, len(xs) - 1, 2)]
        if len(xs) % 2:
            nxt.append(xs[-1])
        xs = nxt
    return xs[0]


def _sorting_network(n):
    pairs = []
    p = 1
    while p < n:
        k = p
        while k >= 1:
            for j in range(k % p, n - k, 2 * k):
                for i in range(min(k, n - j - k)):
                    if (i + j) // (2 * p) == (i + j + k) // (2 * p):
                        pairs.append((i + j, i + j + k))
            k //= 2
        p *= 2
    return pairs


def _bitonic_merge_network(n):
    pairs = []
    stride = n // 2
    while stride >= 1:
        pairs += [(i, i + stride) for i in range(n) if not i & stride]
        stride //= 2
    return pairs


def _compare_exchange(xs, pairs):
    xs = list(xs)
    for i, j in pairs:
        xs[i], xs[j] = jnp.maximum(xs[i], xs[j]), jnp.minimum(xs[i], xs[j])
    return xs


def _peer_select_kernel(s_ref, iz_ref, cnt_ref, atop_ref, btop_ref, top_scr):
    K = PEER_TOPK
    n_vregs = PEER_NKEYS // SUBLANES
    assert n_vregs == K
    sort_pairs = _sorting_network(K)
    merge_pairs = _bitonic_merge_network(K)
    for p in range(PEER_HEADS):
        for c in range(2):
            lists = _compare_exchange(
                [s_ref[2 * p + c, pl.ds(g * SUBLANES, SUBLANES), :] for g in range(n_vregs)], sort_pairs)
            shift = SUBLANES // 2
            while shift >= 1:
                other = [pltpu.roll(x, shift, 0) for x in lists]
                lists = _compare_exchange(
                    [jnp.maximum(lists[r], other[K - 1 - r]) for r in range(K)], merge_pairs)
                shift //= 2
            for r in range(K):
                top_scr[c, r, pl.ds(p, 1), :] = lists[r][0:1, :]
    a = [top_scr[0, r] for r in range(K)]
    b = [top_scr[1, r] for r in range(K)]
    pairs = [(i, j) for i in range(K) for j in range(K) if (i + 1) * (j + 1) <= K]
    cands = [a[i] + b[j] for i, j in pairs]
    top = a[0] + b[0]
    z = jnp.zeros_like(top)
    kth = top
    for r in range(K):
        kth = _tree_max(cands)
        z = z + jnp.exp(kth - top)
        cands = [jnp.where(cv == kth, NEG_INF, cv) for cv in cands]
    iz_ref[...] = 1.0 / z
    for i in range(K):
        cnt = jnp.zeros_like(top)
        for j in range(K):
            if (i, j) in pairs:
                cnt = cnt + jnp.where(a[i] + b[j] >= kth, 1.0, 0.0)
        cnt_ref[i] = cnt
        atop_ref[i] = a[i]
        btop_ref[i] = b[i]


def _peer_select(scores_t, tt):
    npc, nk, T = scores_t.shape
    top = jax.ShapeDtypeStruct((PEER_TOPK, PEER_HEADS, T), F32)
    top_spec = pl.BlockSpec((PEER_TOPK, PEER_HEADS, tt), lambda i: (0, 0, i))
    return pl.pallas_call(
        _peer_select_kernel,
        grid=(T // tt,),
        in_specs=[pl.BlockSpec((npc, nk, tt), lambda i: (0, 0, i))],
        out_specs=[pl.BlockSpec((PEER_HEADS, tt), lambda i: (0, i))] + [top_spec] * 3,
        out_shape=[jax.ShapeDtypeStruct((PEER_HEADS, T), F32)] + [top] * 3,
        scratch_shapes=[pltpu.VMEM((2, PEER_TOPK, PEER_HEADS, tt), F32)],
        compiler_params=_cparams(("parallel",)),
        name="peer_select",
    )(scores_t)


def _gelu_exact(x):
    return 0.5 * x * (1.0 + lax.erf(x * (1.0 / math.sqrt(2.0))))


def _peer_dense_kernel(h_ref, u_ref, vt_ref, s_ref, iz_ref, cnt_ref, atop_ref, btop_ref, x1_ref, o_ref,
                       n_scr, e1_scr, rank_scr, e2_scr, a_scr, p_scr, yt_scr):
    jt = pl.program_id(1)
    te, tt = a_scr.shape
    nb = te // PEER_NKEYS
    n_grp = nb // SUBLANES
    PK = SUBLANES * (4 // jnp.dtype(GATE_DTYPE).itemsize)
    IBQ = 32 * PK // PEER_NKEYS

    @pl.when(jt == 0)
    def _():
        yt_scr[...] = jnp.zeros_like(yt_scr)
        grouped = (PEER_NKEYS // SUBLANES, SUBLANES, LANES)
        for p in range(PEER_HEADS):
            for st in range(tt // LANES):
                sl = pl.ds(st * LANES, LANES)
                s1 = s_ref[2 * p, :, sl]
                s2 = s_ref[2 * p + 1, :, sl]
                n = jnp.zeros_like(s1)
                rank = jnp.zeros_like(s2)
                for r in range(PEER_TOPK):
                    n = jnp.where(s1 == atop_ref[r, pl.ds(p, 1), sl], cnt_ref[r, pl.ds(p, 1), sl], n)
                    rank = jnp.where(btop_ref[r, pl.ds(p, 1), sl] > s2, r + 1.0, rank)
                n_scr[p, :, :, sl] = n.reshape(grouped)
                e1_scr[p, :, :, sl] = jnp.exp(s1 - atop_ref[0, pl.ds(p, 1), sl]).reshape(grouped)
                rank_scr[p, :, sl] = rank.astype(GATE_DTYPE)
                m2 = btop_ref[0, pl.ds(p, 1), sl]
                e2_scr[p, :, sl] = (jnp.exp(s2 - m2) * iz_ref[pl.ds(p, 1), sl]).astype(GATE_DTYPE)

    a_scr[...] = _dot_nt(u_ref[...], h_ref[...])
    zero = jnp.zeros((PK, LANES), GATE_DTYPE)
    for ibg in range(nb // IBQ):
        for st in range(tt // LANES):
            sl = pl.ds(st * LANES, LANES)
            gates = [[zero for _ in range(PEER_NKEYS // PK)] for _ in range(IBQ)]
            for p in range(PEER_HEADS):
                rank = rank_scr[p, :, sl]
                e2 = e2_scr[p, :, sl]
                for q in range(IBQ):
                    ib = ibg * IBQ + q
                    grp = jt * n_grp + ib // SUBLANES
                    sub = pl.ds(ib % SUBLANES, 1)
                    n = jnp.broadcast_to(n_scr[p, grp, sub, sl], (PK, LANES)).astype(GATE_DTYPE)
                    e1 = jnp.broadcast_to(e1_scr[p, grp, sub, sl], (PK, LANES)).astype(GATE_DTYPE)
                    for r in range(PEER_NKEYS // PK):
                        rr = slice(r * PK, (r + 1) * PK)
                        gates[q][r] = gates[q][r] + jnp.where(rank[rr, :] < n, e2[rr, :], zero) * e1
            for q in range(IBQ):
                rows = pl.ds((ibg * IBQ + q) * PEER_NKEYS, PEER_NKEYS)
                gate = jnp.concatenate(gates[q], axis=0)
                p_scr[rows, sl] = (_gelu_exact(a_scr[rows, sl]).astype(GATE_DTYPE) * gate).astype(BF16)
    yt_scr[...] += jnp.dot(vt_ref[...], p_scr[...], preferred_element_type=F32)

    @pl.when(jt == pl.num_programs(1) - 1)
    def _():
        o_ref[...] = x1_ref[...] + yt_scr[...].T


def _peer_dense(h2, u_bf, vt_bf, scores_t, iz, cnt, atop, btop, x1, tt, te):
    T, D = h2.shape
    E = u_bf.shape[0]
    npc, nk, _ = scores_t.shape
    assert te % (PEER_NKEYS * SUBLANES) == 0 and E % te == 0
    top = pl.BlockSpec((PEER_TOPK, PEER_HEADS, tt), lambda t, j: (0, 0, t))
    return pl.pallas_call(
        _peer_dense_kernel,
        grid=(T // tt, E // te),
        in_specs=[
            pl.BlockSpec((tt, D), lambda t, j: (t, 0)),
            pl.BlockSpec((te, D), lambda t, j: (j, 0)),
            pl.BlockSpec((D, te), lambda t, j: (0, j)),
            pl.BlockSpec((npc, nk, tt), lambda t, j: (0, 0, t)),
            pl.BlockSpec((PEER_HEADS, tt), lambda t, j: (0, t)),
            top, top, top,
            pl.BlockSpec((tt, D), lambda t, j: (t, 0)),
        ],
        out_specs=pl.BlockSpec((tt, D), lambda t, j: (t, 0)),
        out_shape=jax.ShapeDtypeStruct((T, D), F32),
        scratch_shapes=[
            pltpu.VMEM((PEER_HEADS, nk // SUBLANES, SUBLANES, tt), F32),
            pltpu.VMEM((PEER_HEADS, nk // SUBLANES, SUBLANES, tt), F32),
            pltpu.VMEM((PEER_HEADS, nk, tt), GATE_DTYPE),
            pltpu.VMEM((PEER_HEADS, nk, tt), GATE_DTYPE),
            pltpu.VMEM((te, tt), F32),
            pltpu.VMEM((te, tt), BF16),
            pltpu.VMEM((D, tt), F32),
        ],
        compiler_params=_cparams(("parallel", "arbitrary")),
        name="peer_dense",
    )(h2, u_bf, vt_bf, scores_t, iz, cnt, atop, btop, x1)


def _pick(n, pref):
    t = min(n, pref)
    assert n % t == 0, (n, t)
    return t


def _layer(x2, pos, B, S, attn_norm, w_in, lb_logits, hg_o_norm, q_a_norm, w_q_up, kv_a_norm,
           w_kv_up, q_norm, k_norm, mla_o_norm, w_out, ffn_norm, peer_w_q, peer_sub_keys, peer_u, peer_v):
    T, D = x2.shape
    n_hg = 5 * HG_HEADS * HG_DK
    row = lambda a: a.reshape(1, -1).astype(F32)

    w_hg = w_in[:, :n_hg].astype(BF16)
    n_mla = w_in.shape[1] - n_hg
    w_mla = jnp.pad(w_in[:, n_hg:], ((0, 0), (0, -n_mla % LANES))).astype(BF16)
    pad_r = LANES - MLA_ROPE
    wq = w_q_up.reshape(MLA_Q_LORA, MLA_HEADS, MLA_QK)
    wq = jnp.pad(wq, ((0, 0), (0, 0), (0, MLA_QPAD - MLA_QK))).reshape(MLA_Q_LORA, MLA_HEADS * MLA_QPAD)
    wq = wq.astype(BF16)
    wkv = w_kv_up.astype(BF16)
    qgn, qgr = row(q_norm[:MLA_NOPE]), row(jnp.pad(q_norm[MLA_NOPE:], (0, pad_r)))
    kgn, kgr = row(k_norm[:MLA_NOPE]), row(jnp.pad(k_norm[MLA_NOPE:], (0, pad_r)))
    inv_freq = 1.0 / (ROPE_THETA ** (jnp.arange(0, MLA_ROPE, 2, dtype=F32) / MLA_ROPE))
    keys = peer_sub_keys.reshape(2 * PEER_HEADS, PEER_NKEYS, PEER_DHALF).astype(BF16)
    u_bf = peer_u.astype(BF16)
    vt_bf = peer_v.astype(BF16).T

    hg, mla = _in_proj(x2, row(attn_norm), w_hg, w_mla, _pick(T, 256))
    tb = _pick(S, 512)
    o_f = _hgrn_scan(hg, lb_logits[:, 0, :], B, S, False, tb)
    o_b = _hgrn_scan(hg, lb_logits[:, 1, :], B, S, True, tb)
    cos, sin = _rope_tables(pos, inv_freq)
    q, k, v = _mla_prep(mla, cos, sin, row(q_a_norm), row(kv_a_norm), wq, wkv, qgn, qgr, kgn, kgr,
                        _pick(T, 256))
    y_mla = _attention(q, k, v, mla_o_norm.reshape(1, -1).astype(F32), B, S, _pick(S, 1024))
    x1, h2, scores_t = _out_proj(o_f, o_b, hg, y_mla, x2, row(hg_o_norm), w_out.astype(BF16),
                                 row(ffn_norm), peer_w_q.astype(BF16), keys, _pick(T, 256))
    iz, cnt, atop, btop = _peer_select(scores_t, _pick(T, 256))
    return _peer_dense(h2, u_bf, vt_bf, scores_t, iz, cnt, atop, btop, x1, _pick(T, 512), 1024)


def kernel(x, positions, attn_norm, w_in, hg_lb_logits, hg_o_norm, q_a_norm, w_q_up, kv_a_norm, w_kv_up,
           q_norm, k_norm, mla_o_norm, w_out, ffn_norm, peer_w_q, peer_sub_keys, peer_u, peer_v):
    B, S, D = x.shape
    x2 = x.reshape(B * S, D)
    pos = positions.reshape(B * S)
    for l in range(attn_norm.shape[0]):
        assert l == 0
        x2 = _layer(x2, pos, B, S, attn_norm[l], w_in[l], hg_lb_logits, hg_o_norm[l], q_a_norm[l],
                    w_q_up[l], kv_a_norm[l], w_kv_up[l], q_norm[l], k_norm[l], mla_o_norm[l], w_out[l],
                    ffn_norm[l], peer_w_q[l], peer_sub_keys[l], peer_u[l], peer_v[l])
    return x2.reshape(B, S, D)
```

```python
import functools
import math

import jax
import jax.numpy as jnp
from jax import lax
from jax.experimental import pallas as pl
from jax.experimental.pallas import tpu as pltpu

F32 = jnp.float32
BF16 = jnp.bfloat16
EPS = 1e-6
LANES = 128
SUBLANES = 8
NEG_INF = float("-inf")

HG_HEADS = 4
HG_DK = 128
HG_CHUNK = 64
HG_HALF = HG_CHUNK // 2
HG_QUARTER = HG_CHUNK // 4
HG_BATCH_PER_STEP = 4
MLA_HEADS = 4
MLA_Q_LORA = 384
MLA_KV_LORA = 256
MLA_NOPE = 128
MLA_ROPE = 64
MLA_V = 128
MLA_QK = MLA_NOPE + MLA_ROPE
MLA_QPAD = 2 * LANES
ROPE_THETA = 10000.0
ATTN_CHUNKS = 4
PEER_HEADS = 8
PEER_NKEYS = 128
PEER_TOPK = 16
PEER_DHALF = 128
GATE_DTYPE = BF16

VMEM_LIMIT = 56 * 1024 * 1024


def _cparams(sem):
    return pltpu.CompilerParams(dimension_semantics=sem, vmem_limit_bytes=VMEM_LIMIT)


def _rms(x, gain):
    ms = jnp.mean(x * x, axis=-1, keepdims=True)
    return x * lax.rsqrt(ms + EPS) * gain


def _sigmoid(x):
    return 1.0 / (1.0 + jnp.exp(-x))


def _dot_nt(a, b):
    return lax.dot_general(a, b, (((1,), (1,)), ((), ())), preferred_element_type=F32)


def _dot_tn(a, b):
    return lax.dot_general(a, b, (((0,), (0,)), ((), ())), preferred_element_type=F32)


def _in_proj_kernel(x_ref, g_ref, whg_ref, wmla_ref, hg_ref, mla_ref):
    h = _rms(x_ref[...], g_ref[...]).astype(BF16)
    hg_ref[...] = jnp.dot(h, whg_ref[...], preferred_element_type=F32)
    mla_ref[...] = jnp.dot(h, wmla_ref[...], preferred_element_type=F32)


def _in_proj(x2, gain, w_hg, w_mla, tm):
    T, D = x2.shape
    n_hg, n_mla = w_hg.shape[1], w_mla.shape[1]
    return pl.pallas_call(
        _in_proj_kernel,
        grid=(T // tm,),
        in_specs=[
            pl.BlockSpec((tm, D), lambda i: (i, 0)),
            pl.BlockSpec((1, D), lambda i: (0, 0)),
            pl.BlockSpec((D, n_hg), lambda i: (0, 0)),
            pl.BlockSpec((D, n_mla), lambda i: (0, 0)),
        ],
        out_specs=[
            pl.BlockSpec((tm, n_hg), lambda i: (i, 0)),
            pl.BlockSpec((tm, n_mla), lambda i: (i, 0)),
        ],
        out_shape=[jax.ShapeDtypeStruct((T, n_hg), F32), jax.ShapeDtypeStruct((T, n_mla), F32)],
        compiler_params=_cparams(("parallel",)),
        name="in_proj",
    )(x2, gain, w_hg, w_mla)


def _hgrn_kernel(q_ref, f_ref, v_ref, lb_ref, o_ref, st_ref, b_scr, k_scr, p_scr, *, reverse, n_chunks):
    C, Hf, Qt, G = HG_CHUNK, HG_HALF, HG_QUARTER, SUBLANES
    n_batch = q_ref.shape[0]
    n_heads = st_ref.shape[0] // n_batch
    chains = [(bi, hh) for bi in range(n_batch) for hh in range(n_heads)]

    @pl.when(pl.program_id(1) == 0)
    def _():
        st_ref[...] = jnp.zeros_like(st_ref)

    logits = lb_ref[...]
    ex = jnp.exp(logits - jnp.max(logits, axis=0, keepdims=True))
    lb_all = ex[0:1, :] / jnp.sum(ex, axis=0, keepdims=True)

    row = lax.broadcasted_iota(jnp.int32, (C, C), 0)
    col = lax.broadcasted_iota(jnp.int32, (C, C), 1)
    tri = (col >= row) if reverse else (col <= row)
    cum_mat = tri.astype(F32)
    same_half = (row >= Hf) == (col >= Hf)
    row_c = lax.broadcasted_iota(jnp.int32, (C, LANES), 0)
    q_side = (row_c < Hf) if reverse else (row_c >= Hf)
    odd_quarter = (row_c // Qt) % 2 == 1
    q_side4 = jnp.logical_not(odd_quarter) if reverse else odd_quarter
    low_half = row_c < Hf
    lane_g = lax.broadcasted_iota(jnp.int32, (G, LANES), 1)
    ones = jnp.ones((LANES, LANES), BF16)
    mid_row = Hf if reverse else Hf - 1
    qmid_rows = (Qt, Hf + Qt) if reverse else (Qt - 1, Hf + Qt - 1)
    end_row = 0 if reverse else C - 1

    def live_groups(s):
        return range(0, s // G + 1) if reverse else range(s // G, Qt // G)

    def gates(ci, r0):
        bi, hh = chains[ci]
        cols = slice(hh * HG_DK, (hh + 1) * HG_DK)
        lb = lb_all[:, cols]
        q_raw = q_ref[bi, pl.ds(r0, C), cols]
        q = q_raw * _sigmoid(q_raw)
        f = lb + (1.0 - lb) * _sigmoid(f_ref[bi, pl.ds(r0, C), cols])
        k = 1.0 - f
        vb = v_ref[bi, pl.ds(r0, C), cols].astype(BF16)
        b = jnp.dot(cum_mat, jnp.log2(f), precision=lax.Precision.HIGHEST,
                    preferred_element_type=F32)
        b_scr[ci] = b
        k_scr[ci] = k
        return q, k, vb, b

    def state_and_offdiag(ci, q, k, vb, b):
        b_mid = b_scr[ci, pl.ds(mid_row, 1), :]
        b_end = b_scr[ci, pl.ds(end_row, 1), :]
        st = st_ref[ci]
        o = _dot_nt((q * jnp.exp2(b)).astype(BF16), st.astype(BF16))
        kd = (k * jnp.exp2(b_end - b)).astype(BF16)
        st_ref[ci] = st * jnp.exp2(b_end) + _dot_tn(vb, kd)
        qt = jnp.where(q_side, q * jnp.exp2(b - b_mid), 0.0).astype(BF16)
        kt = jnp.where(q_side, 0.0, k * jnp.exp2(b_mid - b)).astype(BF16)
        b_q = jnp.where(low_half, b_scr[ci, pl.ds(qmid_rows[0], 1), :], b_scr[ci, pl.ds(qmid_rows[1], 1), :])
        qt4 = jnp.where(q_side4, q * jnp.exp2(b - b_q), 0.0).astype(BF16)
        kt4 = jnp.where(q_side4, 0.0, k * jnp.exp2(b_q - b)).astype(BF16)
        return o, _dot_nt(qt, kt), _dot_nt(qt4, kt4)

    def diag_products(ci, q, b):
        n_rows = 0
        for qb in range(C // Qt):
            base = qb * Qt
            for s in range(Qt):
                g0, g1 = live_groups(s)[0], live_groups(s)[-1] + 1
                rows = slice(base + g0 * G, base + g1 * G)
                bs = b_scr[ci, pl.ds(base + s, 1), :]
                ks = k_scr[ci, pl.ds(base + s, 1), :]
                p_scr[ci, pl.ds(n_rows, (g1 - g0) * G), :] = (q[rows, :] * ks) * jnp.exp2(b[rows, :] - bs)
                n_rows += (g1 - g0) * G

    def diag_scores(rs):
        n_rows = 0
        groups = []
        for qb in range(C // Qt):
            base = qb * Qt
            acc = [jnp.zeros((G, LANES), F32) for _ in range(Qt // G)]
            for s in range(Qt):
                for g in live_groups(s):
                    acc[g] = jnp.where(lane_g == base + s, rs[n_rows:n_rows + G, :], acc[g])
                    n_rows += G
            groups.extend(acc)
        return jnp.concatenate(groups, axis=0)[:, :C]

    def chunk_body(ci, carry):
        c = (n_chunks - 1 - ci) if reverse else ci
        r0 = pl.multiple_of(c * C, C)
        ids = range(len(chains))
        qkvb = [gates(ci, r0) for ci in ids]
        o_sc = [state_and_offdiag(ci, *qkvb[ci]) for ci in ids]
        for ci in ids:
            diag_products(ci, qkvb[ci][0], qkvb[ci][3])
        rs = [jnp.dot(p_scr[ci].astype(BF16), ones, preferred_element_type=F32) for ci in ids]
        for ci in ids:
            bi, hh = chains[ci]
            o, sc_half, sc_quarter = o_sc[ci]
            scores = jnp.where(same_half, jnp.where(tri, diag_scores(rs[ci]) + sc_quarter, 0.0), sc_half)
            o = o + jnp.dot(scores.astype(BF16), qkvb[ci][2], preferred_element_type=F32)
            o_ref[bi, pl.ds(r0, C), hh * HG_DK:(hh + 1) * HG_DK] = o
        return carry

    lax.fori_loop(0, n_chunks, chunk_body, 0)


def _hgrn_diag_rows():
    return (HG_CHUNK // HG_QUARTER) * sum(HG_QUARTER - (s // SUBLANES) * SUBLANES for s in range(HG_QUARTER))


def _hgrn_scan(hg, lb_logits_dir, B, S, reverse, tb):
    T = hg.shape[0]
    W = HG_HEADS * HG_DK
    nblk = S // tb
    f_blk = 2 if reverse else 1
    v_blk = 3
    nbat = HG_BATCH_PER_STEP if B % HG_BATCH_PER_STEP == 0 else 1
    n_chains = nbat * HG_HEADS
    hg4 = hg.reshape(B // nbat, nbat, S, hg.shape[1])

    def spec(col_blk):
        return pl.BlockSpec((None, nbat, tb, W),
                            lambda g, i: (g, 0, (nblk - 1 - i) if reverse else i, col_blk))

    kern = functools.partial(_hgrn_kernel, reverse=reverse, n_chunks=tb // HG_CHUNK)
    out = pl.pallas_call(
        kern,
        grid=(B // nbat, nblk),
        in_specs=[spec(0), spec(f_blk), spec(v_blk),
                  pl.BlockSpec((lb_logits_dir.shape[0], W), lambda g, i: (0, 0))],
        out_specs=spec(0),
        out_shape=jax.ShapeDtypeStruct((B // nbat, nbat, S, W), F32),
        scratch_shapes=[
            pltpu.VMEM((n_chains, HG_DK, HG_DK), F32),
            pltpu.VMEM((n_chains, HG_CHUNK, HG_DK), F32),
            pltpu.VMEM((n_chains, HG_CHUNK, HG_DK), F32),
            pltpu.VMEM((n_chains, _hgrn_diag_rows(), HG_DK), F32),
        ],
        compiler_params=_cparams(("parallel", "arbitrary")),
        name="hgrn_rev" if reverse else "hgrn_fwd",
    )(hg4, hg4, hg4, lb_logits_dir)
    return out.reshape(T, W)


def _rope_table_kernel(invf_ref, pos_ref, cos_ref, sin_ref):
    pos = pos_ref[...].astype(F32)
    for f in range(cos_ref.shape[0]):
        ang = pos * invf_ref[f]
        cos_ref[f] = jnp.cos(ang)
        sin_ref[f] = jnp.sin(ang)


def _rope_tables(pos, inv_freq):
    T = pos.shape[0]
    nf = inv_freq.shape[0]
    rows = T // LANES
    tr = _pick(rows, SUBLANES)
    plane = jax.ShapeDtypeStruct((nf, rows, LANES), F32)
    cos, sin = pl.pallas_call(
        _rope_table_kernel,
        grid=(rows // tr,),
        in_specs=[pl.BlockSpec(memory_space=pltpu.SMEM), pl.BlockSpec((tr, LANES), lambda i: (i, 0))],
        out_specs=[pl.BlockSpec((nf, tr, LANES), lambda i: (0, i, 0))] * 2,
        out_shape=[plane, plane],
        compiler_params=_cparams(("parallel",)),
        name="rope_tables",
    )(inv_freq, pos.reshape(rows, LANES))

    def per_token(t):
        t = t.reshape(nf, T).T
        return jnp.concatenate([t, t, jnp.zeros((T, LANES - 2 * nf), F32)], axis=1)

    return per_token(cos), per_token(sin)


def _mla_prep_kernel(mla_ref, cos_ref, sin_ref, qa_ref, kva_ref, wq_ref, wkv_ref,
                     qgn_ref, qgr_ref, kgn_ref, kgr_ref, q_ref, k_ref, v_ref):
    tm = mla_ref.shape[0]
    lane = lax.broadcasted_iota(jnp.int32, (tm, LANES), 1)
    cos_t = cos_ref[...]
    sin_t = jnp.where(lane < MLA_ROPE // 2, -sin_ref[...], sin_ref[...])

    def rope(t):
        swapped = jnp.where(lane < MLA_ROPE // 2,
                            pltpu.roll(t, LANES - MLA_ROPE // 2, 1),
                            pltpu.roll(t, MLA_ROPE // 2, 1))
        return t * cos_t + swapped * sin_t

    def norm_rope_part(t, gain):
        ms = jnp.sum(t * t, axis=-1, keepdims=True) * (1.0 / MLA_ROPE)
        return t * lax.rsqrt(ms + EPS) * gain

    cq = mla_ref[:, 0:MLA_Q_LORA]
    ckv = mla_ref[:, MLA_Q_LORA:MLA_Q_LORA + MLA_KV_LORA]
    kr = mla_ref[:, MLA_Q_LORA + MLA_KV_LORA:]
    q_up = jnp.dot(_rms(cq, qa_ref[...]).astype(BF16), wq_ref[...], preferred_element_type=F32)
    kv_up = jnp.dot(_rms(ckv, kva_ref[...]).astype(BF16), wkv_ref[...], preferred_element_type=F32)
    k_rope = rope(norm_rope_part(kr, kgr_ref[...])).astype(BF16)
    for h in range(MLA_HEADS):
        c0 = h * MLA_QPAD
        q_ref[:, c0:c0 + LANES] = _rms(q_up[:, c0:c0 + LANES], qgn_ref[...]).astype(BF16)
        q_ref[:, c0 + LANES:c0 + 2 * LANES] = rope(
            norm_rope_part(q_up[:, c0 + LANES:c0 + 2 * LANES], qgr_ref[...])).astype(BF16)
        k_ref[:, c0:c0 + LANES] = _rms(kv_up[:, c0:c0 + LANES], kgn_ref[...]).astype(BF16)
        k_ref[:, c0 + LANES:c0 + 2 * LANES] = k_rope
        v_ref[:, h * MLA_V:(h + 1) * MLA_V] = kv_up[:, c0 + LANES:c0 + 2 * LANES].astype(BF16)


def _mla_prep(mla, cos, sin, qa, kva, wq, wkv, qgn, qgr, kgn, kgr, tm):
    T, n_mla = mla.shape
    full = lambda a: pl.BlockSpec(a.shape, lambda i: (0, 0))
    return pl.pallas_call(
        _mla_prep_kernel,
        grid=(T // tm,),
        in_specs=[
            pl.BlockSpec((tm, n_mla), lambda i: (i, 0)),
            pl.BlockSpec((tm, LANES), lambda i: (i, 0)),
            pl.BlockSpec((tm, LANES), lambda i: (i, 0)),
            full(qa), full(kva), full(wq), full(wkv),
            full(qgn), full(qgr), full(kgn), full(kgr),
        ],
        out_specs=[
            pl.BlockSpec((tm, MLA_HEADS * MLA_QPAD), lambda i: (i, 0)),
            pl.BlockSpec((tm, MLA_HEADS * MLA_QPAD), lambda i: (i, 0)),
            pl.BlockSpec((tm, MLA_HEADS * MLA_V), lambda i: (i, 0)),
        ],
        out_shape=[
            jax.ShapeDtypeStruct((T, MLA_HEADS * MLA_QPAD), BF16),
            jax.ShapeDtypeStruct((T, MLA_HEADS * MLA_QPAD), BF16),
            jax.ShapeDtypeStruct((T, MLA_HEADS * MLA_V), BF16),
        ],
        compiler_params=_cparams(("parallel",)),
        name="mla_prep",
    )(mla, cos, sin, qa, kva, wq, wkv, qgn, qgr, kgn, kgr)


def _attn_kernel(q_ref, k_ref, v_ref, g_ref, o_ref):
    c = MLA_QK ** -0.5 * math.log2(math.e)
    tq = q_ref.shape[0]
    rc = tq // ATTN_CHUNKS
    rows = [pl.ds(i * rc, rc) for i in range(ATTN_CHUNKS)]
    s = [_dot_nt(q_ref[r, :], k_ref[...]) for r in rows]
    p = [jnp.exp2((si - jnp.max(si, axis=-1, keepdims=True)) * c) for si in s]
    l = [jnp.sum(pi, axis=-1, keepdims=True) for pi in p]
    o = [jnp.dot(pi.astype(BF16), v_ref[...], preferred_element_type=F32) for pi in p]
    for r, oi, li in zip(rows, o, l):
        o_ref[r, :] = _rms(oi / li, g_ref[...]).astype(o_ref.dtype)


def _attention(q, k, v, o_gain, B, S, tq):
    T = q.shape[0]
    nq = S // tq
    return pl.pallas_call(
        _attn_kernel,
        grid=(B, MLA_HEADS, nq),
        in_specs=[
            pl.BlockSpec((tq, MLA_QPAD), lambda b, h, i: (b * nq + i, h)),
            pl.BlockSpec((S, MLA_QPAD), lambda b, h, i: (b, h)),
            pl.BlockSpec((S, MLA_V), lambda b, h, i: (b, h)),
            pl.BlockSpec((1, MLA_V), lambda b, h, i: (0, h)),
        ],
        out_specs=pl.BlockSpec((tq, MLA_V), lambda b, h, i: (b * nq + i, h)),
        out_shape=jax.ShapeDtypeStruct((T, MLA_HEADS * MLA_V), BF16),
        compiler_params=_cparams(("parallel", "parallel", "arbitrary")),
        name="mla_attention",
    )(q, k, v, o_gain)


def _out_proj_kernel(of_ref, ob_ref, g_ref, ym_ref, x_ref, hgg_ref, wo_ref, fg_ref, wq_ref, keys_ref,
                     x1_ref, h2_ref, st_ref):
    o = of_ref[...] + ob_ref[...]
    gate_raw = g_ref[...]
    gate = gate_raw * _sigmoid(gate_raw)
    parts = []
    for h in range(HG_HEADS):
        sl = slice(h * HG_DK, (h + 1) * HG_DK)
        parts.append(_rms(o[:, sl], hgg_ref[:, sl]) * gate[:, sl])
    y_hg = jnp.concatenate(parts, axis=-1).astype(BF16)
    w_hg_rows = y_hg.shape[1]
    mix = jnp.dot(y_hg, wo_ref[0:w_hg_rows, :], preferred_element_type=F32)
    mix = mix + jnp.dot(ym_ref[...], wo_ref[w_hg_rows:, :], preferred_element_type=F32)
    x1 = x_ref[...] + mix
    x1_ref[...] = x1
    h2 = _rms(x1, fg_ref[...]).astype(BF16)
    h2_ref[...] = h2
    pq = jnp.dot(h2, wq_ref[...], preferred_element_type=F32).astype(BF16)
    for pc in range(keys_ref.shape[0]):
        st_ref[pc] = _dot_nt(keys_ref[pc], pq[:, pc * PEER_DHALF:(pc + 1) * PEER_DHALF])


def _out_proj(o_f, o_b, hg, y_mla, x2, hg_gain, w_out, ffn_gain, w_pq, keys, tm):
    T, D = x2.shape
    W = o_f.shape[1]
    npc = keys.shape[0]
    full2 = lambda a: pl.BlockSpec(a.shape, lambda i: (0, 0))
    return pl.pallas_call(
        _out_proj_kernel,
        grid=(T // tm,),
        in_specs=[
            pl.BlockSpec((tm, W), lambda i: (i, 0)),
            pl.BlockSpec((tm, W), lambda i: (i, 0)),
            pl.BlockSpec((tm, W), lambda i: (i, 4)),
            pl.BlockSpec((tm, y_mla.shape[1]), lambda i: (i, 0)),
            pl.BlockSpec((tm, D), lambda i: (i, 0)),
            full2(hg_gain), full2(w_out), full2(ffn_gain), full2(w_pq),
            pl.BlockSpec(keys.shape, lambda i: (0, 0, 0)),
        ],
        out_specs=[
            pl.BlockSpec((tm, D), lambda i: (i, 0)),
            pl.BlockSpec((tm, D), lambda i: (i, 0)),
            pl.BlockSpec((npc, PEER_NKEYS, tm), lambda i: (0, 0, i)),
        ],
        out_shape=[
            jax.ShapeDtypeStruct((T, D), F32),
            jax.ShapeDtypeStruct((T, D), BF16),
            jax.ShapeDtypeStruct((npc, PEER_NKEYS, T), F32),
        ],
        compiler_params=_cparams(("parallel",)),
        name="out_proj_peer_query",
    )(o_f, o_b, hg, y_mla, x2, hg_gain, w_out, ffn_gain, w_pq, keys)


def _tree_max(xs):
    xs = list(xs)
    while len(xs) > 1:
        nxt = [jnp.maximum(xs[i], xs[i + 1]) for i in range(0, len(xs) - 1, 2)]
        if len(xs) % 2:
            nxt.append(xs[-1])
        xs = nxt
    return xs[0]


def _sorting_network(n):
    pairs = []
    p = 1
    while p < n:
        k = p
        while k >= 1:
            for j in range(k % p, n - k, 2 * k):
                for i in range(min(k, n - j - k)):
                    if (i + j) // (2 * p) == (i + j + k) // (2 * p):
                        pairs.append((i + j, i + j + k))
            k //= 2
        p *= 2
    return pairs


def _bitonic_merge_network(n):
    pairs = []
    stride = n // 2
    while stride >= 1:
        pairs += [(i, i + stride) for i in range(n) if not i & stride]
        stride //= 2
    return pairs


def _compare_exchange(xs, pairs):
    xs = list(xs)
    for i, j in pairs:
        xs[i], xs[j] = jnp.maximum(xs[i], xs[j]), jnp.minimum(xs[i], xs[j])
    return xs


def _peer_select_kernel(s_ref, iz_ref, cnt_ref, atop_ref, btop_ref, top_scr):
    K = PEER_TOPK
    n_vregs = PEER_NKEYS // SUBLANES
    assert n_vregs == K
    sort_pairs = _sorting_network(K)
    merge_pairs = _bitonic_merge_network(K)
    for p in range(PEER_HEADS):
        for c in range(2):
            lists = _compare_exchange(
                [s_ref[2 * p + c, pl.ds(g * SUBLANES, SUBLANES), :] for g in range(n_vregs)], sort_pairs)
            shift = SUBLANES // 2
            while shift >= 1:
                other = [pltpu.roll(x, shift, 0) for x in lists]
                lists = _compare_exchange(
                    [jnp.maximum(lists[r], other[K - 1 - r]) for r in range(K)], merge_pairs)
                shift //= 2
            for r in range(K):
                top_scr[c, r, pl.ds(p, 1), :] = lists[r][0:1, :]
    a = [top_scr[0, r] for r in range(K)]
    b = [top_scr[1, r] for r in range(K)]
    pairs = [(i, j) for i in range(K) for j in range(K) if (i + 1) * (j + 1) <= K]
    cands = [a[i] + b[j] for i, j in pairs]
    top = a[0] + b[0]
    z = jnp.zeros_like(top)
    kth = top
    for r in range(K):
        kth = _tree_max(cands)
        z = z + jnp.exp(kth - top)
        cands = [jnp.where(cv == kth, NEG_INF, cv) for cv in cands]
    iz_ref[...] = 1.0 / z
    for i in range(K):
        cnt = jnp.zeros_like(top)
        for j in range(K):
            if (i, j) in pairs:
                cnt = cnt + jnp.where(a[i] + b[j] >= kth, 1.0, 0.0)
        cnt_ref[i] = cnt
        atop_ref[i] = a[i]
        btop_ref[i] = b[i]


def _peer_select(scores_t, tt):
    npc, nk, T = scores_t.shape
    top = jax.ShapeDtypeStruct((PEER_TOPK, PEER_HEADS, T), F32)
    top_spec = pl.BlockSpec((PEER_TOPK, PEER_HEADS, tt), lambda i: (0, 0, i))
    return pl.pallas_call(
        _peer_select_kernel,
        grid=(T // tt,),
        in_specs=[pl.BlockSpec((npc, nk, tt), lambda i: (0, 0, i))],
        out_specs=[pl.BlockSpec((PEER_HEADS, tt), lambda i: (0, i))] + [top_spec] * 3,
        out_shape=[jax.ShapeDtypeStruct((PEER_HEADS, T), F32)] + [top] * 3,
        scratch_shapes=[pltpu.VMEM((2, PEER_TOPK, PEER_HEADS, tt), F32)],
        compiler_params=_cparams(("parallel",)),
        name="peer_select",
    )(scores_t)


def _gelu_exact(x):
    return 0.5 * x * (1.0 + lax.erf(x * (1.0 / math.sqrt(2.0))))


def _peer_dense_kernel(h_ref, u_ref, vt_ref, s_ref, iz_ref, cnt_ref, atop_ref, btop_ref, x1_ref, o_ref,
                       n_scr, e1_scr, rank_scr, e2_scr, a_scr, p_scr, yt_scr):
    jt = pl.program_id(1)
    te, tt = a_scr.shape
    nb = te // PEER_NKEYS
    n_grp = nb // SUBLANES
    PK = SUBLANES * (4 // jnp.dtype(GATE_DTYPE).itemsize)
    IBQ = 32 * PK // PEER_NKEYS

    @pl.when(jt == 0)
    def _():
        yt_scr[...] = jnp.zeros_like(yt_scr)
        grouped = (PEER_NKEYS // SUBLANES, SUBLANES, LANES)
        for p in range(PEER_HEADS):
            for st in range(tt // LANES):
                sl = pl.ds(st * LANES, LANES)
                s1 = s_ref[2 * p, :, sl]
                s2 = s_ref[2 * p + 1, :, sl]
                n = jnp.zeros_like(s1)
                rank = jnp.zeros_like(s2)
                for r in range(PEER_TOPK):
                    n = jnp.where(s1 == atop_ref[r, pl.ds(p, 1), sl], cnt_ref[r, pl.ds(p, 1), sl], n)
                    rank = jnp.where(btop_ref[r, pl.ds(p, 1), sl] > s2, r + 1.0, rank)
                n_scr[p, :, :, sl] = n.reshape(grouped)
                e1_scr[p, :, :, sl] = jnp.exp(s1 - atop_ref[0, pl.ds(p, 1), sl]).reshape(grouped)
                rank_scr[p, :, sl] = rank.astype(GATE_DTYPE)
                m2 = btop_ref[0, pl.ds(p, 1), sl]
                e2_scr[p, :, sl] = (jnp.exp(s2 - m2) * iz_ref[pl.ds(p, 1), sl]).astype(GATE_DTYPE)

    a_scr[...] = _dot_nt(u_ref[...], h_ref[...])
    zero = jnp.zeros((PK, LANES), GATE_DTYPE)
    for ibg in range(nb // IBQ):
        for st in range(tt // LANES):
            sl = pl.ds(st * LANES, LANES)
            gates = [[zero for _ in range(PEER_NKEYS // PK)] for _ in range(IBQ)]
            for p in range(PEER_HEADS):
                rank = rank_scr[p, :, sl]
                e2 = e2_scr[p, :, sl]
                for q in range(IBQ):
                    ib = ibg * IBQ + q
                    grp = jt * n_grp + ib // SUBLANES
                    sub = pl.ds(ib % SUBLANES, 1)
                    n = jnp.broadcast_to(n_scr[p, grp, sub, sl], (PK, LANES)).astype(GATE_DTYPE)
                    e1 = jnp.broadcast_to(e1_scr[p, grp, sub, sl], (PK, LANES)).astype(GATE_DTYPE)
                    for r in range(PEER_NKEYS // PK):
                        rr = slice(r * PK, (r + 1) * PK)
                        gates[q][r] = gates[q][r] + jnp.where(rank[rr, :] < n, e2[rr, :], zero) * e1
            for q in range(IBQ):
                rows = pl.ds((ibg * IBQ + q) * PEER_NKEYS, PEER_NKEYS)
                gate = jnp.concatenate(gates[q], axis=0)
                p_scr[rows, sl] = (_gelu_exact(a_scr[rows, sl]).astype(GATE_DTYPE) * gate).astype(BF16)
    yt_scr[...] += jnp.dot(vt_ref[...], p_scr[...], preferred_element_type=F32)

    @pl.when(jt == pl.num_programs(1) - 1)
    def _():
        o_ref[...] = x1_ref[...] + yt_scr[...].T


def _peer_dense(h2, u_bf, vt_bf, scores_t, iz, cnt, atop, btop, x1, tt, te):
    T, D = h2.shape
    E = u_bf.shape[0]
    npc, nk, _ = scores_t.shape
    assert te % (PEER_NKEYS * SUBLANES) == 0 and E % te == 0
    top = pl.BlockSpec((PEER_TOPK, PEER_HEADS, tt), lambda t, j: (0, 0, t))
    return pl.pallas_call(
        _peer_dense_kernel,
        grid=(T // tt, E // te),
        in_specs=[
            pl.BlockSpec((tt, D), lambda t, j: (t, 0)),
            pl.BlockSpec((te, D), lambda t, j: (j, 0)),
            pl.BlockSpec((D, te), lambda t, j: (0, j)),
            pl.BlockSpec((npc, nk, tt), lambda t, j: (0, 0, t)),
            pl.BlockSpec((PEER_HEADS, tt), lambda t, j: (0, t)),
            top, top, top,
            pl.BlockSpec((tt, D), lambda t, j: (t, 0)),
        ],
        out_specs=pl.BlockSpec((tt, D), lambda t, j: (t, 0)),
        out_shape=jax.ShapeDtypeStruct((T, D), F32),
        scratch_shapes=[
            pltpu.VMEM((PEER_HEADS, nk // SUBLANES, SUBLANES, tt), F32),
            pltpu.VMEM((PEER_HEADS, nk // SUBLANES, SUBLANES, tt), F32),
            pltpu.VMEM((PEER_HEADS, nk, tt), GATE_DTYPE),
            pltpu.VMEM((PEER_HEADS, nk, tt), GATE_DTYPE),
            pltpu.VMEM((te, tt), F32),
            pltpu.VMEM((te, tt), BF16),
            pltpu.VMEM((D, tt), F32),
        ],
        compiler_params=_cparams(("parallel", "arbitrary")),
        name="peer_dense",
    )(h2, u_bf, vt_bf, scores_t, iz, cnt, atop, btop, x1)


def _pick(n, pref):
    t = min(n, pref)
    assert n % t == 0, (n, t)
    return t


def _layer(x2, pos, B, S, attn_norm, w_in, lb_logits, hg_o_norm, q_a_norm, w_q_up, kv_a_norm,
           w_kv_up, q_norm, k_norm, mla_o_norm, w_out, ffn_norm, peer_w_q, peer_sub_keys, peer_u, peer_v):
    T, D = x2.shape
    n_hg = 5 * HG_HEADS * HG_DK
    row = lambda a: a.reshape(1, -1).astype(F32)

    w_hg = w_in[:, :n_hg].astype(BF16)
    n_mla = w_in.shape[1] - n_hg
    w_mla = jnp.pad(w_in[:, n_hg:], ((0, 0), (0, -n_mla % LANES))).astype(BF16)
    pad_r = LANES - MLA_ROPE
    wq = w_q_up.reshape(MLA_Q_LORA, MLA_HEADS, MLA_QK)
    wq = jnp.pad(wq, ((0, 0), (0, 0), (0, MLA_QPAD - MLA_QK))).reshape(MLA_Q_LORA, MLA_HEADS * MLA_QPAD)
    wq = wq.astype(BF16)
    wkv = w_kv_up.astype(BF16)
    qgn, qgr = row(q_norm[:MLA_NOPE]), row(jnp.pad(q_norm[MLA_NOPE:], (0, pad_r)))
    kgn, kgr = row(k_norm[:MLA_NOPE]), row(jnp.pad(k_norm[MLA_NOPE:], (0, pad_r)))
    inv_freq = 1.0 / (ROPE_THETA ** (jnp.arange(0, MLA_ROPE, 2, dtype=F32) / MLA_ROPE))
    keys = peer_sub_keys.reshape(2 * PEER_HEADS, PEER_NKEYS, PEER_DHALF).astype(BF16)
    u_bf = peer_u.astype(BF16)
    vt_bf = peer_v.astype(BF16).T

    hg, mla = _in_proj(x2, row(attn_norm), w_hg, w_mla, _pick(T, 256))
    tb = _pick(S, 512)
    o_f = _hgrn_scan(hg, lb_logits[:, 0, :], B, S, False, tb)
    o_b = _hgrn_scan(hg, lb_logits[:, 1, :], B, S, True, tb)
    cos, sin = _rope_tables(pos, inv_freq)
    q, k, v = _mla_prep(mla, cos, sin, row(q_a_norm), row(kv_a_norm), wq, wkv, qgn, qgr, kgn, kgr,
                        _pick(T, 256))
    y_mla = _attention(q, k, v, mla_o_norm.reshape(1, -1).astype(F32), B, S, _pick(S, 1024))
    x1, h2, scores_t = _out_proj(o_f, o_b, hg, y_mla, x2, row(hg_o_norm), w_out.astype(BF16),
                                 row(ffn_norm), peer_w_q.astype(BF16), keys, _pick(T, 256))
    iz, cnt, atop, btop = _peer_select(scores_t, _pick(T, 256))
    return _peer_dense(h2, u_bf, vt_bf, scores_t, iz, cnt, atop, btop, x1, _pick(T, 512), 2048)


def kernel(x, positions, attn_norm, w_in, hg_lb_logits, hg_o_norm, q_a_norm, w_q_up, kv_a_norm, w_kv_up,
           q_norm, k_norm, mla_o_norm, w_out, ffn_norm, peer_w_q, peer_sub_keys, peer_u, peer_v):
    B, S, D = x.shape
    x2 = x.reshape(B * S, D)
    pos = positions.reshape(B * S)
    for l in range(attn_norm.shape[0]):
        assert l == 0
        x2 = _layer(x2, pos, B, S, attn_norm[l], w_in[l], hg_lb_logits, hg_o_norm[l], q_a_norm[l],
                    w_q_up[l], kv_a_norm[l], w_kv_up[l], q_norm[l], k_norm[l], mla_o_norm[l], w_out[l],
                    ffn_norm[l], peer_w_q[l], peer_sub_keys[l], peer_u[l], peer_v[l])
    return x2.reshape(B, S, D)
```

```python
import functools
import math

import jax
import jax.numpy as jnp
from jax import lax
from jax.experimental import pallas as pl
from jax.experimental.pallas import tpu as pltpu

F32 = jnp.float32
BF16 = jnp.bfloat16
EPS = 1e-6
LANES = 128
SUBLANES = 8
NEG_INF = float("-inf")

HG_HEADS = 4
HG_DK = 128
HG_CHUNK = 64
HG_HALF = HG_CHUNK // 2
HG_QUARTER = HG_CHUNK // 4
HG_BATCH_PER_STEP = 4
MLA_HEADS = 4
MLA_Q_LORA = 384
MLA_KV_LORA = 256
MLA_NOPE = 128
MLA_ROPE = 64
MLA_V = 128
MLA_QK = MLA_NOPE + MLA_ROPE
MLA_QPAD = 2 * LANES
ROPE_THETA = 10000.0
ATTN_CHUNKS = 4
OUT_CHUNKS = 2
PEER_HEADS = 8
PEER_NKEYS = 128
PEER_TOPK = 16
PEER_DHALF = 128
GATE_DTYPE = BF16

VMEM_LIMIT = 56 * 1024 * 1024


def _cparams(sem):
    return pltpu.CompilerParams(dimension_semantics=sem, vmem_limit_bytes=VMEM_LIMIT)


def _rms(x, gain):
    ms = jnp.mean(x * x, axis=-1, keepdims=True)
    return x * lax.rsqrt(ms + EPS) * gain


def _sigmoid(x):
    return 1.0 / (1.0 + jnp.exp(-x))


def _dot_nt(a, b):
    return lax.dot_general(a, b, (((1,), (1,)), ((), ())), preferred_element_type=F32)


def _dot_tn(a, b):
    return lax.dot_general(a, b, (((0,), (0,)), ((), ())), preferred_element_type=F32)


def _in_proj_kernel(x_ref, g_ref, whg_ref, wmla_ref, hg_ref, mla_ref):
    h = _rms(x_ref[...], g_ref[...]).astype(BF16)
    hg_ref[...] = jnp.dot(h, whg_ref[...], preferred_element_type=F32)
    mla_ref[...] = jnp.dot(h, wmla_ref[...], preferred_element_type=F32)


def _in_proj(x2, gain, w_hg, w_mla, tm):
    T, D = x2.shape
    n_hg, n_mla = w_hg.shape[1], w_mla.shape[1]
    return pl.pallas_call(
        _in_proj_kernel,
        grid=(T // tm,),
        in_specs=[
            pl.BlockSpec((tm, D), lambda i: (i, 0)),
            pl.BlockSpec((1, D), lambda i: (0, 0)),
            pl.BlockSpec((D, n_hg), lambda i: (0, 0)),
            pl.BlockSpec((D, n_mla), lambda i: (0, 0)),
        ],
        out_specs=[
            pl.BlockSpec((tm, n_hg), lambda i: (i, 0)),
            pl.BlockSpec((tm, n_mla), lambda i: (i, 0)),
        ],
        out_shape=[jax.ShapeDtypeStruct((T, n_hg), F32), jax.ShapeDtypeStruct((T, n_mla), F32)],
        compiler_params=_cparams(("parallel",)),
        name="in_proj",
    )(x2, gain, w_hg, w_mla)


def _hgrn_kernel(q_ref, f_ref, v_ref, lb_ref, o_ref, st_ref, b_scr, k_scr, p_scr, *, reverse, n_chunks):
    C, Hf, Qt, G = HG_CHUNK, HG_HALF, HG_QUARTER, SUBLANES
    n_batch = q_ref.shape[0]
    n_heads = st_ref.shape[0] // n_batch
    chains = [(bi, hh) for bi in range(n_batch) for hh in range(n_heads)]

    @pl.when(pl.program_id(1) == 0)
    def _():
        st_ref[...] = jnp.zeros_like(st_ref)

    logits = lb_ref[...]
    ex = jnp.exp(logits - jnp.max(logits, axis=0, keepdims=True))
    lb_all = ex[0:1, :] / jnp.sum(ex, axis=0, keepdims=True)

    row = lax.broadcasted_iota(jnp.int32, (C, C), 0)
    col = lax.broadcasted_iota(jnp.int32, (C, C), 1)
    tri = (col >= row) if reverse else (col <= row)
    cum_mat = tri.astype(F32)
    same_half = (row >= Hf) == (col >= Hf)
    row_c = lax.broadcasted_iota(jnp.int32, (C, LANES), 0)
    q_side = (row_c < Hf) if reverse else (row_c >= Hf)
    odd_quarter = (row_c // Qt) % 2 == 1
    q_side4 = jnp.logical_not(odd_quarter) if reverse else odd_quarter
    low_half = row_c < Hf
    lane_g = lax.broadcasted_iota(jnp.int32, (G, LANES), 1)
    ones = jnp.ones((LANES, LANES), BF16)
    mid_row = Hf if reverse else Hf - 1
    qmid_rows = (Qt, Hf + Qt) if reverse else (Qt - 1, Hf + Qt - 1)
    end_row = 0 if reverse else C - 1

    def live_groups(s):
        return range(0, s // G + 1) if reverse else range(s // G, Qt // G)

    def gates(ci, r0):
        bi, hh = chains[ci]
        cols = slice(hh * HG_DK, (hh + 1) * HG_DK)
        lb = lb_all[:, cols]
        q_raw = q_ref[bi, pl.ds(r0, C), cols]
        q = q_raw * _sigmoid(q_raw)
        f = lb + (1.0 - lb) * _sigmoid(f_ref[bi, pl.ds(r0, C), cols])
        k = 1.0 - f
        vb = v_ref[bi, pl.ds(r0, C), cols].astype(BF16)
        b = jnp.dot(cum_mat, jnp.log2(f), precision=lax.Precision.HIGHEST,
                    preferred_element_type=F32)
        b_scr[ci] = b
        k_scr[ci] = k
        return q, k, vb, b

    def state_and_offdiag(ci, q, k, vb, b):
        b_mid = b_scr[ci, pl.ds(mid_row, 1), :]
        b_end = b_scr[ci, pl.ds(end_row, 1), :]
        st = st_ref[ci]
        o = _dot_nt((q * jnp.exp2(b)).astype(BF16), st.astype(BF16))
        kd = (k * jnp.exp2(b_end - b)).astype(BF16)
        st_ref[ci] = st * jnp.exp2(b_end) + _dot_tn(vb, kd)
        qt = jnp.where(q_side, q * jnp.exp2(b - b_mid), 0.0).astype(BF16)
        kt = jnp.where(q_side, 0.0, k * jnp.exp2(b_mid - b)).astype(BF16)
        b_q = jnp.where(low_half, b_scr[ci, pl.ds(qmid_rows[0], 1), :], b_scr[ci, pl.ds(qmid_rows[1], 1), :])
        qt4 = jnp.where(q_side4, q * jnp.exp2(b - b_q), 0.0).astype(BF16)
        kt4 = jnp.where(q_side4, 0.0, k * jnp.exp2(b_q - b)).astype(BF16)
        return o, _dot_nt(qt, kt), _dot_nt(qt4, kt4)

    def diag_products(ci, q, b):
        n_rows = 0
        for qb in range(C // Qt):
            base = qb * Qt
            for s in range(Qt):
                g0, g1 = live_groups(s)[0], live_groups(s)[-1] + 1
                rows = slice(base + g0 * G, base + g1 * G)
                bs = b_scr[ci, pl.ds(base + s, 1), :]
                ks = k_scr[ci, pl.ds(base + s, 1), :]
                p_scr[ci, pl.ds(n_rows, (g1 - g0) * G), :] = (q[rows, :] * ks) * jnp.exp2(b[rows, :] - bs)
                n_rows += (g1 - g0) * G

    def diag_scores(rs):
        n_rows = 0
        groups = []
        for qb in range(C // Qt):
            base = qb * Qt
            acc = [jnp.zeros((G, LANES), F32) for _ in range(Qt // G)]
            for s in range(Qt):
                for g in live_groups(s):
                    acc[g] = jnp.where(lane_g == base + s, rs[n_rows:n_rows + G, :], acc[g])
                    n_rows += G
            groups.extend(acc)
        return jnp.concatenate(groups, axis=0)[:, :C]

    def chunk_body(ci, carry):
        c = (n_chunks - 1 - ci) if reverse else ci
        r0 = pl.multiple_of(c * C, C)
        ids = range(len(chains))
        qkvb = [gates(ci, r0) for ci in ids]
        o_sc = [state_and_offdiag(ci, *qkvb[ci]) for ci in ids]
        for ci in ids:
            diag_products(ci, qkvb[ci][0], qkvb[ci][3])
        rs = [jnp.dot(p_scr[ci].astype(BF16), ones, preferred_element_type=F32) for ci in ids]
        for ci in ids:
            bi, hh = chains[ci]
            o, sc_half, sc_quarter = o_sc[ci]
            scores = jnp.where(same_half, jnp.where(tri, diag_scores(rs[ci]) + sc_quarter, 0.0), sc_half)
            o = o + jnp.dot(scores.astype(BF16), qkvb[ci][2], preferred_element_type=F32)
            o_ref[bi, pl.ds(r0, C), hh * HG_DK:(hh + 1) * HG_DK] = o
        return carry

    lax.fori_loop(0, n_chunks, chunk_body, 0)


def _hgrn_diag_rows():
    return (HG_CHUNK // HG_QUARTER) * sum(HG_QUARTER - (s // SUBLANES) * SUBLANES for s in range(HG_QUARTER))


def _hgrn_scan(hg, lb_logits_dir, B, S, reverse, tb):
    T = hg.shape[0]
    W = HG_HEADS * HG_DK
    nblk = S // tb
    f_blk = 2 if reverse else 1
    v_blk = 3
    nbat = HG_BATCH_PER_STEP if B % HG_BATCH_PER_STEP == 0 else 1
    n_chains = nbat * HG_HEADS
    hg4 = hg.reshape(B // nbat, nbat, S, hg.shape[1])

    def spec(col_blk):
        return pl.BlockSpec((None, nbat, tb, W),
                            lambda g, i: (g, 0, (nblk - 1 - i) if reverse else i, col_blk))

    kern = functools.partial(_hgrn_kernel, reverse=reverse, n_chunks=tb // HG_CHUNK)
    out = pl.pallas_call(
        kern,
        grid=(B // nbat, nblk),
        in_specs=[spec(0), spec(f_blk), spec(v_blk),
                  pl.BlockSpec((lb_logits_dir.shape[0], W), lambda g, i: (0, 0))],
        out_specs=spec(0),
        out_shape=jax.ShapeDtypeStruct((B // nbat, nbat, S, W), F32),
        scratch_shapes=[
            pltpu.VMEM((n_chains, HG_DK, HG_DK), F32),
            pltpu.VMEM((n_chains, HG_CHUNK, HG_DK), F32),
            pltpu.VMEM((n_chains, HG_CHUNK, HG_DK), F32),
            pltpu.VMEM((n_chains, _hgrn_diag_rows(), HG_DK), F32),
        ],
        compiler_params=_cparams(("parallel", "arbitrary")),
        name="hgrn_rev" if reverse else "hgrn_fwd",
    )(hg4, hg4, hg4, lb_logits_dir)
    return out.reshape(T, W)


def _mla_prep_kernel(mla_ref, pos_ref, invf_ref, qa_ref, kva_ref, wq_ref, wkv_ref,
                     qgn_ref, qgr_ref, kgn_ref, kgr_ref, q_ref, k_ref, v_ref):
    tm = mla_ref.shape[0]
    lane = lax.broadcasted_iota(jnp.int32, (tm, LANES), 1)
    ang = pos_ref[...].astype(F32) * invf_ref[...]
    cos_t = jnp.where(lane < MLA_ROPE, jnp.cos(ang), 0.0)
    sin_a = jnp.sin(ang)
    sin_t = jnp.where(lane < MLA_ROPE // 2, -sin_a, jnp.where(lane < MLA_ROPE, sin_a, 0.0))

    def rope(t):
        swapped = jnp.where(lane < MLA_ROPE // 2,
                            pltpu.roll(t, LANES - MLA_ROPE // 2, 1),
                            pltpu.roll(t, MLA_ROPE // 2, 1))
        return t * cos_t + swapped * sin_t

    def norm_rope_part(t, gain):
        ms = jnp.sum(t * t, axis=-1, keepdims=True) * (1.0 / MLA_ROPE)
        return t * lax.rsqrt(ms + EPS) * gain

    cq = mla_ref[:, 0:MLA_Q_LORA]
    ckv = mla_ref[:, MLA_Q_LORA:MLA_Q_LORA + MLA_KV_LORA]
    kr = mla_ref[:, MLA_Q_LORA + MLA_KV_LORA:]
    q_up = jnp.dot(_rms(cq, qa_ref[...]).astype(BF16), wq_ref[...], preferred_element_type=F32)
    kv_up = jnp.dot(_rms(ckv, kva_ref[...]).astype(BF16), wkv_ref[...], preferred_element_type=F32)
    k_rope = rope(norm_rope_part(kr, kgr_ref[...])).astype(BF16)
    for h in range(MLA_HEADS):
        c0 = h * MLA_QPAD
        q_ref[:, c0:c0 + LANES] = _rms(q_up[:, c0:c0 + LANES], qgn_ref[...]).astype(BF16)
        q_ref[:, c0 + LANES:c0 + 2 * LANES] = rope(
            norm_rope_part(q_up[:, c0 + LANES:c0 + 2 * LANES], qgr_ref[...])).astype(BF16)
        k_ref[:, c0:c0 + LANES] = _rms(kv_up[:, c0:c0 + LANES], kgn_ref[...]).astype(BF16)
        k_ref[:, c0 + LANES:c0 + 2 * LANES] = k_rope
        v_ref[:, h * MLA_V:(h + 1) * MLA_V] = kv_up[:, c0 + LANES:c0 + 2 * LANES].astype(BF16)


def _mla_prep(mla, pos, invf, qa, kva, wq, wkv, qgn, qgr, kgn, kgr, tm):
    T, n_mla = mla.shape
    full = lambda a: pl.BlockSpec(a.shape, lambda i: (0, 0))
    return pl.pallas_call(
        _mla_prep_kernel,
        grid=(T // tm,),
        in_specs=[
            pl.BlockSpec((tm, n_mla), lambda i: (i, 0)),
            pl.BlockSpec((tm, 1), lambda i: (i, 0)),
            full(invf), full(qa), full(kva), full(wq), full(wkv),
            full(qgn), full(qgr), full(kgn), full(kgr),
        ],
        out_specs=[
            pl.BlockSpec((tm, MLA_HEADS * MLA_QPAD), lambda i: (i, 0)),
            pl.BlockSpec((tm, MLA_HEADS * MLA_QPAD), lambda i: (i, 0)),
            pl.BlockSpec((tm, MLA_HEADS * MLA_V), lambda i: (i, 0)),
        ],
        out_shape=[
            jax.ShapeDtypeStruct((T, MLA_HEADS * MLA_QPAD), BF16),
            jax.ShapeDtypeStruct((T, MLA_HEADS * MLA_QPAD), BF16),
            jax.ShapeDtypeStruct((T, MLA_HEADS * MLA_V), BF16),
        ],
        compiler_params=_cparams(("parallel",)),
        name="mla_prep",
    )(mla, pos, invf, qa, kva, wq, wkv, qgn, qgr, kgn, kgr)


def _attn_kernel(q_ref, k_ref, v_ref, g_ref, o_ref):
    c = MLA_QK ** -0.5 * math.log2(math.e)
    tq = q_ref.shape[0]
    rc = tq // ATTN_CHUNKS
    rows = [pl.ds(i * rc, rc) for i in range(ATTN_CHUNKS)]
    s = [_dot_nt(q_ref[r, :], k_ref[...]) for r in rows]
    p = [jnp.exp2((si - jnp.max(si, axis=-1, keepdims=True)) * c) for si in s]
    l = [jnp.sum(pi, axis=-1, keepdims=True) for pi in p]
    o = [jnp.dot(pi.astype(BF16), v_ref[...], preferred_element_type=F32) for pi in p]
    for r, oi, li in zip(rows, o, l):
        o_ref[r, :] = _rms(oi / li, g_ref[...]).astype(o_ref.dtype)


def _attention(q, k, v, o_gain, B, S, tq):
    T = q.shape[0]
    nq = S // tq
    return pl.pallas_call(
        _attn_kernel,
        grid=(B, MLA_HEADS, nq),
        in_specs=[
            pl.BlockSpec((tq, MLA_QPAD), lambda b, h, i: (b * nq + i, h)),
            pl.BlockSpec((S, MLA_QPAD), lambda b, h, i: (b, h)),
            pl.BlockSpec((S, MLA_V), lambda b, h, i: (b, h)),
            pl.BlockSpec((1, MLA_V), lambda b, h, i: (0, h)),
        ],
        out_specs=pl.BlockSpec((tq, MLA_V), lambda b, h, i: (b * nq + i, h)),
        out_shape=jax.ShapeDtypeStruct((T, MLA_HEADS * MLA_V), BF16),
        compiler_params=_cparams(("parallel", "parallel", "arbitrary")),
        name="mla_attention",
    )(q, k, v, o_gain)


def _out_proj_kernel(of_ref, ob_ref, g_ref, ym_ref, x_ref, hgg_ref, wo_ref, fg_ref, wq_ref, keys_ref,
                     x1_ref, h2_ref, st_ref):
    tm = x_ref.shape[0]
    rc = tm // OUT_CHUNKS
    rows = [pl.ds(i * rc, rc) for i in range(OUT_CHUNKS)]
    w_hg_rows = of_ref.shape[1]

    def mixer_out(r):
        o = of_ref[r, :] + ob_ref[r, :]
        gate_raw = g_ref[r, :]
        gate = gate_raw * _sigmoid(gate_raw)
        parts = []
        for h in range(HG_HEADS):
            sl = slice(h * HG_DK, (h + 1) * HG_DK)
            parts.append(_rms(o[:, sl], hgg_ref[:, sl]) * gate[:, sl])
        return jnp.concatenate(parts, axis=-1).astype(BF16)

    y_hg = [mixer_out(r) for r in rows]
    mix = [jnp.dot(y, wo_ref[0:w_hg_rows, :], preferred_element_type=F32)
           + jnp.dot(ym_ref[r, :], wo_ref[w_hg_rows:, :], preferred_element_type=F32)
           for y, r in zip(y_hg, rows)]
    x1 = [x_ref[r, :] + m for r, m in zip(rows, mix)]
    h2 = [_rms(x, fg_ref[...]).astype(BF16) for x in x1]
    for r, x, h in zip(rows, x1, h2):
        x1_ref[r, :] = x
        h2_ref[r, :] = h
    pq = [jnp.dot(h, wq_ref[...], preferred_element_type=F32).astype(BF16) for h in h2]
    pq = jnp.concatenate(pq, axis=0)
    for pc in range(keys_ref.shape[0]):
        st_ref[pc] = _dot_nt(keys_ref[pc], pq[:, pc * PEER_DHALF:(pc + 1) * PEER_DHALF])


def _out_proj(o_f, o_b, hg, y_mla, x2, hg_gain, w_out, ffn_gain, w_pq, keys, tm):
    T, D = x2.shape
    W = o_f.shape[1]
    npc = keys.shape[0]
    full2 = lambda a: pl.BlockSpec(a.shape, lambda i: (0, 0))
    return pl.pallas_call(
        _out_proj_kernel,
        grid=(T // tm,),
        in_specs=[
            pl.BlockSpec((tm, W), lambda i: (i, 0)),
            pl.BlockSpec((tm, W), lambda i: (i, 0)),
            pl.BlockSpec((tm, W), lambda i: (i, 4)),
            pl.BlockSpec((tm, y_mla.shape[1]), lambda i: (i, 0)),
            pl.BlockSpec((tm, D), lambda i: (i, 0)),
            full2(hg_gain), full2(w_out), full2(ffn_gain), full2(w_pq),
            pl.BlockSpec(keys.shape, lambda i: (0, 0, 0)),
        ],
        out_specs=[
            pl.BlockSpec((tm, D), lambda i: (i, 0)),
            pl.BlockSpec((tm, D), lambda i: (i, 0)),
            pl.BlockSpec((npc, PEER_NKEYS, tm), lambda i: (0, 0, i)),
        ],
        out_shape=[
            jax.ShapeDtypeStruct((T, D), F32),
            jax.ShapeDtypeStruct((T, D), BF16),
            jax.ShapeDtypeStruct((npc, PEER_NKEYS, T), F32),
        ],
        compiler_params=_cparams(("parallel",)),
        name="out_proj_peer_query",
    )(o_f, o_b, hg, y_mla, x2, hg_gain, w_out, ffn_gain, w_pq, keys)


def _tree_max(xs):
    xs = list(xs)
    while len(xs) > 1:
        nxt = [jnp.maximum(xs[i], xs[i + 1]) for i in range(0, len(xs) - 1, 2)]
        if len(xs) % 2:
            nxt.append(xs[-1])
        xs = nxt
    return xs[0]


def _sorting_network(n):
    pairs = []
    p = 1
    while p < n:
        k = p
        while k >= 1:
            for j in range(k % p, n - k, 2 * k):
                for i in range(min(k, n - j - k)):
                    if (i + j) // (2 * p) == (i + j + k) // (2 * p):
                        pairs.append((i + j, i + j + k))
            k //= 2
        p *= 2
    return pairs


def _bitonic_merge_network(n):
    pairs = []
    stride = n // 2
    while stride >= 1:
        pairs += [(i, i + stride) for i in range(n) if not i & stride]
        stride //= 2
    return pairs


def _compare_exchange(xs, pairs):
    xs = list(xs)
    for i, j in pairs:
        xs[i], xs[j] = jnp.maximum(xs[i], xs[j]), jnp.minimum(xs[i], xs[j])
    return xs


def _peer_select_kernel(s_ref, iz_ref, cnt_ref, atop_ref, btop_ref, top_scr):
    K = PEER_TOPK
    n_vregs = PEER_NKEYS // SUBLANES
    assert n_vregs == K
    sort_pairs = _sorting_network(K)
    merge_pairs = _bitonic_merge_network(K)
    for p in range(PEER_HEADS):
        for c in range(2):
            lists = _compare_exchange(
                [s_ref[2 * p + c, pl.ds(g * SUBLANES, SUBLANES), :] for g in range(n_vregs)], sort_pairs)
            shift = SUBLANES // 2
            while shift >= 1:
                other = [pltpu.roll(x, shift, 0) for x in lists]
                lists = _compare_exchange(
                    [jnp.maximum(lists[r], other[K - 1 - r]) for r in range(K)], merge_pairs)
                shift //= 2
            for r in range(K):
                top_scr[c, r, pl.ds(p, 1), :] = lists[r][0:1, :]
    a = [top_scr[0, r] for r in range(K)]
    b = [top_scr[1, r] for r in range(K)]
    pairs = [(i, j) for i in range(K) for j in range(K) if (i + 1) * (j + 1) <= K]
    cands = [a[i] + b[j] for i, j in pairs]
    top = a[0] + b[0]
    z = jnp.zeros_like(top)
    kth = top
    for r in range(K):
        kth = _tree_max(cands)
        z = z + jnp.exp(kth - top)
        cands = [jnp.where(cv == kth, NEG_INF, cv) for cv in cands]
    iz_ref[...] = 1.0 / z
    for i in range(K):
        cnt = jnp.zeros_like(top)
        for j in range(K):
            if (i, j) in pairs:
                cnt = cnt + jnp.where(a[i] + b[j] >= kth, 1.0, 0.0)
        cnt_ref[i] = cnt
        atop_ref[i] = a[i]
        btop_ref[i] = b[i]


def _peer_select(scores_t, tt):
    npc, nk, T = scores_t.shape
    top = jax.ShapeDtypeStruct((PEER_TOPK, PEER_HEADS, T), F32)
    top_spec = pl.BlockSpec((PEER_TOPK, PEER_HEADS, tt), lambda i: (0, 0, i))
    return pl.pallas_call(
        _peer_select_kernel,
        grid=(T // tt,),
        in_specs=[pl.BlockSpec((npc, nk, tt), lambda i: (0, 0, i))],
        out_specs=[pl.BlockSpec((PEER_HEADS, tt), lambda i: (0, i))] + [top_spec] * 3,
        out_shape=[jax.ShapeDtypeStruct((PEER_HEADS, T), F32)] + [top] * 3,
        scratch_shapes=[pltpu.VMEM((2, PEER_TOPK, PEER_HEADS, tt), F32)],
        compiler_params=_cparams(("parallel",)),
        name="peer_select",
    )(scores_t)


def _gelu_exact_x2(x):
    return x * (1.0 + lax.erf(x * (1.0 / math.sqrt(2.0))))


def _peer_dense_kernel(h_ref, u_ref, vt_ref, s_ref, iz_ref, cnt_ref, atop_ref, btop_ref, x1_ref, o_ref,
                       n_scr, e1_scr, rank_scr, e2_scr, a_scr, p_scr, yt_scr):
    jt = pl.program_id(1)
    te, tt = a_scr.shape
    nb = te // PEER_NKEYS
    n_grp = nb // SUBLANES
    PK = SUBLANES * (4 // jnp.dtype(GATE_DTYPE).itemsize)
    IBQ = 32 * PK // PEER_NKEYS

    @pl.when(jt == 0)
    def _():
        yt_scr[...] = jnp.zeros_like(yt_scr)
        grouped = (PEER_NKEYS // SUBLANES, SUBLANES, LANES)
        for p in range(PEER_HEADS):
            for st in range(tt // LANES):
                sl = pl.ds(st * LANES, LANES)
                s1 = s_ref[2 * p, :, sl]
                s2 = s_ref[2 * p + 1, :, sl]
                n = jnp.zeros_like(s1)
                rank = jnp.zeros_like(s2)
                for r in range(PEER_TOPK):
                    n = jnp.where(s1 == atop_ref[r, pl.ds(p, 1), sl], cnt_ref[r, pl.ds(p, 1), sl], n)
                    rank = jnp.where(btop_ref[r, pl.ds(p, 1), sl] > s2, r + 1.0, rank)
                n_scr[p, :, :, sl] = n.reshape(grouped)
                e1_scr[p, :, :, sl] = jnp.exp(s1 - atop_ref[0, pl.ds(p, 1), sl]).reshape(grouped)
                rank_scr[p, :, sl] = rank.astype(GATE_DTYPE)
                m2 = btop_ref[0, pl.ds(p, 1), sl]
                e2_scr[p, :, sl] = (jnp.exp(s2 - m2) * (0.5 * iz_ref[pl.ds(p, 1), sl])).astype(GATE_DTYPE)

    a_scr[...] = _dot_nt(u_ref[...], h_ref[...])
    zero = jnp.zeros((PK, LANES), GATE_DTYPE)
    for ibg in range(nb // IBQ):
        for st in range(tt // LANES):
            sl = pl.ds(st * LANES, LANES)
            gates = [[zero for _ in range(PEER_NKEYS // PK)] for _ in range(IBQ)]
            for p in range(PEER_HEADS):
                rank = rank_scr[p, :, sl]
                e2 = e2_scr[p, :, sl]
                for q in range(IBQ):
                    ib = ibg * IBQ + q
                    grp = jt * n_grp + ib // SUBLANES
                    sub = pl.ds(ib % SUBLANES, 1)
                    n = jnp.broadcast_to(n_scr[p, grp, sub, sl], (PK, LANES)).astype(GATE_DTYPE)
                    e1 = jnp.broadcast_to(e1_scr[p, grp, sub, sl], (PK, LANES)).astype(GATE_DTYPE)
                    for r in range(PEER_NKEYS // PK):
                        rr = slice(r * PK, (r + 1) * PK)
                        gates[q][r] = gates[q][r] + jnp.where(rank[rr, :] < n, e2[rr, :], zero) * e1
            for q in range(IBQ):
                rows = pl.ds((ibg * IBQ + q) * PEER_NKEYS, PEER_NKEYS)
                gate = jnp.concatenate(gates[q], axis=0)
                p_scr[rows, sl] = (_gelu_exact_x2(a_scr[rows, sl]).astype(GATE_DTYPE) * gate).astype(BF16)
    yt_scr[...] += jnp.dot(vt_ref[...], p_scr[...], preferred_element_type=F32)

    @pl.when(jt == pl.num_programs(1) - 1)
    def _():
        o_ref[...] = x1_ref[...] + yt_scr[...].T


def _peer_dense(h2, u_bf, vt_bf, scores_t, iz, cnt, atop, btop, x1, tt, te):
    T, D = h2.shape
    E = u_bf.shape[0]
    npc, nk, _ = scores_t.shape
    assert te % (PEER_NKEYS * SUBLANES) == 0 and E % te == 0
    top = pl.BlockSpec((PEER_TOPK, PEER_HEADS, tt), lambda t, j: (0, 0, t))
    return pl.pallas_call(
        _peer_dense_kernel,
        grid=(T // tt, E // te),
        in_specs=[
            pl.BlockSpec((tt, D), lambda t, j: (t, 0)),
            pl.BlockSpec((te, D), lambda t, j: (j, 0)),
            pl.BlockSpec((D, te), lambda t, j: (0, j)),
            pl.BlockSpec((npc, nk, tt), lambda t, j: (0, 0, t)),
            pl.BlockSpec((PEER_HEADS, tt), lambda t, j: (0, t)),
            top, top, top,
            pl.BlockSpec((tt, D), lambda t, j: (t, 0)),
        ],
        out_specs=pl.BlockSpec((tt, D), lambda t, j: (t, 0)),
        out_shape=jax.ShapeDtypeStruct((T, D), F32),
        scratch_shapes=[
            pltpu.VMEM((PEER_HEADS, nk // SUBLANES, SUBLANES, tt), F32),
            pltpu.VMEM((PEER_HEADS, nk // SUBLANES, SUBLANES, tt), F32),
            pltpu.VMEM((PEER_HEADS, nk, tt), GATE_DTYPE),
            pltpu.VMEM((PEER_HEADS, nk, tt), GATE_DTYPE),
            pltpu.VMEM((te, tt), F32),
            pltpu.VMEM((te, tt), BF16),
            pltpu.VMEM((D, tt), F32),
        ],
        compiler_params=_cparams(("parallel", "arbitrary")),
        name="peer_dense",
    )(h2, u_bf, vt_bf, scores_t, iz, cnt, atop, btop, x1)


def _pick(n, pref):
    t = min(n, pref)
    assert n % t == 0, (n, t)
    return t


def _tiles(T, S):
    return {
        "in_proj_rows": _pick(T, 256),
        "scan_steps": _pick(S, 512),
        "mla_rows": _pick(T, 256),
        "attn_queries": _pick(S, 1024),
        "out_proj_rows": _pick(T, 512),
        "select_tokens": _pick(T, 256),
        "dense_tokens": _pick(T, 512),
        "dense_experts": 2 * PEER_NKEYS * SUBLANES,
    }


def _layer(x2, pos, B, S, attn_norm, w_in, lb_logits, hg_o_norm, q_a_norm, w_q_up, kv_a_norm,
           w_kv_up, q_norm, k_norm, mla_o_norm, w_out, ffn_norm, peer_w_q, peer_sub_keys, peer_u, peer_v):
    T, D = x2.shape
    n_hg = 5 * HG_HEADS * HG_DK
    row = lambda a: a.reshape(1, -1).astype(F32)

    w_hg = w_in[:, :n_hg].astype(BF16)
    n_mla = w_in.shape[1] - n_hg
    w_mla = jnp.pad(w_in[:, n_hg:], ((0, 0), (0, -n_mla % LANES))).astype(BF16)
    pad_r = LANES - MLA_ROPE
    wq = w_q_up.reshape(MLA_Q_LORA, MLA_HEADS, MLA_QK)
    wq = jnp.pad(wq, ((0, 0), (0, 0), (0, MLA_QPAD - MLA_QK))).reshape(MLA_Q_LORA, MLA_HEADS * MLA_QPAD)
    wq = wq.astype(BF16)
    wkv = w_kv_up.astype(BF16)
    qgn, qgr = row(q_norm[:MLA_NOPE]), row(jnp.pad(q_norm[MLA_NOPE:], (0, pad_r)))
    kgn, kgr = row(k_norm[:MLA_NOPE]), row(jnp.pad(k_norm[MLA_NOPE:], (0, pad_r)))
    inv_freq = 1.0 / (ROPE_THETA ** (jnp.arange(0, MLA_ROPE, 2, dtype=F32) / MLA_ROPE))
    invf = row(jnp.concatenate([inv_freq, inv_freq, jnp.zeros((pad_r,), F32)]))
    keys = peer_sub_keys.reshape(2 * PEER_HEADS, PEER_NKEYS, PEER_DHALF).astype(BF16)
    u_bf = peer_u.astype(BF16)
    vt_bf = peer_v.astype(BF16).T

    tile = _tiles(T, S)
    hg, mla = _in_proj(x2, row(attn_norm), w_hg, w_mla, tile["in_proj_rows"])
    o_f = _hgrn_scan(hg, lb_logits[:, 0, :], B, S, False, tile["scan_steps"])
    o_b = _hgrn_scan(hg, lb_logits[:, 1, :], B, S, True, tile["scan_steps"])
    q, k, v = _mla_prep(mla, pos, invf, row(q_a_norm), row(kv_a_norm), wq, wkv, qgn, qgr, kgn, kgr,
                        tile["mla_rows"])
    y_mla = _attention(q, k, v, mla_o_norm.reshape(1, -1).astype(F32), B, S, tile["attn_queries"])
    x1, h2, scores_t = _out_proj(o_f, o_b, hg, y_mla, x2, row(hg_o_norm), w_out.astype(BF16),
                                 row(ffn_norm), peer_w_q.astype(BF16), keys, tile["out_proj_rows"])
    iz, cnt, atop, btop = _peer_select(scores_t, tile["select_tokens"])
    return _peer_dense(h2, u_bf, vt_bf, scores_t, iz, cnt, atop, btop, x1,
                       tile["dense_tokens"], tile["dense_experts"])


def kernel(x, positions, attn_norm, w_in, hg_lb_logits, hg_o_norm, q_a_norm, w_q_up, kv_a_norm, w_kv_up,
           q_norm, k_norm, mla_o_norm, w_out, ffn_norm, peer_w_q, peer_sub_keys, peer_u, peer_v):
    B, S, D = x.shape
    x2 = x.reshape(B * S, D)
    pos = positions.reshape(B * S, 1)
    for l in range(attn_norm.shape[0]):
        assert l == 0
        x2 = _layer(x2, pos, B, S, attn_norm[l], w_in[l], hg_lb_logits, hg_o_norm[l], q_a_norm[l],
                    w_q_up[l], kv_a_norm[l], w_kv_up[l], q_norm[l], k_norm[l], mla_o_norm[l], w_out[l],
                    ffn_norm[l], peer_w_q[l], peer_sub_keys[l], peer_u[l], peer_v[l])
    return x2.reshape(B, S, D)
```

```python
import functools
import math

import jax
import jax.numpy as jnp
from jax import lax
from jax.experimental import pallas as pl
from jax.experimental.pallas import tpu as pltpu

F32 = jnp.float32
BF16 = jnp.bfloat16
EPS = 1e-6
LANES = 128
SUBLANES = 8
NEG_INF = float("-inf")

HG_HEADS = 4
HG_DK = 128
HG_CHUNK = 64
HG_HALF = HG_CHUNK // 2
HG_QUARTER = HG_CHUNK // 4
HG_BATCH_PER_STEP = 4
MLA_HEADS = 4
MLA_Q_LORA = 384
MLA_KV_LORA = 256
MLA_NOPE = 128
MLA_ROPE = 64
MLA_V = 128
MLA_QK = MLA_NOPE + MLA_ROPE
MLA_QPAD = 2 * LANES
ROPE_THETA = 10000.0
ATTN_CHUNKS = 4
OUT_CHUNKS = 2
PEER_HEADS = 8
PEER_NKEYS = 128
PEER_TOPK = 16
PEER_DHALF = 128
GATE_DTYPE = BF16

VMEM_LIMIT = 56 * 1024 * 1024


def _cparams(sem):
    return pltpu.CompilerParams(dimension_semantics=sem, vmem_limit_bytes=VMEM_LIMIT)


def _rms(x, gain):
    ms = jnp.mean(x * x, axis=-1, keepdims=True)
    return x * lax.rsqrt(ms + EPS) * gain


def _sigmoid(x):
    return 1.0 / (1.0 + jnp.exp(-x))


def _dot_nt(a, b):
    return lax.dot_general(a, b, (((1,), (1,)), ((), ())), preferred_element_type=F32)


def _dot_tn(a, b):
    return lax.dot_general(a, b, (((0,), (0,)), ((), ())), preferred_element_type=F32)


def _in_proj_kernel(x_ref, g_ref, whg_ref, wmla_ref, hg_ref, mla_ref):
    h = _rms(x_ref[...], g_ref[...]).astype(BF16)
    hg_ref[...] = jnp.dot(h, whg_ref[...], preferred_element_type=F32)
    mla_ref[...] = jnp.dot(h, wmla_ref[...], preferred_element_type=F32)


def _in_proj(x2, gain, w_hg, w_mla, tm):
    T, D = x2.shape
    n_hg, n_mla = w_hg.shape[1], w_mla.shape[1]
    return pl.pallas_call(
        _in_proj_kernel,
        grid=(T // tm,),
        in_specs=[
            pl.BlockSpec((tm, D), lambda i: (i, 0)),
            pl.BlockSpec((1, D), lambda i: (0, 0)),
            pl.BlockSpec((D, n_hg), lambda i: (0, 0)),
            pl.BlockSpec((D, n_mla), lambda i: (0, 0)),
        ],
        out_specs=[
            pl.BlockSpec((tm, n_hg), lambda i: (i, 0)),
            pl.BlockSpec((tm, n_mla), lambda i: (i, 0)),
        ],
        out_shape=[jax.ShapeDtypeStruct((T, n_hg), F32), jax.ShapeDtypeStruct((T, n_mla), F32)],
        compiler_params=_cparams(("parallel",)),
        name="in_proj",
    )(x2, gain, w_hg, w_mla)


def _hgrn_kernel(q_ref, f_ref, v_ref, lb_ref, o_ref, st_ref, b_scr, k_scr, p_scr, *, reverse, n_chunks):
    C, Hf, Qt, G = HG_CHUNK, HG_HALF, HG_QUARTER, SUBLANES
    n_batch = q_ref.shape[0]
    n_heads = st_ref.shape[0] // n_batch
    chains = [(bi, hh) for bi in range(n_batch) for hh in range(n_heads)]

    @pl.when(pl.program_id(1) == 0)
    def _():
        st_ref[...] = jnp.zeros_like(st_ref)

    logits = lb_ref[...]
    ex = jnp.exp(logits - jnp.max(logits, axis=0, keepdims=True))
    lb_all = ex[0:1, :] / jnp.sum(ex, axis=0, keepdims=True)

    row = lax.broadcasted_iota(jnp.int32, (C, C), 0)
    col = lax.broadcasted_iota(jnp.int32, (C, C), 1)
    tri = (col >= row) if reverse else (col <= row)
    cum_mat = tri.astype(F32)
    same_half = (row >= Hf) == (col >= Hf)
    row_c = lax.broadcasted_iota(jnp.int32, (C, LANES), 0)
    q_side = (row_c < Hf) if reverse else (row_c >= Hf)
    odd_quarter = (row_c // Qt) % 2 == 1
    q_side4 = jnp.logical_not(odd_quarter) if reverse else odd_quarter
    low_half = row_c < Hf
    lane_g = lax.broadcasted_iota(jnp.int32, (G, LANES), 1)
    ones = jnp.ones((LANES, LANES), BF16)
    mid_row = Hf if reverse else Hf - 1
    qmid_rows = (Qt, Hf + Qt) if reverse else (Qt - 1, Hf + Qt - 1)
    end_row = 0 if reverse else C - 1

    def live_groups(s):
        return range(0, s // G + 1) if reverse else range(s // G, Qt // G)

    def gates(ci, r0):
        bi, hh = chains[ci]
        cols = slice(hh * HG_DK, (hh + 1) * HG_DK)
        lb = lb_all[:, cols]
        q_raw = q_ref[bi, pl.ds(r0, C), cols]
        q = q_raw * _sigmoid(q_raw)
        f = lb + (1.0 - lb) * _sigmoid(f_ref[bi, pl.ds(r0, C), cols])
        k = 1.0 - f
        vb = v_ref[bi, pl.ds(r0, C), cols].astype(BF16)
        b = jnp.dot(cum_mat, jnp.log2(f), precision=lax.Precision.HIGHEST,
                    preferred_element_type=F32)
        b_scr[ci] = b
        k_scr[ci] = k
        return q, k, vb, b

    def state_and_offdiag(ci, q, k, vb, b):
        b_mid = b_scr[ci, pl.ds(mid_row, 1), :]
        b_end = b_scr[ci, pl.ds(end_row, 1), :]
        st = st_ref[ci]
        o = _dot_nt((q * jnp.exp2(b)).astype(BF16), st.astype(BF16))
        kd = (k * jnp.exp2(b_end - b)).astype(BF16)
        st_ref[ci] = st * jnp.exp2(b_end) + _dot_tn(vb, kd)
        qt = jnp.where(q_side, q * jnp.exp2(b - b_mid), 0.0).astype(BF16)
        kt = jnp.where(q_side, 0.0, k * jnp.exp2(b_mid - b)).astype(BF16)
        b_q = jnp.where(low_half, b_scr[ci, pl.ds(qmid_rows[0], 1), :], b_scr[ci, pl.ds(qmid_rows[1], 1), :])
        qt4 = jnp.where(q_side4, q * jnp.exp2(b - b_q), 0.0).astype(BF16)
        kt4 = jnp.where(q_side4, 0.0, k * jnp.exp2(b_q - b)).astype(BF16)
        return o, _dot_nt(qt, kt), _dot_nt(qt4, kt4)

    def diag_products(ci, q, b):
        n_rows = 0
        for qb in range(C // Qt):
            base = qb * Qt
            for s in range(Qt):
                g0, g1 = live_groups(s)[0], live_groups(s)[-1] + 1
                rows = slice(base + g0 * G, base + g1 * G)
                bs = b_scr[ci, pl.ds(base + s, 1), :]
                ks = k_scr[ci, pl.ds(base + s, 1), :]
                p_scr[ci, pl.ds(n_rows, (g1 - g0) * G), :] = (q[rows, :] * ks) * jnp.exp2(b[rows, :] - bs)
                n_rows += (g1 - g0) * G

    def diag_scores(rs):
        n_rows = 0
        groups = []
        for qb in range(C // Qt):
            base = qb * Qt
            acc = [jnp.zeros((G, LANES), F32) for _ in range(Qt // G)]
            for s in range(Qt):
                for g in live_groups(s):
                    acc[g] = jnp.where(lane_g == base + s, rs[n_rows:n_rows + G, :], acc[g])
                    n_rows += G
            groups.extend(acc)
        return jnp.concatenate(groups, axis=0)[:, :C]

    def chunk_body(ci, carry):
        c = (n_chunks - 1 - ci) if reverse else ci
        r0 = pl.multiple_of(c * C, C)
        ids = range(len(chains))
        qkvb = [gates(ci, r0) for ci in ids]
        o_sc = [state_and_offdiag(ci, *qkvb[ci]) for ci in ids]
        for ci in ids:
            diag_products(ci, qkvb[ci][0], qkvb[ci][3])
        rs = [jnp.dot(p_scr[ci].astype(BF16), ones, preferred_element_type=F32) for ci in ids]
        for ci in ids:
            bi, hh = chains[ci]
            o, sc_half, sc_quarter = o_sc[ci]
            scores = jnp.where(same_half, jnp.where(tri, diag_scores(rs[ci]) + sc_quarter, 0.0), sc_half)
            o = o + jnp.dot(scores.astype(BF16), qkvb[ci][2], preferred_element_type=F32)
            o_ref[bi, pl.ds(r0, C), hh * HG_DK:(hh + 1) * HG_DK] = o
        return carry

    lax.fori_loop(0, n_chunks, chunk_body, 0)


def _hgrn_diag_rows():
    return (HG_CHUNK // HG_QUARTER) * sum(HG_QUARTER - (s // SUBLANES) * SUBLANES for s in range(HG_QUARTER))


def _hgrn_scan(hg, lb_logits_dir, B, S, reverse, tb):
    T = hg.shape[0]
    W = HG_HEADS * HG_DK
    nblk = S // tb
    f_blk = 2 if reverse else 1
    v_blk = 3
    nbat = HG_BATCH_PER_STEP if B % HG_BATCH_PER_STEP == 0 else 1
    n_chains = nbat * HG_HEADS
    hg4 = hg.reshape(B // nbat, nbat, S, hg.shape[1])

    def spec(col_blk):
        return pl.BlockSpec((None, nbat, tb, W),
                            lambda g, i: (g, 0, (nblk - 1 - i) if reverse else i, col_blk))

    kern = functools.partial(_hgrn_kernel, reverse=reverse, n_chunks=tb // HG_CHUNK)
    out = pl.pallas_call(
        kern,
        grid=(B // nbat, nblk),
        in_specs=[spec(0), spec(f_blk), spec(v_blk),
                  pl.BlockSpec((lb_logits_dir.shape[0], W), lambda g, i: (0, 0))],
        out_specs=spec(0),
        out_shape=jax.ShapeDtypeStruct((B // nbat, nbat, S, W), F32),
        scratch_shapes=[
            pltpu.VMEM((n_chains, HG_DK, HG_DK), F32),
            pltpu.VMEM((n_chains, HG_CHUNK, HG_DK), F32),
            pltpu.VMEM((n_chains, HG_CHUNK, HG_DK), F32),
            pltpu.VMEM((n_chains, _hgrn_diag_rows(), HG_DK), F32),
        ],
        compiler_params=_cparams(("parallel", "arbitrary")),
        name="hgrn_rev" if reverse else "hgrn_fwd",
    )(hg4, hg4, hg4, lb_logits_dir)
    return out.reshape(T, W)


def _mla_prep_kernel(mla_ref, pos_ref, invf_ref, qa_ref, kva_ref, wq_ref, wkv_ref,
                     qgn_ref, qgr_ref, kgn_ref, kgr_ref, q_ref, k_ref, v_ref):
    tm = mla_ref.shape[0]
    lane = lax.broadcasted_iota(jnp.int32, (tm, LANES), 1)
    ang = pos_ref[...].astype(F32) * invf_ref[...]
    cos_t = jnp.where(lane < MLA_ROPE, jnp.cos(ang), 0.0)
    sin_a = jnp.sin(ang)
    sin_t = jnp.where(lane < MLA_ROPE // 2, -sin_a, jnp.where(lane < MLA_ROPE, sin_a, 0.0))

    def rope(t):
        swapped = jnp.where(lane < MLA_ROPE // 2,
                            pltpu.roll(t, LANES - MLA_ROPE // 2, 1),
                            pltpu.roll(t, MLA_ROPE // 2, 1))
        return t * cos_t + swapped * sin_t

    def norm_rope_part(t, gain):
        ms = jnp.sum(t * t, axis=-1, keepdims=True) * (1.0 / MLA_ROPE)
        return t * lax.rsqrt(ms + EPS) * gain

    cq = mla_ref[:, 0:MLA_Q_LORA]
    ckv = mla_ref[:, MLA_Q_LORA:MLA_Q_LORA + MLA_KV_LORA]
    kr = mla_ref[:, MLA_Q_LORA + MLA_KV_LORA:]
    q_up = jnp.dot(_rms(cq, qa_ref[...]).astype(BF16), wq_ref[...], preferred_element_type=F32)
    kv_up = jnp.dot(_rms(ckv, kva_ref[...]).astype(BF16), wkv_ref[...], preferred_element_type=F32)
    k_rope = rope(norm_rope_part(kr, kgr_ref[...])).astype(BF16)
    for h in range(MLA_HEADS):
        c0 = h * MLA_QPAD
        q_ref[:, c0:c0 + LANES] = _rms(q_up[:, c0:c0 + LANES], qgn_ref[...]).astype(BF16)
        q_ref[:, c0 + LANES:c0 + 2 * LANES] = rope(
            norm_rope_part(q_up[:, c0 + LANES:c0 + 2 * LANES], qgr_ref[...])).astype(BF16)
        k_ref[:, c0:c0 + LANES] = _rms(kv_up[:, c0:c0 + LANES], kgn_ref[...]).astype(BF16)
        k_ref[:, c0 + LANES:c0 + 2 * LANES] = k_rope
        v_ref[:, h * MLA_V:(h + 1) * MLA_V] = kv_up[:, c0 + LANES:c0 + 2 * LANES].astype(BF16)


def _mla_prep(mla, pos, invf, qa, kva, wq, wkv, qgn, qgr, kgn, kgr, tm):
    T, n_mla = mla.shape
    full = lambda a: pl.BlockSpec(a.shape, lambda i: (0, 0))
    return pl.pallas_call(
        _mla_prep_kernel,
        grid=(T // tm,),
        in_specs=[
            pl.BlockSpec((tm, n_mla), lambda i: (i, 0)),
            pl.BlockSpec((tm, 1), lambda i: (i, 0)),
            full(invf), full(qa), full(kva), full(wq), full(wkv),
            full(qgn), full(qgr), full(kgn), full(kgr),
        ],
        out_specs=[
            pl.BlockSpec((tm, MLA_HEADS * MLA_QPAD), lambda i: (i, 0)),
            pl.BlockSpec((tm, MLA_HEADS * MLA_QPAD), lambda i: (i, 0)),
            pl.BlockSpec((tm, MLA_HEADS * MLA_V), lambda i: (i, 0)),
        ],
        out_shape=[
            jax.ShapeDtypeStruct((T, MLA_HEADS * MLA_QPAD), BF16),
            jax.ShapeDtypeStruct((T, MLA_HEADS * MLA_QPAD), BF16),
            jax.ShapeDtypeStruct((T, MLA_HEADS * MLA_V), BF16),
        ],
        compiler_params=_cparams(("parallel",)),
        name="mla_prep",
    )(mla, pos, invf, qa, kva, wq, wkv, qgn, qgr, kgn, kgr)


def _attn_kernel(q_ref, k_ref, v_ref, g_ref, o_ref):
    c = MLA_QK ** -0.5 * math.log2(math.e)
    tq = q_ref.shape[0]
    rc = tq // ATTN_CHUNKS
    rows = [pl.ds(i * rc, rc) for i in range(ATTN_CHUNKS)]
    s = [_dot_nt(q_ref[r, :], k_ref[...]) for r in rows]
    p = [jnp.exp2((si - jnp.max(si, axis=-1, keepdims=True)) * c) for si in s]
    l = [jnp.sum(pi, axis=-1, keepdims=True) for pi in p]
    o = [jnp.dot(pi.astype(BF16), v_ref[...], preferred_element_type=F32) for pi in p]
    for r, oi, li in zip(rows, o, l):
        o_ref[r, :] = _rms(oi / li, g_ref[...]).astype(o_ref.dtype)


def _attention(q, k, v, o_gain, B, S, tq):
    T = q.shape[0]
    nq = S // tq
    return pl.pallas_call(
        _attn_kernel,
        grid=(B, MLA_HEADS, nq),
        in_specs=[
            pl.BlockSpec((tq, MLA_QPAD), lambda b, h, i: (b * nq + i, h)),
            pl.BlockSpec((S, MLA_QPAD), lambda b, h, i: (b, h)),
            pl.BlockSpec((S, MLA_V), lambda b, h, i: (b, h)),
            pl.BlockSpec((1, MLA_V), lambda b, h, i: (0, h)),
        ],
        out_specs=pl.BlockSpec((tq, MLA_V), lambda b, h, i: (b * nq + i, h)),
        out_shape=jax.ShapeDtypeStruct((T, MLA_HEADS * MLA_V), BF16),
        compiler_params=_cparams(("parallel", "parallel", "arbitrary")),
        name="mla_attention",
    )(q, k, v, o_gain)


def _out_proj_kernel(of_ref, ob_ref, g_ref, ym_ref, x_ref, hgg_ref, wo_ref, fg_ref, wq_ref, keys_ref,
                     x1_ref, h2_ref, st_ref):
    tm = x_ref.shape[0]
    rc = tm // OUT_CHUNKS
    rows = [pl.ds(i * rc, rc) for i in range(OUT_CHUNKS)]
    w_hg_rows = of_ref.shape[1]

    def mixer_out(r):
        o = of_ref[r, :] + ob_ref[r, :]
        gate_raw = g_ref[r, :]
        gate = gate_raw * _sigmoid(gate_raw)
        parts = []
        for h in range(HG_HEADS):
            sl = slice(h * HG_DK, (h + 1) * HG_DK)
            parts.append(_rms(o[:, sl], hgg_ref[:, sl]) * gate[:, sl])
        return jnp.concatenate(parts, axis=-1).astype(BF16)

    y_hg = [mixer_out(r) for r in rows]
    mix = [jnp.dot(y, wo_ref[0:w_hg_rows, :], preferred_element_type=F32)
           + jnp.dot(ym_ref[r, :], wo_ref[w_hg_rows:, :], preferred_element_type=F32)
           for y, r in zip(y_hg, rows)]
    x1 = [x_ref[r, :] + m for r, m in zip(rows, mix)]
    h2 = [_rms(x, fg_ref[...]).astype(BF16) for x in x1]
    for r, x, h in zip(rows, x1, h2):
        x1_ref[r, :] = x
        h2_ref[r, :] = h
    pq = [jnp.dot(h, wq_ref[...], preferred_element_type=F32).astype(BF16) for h in h2]
    pq = jnp.concatenate(pq, axis=0)
    for pc in range(keys_ref.shape[0]):
        st_ref[pc] = _dot_nt(keys_ref[pc], pq[:, pc * PEER_DHALF:(pc + 1) * PEER_DHALF])


def _out_proj(o_f, o_b, hg, y_mla, x2, hg_gain, w_out, ffn_gain, w_pq, keys, tm):
    T, D = x2.shape
    W = o_f.shape[1]
    npc = keys.shape[0]
    full2 = lambda a: pl.BlockSpec(a.shape, lambda i: (0, 0))
    return pl.pallas_call(
        _out_proj_kernel,
        grid=(T // tm,),
        in_specs=[
            pl.BlockSpec((tm, W), lambda i: (i, 0)),
            pl.BlockSpec((tm, W), lambda i: (i, 0)),
            pl.BlockSpec((tm, W), lambda i: (i, 4)),
            pl.BlockSpec((tm, y_mla.shape[1]), lambda i: (i, 0)),
            pl.BlockSpec((tm, D), lambda i: (i, 0)),
            full2(hg_gain), full2(w_out), full2(ffn_gain), full2(w_pq),
            pl.BlockSpec(keys.shape, lambda i: (0, 0, 0)),
        ],
        out_specs=[
            pl.BlockSpec((tm, D), lambda i: (i, 0)),
            pl.BlockSpec((tm, D), lambda i: (i, 0)),
            pl.BlockSpec((npc, PEER_NKEYS, tm), lambda i: (0, 0, i)),
        ],
        out_shape=[
            jax.ShapeDtypeStruct((T, D), F32),
            jax.ShapeDtypeStruct((T, D), BF16),
            jax.ShapeDtypeStruct((npc, PEER_NKEYS, T), F32),
        ],
        compiler_params=_cparams(("parallel",)),
        name="out_proj_peer_query",
    )(o_f, o_b, hg, y_mla, x2, hg_gain, w_out, ffn_gain, w_pq, keys)


def _tree_max(xs):
    xs = list(xs)
    while len(xs) > 1:
        nxt = [jnp.maximum(xs[i], xs[i + 1]) for i in range(0, len(xs) - 1, 2)]
        if len(xs) % 2:
            nxt.append(xs[-1])
        xs = nxt
    return xs[0]


def _sorting_network(n):
    pairs = []
    p = 1
    while p < n:
        k = p
        while k >= 1:
            for j in range(k % p, n - k, 2 * k):
                for i in range(min(k, n - j - k)):
                    if (i + j) // (2 * p) == (i + j + k) // (2 * p):
                        pairs.append((i + j, i + j + k))
            k //= 2
        p *= 2
    return pairs


def _bitonic_merge_network(n):
    pairs = []
    stride = n // 2
    while stride >= 1:
        pairs += [(i, i + stride) for i in range(n) if not i & stride]
        stride //= 2
    return pairs


def _compare_exchange(xs, pairs):
    xs = list(xs)
    for i, j in pairs:
        xs[i], xs[j] = jnp.maximum(xs[i], xs[j]), jnp.minimum(xs[i], xs[j])
    return xs


def _peer_select_kernel(s_ref, iz_ref, cnt_ref, atop_ref, btop_ref, top_scr):
    K = PEER_TOPK
    n_vregs = PEER_NKEYS // SUBLANES
    assert n_vregs == K
    sort_pairs = _sorting_network(K)
    merge_pairs = _bitonic_merge_network(K)
    for p in range(PEER_HEADS):
        for c in range(2):
            lists = _compare_exchange(
                [s_ref[2 * p + c, pl.ds(g * SUBLANES, SUBLANES), :] for g in range(n_vregs)], sort_pairs)
            shift = SUBLANES // 2
            while shift >= 1:
                other = [pltpu.roll(x, shift, 0) for x in lists]
                lists = _compare_exchange(
                    [jnp.maximum(lists[r], other[K - 1 - r]) for r in range(K)], merge_pairs)
                shift //= 2
            for r in range(K):
                top_scr[c, r, pl.ds(p, 1), :] = lists[r][0:1, :]
    a = [top_scr[0, r] for r in range(K)]
    b = [top_scr[1, r] for r in range(K)]
    pairs = [(i, j) for i in range(K) for j in range(K) if (i + 1) * (j + 1) <= K]
    cands = [a[i] + b[j] for i, j in pairs]
    top = a[0] + b[0]
    z = jnp.zeros_like(top)
    kth = top
    for r in range(K):
        kth = _tree_max(cands)
        z = z + jnp.exp(kth - top)
        cands = [jnp.where(cv == kth, NEG_INF, cv) for cv in cands]
    iz_ref[...] = 1.0 / z
    for i in range(K):
        cnt = jnp.zeros_like(top)
        for j in range(K):
            if (i, j) in pairs:
                cnt = cnt + jnp.where(a[i] + b[j] >= kth, 1.0, 0.0)
        cnt_ref[i] = cnt
        atop_ref[i] = a[i]
        btop_ref[i] = b[i]


def _peer_select(scores_t, tt):
    npc, nk, T = scores_t.shape
    top = jax.ShapeDtypeStruct((PEER_TOPK, PEER_HEADS, T), F32)
    top_spec = pl.BlockSpec((PEER_TOPK, PEER_HEADS, tt), lambda i: (0, 0, i))
    return pl.pallas_call(
        _peer_select_kernel,
        grid=(T // tt,),
        in_specs=[pl.BlockSpec((npc, nk, tt), lambda i: (0, 0, i))],
        out_specs=[pl.BlockSpec((PEER_HEADS, tt), lambda i: (0, i))] + [top_spec] * 3,
        out_shape=[jax.ShapeDtypeStruct((PEER_HEADS, T), F32)] + [top] * 3,
        scratch_shapes=[pltpu.VMEM((2, PEER_TOPK, PEER_HEADS, tt), F32)],
        compiler_params=_cparams(("parallel",)),
        name="peer_select",
    )(scores_t)


def _gelu_exact_x2(x):
    return x * (1.0 + lax.erf(x * (1.0 / math.sqrt(2.0))))


def _peer_dense_kernel(h_ref, u_ref, vt_ref, s_ref, iz_ref, cnt_ref, atop_ref, btop_ref, x1_ref, o_ref,
                       n_scr, e1_scr, rank_scr, e2_scr, a_scr, p_scr, yt_scr):
    jt = pl.program_id(1)
    te, tt = a_scr.shape
    nb = te // PEER_NKEYS
    n_grp = nb // SUBLANES
    PK = SUBLANES * (4 // jnp.dtype(GATE_DTYPE).itemsize)
    IBQ = 32 * PK // PEER_NKEYS

    @pl.when(jt == 0)
    def _():
        yt_scr[...] = jnp.zeros_like(yt_scr)
        grouped = (PEER_NKEYS // SUBLANES, SUBLANES, LANES)
        for p in range(PEER_HEADS):
            for st in range(tt // LANES):
                sl = pl.ds(st * LANES, LANES)
                s1 = s_ref[2 * p, :, sl]
                s2 = s_ref[2 * p + 1, :, sl]
                n = jnp.zeros_like(s1)
                rank = jnp.zeros_like(s2)
                for r in range(PEER_TOPK):
                    n = jnp.where(s1 == atop_ref[r, pl.ds(p, 1), sl], cnt_ref[r, pl.ds(p, 1), sl], n)
                    rank = jnp.where(btop_ref[r, pl.ds(p, 1), sl] > s2, r + 1.0, rank)
                n_scr[p, :, :, sl] = n.reshape(grouped)
                e1_scr[p, :, :, sl] = jnp.exp(s1 - atop_ref[0, pl.ds(p, 1), sl]).reshape(grouped)
                rank_scr[p, :, sl] = rank.astype(GATE_DTYPE)
                m2 = btop_ref[0, pl.ds(p, 1), sl]
                e2_scr[p, :, sl] = (jnp.exp(s2 - m2) * (0.5 * iz_ref[pl.ds(p, 1), sl])).astype(GATE_DTYPE)

    a_scr[...] = _dot_nt(u_ref[...], h_ref[...])
    zero = jnp.zeros((PK, LANES), GATE_DTYPE)
    for ibg in range(nb // IBQ):
        for st in range(tt // LANES):
            sl = pl.ds(st * LANES, LANES)
            gates = [[zero for _ in range(PEER_NKEYS // PK)] for _ in range(IBQ)]
            for p in range(PEER_HEADS):
                rank = rank_scr[p, :, sl]
                e2 = e2_scr[p, :, sl]
                for q in range(IBQ):
                    ib = ibg * IBQ + q
                    grp = jt * n_grp + ib // SUBLANES
                    sub = pl.ds(ib % SUBLANES, 1)
                    n = jnp.broadcast_to(n_scr[p, grp, sub, sl], (PK, LANES)).astype(GATE_DTYPE)
                    e1 = jnp.broadcast_to(e1_scr[p, grp, sub, sl], (PK, LANES)).astype(GATE_DTYPE)
                    for r in range(PEER_NKEYS // PK):
                        rr = slice(r * PK, (r + 1) * PK)
                        gates[q][r] = gates[q][r] + jnp.where(rank[rr, :] < n, e2[rr, :], zero) * e1
            for q in range(IBQ):
                rows = pl.ds((ibg * IBQ + q) * PEER_NKEYS, PEER_NKEYS)
                gate = jnp.concatenate(gates[q], axis=0)
                p_scr[rows, sl] = (_gelu_exact_x2(a_scr[rows, sl]).astype(GATE_DTYPE) * gate).astype(BF16)
    yt_scr[...] += jnp.dot(vt_ref[...], p_scr[...], preferred_element_type=F32)

    @pl.when(jt == pl.num_programs(1) - 1)
    def _():
        o_ref[...] = x1_ref[...] + yt_scr[...].T


def _peer_dense(h2, u_bf, vt_bf, scores_t, iz, cnt, atop, btop, x1, tt, te):
    T, D = h2.shape
    E = u_bf.shape[0]
    npc, nk, _ = scores_t.shape
    assert te % (PEER_NKEYS * SUBLANES) == 0 and E % te == 0
    top = pl.BlockSpec((PEER_TOPK, PEER_HEADS, tt), lambda t, j: (0, 0, t))
    return pl.pallas_call(
        _peer_dense_kernel,
        grid=(T // tt, E // te),
        in_specs=[
            pl.BlockSpec((tt, D), lambda t, j: (t, 0)),
            pl.BlockSpec((te, D), lambda t, j: (j, 0)),
            pl.BlockSpec((D, te), lambda t, j: (0, j)),
            pl.BlockSpec((npc, nk, tt), lambda t, j: (0, 0, t)),
            pl.BlockSpec((PEER_HEADS, tt), lambda t, j: (0, t)),
            top, top, top,
            pl.BlockSpec((tt, D), lambda t, j: (t, 0)),
        ],
        out_specs=pl.BlockSpec((tt, D), lambda t, j: (t, 0)),
        out_shape=jax.ShapeDtypeStruct((T, D), F32),
        scratch_shapes=[
            pltpu.VMEM((PEER_HEADS, nk // SUBLANES, SUBLANES, tt), F32),
            pltpu.VMEM((PEER_HEADS, nk // SUBLANES, SUBLANES, tt), F32),
            pltpu.VMEM((PEER_HEADS, nk, tt), GATE_DTYPE),
            pltpu.VMEM((PEER_HEADS, nk, tt), GATE_DTYPE),
            pltpu.VMEM((te, tt), F32),
            pltpu.VMEM((te, tt), BF16),
            pltpu.VMEM((D, tt), F32),
        ],
        compiler_params=_cparams(("parallel", "arbitrary")),
        name="peer_dense",
    )(h2, u_bf, vt_bf, scores_t, iz, cnt, atop, btop, x1)


def _pick(n, pref):
    t = min(n, pref)
    assert n % t == 0, (n, t)
    return t


def _tiles(T, S):
    return {
        "in_proj_rows": _pick(T, 512),
        "scan_steps": _pick(S, 512),
        "mla_rows": _pick(T, 1024),
        "attn_queries": _pick(S, 1024),
        "out_proj_rows": _pick(T, 512),
        "select_tokens": _pick(T, 256),
        "dense_tokens": _pick(T, 512),
        "dense_experts": 2 * PEER_NKEYS * SUBLANES,
    }


def _layer(x2, pos, B, S, attn_norm, w_in, lb_logits, hg_o_norm, q_a_norm, w_q_up, kv_a_norm,
           w_kv_up, q_norm, k_norm, mla_o_norm, w_out, ffn_norm, peer_w_q, peer_sub_keys, peer_u, peer_v):
    T, D = x2.shape
    n_hg = 5 * HG_HEADS * HG_DK
    row = lambda a: a.reshape(1, -1).astype(F32)

    w_hg = w_in[:, :n_hg].astype(BF16)
    n_mla = w_in.shape[1] - n_hg
    w_mla = jnp.pad(w_in[:, n_hg:], ((0, 0), (0, -n_mla % LANES))).astype(BF16)
    pad_r = LANES - MLA_ROPE
    wq = w_q_up.reshape(MLA_Q_LORA, MLA_HEADS, MLA_QK)
    wq = jnp.pad(wq, ((0, 0), (0, 0), (0, MLA_QPAD - MLA_QK))).reshape(MLA_Q_LORA, MLA_HEADS * MLA_QPAD)
    wq = wq.astype(BF16)
    wkv = w_kv_up.astype(BF16)
    qgn, qgr = row(q_norm[:MLA_NOPE]), row(jnp.pad(q_norm[MLA_NOPE:], (0, pad_r)))
    kgn, kgr = row(k_norm[:MLA_NOPE]), row(jnp.pad(k_norm[MLA_NOPE:], (0, pad_r)))
    inv_freq = 1.0 / (ROPE_THETA ** (jnp.arange(0, MLA_ROPE, 2, dtype=F32) / MLA_ROPE))
    invf = row(jnp.concatenate([inv_freq, inv_freq, jnp.zeros((pad_r,), F32)]))
    keys = peer_sub_keys.reshape(2 * PEER_HEADS, PEER_NKEYS, PEER_DHALF).astype(BF16)
    u_bf = peer_u.astype(BF16)
    vt_bf = peer_v.astype(BF16).T

    tile = _tiles(T, S)
    hg, mla = _in_proj(x2, row(attn_norm), w_hg, w_mla, tile["in_proj_rows"])
    o_f = _hgrn_scan(hg, lb_logits[:, 0, :], B, S, False, tile["scan_steps"])
    o_b = _hgrn_scan(hg, lb_logits[:, 1, :], B, S, True, tile["scan_steps"])
    q, k, v = _mla_prep(mla, pos, invf, row(q_a_norm), row(kv_a_norm), wq, wkv, qgn, qgr, kgn, kgr,
                        tile["mla_rows"])
    y_mla = _attention(q, k, v, mla_o_norm.reshape(1, -1).astype(F32), B, S, tile["attn_queries"])
    x1, h2, scores_t = _out_proj(o_f, o_b, hg, y_mla, x2, row(hg_o_norm), w_out.astype(BF16),
                                 row(ffn_norm), peer_w_q.astype(BF16), keys, tile["out_proj_rows"])
    iz, cnt, atop, btop = _peer_select(scores_t, tile["select_tokens"])
    return _peer_dense(h2, u_bf, vt_bf, scores_t, iz, cnt, atop, btop, x1,
                       tile["dense_tokens"], tile["dense_experts"])


def kernel(x, positions, attn_norm, w_in, hg_lb_logits, hg_o_norm, q_a_norm, w_q_up, kv_a_norm, w_kv_up,
           q_norm, k_norm, mla_o_norm, w_out, ffn_norm, peer_w_q, peer_sub_keys, peer_u, peer_v):
    B, S, D = x.shape
    x2 = x.reshape(B * S, D)
    pos = positions.reshape(B * S, 1)
    for l in range(attn_norm.shape[0]):
        assert l == 0
        x2 = _layer(x2, pos, B, S, attn_norm[l], w_in[l], hg_lb_logits, hg_o_norm[l], q_a_norm[l],
                    w_q_up[l], kv_a_norm[l], w_kv_up[l], q_norm[l], k_norm[l], mla_o_norm[l], w_out[l],
                    ffn_norm[l], peer_w_q[l], peer_sub_keys[l], peer_u[l], peer_v[l])
    return x2.reshape(B, S, D)
```

```python
import functools
import math

import jax
import jax.numpy as jnp
from jax import lax
from jax.experimental import pallas as pl
from jax.experimental.pallas import tpu as pltpu

F32 = jnp.float32
BF16 = jnp.bfloat16
EPS = 1e-6
LANES = 128
SUBLANES = 8
NEG_INF = float("-inf")

HG_HEADS = 4
HG_DK = 128
HG_CHUNK = 64
HG_HALF = HG_CHUNK // 2
HG_QUARTER = HG_CHUNK // 4
HG_BATCH_PER_STEP = 4
MLA_HEADS = 4
MLA_Q_LORA = 384
MLA_KV_LORA = 256
MLA_NOPE = 128
MLA_ROPE = 64
MLA_V = 128
MLA_QK = MLA_NOPE + MLA_ROPE
MLA_QPAD = 2 * LANES
ROPE_THETA = 10000.0
ATTN_CHUNKS = 4
OUT_CHUNKS = 2
PEER_HEADS = 8
PEER_NKEYS = 128
PEER_TOPK = 16
PEER_DHALF = 128
GATE_DTYPE = BF16

VMEM_LIMIT = 56 * 1024 * 1024


def _cparams(sem):
    return pltpu.CompilerParams(dimension_semantics=sem, vmem_limit_bytes=VMEM_LIMIT)


def _rms(x, gain):
    ms = jnp.mean(x * x, axis=-1, keepdims=True)
    return x * lax.rsqrt(ms + EPS) * gain


def _sigmoid(x):
    return 1.0 / (1.0 + jnp.exp(-x))


def _dot_nt(a, b):
    return lax.dot_general(a, b, (((1,), (1,)), ((), ())), preferred_element_type=F32)


def _dot_tn(a, b):
    return lax.dot_general(a, b, (((0,), (0,)), ((), ())), preferred_element_type=F32)


def _in_proj_kernel(x_ref, g_ref, whg_ref, wmla_ref, hg_ref, mla_ref):
    h = _rms(x_ref[...], g_ref[...]).astype(BF16)
    hg_ref[...] = jnp.dot(h, whg_ref[...], preferred_element_type=F32)
    mla_ref[...] = jnp.dot(h, wmla_ref[...], preferred_element_type=F32)


def _in_proj(x2, gain, w_hg, w_mla, tm):
    T, D = x2.shape
    n_hg, n_mla = w_hg.shape[1], w_mla.shape[1]
    return pl.pallas_call(
        _in_proj_kernel,
        grid=(T // tm,),
        in_specs=[
            pl.BlockSpec((tm, D), lambda i: (i, 0)),
            pl.BlockSpec((1, D), lambda i: (0, 0)),
            pl.BlockSpec((D, n_hg), lambda i: (0, 0)),
            pl.BlockSpec((D, n_mla), lambda i: (0, 0)),
        ],
        out_specs=[
            pl.BlockSpec((tm, n_hg), lambda i: (i, 0)),
            pl.BlockSpec((tm, n_mla), lambda i: (i, 0)),
        ],
        out_shape=[jax.ShapeDtypeStruct((T, n_hg), F32), jax.ShapeDtypeStruct((T, n_mla), F32)],
        compiler_params=_cparams(("parallel",)),
        name="in_proj",
    )(x2, gain, w_hg, w_mla)


def _hgrn_kernel(q_ref, f_ref, v_ref, lb_ref, o_ref, st_ref, b_scr, k_scr, p_scr, *, reverse, n_chunks):
    C, Hf, Qt, G = HG_CHUNK, HG_HALF, HG_QUARTER, SUBLANES
    n_batch = q_ref.shape[0]
    n_heads = st_ref.shape[0] // n_batch
    chains = [(bi, hh) for bi in range(n_batch) for hh in range(n_heads)]

    @pl.when(pl.program_id(1) == 0)
    def _():
        st_ref[...] = jnp.zeros_like(st_ref)

    logits = lb_ref[...]
    ex = jnp.exp(logits - jnp.max(logits, axis=0, keepdims=True))
    lb_all = ex[0:1, :] / jnp.sum(ex, axis=0, keepdims=True)

    row = lax.broadcasted_iota(jnp.int32, (C, C), 0)
    col = lax.broadcasted_iota(jnp.int32, (C, C), 1)
    tri = (col >= row) if reverse else (col <= row)
    cum_mat = tri.astype(F32)
    same_half = (row >= Hf) == (col >= Hf)
    row_c = lax.broadcasted_iota(jnp.int32, (C, LANES), 0)
    q_side = (row_c < Hf) if reverse else (row_c >= Hf)
    odd_quarter = (row_c // Qt) % 2 == 1
    q_side4 = jnp.logical_not(odd_quarter) if reverse else odd_quarter
    low_half = row_c < Hf
    lane_g = lax.broadcasted_iota(jnp.int32, (G, LANES), 1)
    ones = jnp.ones((LANES, LANES), BF16)
    mid_row = Hf if reverse else Hf - 1
    qmid_rows = (Qt, Hf + Qt) if reverse else (Qt - 1, Hf + Qt - 1)
    end_row = 0 if reverse else C - 1

    def live_groups(s):
        return range(0, s // G + 1) if reverse else range(s // G, Qt // G)

    def gates(ci, r0):
        bi, hh = chains[ci]
        cols = slice(hh * HG_DK, (hh + 1) * HG_DK)
        lb = lb_all[:, cols]
        q_raw = q_ref[bi, pl.ds(r0, C), cols]
        q = q_raw * _sigmoid(q_raw)
        f = lb + (1.0 - lb) * _sigmoid(f_ref[bi, pl.ds(r0, C), cols])
        k = 1.0 - f
        vb = v_ref[bi, pl.ds(r0, C), cols].astype(BF16)
        b = jnp.dot(cum_mat, jnp.log2(f), precision=lax.Precision.HIGHEST,
                    preferred_element_type=F32)
        b_scr[ci] = b
        k_scr[ci] = k
        return q, k, vb, b

    def state_and_offdiag(ci, q, k, vb, b):
        b_mid = b_scr[ci, pl.ds(mid_row, 1), :]
        b_end = b_scr[ci, pl.ds(end_row, 1), :]
        st = st_ref[ci]
        o = _dot_nt((q * jnp.exp2(b)).astype(BF16), st.astype(BF16))
        kd = (k * jnp.exp2(b_end - b)).astype(BF16)
        st_ref[ci] = st * jnp.exp2(b_end) + _dot_tn(vb, kd)
        qt = jnp.where(q_side, q * jnp.exp2(b - b_mid), 0.0).astype(BF16)
        kt = jnp.where(q_side, 0.0, k * jnp.exp2(b_mid - b)).astype(BF16)
        b_q = jnp.where(low_half, b_scr[ci, pl.ds(qmid_rows[0], 1), :], b_scr[ci, pl.ds(qmid_rows[1], 1), :])
        qt4 = jnp.where(q_side4, q * jnp.exp2(b - b_q), 0.0).astype(BF16)
        kt4 = jnp.where(q_side4, 0.0, k * jnp.exp2(b_q - b)).astype(BF16)
        return o, _dot_nt(qt, kt), _dot_nt(qt4, kt4)

    def diag_products(ci, q, b):
        n_rows = 0
        for qb in range(C // Qt):
            base = qb * Qt
            for s in range(Qt):
                g0, g1 = live_groups(s)[0], live_groups(s)[-1] + 1
                rows = slice(base + g0 * G, base + g1 * G)
                bs = b_scr[ci, pl.ds(base + s, 1), :]
                ks = k_scr[ci, pl.ds(base + s, 1), :]
                p_scr[ci, pl.ds(n_rows, (g1 - g0) * G), :] = (q[rows, :] * ks) * jnp.exp2(b[rows, :] - bs)
                n_rows += (g1 - g0) * G

    def diag_scores(rs):
        n_rows = 0
        groups = []
        for qb in range(C // Qt):
            base = qb * Qt
            acc = [jnp.zeros((G, LANES), F32) for _ in range(Qt // G)]
            for s in range(Qt):
                for g in live_groups(s):
                    acc[g] = jnp.where(lane_g == base + s, rs[n_rows:n_rows + G, :], acc[g])
                    n_rows += G
            groups.extend(acc)
        return jnp.concatenate(groups, axis=0)[:, :C]

    def chunk_body(ci, carry):
        c = (n_chunks - 1 - ci) if reverse else ci
        r0 = pl.multiple_of(c * C, C)
        ids = range(len(chains))
        qkvb = [gates(ci, r0) for ci in ids]
        o_sc = [state_and_offdiag(ci, *qkvb[ci]) for ci in ids]
        for ci in ids:
            diag_products(ci, qkvb[ci][0], qkvb[ci][3])
        rs = [jnp.dot(p_scr[ci].astype(BF16), ones, preferred_element_type=F32) for ci in ids]
        for ci in ids:
            bi, hh = chains[ci]
            o, sc_half, sc_quarter = o_sc[ci]
            scores = jnp.where(same_half, jnp.where(tri, diag_scores(rs[ci]) + sc_quarter, 0.0), sc_half)
            o = o + jnp.dot(scores.astype(BF16), qkvb[ci][2], preferred_element_type=F32)
            o_ref[bi, pl.ds(r0, C), hh * HG_DK:(hh + 1) * HG_DK] = o
        return carry

    lax.fori_loop(0, n_chunks, chunk_body, 0)


def _hgrn_diag_rows():
    return (HG_CHUNK // HG_QUARTER) * sum(HG_QUARTER - (s // SUBLANES) * SUBLANES for s in range(HG_QUARTER))


def _hgrn_scan(hg, lb_logits_dir, B, S, reverse, tb):
    T = hg.shape[0]
    W = HG_HEADS * HG_DK
    nblk = S // tb
    f_blk = 2 if reverse else 1
    v_blk = 3
    nbat = HG_BATCH_PER_STEP if B % HG_BATCH_PER_STEP == 0 else 1
    n_chains = nbat * HG_HEADS
    hg4 = hg.reshape(B // nbat, nbat, S, hg.shape[1])

    def spec(col_blk):
        return pl.BlockSpec((None, nbat, tb, W),
                            lambda g, i: (g, 0, (nblk - 1 - i) if reverse else i, col_blk))

    kern = functools.partial(_hgrn_kernel, reverse=reverse, n_chunks=tb // HG_CHUNK)
    out = pl.pallas_call(
        kern,
        grid=(B // nbat, nblk),
        in_specs=[spec(0), spec(f_blk), spec(v_blk),
                  pl.BlockSpec((lb_logits_dir.shape[0], W), lambda g, i: (0, 0))],
        out_specs=spec(0),
        out_shape=jax.ShapeDtypeStruct((B // nbat, nbat, S, W), F32),
        scratch_shapes=[
            pltpu.VMEM((n_chains, HG_DK, HG_DK), F32),
            pltpu.VMEM((n_chains, HG_CHUNK, HG_DK), F32),
            pltpu.VMEM((n_chains, HG_CHUNK, HG_DK), F32),
            pltpu.VMEM((n_chains, _hgrn_diag_rows(), HG_DK), F32),
        ],
        compiler_params=_cparams(("parallel", "arbitrary")),
        name="hgrn_rev" if reverse else "hgrn_fwd",
    )(hg4, hg4, hg4, lb_logits_dir)
    return out.reshape(T, W)


def _mla_prep_kernel(mla_ref, pos_ref, invf_ref, qa_ref, kva_ref, wq_ref, wkv_ref,
                     qgn_ref, qgr_ref, kgn_ref, kgr_ref, q_ref, k_ref, v_ref):
    tm = mla_ref.shape[0]
    lane = lax.broadcasted_iota(jnp.int32, (tm, LANES), 1)
    ang = pos_ref[...].astype(F32) * invf_ref[...]
    cos_t = jnp.where(lane < MLA_ROPE, jnp.cos(ang), 0.0)
    sin_a = jnp.sin(ang)
    sin_t = jnp.where(lane < MLA_ROPE // 2, -sin_a, jnp.where(lane < MLA_ROPE, sin_a, 0.0))

    def rope(t):
        swapped = jnp.where(lane < MLA_ROPE // 2,
                            pltpu.roll(t, LANES - MLA_ROPE // 2, 1),
                            pltpu.roll(t, MLA_ROPE // 2, 1))
        return t * cos_t + swapped * sin_t

    def norm_rope_part(t, gain):
        ms = jnp.sum(t * t, axis=-1, keepdims=True) * (1.0 / MLA_ROPE)
        return t * lax.rsqrt(ms + EPS) * gain

    cq = mla_ref[:, 0:MLA_Q_LORA]
    ckv = mla_ref[:, MLA_Q_LORA:MLA_Q_LORA + MLA_KV_LORA]
    kr = mla_ref[:, MLA_Q_LORA + MLA_KV_LORA:]
    q_up = jnp.dot(_rms(cq, qa_ref[...]).astype(BF16), wq_ref[...], preferred_element_type=F32)
    kv_up = jnp.dot(_rms(ckv, kva_ref[...]).astype(BF16), wkv_ref[...], preferred_element_type=F32)
    k_rope = rope(norm_rope_part(kr, kgr_ref[...])).astype(BF16)
    for h in range(MLA_HEADS):
        c0 = h * MLA_QPAD
        q_ref[:, c0:c0 + LANES] = _rms(q_up[:, c0:c0 + LANES], qgn_ref[...]).astype(BF16)
        q_ref[:, c0 + LANES:c0 + 2 * LANES] = rope(
            norm_rope_part(q_up[:, c0 + LANES:c0 + 2 * LANES], qgr_ref[...])).astype(BF16)
        k_ref[:, c0:c0 + LANES] = _rms(kv_up[:, c0:c0 + LANES], kgn_ref[...]).astype(BF16)
        k_ref[:, c0 + LANES:c0 + 2 * LANES] = k_rope
        v_ref[:, h * MLA_V:(h + 1) * MLA_V] = kv_up[:, c0 + LANES:c0 + 2 * LANES].astype(BF16)


def _mla_prep(mla, pos, invf, qa, kva, wq, wkv, qgn, qgr, kgn, kgr, tm):
    T, n_mla = mla.shape
    full = lambda a: pl.BlockSpec(a.shape, lambda i: (0, 0))
    return pl.pallas_call(
        _mla_prep_kernel,
        grid=(T // tm,),
        in_specs=[
            pl.BlockSpec((tm, n_mla), lambda i: (i, 0)),
            pl.BlockSpec((tm, 1), lambda i: (i, 0)),
            full(invf), full(qa), full(kva), full(wq), full(wkv),
            full(qgn), full(qgr), full(kgn), full(kgr),
        ],
        out_specs=[
            pl.BlockSpec((tm, MLA_HEADS * MLA_QPAD), lambda i: (i, 0)),
            pl.BlockSpec((tm, MLA_HEADS * MLA_QPAD), lambda i: (i, 0)),
            pl.BlockSpec((tm, MLA_HEADS * MLA_V), lambda i: (i, 0)),
        ],
        out_shape=[
            jax.ShapeDtypeStruct((T, MLA_HEADS * MLA_QPAD), BF16),
            jax.ShapeDtypeStruct((T, MLA_HEADS * MLA_QPAD), BF16),
            jax.ShapeDtypeStruct((T, MLA_HEADS * MLA_V), BF16),
        ],
        compiler_params=_cparams(("parallel",)),
        name="mla_prep",
    )(mla, pos, invf, qa, kva, wq, wkv, qgn, qgr, kgn, kgr)


def _attn_kernel(q_ref, k_ref, v_ref, g_ref, o_ref):
    c = MLA_QK ** -0.5 * math.log2(math.e)
    tq = q_ref.shape[0]
    rc = tq // ATTN_CHUNKS
    rows = [pl.ds(i * rc, rc) for i in range(ATTN_CHUNKS)]
    s = [_dot_nt(q_ref[r, :], k_ref[...]) for r in rows]
    p = [jnp.exp2((si - jnp.max(si, axis=-1, keepdims=True)) * c) for si in s]
    l = [jnp.sum(pi, axis=-1, keepdims=True) for pi in p]
    o = [jnp.dot(pi.astype(BF16), v_ref[...], preferred_element_type=F32) for pi in p]
    for r, oi, li in zip(rows, o, l):
        o_ref[r, :] = _rms(oi / li, g_ref[...]).astype(o_ref.dtype)


def _attention(q, k, v, o_gain, B, S, tq):
    T = q.shape[0]
    nq = S // tq
    return pl.pallas_call(
        _attn_kernel,
        grid=(B, MLA_HEADS, nq),
        in_specs=[
            pl.BlockSpec((tq, MLA_QPAD), lambda b, h, i: (b * nq + i, h)),
            pl.BlockSpec((S, MLA_QPAD), lambda b, h, i: (b, h)),
            pl.BlockSpec((S, MLA_V), lambda b, h, i: (b, h)),
            pl.BlockSpec((1, MLA_V), lambda b, h, i: (0, h)),
        ],
        out_specs=pl.BlockSpec((tq, MLA_V), lambda b, h, i: (b * nq + i, h)),
        out_shape=jax.ShapeDtypeStruct((T, MLA_HEADS * MLA_V), BF16),
        compiler_params=_cparams(("parallel", "parallel", "arbitrary")),
        name="mla_attention",
    )(q, k, v, o_gain)


def _out_proj_kernel(of_ref, ob_ref, g_ref, ym_ref, x_ref, hgg_ref, wo_ref, fg_ref, wq_ref, keys_ref,
                     x1_ref, h2_ref, st_ref):
    tm = x_ref.shape[0]
    rc = tm // OUT_CHUNKS
    rows = [pl.ds(i * rc, rc) for i in range(OUT_CHUNKS)]
    w_hg_rows = of_ref.shape[1]

    def mixer_out(r):
        o = of_ref[r, :] + ob_ref[r, :]
        gate_raw = g_ref[r, :]
        gate = gate_raw * _sigmoid(gate_raw)
        parts = []
        for h in range(HG_HEADS):
            sl = slice(h * HG_DK, (h + 1) * HG_DK)
            parts.append(_rms(o[:, sl], hgg_ref[:, sl]) * gate[:, sl])
        return jnp.concatenate(parts, axis=-1).astype(BF16)

    y_hg = [mixer_out(r) for r in rows]
    mix = [jnp.dot(y, wo_ref[0:w_hg_rows, :], preferred_element_type=F32)
           + jnp.dot(ym_ref[r, :], wo_ref[w_hg_rows:, :], preferred_element_type=F32)
           for y, r in zip(y_hg, rows)]
    x1 = [x_ref[r, :] + m for r, m in zip(rows, mix)]
    h2 = [_rms(x, fg_ref[...]).astype(BF16) for x in x1]
    for r, x, h in zip(rows, x1, h2):
        x1_ref[r, :] = x
        h2_ref[r, :] = h
    pq = [jnp.dot(h, wq_ref[...], preferred_element_type=F32).astype(BF16) for h in h2]
    pq = jnp.concatenate(pq, axis=0)
    for pc in range(keys_ref.shape[0]):
        st_ref[pc] = _dot_nt(keys_ref[pc], pq[:, pc * PEER_DHALF:(pc + 1) * PEER_DHALF])


def _out_proj(o_f, o_b, hg, y_mla, x2, hg_gain, w_out, ffn_gain, w_pq, keys, tm):
    T, D = x2.shape
    W = o_f.shape[1]
    npc = keys.shape[0]
    full2 = lambda a: pl.BlockSpec(a.shape, lambda i: (0, 0))
    return pl.pallas_call(
        _out_proj_kernel,
        grid=(T // tm,),
        in_specs=[
            pl.BlockSpec((tm, W), lambda i: (i, 0)),
            pl.BlockSpec((tm, W), lambda i: (i, 0)),
            pl.BlockSpec((tm, W), lambda i: (i, 4)),
            pl.BlockSpec((tm, y_mla.shape[1]), lambda i: (i, 0)),
            pl.BlockSpec((tm, D), lambda i: (i, 0)),
            full2(hg_gain), full2(w_out), full2(ffn_gain), full2(w_pq),
            pl.BlockSpec(keys.shape, lambda i: (0, 0, 0)),
        ],
        out_specs=[
            pl.BlockSpec((tm, D), lambda i: (i, 0)),
            pl.BlockSpec((tm, D), lambda i: (i, 0)),
            pl.BlockSpec((npc, PEER_NKEYS, tm), lambda i: (0, 0, i)),
        ],
        out_shape=[
            jax.ShapeDtypeStruct((T, D), F32),
            jax.ShapeDtypeStruct((T, D), BF16),
            jax.ShapeDtypeStruct((npc, PEER_NKEYS, T), F32),
        ],
        compiler_params=_cparams(("parallel",)),
        name="out_proj_peer_query",
    )(o_f, o_b, hg, y_mla, x2, hg_gain, w_out, ffn_gain, w_pq, keys)


def _tree_max(xs):
    xs = list(xs)
    while len(xs) > 1:
        nxt = [jnp.maximum(xs[i], xs[i + 1]) for i in range(0, len(xs) - 1, 2)]
        if len(xs) % 2:
            nxt.append(xs[-1])
        xs = nxt
    return xs[0]


def _sorting_network(n):
    pairs = []
    p = 1
    while p < n:
        k = p
        while k >= 1:
            for j in range(k % p, n - k, 2 * k):
                for i in range(min(k, n - j - k)):
                    if (i + j) // (2 * p) == (i + j + k) // (2 * p):
                        pairs.append((i + j, i + j + k))
            k //= 2
        p *= 2
    return pairs


def _bitonic_merge_network(n):
    pairs = []
    stride = n // 2
    while stride >= 1:
        pairs += [(i, i + stride) for i in range(n) if not i & stride]
        stride //= 2
    return pairs


def _compare_exchange(xs, pairs):
    xs = list(xs)
    for i, j in pairs:
        xs[i], xs[j] = jnp.maximum(xs[i], xs[j]), jnp.minimum(xs[i], xs[j])
    return xs


def _peer_select_kernel(s_ref, iz_ref, cnt_ref, atop_ref, btop_ref, top_scr):
    K = PEER_TOPK
    n_vregs = PEER_NKEYS // SUBLANES
    assert n_vregs == K
    sort_pairs = _sorting_network(K)
    merge_pairs = _bitonic_merge_network(K)
    for p in range(PEER_HEADS):
        for c in range(2):
            lists = _compare_exchange(
                [s_ref[2 * p + c, pl.ds(g * SUBLANES, SUBLANES), :] for g in range(n_vregs)], sort_pairs)
            shift = SUBLANES // 2
            while shift >= 1:
                other = [pltpu.roll(x, shift, 0) for x in lists]
                lists = _compare_exchange(
                    [jnp.maximum(lists[r], other[K - 1 - r]) for r in range(K)], merge_pairs)
                shift //= 2
            for r in range(K):
                top_scr[c, r, pl.ds(p, 1), :] = lists[r][0:1, :]
    a = [top_scr[0, r] for r in range(K)]
    b = [top_scr[1, r] for r in range(K)]
    pairs = [(i, j) for i in range(K) for j in range(K) if (i + 1) * (j + 1) <= K]
    cands = [a[i] + b[j] for i, j in pairs]
    top = a[0] + b[0]
    z = jnp.zeros_like(top)
    kth = top
    for r in range(K):
        kth = _tree_max(cands)
        z = z + jnp.exp(kth - top)
        cands = [jnp.where(cv == kth, NEG_INF, cv) for cv in cands]
    iz_ref[...] = 1.0 / z
    for i in range(K):
        cnt = jnp.zeros_like(top)
        for j in range(K):
            if (i, j) in pairs:
                cnt = cnt + jnp.where(a[i] + b[j] >= kth, 1.0, 0.0)
        cnt_ref[i] = cnt
        atop_ref[i] = a[i]
        btop_ref[i] = b[i]


def _peer_select(scores_t, tt):
    npc, nk, T = scores_t.shape
    top = jax.ShapeDtypeStruct((PEER_TOPK, PEER_HEADS, T), F32)
    top_spec = pl.BlockSpec((PEER_TOPK, PEER_HEADS, tt), lambda i: (0, 0, i))
    return pl.pallas_call(
        _peer_select_kernel,
        grid=(T // tt,),
        in_specs=[pl.BlockSpec((npc, nk, tt), lambda i: (0, 0, i))],
        out_specs=[pl.BlockSpec((PEER_HEADS, tt), lambda i: (0, i))] + [top_spec] * 3,
        out_shape=[jax.ShapeDtypeStruct((PEER_HEADS, T), F32)] + [top] * 3,
        scratch_shapes=[pltpu.VMEM((2, PEER_TOPK, PEER_HEADS, tt), F32)],
        compiler_params=_cparams(("parallel",)),
        name="peer_select",
    )(scores_t)


def _gelu_exact_x2(x):
    return x * (1.0 + lax.erf(x * (1.0 / math.sqrt(2.0))))


def _peer_dense_kernel(h_ref, u_ref, vt_ref, s_ref, iz_ref, cnt_ref, atop_ref, btop_ref, x1_ref, o_ref,
                       n_scr, e1_scr, rank_scr, e2_scr, a_scr, p_scr, yt_scr):
    jt = pl.program_id(1)
    te, tt = a_scr.shape
    nb = te // PEER_NKEYS
    n_grp = nb // SUBLANES
    PK = SUBLANES * (4 // jnp.dtype(GATE_DTYPE).itemsize)
    IBQ = 32 * PK // PEER_NKEYS

    @pl.when(jt == 0)
    def _():
        yt_scr[...] = jnp.zeros_like(yt_scr)
        grouped = (PEER_NKEYS // SUBLANES, SUBLANES, LANES)
        for p in range(PEER_HEADS):
            for st in range(tt // LANES):
                sl = pl.ds(st * LANES, LANES)
                s1 = s_ref[2 * p, :, sl]
                s2 = s_ref[2 * p + 1, :, sl]
                n = jnp.zeros_like(s1)
                rank = jnp.zeros_like(s2)
                for r in range(PEER_TOPK):
                    n = jnp.where(s1 == atop_ref[r, pl.ds(p, 1), sl], cnt_ref[r, pl.ds(p, 1), sl], n)
                    rank = jnp.where(btop_ref[r, pl.ds(p, 1), sl] > s2, r + 1.0, rank)
                n_scr[p, :, :, sl] = n.reshape(grouped)
                e1_scr[p, :, :, sl] = jnp.exp(s1 - atop_ref[0, pl.ds(p, 1), sl]).reshape(grouped)
                rank_scr[p, :, sl] = rank.astype(GATE_DTYPE)
                m2 = btop_ref[0, pl.ds(p, 1), sl]
                e2_scr[p, :, sl] = (jnp.exp(s2 - m2) * (0.5 * iz_ref[pl.ds(p, 1), sl])).astype(GATE_DTYPE)

    a_scr[...] = _dot_nt(u_ref[...], h_ref[...])
    zero = jnp.zeros((PK, LANES), GATE_DTYPE)
    for ibg in range(nb // IBQ):
        for st in range(tt // LANES):
            sl = pl.ds(st * LANES, LANES)
            gates = [[zero for _ in range(PEER_NKEYS // PK)] for _ in range(IBQ)]
            for p in range(PEER_HEADS):
                rank = rank_scr[p, :, sl]
                e2 = e2_scr[p, :, sl]
                for q in range(IBQ):
                    ib = ibg * IBQ + q
                    grp = jt * n_grp + ib // SUBLANES
                    sub = pl.ds(ib % SUBLANES, 1)
                    n = jnp.broadcast_to(n_scr[p, grp, sub, sl], (PK, LANES)).astype(GATE_DTYPE)
                    e1 = jnp.broadcast_to(e1_scr[p, grp, sub, sl], (PK, LANES)).astype(GATE_DTYPE)
                    for r in range(PEER_NKEYS // PK):
                        rr = slice(r * PK, (r + 1) * PK)
                        gates[q][r] = gates[q][r] + jnp.where(rank[rr, :] < n, e2[rr, :], zero) * e1
            for q in range(IBQ):
                rows = pl.ds((ibg * IBQ + q) * PEER_NKEYS, PEER_NKEYS)
                gate = jnp.concatenate(gates[q], axis=0)
                p_scr[rows, sl] = (_gelu_exact_x2(a_scr[rows, sl]).astype(GATE_DTYPE) * gate).astype(BF16)
    yt_scr[...] += jnp.dot(vt_ref[...], p_scr[...], preferred_element_type=F32)

    @pl.when(jt == pl.num_programs(1) - 1)
    def _():
        o_ref[...] = x1_ref[...] + yt_scr[...].T


def _peer_dense(h2, u_bf, vt_bf, scores_t, iz, cnt, atop, btop, x1, tt, te):
    T, D = h2.shape
    E = u_bf.shape[0]
    npc, nk, _ = scores_t.shape
    assert te % (PEER_NKEYS * SUBLANES) == 0 and E % te == 0
    top = pl.BlockSpec((PEER_TOPK, PEER_HEADS, tt), lambda t, j: (0, 0, t))
    return pl.pallas_call(
        _peer_dense_kernel,
        grid=(T // tt, E // te),
        in_specs=[
            pl.BlockSpec((tt, D), lambda t, j: (t, 0)),
            pl.BlockSpec((te, D), lambda t, j: (j, 0)),
            pl.BlockSpec((D, te), lambda t, j: (0, j)),
            pl.BlockSpec((npc, nk, tt), lambda t, j: (0, 0, t)),
            pl.BlockSpec((PEER_HEADS, tt), lambda t, j: (0, t)),
            top, top, top,
            pl.BlockSpec((tt, D), lambda t, j: (t, 0)),
        ],
        out_specs=pl.BlockSpec((tt, D), lambda t, j: (t, 0)),
        out_shape=jax.ShapeDtypeStruct((T, D), F32),
        scratch_shapes=[
            pltpu.VMEM((PEER_HEADS, nk // SUBLANES, SUBLANES, tt), F32),
            pltpu.VMEM((PEER_HEADS, nk // SUBLANES, SUBLANES, tt), F32),
            pltpu.VMEM((PEER_HEADS, nk, tt), GATE_DTYPE),
            pltpu.VMEM((PEER_HEADS, nk, tt), GATE_DTYPE),
            pltpu.VMEM((te, tt), F32),
            pltpu.VMEM((te, tt), BF16),
            pltpu.VMEM((D, tt), F32),
        ],
        compiler_params=_cparams(("parallel", "arbitrary")),
        name="peer_dense",
    )(h2, u_bf, vt_bf, scores_t, iz, cnt, atop, btop, x1)


def _transpose_cast_kernel(x_ref, o_ref):
    o_ref[...] = x_ref[...].T.astype(o_ref.dtype)


def _transpose_cast(x, dtype, rows):
    E, D = x.shape
    return pl.pallas_call(
        _transpose_cast_kernel,
        grid=(E // rows,),
        in_specs=[pl.BlockSpec((rows, D), lambda i: (i, 0))],
        out_specs=pl.BlockSpec((D, rows), lambda i: (0, i)),
        out_shape=jax.ShapeDtypeStruct((D, E), dtype),
        compiler_params=_cparams(("parallel",)),
        name="value_table_layout",
    )(x)


def _pick(n, pref):
    t = min(n, pref)
    assert n % t == 0, (n, t)
    return t


def _tiles(T, S):
    return {
        "in_proj_rows": _pick(T, 512),
        "scan_steps": _pick(S, 512),
        "mla_rows": _pick(T, 1024),
        "attn_queries": _pick(S, 1024),
        "out_proj_rows": _pick(T, 512),
        "select_tokens": _pick(T, 256),
        "dense_tokens": _pick(T, 512),
        "dense_experts": 2 * PEER_NKEYS * SUBLANES,
    }


def _layer(x2, pos, B, S, attn_norm, w_in, lb_logits, hg_o_norm, q_a_norm, w_q_up, kv_a_norm,
           w_kv_up, q_norm, k_norm, mla_o_norm, w_out, ffn_norm, peer_w_q, peer_sub_keys, peer_u, peer_v):
    T, D = x2.shape
    n_hg = 5 * HG_HEADS * HG_DK
    row = lambda a: a.reshape(1, -1).astype(F32)

    w_hg = w_in[:, :n_hg].astype(BF16)
    n_mla = w_in.shape[1] - n_hg
    w_mla = jnp.pad(w_in[:, n_hg:], ((0, 0), (0, -n_mla % LANES))).astype(BF16)
    pad_r = LANES - MLA_ROPE
    wq = w_q_up.reshape(MLA_Q_LORA, MLA_HEADS, MLA_QK)
    wq = jnp.pad(wq, ((0, 0), (0, 0), (0, MLA_QPAD - MLA_QK))).reshape(MLA_Q_LORA, MLA_HEADS * MLA_QPAD)
    wq = wq.astype(BF16)
    wkv = w_kv_up.astype(BF16)
    qgn, qgr = row(q_norm[:MLA_NOPE]), row(jnp.pad(q_norm[MLA_NOPE:], (0, pad_r)))
    kgn, kgr = row(k_norm[:MLA_NOPE]), row(jnp.pad(k_norm[MLA_NOPE:], (0, pad_r)))
    inv_freq = 1.0 / (ROPE_THETA ** (jnp.arange(0, MLA_ROPE, 2, dtype=F32) / MLA_ROPE))
    invf = row(jnp.concatenate([inv_freq, inv_freq, jnp.zeros((pad_r,), F32)]))
    keys = peer_sub_keys.reshape(2 * PEER_HEADS, PEER_NKEYS, PEER_DHALF).astype(BF16)
    u_bf = peer_u.astype(BF16)
    vt_bf = _transpose_cast(peer_v, BF16, _pick(peer_v.shape[0], 1024))

    tile = _tiles(T, S)
    hg, mla = _in_proj(x2, row(attn_norm), w_hg, w_mla, tile["in_proj_rows"])
    o_f = _hgrn_scan(hg, lb_logits[:, 0, :], B, S, False, tile["scan_steps"])
    o_b = _hgrn_scan(hg, lb_logits[:, 1, :], B, S, True, tile["scan_steps"])
    q, k, v = _mla_prep(mla, pos, invf, row(q_a_norm), row(kv_a_norm), wq, wkv, qgn, qgr, kgn, kgr,
                        tile["mla_rows"])
    y_mla = _attention(q, k, v, mla_o_norm.reshape(1, -1).astype(F32), B, S, tile["attn_queries"])
    x1, h2, scores_t = _out_proj(o_f, o_b, hg, y_mla, x2, row(hg_o_norm), w_out.astype(BF16),
                                 row(ffn_norm), peer_w_q.astype(BF16), keys, tile["out_proj_rows"])
    iz, cnt, atop, btop = _peer_select(scores_t, tile["select_tokens"])
    return _peer_dense(h2, u_bf, vt_bf, scores_t, iz, cnt, atop, btop, x1,
                       tile["dense_tokens"], tile["dense_experts"])


def kernel(x, positions, attn_norm, w_in, hg_lb_logits, hg_o_norm, q_a_norm, w_q_up, kv_a_norm, w_kv_up,
           q_norm, k_norm, mla_o_norm, w_out, ffn_norm, peer_w_q, peer_sub_keys, peer_u, peer_v):
    B, S, D = x.shape
    x2 = x.reshape(B * S, D)
    pos = positions.reshape(B * S, 1)
    for l in range(attn_norm.shape[0]):
        assert l == 0
        x2 = _layer(x2, pos, B, S, attn_norm[l], w_in[l], hg_lb_logits, hg_o_norm[l], q_a_norm[l],
                    w_q_up[l], kv_a_norm[l], w_kv_up[l], q_norm[l], k_norm[l], mla_o_norm[l], w_out[l],
                    ffn_norm[l], peer_w_q[l], peer_sub_keys[l], peer_u[l], peer_v[l])
    return x2.reshape(B, S, D)
```

```python
import functools
import math

import jax
import jax.numpy as jnp
from jax import lax
from jax.experimental import pallas as pl
from jax.experimental.pallas import tpu as pltpu

F32 = jnp.float32
BF16 = jnp.bfloat16
EPS = 1e-6
LANES = 128
SUBLANES = 8
NEG_INF = float("-inf")

HG_HEADS = 4
HG_DK = 128
HG_CHUNK = 64
HG_HALF = HG_CHUNK // 2
HG_QUARTER = HG_CHUNK // 4
HG_BATCH_PER_STEP = 4
MLA_HEADS = 4
MLA_Q_LORA = 384
MLA_KV_LORA = 256
MLA_NOPE = 128
MLA_ROPE = 64
MLA_V = 128
MLA_QK = MLA_NOPE + MLA_ROPE
MLA_QPAD = 2 * LANES
ROPE_THETA = 10000.0
ATTN_CHUNKS = 4
OUT_CHUNKS = 2
PEER_HEADS = 8
PEER_NKEYS = 128
PEER_TOPK = 16
PEER_DHALF = 128
GATE_DTYPE = BF16
DENSE_SUBTILE = 1024

VMEM_LIMIT = 56 * 1024 * 1024


def _cparams(sem):
    return pltpu.CompilerParams(dimension_semantics=sem, vmem_limit_bytes=VMEM_LIMIT)


def _rms(x, gain):
    ms = jnp.mean(x * x, axis=-1, keepdims=True)
    return x * lax.rsqrt(ms + EPS) * gain


def _sigmoid(x):
    return 1.0 / (1.0 + jnp.exp(-x))


def _dot_nt(a, b):
    return lax.dot_general(a, b, (((1,), (1,)), ((), ())), preferred_element_type=F32)


def _dot_tn(a, b):
    return lax.dot_general(a, b, (((0,), (0,)), ((), ())), preferred_element_type=F32)


def _in_proj_kernel(x_ref, g_ref, whg_ref, wmla_ref, hg_ref, mla_ref):
    h = _rms(x_ref[...], g_ref[...]).astype(BF16)
    hg_ref[...] = jnp.dot(h, whg_ref[...], preferred_element_type=F32)
    mla_ref[...] = jnp.dot(h, wmla_ref[...], preferred_element_type=F32)


def _in_proj(x2, gain, w_hg, w_mla, tm):
    T, D = x2.shape
    n_hg, n_mla = w_hg.shape[1], w_mla.shape[1]
    return pl.pallas_call(
        _in_proj_kernel,
        grid=(T // tm,),
        in_specs=[
            pl.BlockSpec((tm, D), lambda i: (i, 0)),
            pl.BlockSpec((1, D), lambda i: (0, 0)),
            pl.BlockSpec((D, n_hg), lambda i: (0, 0)),
            pl.BlockSpec((D, n_mla), lambda i: (0, 0)),
        ],
        out_specs=[
            pl.BlockSpec((tm, n_hg), lambda i: (i, 0)),
            pl.BlockSpec((tm, n_mla), lambda i: (i, 0)),
        ],
        out_shape=[jax.ShapeDtypeStruct((T, n_hg), F32), jax.ShapeDtypeStruct((T, n_mla), F32)],
        compiler_params=_cparams(("parallel",)),
        name="in_proj",
    )(x2, gain, w_hg, w_mla)


def _hgrn_kernel(q_ref, f_ref, v_ref, lb_ref, o_ref, st_ref, b_scr, k_scr, p_scr, *, reverse, n_chunks):
    C, Hf, Qt, G = HG_CHUNK, HG_HALF, HG_QUARTER, SUBLANES
    n_batch = q_ref.shape[0]
    n_heads = st_ref.shape[0] // n_batch
    chains = [(bi, hh) for bi in range(n_batch) for hh in range(n_heads)]

    @pl.when(pl.program_id(1) == 0)
    def _():
        st_ref[...] = jnp.zeros_like(st_ref)

    logits = lb_ref[...]
    ex = jnp.exp(logits - jnp.max(logits, axis=0, keepdims=True))
    lb_all = ex[0:1, :] / jnp.sum(ex, axis=0, keepdims=True)

    row = lax.broadcasted_iota(jnp.int32, (C, C), 0)
    col = lax.broadcasted_iota(jnp.int32, (C, C), 1)
    tri = (col >= row) if reverse else (col <= row)
    cum_mat = tri.astype(F32)
    same_half = (row >= Hf) == (col >= Hf)
    row_c = lax.broadcasted_iota(jnp.int32, (C, LANES), 0)
    q_side = (row_c < Hf) if reverse else (row_c >= Hf)
    odd_quarter = (row_c // Qt) % 2 == 1
    q_side4 = jnp.logical_not(odd_quarter) if reverse else odd_quarter
    low_half = row_c < Hf
    lane_g = lax.broadcasted_iota(jnp.int32, (G, LANES), 1)
    ones = jnp.ones((LANES, LANES), BF16)
    mid_row = Hf if reverse else Hf - 1
    qmid_rows = (Qt, Hf + Qt) if reverse else (Qt - 1, Hf + Qt - 1)
    end_row = 0 if reverse else C - 1

    def live_groups(s):
        return range(0, s // G + 1) if reverse else range(s // G, Qt // G)

    def gates(ci, r0):
        bi, hh = chains[ci]
        cols = slice(hh * HG_DK, (hh + 1) * HG_DK)
        lb = lb_all[:, cols]
        q_raw = q_ref[bi, pl.ds(r0, C), cols]
        q = q_raw * _sigmoid(q_raw)
        f = lb + (1.0 - lb) * _sigmoid(f_ref[bi, pl.ds(r0, C), cols])
        k = 1.0 - f
        vb = v_ref[bi, pl.ds(r0, C), cols].astype(BF16)
        b = jnp.dot(cum_mat, jnp.log2(f), precision=lax.Precision.HIGHEST,
                    preferred_element_type=F32)
        b_scr[ci] = b
        k_scr[ci] = k
        return q, k, vb, b

    def state_and_offdiag(ci, q, k, vb, b):
        b_mid = b_scr[ci, pl.ds(mid_row, 1), :]
        b_end = b_scr[ci, pl.ds(end_row, 1), :]
        st = st_ref[ci]
        o = _dot_nt((q * jnp.exp2(b)).astype(BF16), st.astype(BF16))
        kd = (k * jnp.exp2(b_end - b)).astype(BF16)
        st_ref[ci] = st * jnp.exp2(b_end) + _dot_tn(vb, kd)
        qt = jnp.where(q_side, q * jnp.exp2(b - b_mid), 0.0).astype(BF16)
        kt = jnp.where(q_side, 0.0, k * jnp.exp2(b_mid - b)).astype(BF16)
        b_q = jnp.where(low_half, b_scr[ci, pl.ds(qmid_rows[0], 1), :], b_scr[ci, pl.ds(qmid_rows[1], 1), :])
        qt4 = jnp.where(q_side4, q * jnp.exp2(b - b_q), 0.0).astype(BF16)
        kt4 = jnp.where(q_side4, 0.0, k * jnp.exp2(b_q - b)).astype(BF16)
        return o, _dot_nt(qt, kt), _dot_nt(qt4, kt4)

    def diag_products(ci, q, b):
        n_rows = 0
        for qb in range(C // Qt):
            base = qb * Qt
            for s in range(Qt):
                g0, g1 = live_groups(s)[0], live_groups(s)[-1] + 1
                rows = slice(base + g0 * G, base + g1 * G)
                bs = b_scr[ci, pl.ds(base + s, 1), :]
                ks = k_scr[ci, pl.ds(base + s, 1), :]
                p_scr[ci, pl.ds(n_rows, (g1 - g0) * G), :] = (q[rows, :] * ks) * jnp.exp2(b[rows, :] - bs)
                n_rows += (g1 - g0) * G

    def diag_scores(rs):
        n_rows = 0
        groups = []
        for qb in range(C // Qt):
            base = qb * Qt
            acc = [jnp.zeros((G, LANES), F32) for _ in range(Qt // G)]
            for s in range(Qt):
                for g in live_groups(s):
                    acc[g] = jnp.where(lane_g == base + s, rs[n_rows:n_rows + G, :], acc[g])
                    n_rows += G
            groups.extend(acc)
        return jnp.concatenate(groups, axis=0)[:, :C]

    def chunk_body(ci, carry):
        c = (n_chunks - 1 - ci) if reverse else ci
        r0 = pl.multiple_of(c * C, C)
        ids = range(len(chains))
        qkvb = [gates(ci, r0) for ci in ids]
        o_sc = [state_and_offdiag(ci, *qkvb[ci]) for ci in ids]
        for ci in ids:
            diag_products(ci, qkvb[ci][0], qkvb[ci][3])
        rs = [jnp.dot(p_scr[ci].astype(BF16), ones, preferred_element_type=F32) for ci in ids]
        for ci in ids:
            bi, hh = chains[ci]
            o, sc_half, sc_quarter = o_sc[ci]
            scores = jnp.where(same_half, jnp.where(tri, diag_scores(rs[ci]) + sc_quarter, 0.0), sc_half)
            o = o + jnp.dot(scores.astype(BF16), qkvb[ci][2], preferred_element_type=F32)
            o_ref[bi, pl.ds(r0, C), hh * HG_DK:(hh + 1) * HG_DK] = o
        return carry

    lax.fori_loop(0, n_chunks, chunk_body, 0)


def _hgrn_diag_rows():
    return (HG_CHUNK // HG_QUARTER) * sum(HG_QUARTER - (s // SUBLANES) * SUBLANES for s in range(HG_QUARTER))


def _hgrn_scan(hg, lb_logits_dir, B, S, reverse, tb):
    T = hg.shape[0]
    W = HG_HEADS * HG_DK
    nblk = S // tb
    f_blk = 2 if reverse else 1
    v_blk = 3
    nbat = HG_BATCH_PER_STEP if B % HG_BATCH_PER_STEP == 0 else 1
    n_chains = nbat * HG_HEADS
    hg4 = hg.reshape(B // nbat, nbat, S, hg.shape[1])

    def spec(col_blk):
        return pl.BlockSpec((None, nbat, tb, W),
                            lambda g, i: (g, 0, (nblk - 1 - i) if reverse else i, col_blk))

    kern = functools.partial(_hgrn_kernel, reverse=reverse, n_chunks=tb // HG_CHUNK)
    out = pl.pallas_call(
        kern,
        grid=(B // nbat, nblk),
        in_specs=[spec(0), spec(f_blk), spec(v_blk),
                  pl.BlockSpec((lb_logits_dir.shape[0], W), lambda g, i: (0, 0))],
        out_specs=spec(0),
        out_shape=jax.ShapeDtypeStruct((B // nbat, nbat, S, W), F32),
        scratch_shapes=[
            pltpu.VMEM((n_chains, HG_DK, HG_DK), F32),
            pltpu.VMEM((n_chains, HG_CHUNK, HG_DK), F32),
            pltpu.VMEM((n_chains, HG_CHUNK, HG_DK), F32),
            pltpu.VMEM((n_chains, _hgrn_diag_rows(), HG_DK), F32),
        ],
        compiler_params=_cparams(("parallel", "arbitrary")),
        name="hgrn_rev" if reverse else "hgrn_fwd",
    )(hg4, hg4, hg4, lb_logits_dir)
    return out.reshape(T, W)


def _mla_prep_kernel(mla_ref, pos_ref, invf_ref, qa_ref, kva_ref, wq_ref, wkv_ref,
                     qgn_ref, qgr_ref, kgn_ref, kgr_ref, q_ref, k_ref, v_ref):
    tm = mla_ref.shape[0]
    lane = lax.broadcasted_iota(jnp.int32, (tm, LANES), 1)
    ang = pos_ref[...].astype(F32) * invf_ref[...]
    cos_t = jnp.where(lane < MLA_ROPE, jnp.cos(ang), 0.0)
    sin_a = jnp.sin(ang)
    sin_t = jnp.where(lane < MLA_ROPE // 2, -sin_a, jnp.where(lane < MLA_ROPE, sin_a, 0.0))

    def rope(t):
        swapped = jnp.where(lane < MLA_ROPE // 2,
                            pltpu.roll(t, LANES - MLA_ROPE // 2, 1),
                            pltpu.roll(t, MLA_ROPE // 2, 1))
        return t * cos_t + swapped * sin_t

    def norm_rope_part(t, gain):
        ms = jnp.sum(t * t, axis=-1, keepdims=True) * (1.0 / MLA_ROPE)
        return t * lax.rsqrt(ms + EPS) * gain

    cq = mla_ref[:, 0:MLA_Q_LORA]
    ckv = mla_ref[:, MLA_Q_LORA:MLA_Q_LORA + MLA_KV_LORA]
    kr = mla_ref[:, MLA_Q_LORA + MLA_KV_LORA:]
    q_up = jnp.dot(_rms(cq, qa_ref[...]).astype(BF16), wq_ref[...], preferred_element_type=F32)
    kv_up = jnp.dot(_rms(ckv, kva_ref[...]).astype(BF16), wkv_ref[...], preferred_element_type=F32)
    k_rope = rope(norm_rope_part(kr, kgr_ref[...])).astype(BF16)
    for h in range(MLA_HEADS):
        c0 = h * MLA_QPAD
        q_ref[:, c0:c0 + LANES] = _rms(q_up[:, c0:c0 + LANES], qgn_ref[...]).astype(BF16)
        q_ref[:, c0 + LANES:c0 + 2 * LANES] = rope(
            norm_rope_part(q_up[:, c0 + LANES:c0 + 2 * LANES], qgr_ref[...])).astype(BF16)
        k_ref[:, c0:c0 + LANES] = _rms(kv_up[:, c0:c0 + LANES], kgn_ref[...]).astype(BF16)
        k_ref[:, c0 + LANES:c0 + 2 * LANES] = k_rope
        v_ref[:, h * MLA_V:(h + 1) * MLA_V] = kv_up[:, c0 + LANES:c0 + 2 * LANES].astype(BF16)


def _mla_prep(mla, pos, invf, qa, kva, wq, wkv, qgn, qgr, kgn, kgr, tm):
    T, n_mla = mla.shape
    full = lambda a: pl.BlockSpec(a.shape, lambda i: (0, 0))
    return pl.pallas_call(
        _mla_prep_kernel,
        grid=(T // tm,),
        in_specs=[
            pl.BlockSpec((tm, n_mla), lambda i: (i, 0)),
            pl.BlockSpec((tm, 1), lambda i: (i, 0)),
            full(invf), full(qa), full(kva), full(wq), full(wkv),
            full(qgn), full(qgr), full(kgn), full(kgr),
        ],
        out_specs=[
            pl.BlockSpec((tm, MLA_HEADS * MLA_QPAD), lambda i: (i, 0)),
            pl.BlockSpec((tm, MLA_HEADS * MLA_QPAD), lambda i: (i, 0)),
            pl.BlockSpec((tm, MLA_HEADS * MLA_V), lambda i: (i, 0)),
        ],
        out_shape=[
            jax.ShapeDtypeStruct((T, MLA_HEADS * MLA_QPAD), BF16),
            jax.ShapeDtypeStruct((T, MLA_HEADS * MLA_QPAD), BF16),
            jax.ShapeDtypeStruct((T, MLA_HEADS * MLA_V), BF16),
        ],
        compiler_params=_cparams(("parallel",)),
        name="mla_prep",
    )(mla, pos, invf, qa, kva, wq, wkv, qgn, qgr, kgn, kgr)


def _attn_kernel(q_ref, k_ref, v_ref, g_ref, o_ref):
    c = MLA_QK ** -0.5 * math.log2(math.e)
    tq = q_ref.shape[0]
    rc = tq // ATTN_CHUNKS
    rows = [pl.ds(i * rc, rc) for i in range(ATTN_CHUNKS)]
    s = [_dot_nt(q_ref[r, :], k_ref[...]) for r in rows]
    p = [jnp.exp2((si - jnp.max(si, axis=-1, keepdims=True)) * c) for si in s]
    l = [jnp.sum(pi, axis=-1, keepdims=True) for pi in p]
    o = [jnp.dot(pi.astype(BF16), v_ref[...], preferred_element_type=F32) for pi in p]
    for r, oi, li in zip(rows, o, l):
        o_ref[r, :] = _rms(oi / li, g_ref[...]).astype(o_ref.dtype)


def _attention(q, k, v, o_gain, B, S, tq):
    T = q.shape[0]
    nq = S // tq
    return pl.pallas_call(
        _attn_kernel,
        grid=(B, MLA_HEADS, nq),
        in_specs=[
            pl.BlockSpec((tq, MLA_QPAD), lambda b, h, i: (b * nq + i, h)),
            pl.BlockSpec((S, MLA_QPAD), lambda b, h, i: (b, h)),
            pl.BlockSpec((S, MLA_V), lambda b, h, i: (b, h)),
            pl.BlockSpec((1, MLA_V), lambda b, h, i: (0, h)),
        ],
        out_specs=pl.BlockSpec((tq, MLA_V), lambda b, h, i: (b * nq + i, h)),
        out_shape=jax.ShapeDtypeStruct((T, MLA_HEADS * MLA_V), BF16),
        compiler_params=_cparams(("parallel", "parallel", "arbitrary")),
        name="mla_attention",
    )(q, k, v, o_gain)


def _out_proj_kernel(of_ref, ob_ref, g_ref, ym_ref, x_ref, hgg_ref, wo_ref, fg_ref, wq_ref, keys_ref,
                     x1_ref, h2_ref, st_ref):
    tm = x_ref.shape[0]
    rc = tm // OUT_CHUNKS
    rows = [pl.ds(i * rc, rc) for i in range(OUT_CHUNKS)]
    w_hg_rows = of_ref.shape[1]

    def mixer_out(r):
        o = of_ref[r, :] + ob_ref[r, :]
        gate_raw = g_ref[r, :]
        gate = gate_raw * _sigmoid(gate_raw)
        parts = []
        for h in range(HG_HEADS):
            sl = slice(h * HG_DK, (h + 1) * HG_DK)
            parts.append(_rms(o[:, sl], hgg_ref[:, sl]) * gate[:, sl])
        return jnp.concatenate(parts, axis=-1).astype(BF16)

    y_hg = [mixer_out(r) for r in rows]
    mix = [jnp.dot(y, wo_ref[0:w_hg_rows, :], preferred_element_type=F32)
           + jnp.dot(ym_ref[r, :], wo_ref[w_hg_rows:, :], preferred_element_type=F32)
           for y, r in zip(y_hg, rows)]
    x1 = [x_ref[r, :] + m for r, m in zip(rows, mix)]
    h2 = [_rms(x, fg_ref[...]).astype(BF16) for x in x1]
    for r, x, h in zip(rows, x1, h2):
        x1_ref[r, :] = x
        h2_ref[r, :] = h
    pq = [jnp.dot(h, wq_ref[...], preferred_element_type=F32).astype(BF16) for h in h2]
    pq = jnp.concatenate(pq, axis=0)
    for pc in range(keys_ref.shape[0]):
        st_ref[pc] = _dot_nt(keys_ref[pc], pq[:, pc * PEER_DHALF:(pc + 1) * PEER_DHALF])


def _out_proj(o_f, o_b, hg, y_mla, x2, hg_gain, w_out, ffn_gain, w_pq, keys, tm):
    T, D = x2.shape
    W = o_f.shape[1]
    npc = keys.shape[0]
    full2 = lambda a: pl.BlockSpec(a.shape, lambda i: (0, 0))
    return pl.pallas_call(
        _out_proj_kernel,
        grid=(T // tm,),
        in_specs=[
            pl.BlockSpec((tm, W), lambda i: (i, 0)),
            pl.BlockSpec((tm, W), lambda i: (i, 0)),
            pl.BlockSpec((tm, W), lambda i: (i, 4)),
            pl.BlockSpec((tm, y_mla.shape[1]), lambda i: (i, 0)),
            pl.BlockSpec((tm, D), lambda i: (i, 0)),
            full2(hg_gain), full2(w_out), full2(ffn_gain), full2(w_pq),
            pl.BlockSpec(keys.shape, lambda i: (0, 0, 0)),
        ],
        out_specs=[
            pl.BlockSpec((tm, D), lambda i: (i, 0)),
            pl.BlockSpec((tm, D), lambda i: (i, 0)),
            pl.BlockSpec((npc, PEER_NKEYS, tm), lambda i: (0, 0, i)),
        ],
        out_shape=[
            jax.ShapeDtypeStruct((T, D), F32),
            jax.ShapeDtypeStruct((T, D), BF16),
            jax.ShapeDtypeStruct((npc, PEER_NKEYS, T), F32),
        ],
        compiler_params=_cparams(("parallel",)),
        name="out_proj_peer_query",
    )(o_f, o_b, hg, y_mla, x2, hg_gain, w_out, ffn_gain, w_pq, keys)


def _tree_max(xs):
    xs = list(xs)
    while len(xs) > 1:
        nxt = [jnp.maximum(xs[i], xs[i + 1]) for i in range(0, len(xs) - 1, 2)]
        if len(xs) % 2:
            nxt.append(xs[-1])
        xs = nxt
    return xs[0]


def _sorting_network(n):
    pairs = []
    p = 1
    while p < n:
        k = p
        while k >= 1:
            for j in range(k % p, n - k, 2 * k):
                for i in range(min(k, n - j - k)):
                    if (i + j) // (2 * p) == (i + j + k) // (2 * p):
                        pairs.append((i + j, i + j + k))
            k //= 2
        p *= 2
    return pairs


def _bitonic_merge_network(n):
    pairs = []
    stride = n // 2
    while stride >= 1:
        pairs += [(i, i + stride) for i in range(n) if not i & stride]
        stride //= 2
    return pairs


def _compare_exchange(xs, pairs):
    xs = list(xs)
    for i, j in pairs:
        xs[i], xs[j] = jnp.maximum(xs[i], xs[j]), jnp.minimum(xs[i], xs[j])
    return xs


def _peer_select_kernel(s_ref, iz_ref, cnt_ref, atop_ref, btop_ref, top_scr):
    K = PEER_TOPK
    n_vregs = PEER_NKEYS // SUBLANES
    assert n_vregs == K
    sort_pairs = _sorting_network(K)
    merge_pairs = _bitonic_merge_network(K)
    for p in range(PEER_HEADS):
        for c in range(2):
            lists = _compare_exchange(
                [s_ref[2 * p + c, pl.ds(g * SUBLANES, SUBLANES), :] for g in range(n_vregs)], sort_pairs)
            shift = SUBLANES // 2
            while shift >= 1:
                other = [pltpu.roll(x, shift, 0) for x in lists]
                lists = _compare_exchange(
                    [jnp.maximum(lists[r], other[K - 1 - r]) for r in range(K)], merge_pairs)
                shift //= 2
            for r in range(K):
                top_scr[c, r, pl.ds(p, 1), :] = lists[r][0:1, :]
    a = [top_scr[0, r] for r in range(K)]
    b = [top_scr[1, r] for r in range(K)]
    pairs = [(i, j) for i in range(K) for j in range(K) if (i + 1) * (j + 1) <= K]
    cands = [a[i] + b[j] for i, j in pairs]
    top = a[0] + b[0]
    z = jnp.zeros_like(top)
    kth = top
    for r in range(K):
        kth = _tree_max(cands)
        z = z + jnp.exp(kth - top)
        cands = [jnp.where(cv == kth, NEG_INF, cv) for cv in cands]
    iz_ref[...] = 1.0 / z
    for i in range(K):
        cnt = jnp.zeros_like(top)
        for j in range(K):
            if (i, j) in pairs:
                cnt = cnt + jnp.where(a[i] + b[j] >= kth, 1.0, 0.0)
        cnt_ref[i] = cnt
        atop_ref[i] = a[i]
        btop_ref[i] = b[i]


def _peer_select(scores_t, tt):
    npc, nk, T = scores_t.shape
    top = jax.ShapeDtypeStruct((PEER_TOPK, PEER_HEADS, T), F32)
    top_spec = pl.BlockSpec((PEER_TOPK, PEER_HEADS, tt), lambda i: (0, 0, i))
    return pl.pallas_call(
        _peer_select_kernel,
        grid=(T // tt,),
        in_specs=[pl.BlockSpec((npc, nk, tt), lambda i: (0, 0, i))],
        out_specs=[pl.BlockSpec((PEER_HEADS, tt), lambda i: (0, i))] + [top_spec] * 3,
        out_shape=[jax.ShapeDtypeStruct((PEER_HEADS, T), F32)] + [top] * 3,
        scratch_shapes=[pltpu.VMEM((2, PEER_TOPK, PEER_HEADS, tt), F32)],
        compiler_params=_cparams(("parallel",)),
        name="peer_select",
    )(scores_t)


def _gelu_exact_x2(x):
    return x * (1.0 + lax.erf(x * (1.0 / math.sqrt(2.0))))


def _peer_dense_kernel(h_ref, u_ref, vt_ref, s_ref, iz_ref, cnt_ref, atop_ref, btop_ref, x1_ref, o_ref,
                       n_scr, e1_scr, rank_scr, e2_scr, a_scr, p_scr, yt_scr):
    jt = pl.program_id(1)
    te, tt = a_scr.shape
    nb = te // PEER_NKEYS
    n_grp = nb // SUBLANES
    PK = SUBLANES * (4 // jnp.dtype(GATE_DTYPE).itemsize)
    IBQ = 32 * PK // PEER_NKEYS

    @pl.when(jt == 0)
    def _():
        yt_scr[...] = jnp.zeros_like(yt_scr)
        grouped = (PEER_NKEYS // SUBLANES, SUBLANES, LANES)
        for p in range(PEER_HEADS):
            for st in range(tt // LANES):
                sl = pl.ds(st * LANES, LANES)
                s1 = s_ref[2 * p, :, sl]
                s2 = s_ref[2 * p + 1, :, sl]
                n = jnp.zeros_like(s1)
                rank = jnp.zeros_like(s2)
                for r in range(PEER_TOPK):
                    n = jnp.where(s1 == atop_ref[r, pl.ds(p, 1), sl], cnt_ref[r, pl.ds(p, 1), sl], n)
                    rank = jnp.where(btop_ref[r, pl.ds(p, 1), sl] > s2, r + 1.0, rank)
                n_scr[p, :, :, sl] = n.reshape(grouped)
                e1_scr[p, :, :, sl] = jnp.exp(s1 - atop_ref[0, pl.ds(p, 1), sl]).reshape(grouped)
                rank_scr[p, :, sl] = rank.astype(GATE_DTYPE)
                m2 = btop_ref[0, pl.ds(p, 1), sl]
                e2_scr[p, :, sl] = (jnp.exp(s2 - m2) * (0.5 * iz_ref[pl.ds(p, 1), sl])).astype(GATE_DTYPE)

    zero = jnp.zeros((PK, LANES), GATE_DTYPE)
    for sub_tile in range(te // DENSE_SUBTILE):
        erows = pl.ds(sub_tile * DENSE_SUBTILE, DENSE_SUBTILE)
        a_scr[erows, :] = _dot_nt(u_ref[erows, :], h_ref[...])
        for ibg in range(sub_tile * DENSE_SUBTILE // (IBQ * PEER_NKEYS),
                         (sub_tile + 1) * DENSE_SUBTILE // (IBQ * PEER_NKEYS)):
            for st in range(tt // LANES):
                sl = pl.ds(st * LANES, LANES)
                gates = [[zero for _ in range(PEER_NKEYS // PK)] for _ in range(IBQ)]
                for p in range(PEER_HEADS):
                    rank = rank_scr[p, :, sl]
                    e2 = e2_scr[p, :, sl]
                    for q in range(IBQ):
                        ib = ibg * IBQ + q
                        grp = jt * n_grp + ib // SUBLANES
                        sub = pl.ds(ib % SUBLANES, 1)
                        n = jnp.broadcast_to(n_scr[p, grp, sub, sl], (PK, LANES)).astype(GATE_DTYPE)
                        e1 = jnp.broadcast_to(e1_scr[p, grp, sub, sl], (PK, LANES)).astype(GATE_DTYPE)
                        for r in range(PEER_NKEYS // PK):
                            rr = slice(r * PK, (r + 1) * PK)
                            gates[q][r] = gates[q][r] + jnp.where(rank[rr, :] < n, e2[rr, :], zero) * e1
                for q in range(IBQ):
                    rows = pl.ds((ibg * IBQ + q) * PEER_NKEYS, PEER_NKEYS)
                    gate = jnp.concatenate(gates[q], axis=0)
                    p_scr[rows, sl] = (_gelu_exact_x2(a_scr[rows, sl]).astype(GATE_DTYPE) * gate).astype(BF16)
        yt_scr[...] += jnp.dot(vt_ref[:, erows], p_scr[erows, :], preferred_element_type=F32)

    @pl.when(jt == pl.num_programs(1) - 1)
    def _():
        o_ref[...] = x1_ref[...] + yt_scr[...].T


def _peer_dense(h2, u_bf, vt_bf, scores_t, iz, cnt, atop, btop, x1, tt, te):
    T, D = h2.shape
    E = u_bf.shape[0]
    npc, nk, _ = scores_t.shape
    assert te % (PEER_NKEYS * SUBLANES) == 0 and E % te == 0
    top = pl.BlockSpec((PEER_TOPK, PEER_HEADS, tt), lambda t, j: (0, 0, t))
    return pl.pallas_call(
        _peer_dense_kernel,
        grid=(T // tt, E // te),
        in_specs=[
            pl.BlockSpec((tt, D), lambda t, j: (t, 0)),
            pl.BlockSpec((te, D), lambda t, j: (j, 0)),
            pl.BlockSpec((D, te), lambda t, j: (0, j)),
            pl.BlockSpec((npc, nk, tt), lambda t, j: (0, 0, t)),
            pl.BlockSpec((PEER_HEADS, tt), lambda t, j: (0, t)),
            top, top, top,
            pl.BlockSpec((tt, D), lambda t, j: (t, 0)),
        ],
        out_specs=pl.BlockSpec((tt, D), lambda t, j: (t, 0)),
        out_shape=jax.ShapeDtypeStruct((T, D), F32),
        scratch_shapes=[
            pltpu.VMEM((PEER_HEADS, nk // SUBLANES, SUBLANES, tt), F32),
            pltpu.VMEM((PEER_HEADS, nk // SUBLANES, SUBLANES, tt), F32),
            pltpu.VMEM((PEER_HEADS, nk, tt), GATE_DTYPE),
            pltpu.VMEM((PEER_HEADS, nk, tt), GATE_DTYPE),
            pltpu.VMEM((te, tt), F32),
            pltpu.VMEM((te, tt), BF16),
            pltpu.VMEM((D, tt), F32),
        ],
        compiler_params=_cparams(("parallel", "arbitrary")),
        name="peer_dense",
    )(h2, u_bf, vt_bf, scores_t, iz, cnt, atop, btop, x1)


def _transpose_cast_kernel(x_ref, o_ref):
    o_ref[...] = x_ref[...].T.astype(o_ref.dtype)


def _transpose_cast(x, dtype, rows):
    E, D = x.shape
    return pl.pallas_call(
        _transpose_cast_kernel,
        grid=(E // rows,),
        in_specs=[pl.BlockSpec((rows, D), lambda i: (i, 0))],
        out_specs=pl.BlockSpec((D, rows), lambda i: (0, i)),
        out_shape=jax.ShapeDtypeStruct((D, E), dtype),
        compiler_params=_cparams(("parallel",)),
        name="value_table_layout",
    )(x)


def _pick(n, pref):
    t = min(n, pref)
    assert n % t == 0, (n, t)
    return t


def _tiles(T, S):
    return {
        "in_proj_rows": _pick(T, 512),
        "scan_steps": _pick(S, 512),
        "mla_rows": _pick(T, 1024),
        "attn_queries": _pick(S, 1024),
        "out_proj_rows": _pick(T, 512),
        "select_tokens": _pick(T, 256),
        "dense_tokens": _pick(T, 512),
        "dense_experts": 2 * PEER_NKEYS * SUBLANES,
    }


def _layer(x2, pos, B, S, attn_norm, w_in, lb_logits, hg_o_norm, q_a_norm, w_q_up, kv_a_norm,
           w_kv_up, q_norm, k_norm, mla_o_norm, w_out, ffn_norm, peer_w_q, peer_sub_keys, peer_u, peer_v):
    T, D = x2.shape
    n_hg = 5 * HG_HEADS * HG_DK
    row = lambda a: a.reshape(1, -1).astype(F32)

    w_hg = w_in[:, :n_hg].astype(BF16)
    n_mla = w_in.shape[1] - n_hg
    w_mla = jnp.pad(w_in[:, n_hg:], ((0, 0), (0, -n_mla % LANES))).astype(BF16)
    pad_r = LANES - MLA_ROPE
    wq = w_q_up.reshape(MLA_Q_LORA, MLA_HEADS, MLA_QK)
    wq = jnp.pad(wq, ((0, 0), (0, 0), (0, MLA_QPAD - MLA_QK))).reshape(MLA_Q_LORA, MLA_HEADS * MLA_QPAD)
    wq = wq.astype(BF16)
    wkv = w_kv_up.astype(BF16)
    qgn, qgr = row(q_norm[:MLA_NOPE]), row(jnp.pad(q_norm[MLA_NOPE:], (0, pad_r)))
    kgn, kgr = row(k_norm[:MLA_NOPE]), row(jnp.pad(k_norm[MLA_NOPE:], (0, pad_r)))
    inv_freq = 1.0 / (ROPE_THETA ** (jnp.arange(0, MLA_ROPE, 2, dtype=F32) / MLA_ROPE))
    invf = row(jnp.concatenate([inv_freq, inv_freq, jnp.zeros((pad_r,), F32)]))
    keys = peer_sub_keys.reshape(2 * PEER_HEADS, PEER_NKEYS, PEER_DHALF).astype(BF16)
    u_bf = peer_u.astype(BF16)
    vt_bf = _transpose_cast(peer_v, BF16, _pick(peer_v.shape[0], 1024))

    tile = _tiles(T, S)
    hg, mla = _in_proj(x2, row(attn_norm), w_hg, w_mla, tile["in_proj_rows"])
    o_f = _hgrn_scan(hg, lb_logits[:, 0, :], B, S, False, tile["scan_steps"])
    o_b = _hgrn_scan(hg, lb_logits[:, 1, :], B, S, True, tile["scan_steps"])
    q, k, v = _mla_prep(mla, pos, invf, row(q_a_norm), row(kv_a_norm), wq, wkv, qgn, qgr, kgn, kgr,
                        tile["mla_rows"])
    y_mla = _attention(q, k, v, mla_o_norm.reshape(1, -1).astype(F32), B, S, tile["attn_queries"])
    x1, h2, scores_t = _out_proj(o_f, o_b, hg, y_mla, x2, row(hg_o_norm), w_out.astype(BF16),
                                 row(ffn_norm), peer_w_q.astype(BF16), keys, tile["out_proj_rows"])
    iz, cnt, atop, btop = _peer_select(scores_t, tile["select_tokens"])
    return _peer_dense(h2, u_bf, vt_bf, scores_t, iz, cnt, atop, btop, x1,
                       tile["dense_tokens"], tile["dense_experts"])


def kernel(x, positions, attn_norm, w_in, hg_lb_logits, hg_o_norm, q_a_norm, w_q_up, kv_a_norm, w_kv_up,
           q_norm, k_norm, mla_o_norm, w_out, ffn_norm, peer_w_q, peer_sub_keys, peer_u, peer_v):
    B, S, D = x.shape
    x2 = x.reshape(B * S, D)
    pos = positions.reshape(B * S, 1)
    for l in range(attn_norm.shape[0]):
        assert l == 0
        x2 = _layer(x2, pos, B, S, attn_norm[l], w_in[l], hg_lb_logits, hg_o_norm[l], q_a_norm[l],
                    w_q_up[l], kv_a_norm[l], w_kv_up[l], q_norm[l], k_norm[l], mla_o_norm[l], w_out[l],
                    ffn_norm[l], peer_w_q[l], peer_sub_keys[l], peer_u[l], peer_v[l])
    return x2.reshape(B, S, D)
```

```python
import functools
import math

import jax
import jax.numpy as jnp
from jax import lax
from jax.experimental import pallas as pl
from jax.experimental.pallas import tpu as pltpu

F32 = jnp.float32
BF16 = jnp.bfloat16
EPS = 1e-6
LANES = 128
SUBLANES = 8
NEG_INF = float("-inf")

HG_HEADS = 4
HG_DK = 128
HG_CHUNK = 64
HG_HALF = HG_CHUNK // 2
HG_QUARTER = HG_CHUNK // 4
HG_BATCH_PER_STEP = 4
MLA_HEADS = 4
MLA_Q_LORA = 384
MLA_KV_LORA = 256
MLA_NOPE = 128
MLA_ROPE = 64
MLA_V = 128
MLA_QK = MLA_NOPE + MLA_ROPE
MLA_QPAD = 2 * LANES
ROPE_THETA = 10000.0
ATTN_CHUNKS = 4
OUT_CHUNKS = 2
PEER_HEADS = 8
PEER_NKEYS = 128
PEER_TOPK = 16
PEER_DHALF = 128
GATE_DTYPE = BF16
DENSE_SUBTILE = 1024

VMEM_LIMIT = 56 * 1024 * 1024


def _cparams(sem):
    return pltpu.CompilerParams(dimension_semantics=sem, vmem_limit_bytes=VMEM_LIMIT)


def _rms(x, gain):
    ms = jnp.mean(x * x, axis=-1, keepdims=True)
    return x * lax.rsqrt(ms + EPS) * gain


def _sigmoid(x):
    return 1.0 / (1.0 + jnp.exp(-x))


def _dot_nt(a, b):
    return lax.dot_general(a, b, (((1,), (1,)), ((), ())), preferred_element_type=F32)


def _dot_tn(a, b):
    return lax.dot_general(a, b, (((0,), (0,)), ((), ())), preferred_element_type=F32)


def _in_proj_kernel(x_ref, g_ref, whg_ref, wmla_ref, hg_ref, mla_ref):
    h = _rms(x_ref[...], g_ref[...]).astype(BF16)
    hg_ref[...] = jnp.dot(h, whg_ref[...], preferred_element_type=F32)
    mla_ref[...] = jnp.dot(h, wmla_ref[...], preferred_element_type=F32)


def _in_proj(x2, gain, w_hg, w_mla, tm):
    T, D = x2.shape
    n_hg, n_mla = w_hg.shape[1], w_mla.shape[1]
    return pl.pallas_call(
        _in_proj_kernel,
        grid=(T // tm,),
        in_specs=[
            pl.BlockSpec((tm, D), lambda i: (i, 0)),
            pl.BlockSpec((1, D), lambda i: (0, 0)),
            pl.BlockSpec((D, n_hg), lambda i: (0, 0)),
            pl.BlockSpec((D, n_mla), lambda i: (0, 0)),
        ],
        out_specs=[
            pl.BlockSpec((tm, n_hg), lambda i: (i, 0)),
            pl.BlockSpec((tm, n_mla), lambda i: (i, 0)),
        ],
        out_shape=[jax.ShapeDtypeStruct((T, n_hg), F32), jax.ShapeDtypeStruct((T, n_mla), F32)],
        compiler_params=_cparams(("parallel",)),
        name="in_proj",
    )(x2, gain, w_hg, w_mla)


def _hgrn_kernel(q_ref, f_ref, v_ref, lb_ref, o_ref, st_ref, b_scr, k_scr, p_scr, *, reverse, n_chunks):
    C, Hf, Qt, G = HG_CHUNK, HG_HALF, HG_QUARTER, SUBLANES
    Et = G
    assert C == 8 * Et
    n_batch = q_ref.shape[0]
    n_heads = st_ref.shape[0] // n_batch
    chains = [(bi, hh) for bi in range(n_batch) for hh in range(n_heads)]

    @pl.when(pl.program_id(1) == 0)
    def _():
        st_ref[...] = jnp.zeros_like(st_ref)

    logits = lb_ref[...]
    ex = jnp.exp(logits - jnp.max(logits, axis=0, keepdims=True))
    lb_all = ex[0:1, :] / jnp.sum(ex, axis=0, keepdims=True)

    row = lax.broadcasted_iota(jnp.int32, (C, C), 0)
    col = lax.broadcasted_iota(jnp.int32, (C, C), 1)
    tri = (col >= row) if reverse else (col <= row)
    cum_mat = tri.astype(F32)
    same_half = (row >= Hf) == (col >= Hf)
    same_quarter = (row // Qt) == (col // Qt)
    row_c = lax.broadcasted_iota(jnp.int32, (C, LANES), 0)
    q_side = (row_c < Hf) if reverse else (row_c >= Hf)
    odd_quarter = (row_c // Qt) % 2 == 1
    q_side4 = jnp.logical_not(odd_quarter) if reverse else odd_quarter
    low_half = row_c < Hf
    odd_eighth = (row_c // Et) % 2 == 1
    q_side8 = jnp.logical_not(odd_eighth) if reverse else odd_eighth
    quarter_c = row_c // Qt
    lane_g = lax.broadcasted_iota(jnp.int32, (G, LANES), 1)
    ones = jnp.ones((LANES, LANES), BF16)
    mid_row = Hf if reverse else Hf - 1
    qmid_rows = (Qt, Hf + Qt) if reverse else (Qt - 1, Hf + Qt - 1)
    emid_rows = [qb * Qt + (Et if reverse else Et - 1) for qb in range(C // Qt)]
    end_row = 0 if reverse else C - 1

    def gates(ci, r0):
        bi, hh = chains[ci]
        cols = slice(hh * HG_DK, (hh + 1) * HG_DK)
        lb = lb_all[:, cols]
        q_raw = q_ref[bi, pl.ds(r0, C), cols]
        q = q_raw * _sigmoid(q_raw)
        f = lb + (1.0 - lb) * _sigmoid(f_ref[bi, pl.ds(r0, C), cols])
        k = 1.0 - f
        vb = v_ref[bi, pl.ds(r0, C), cols].astype(BF16)
        b = jnp.dot(cum_mat, jnp.log2(f), precision=lax.Precision.HIGHEST,
                    preferred_element_type=F32)
        b_scr[ci] = b
        k_scr[ci] = k
        return q, k, vb, b

    def state_and_offdiag(ci, q, k, vb, b):
        b_mid = b_scr[ci, pl.ds(mid_row, 1), :]
        b_end = b_scr[ci, pl.ds(end_row, 1), :]
        st = st_ref[ci]
        o = _dot_nt((q * jnp.exp2(b)).astype(BF16), st.astype(BF16))
        kd = (k * jnp.exp2(b_end - b)).astype(BF16)
        st_ref[ci] = st * jnp.exp2(b_end) + _dot_tn(vb, kd)
        qt = jnp.where(q_side, q * jnp.exp2(b - b_mid), 0.0).astype(BF16)
        kt = jnp.where(q_side, 0.0, k * jnp.exp2(b_mid - b)).astype(BF16)
        b_q = jnp.where(low_half, b_scr[ci, pl.ds(qmid_rows[0], 1), :], b_scr[ci, pl.ds(qmid_rows[1], 1), :])
        qt4 = jnp.where(q_side4, q * jnp.exp2(b - b_q), 0.0).astype(BF16)
        kt4 = jnp.where(q_side4, 0.0, k * jnp.exp2(b_q - b)).astype(BF16)
        b_e = b_scr[ci, pl.ds(emid_rows[-1], 1), :]
        for qb in reversed(range(C // Qt - 1)):
            b_e = jnp.where(quarter_c == qb, b_scr[ci, pl.ds(emid_rows[qb], 1), :], b_e)
        qt8 = jnp.where(q_side8, q * jnp.exp2(b - b_e), 0.0).astype(BF16)
        kt8 = jnp.where(q_side8, 0.0, k * jnp.exp2(b_e - b)).astype(BF16)
        return o, _dot_nt(qt, kt), _dot_nt(qt4, kt4), _dot_nt(qt8, kt8)

    def diag_products(ci, q, b):
        for s in range(C):
            rows = slice(s // G * G, s // G * G + G)
            bs = b_scr[ci, pl.ds(s, 1), :]
            ks = k_scr[ci, pl.ds(s, 1), :]
            p_scr[ci, pl.ds(s * G, G), :] = (q[rows, :] * ks) * jnp.exp2(b[rows, :] - bs)

    def diag_scores(rs):
        groups = []
        for g in range(C // G):
            acc = jnp.zeros((G, LANES), F32)
            for s in range(g * G, (g + 1) * G):
                acc = jnp.where(lane_g == s, rs[s * G:(s + 1) * G, :], acc)
            groups.append(acc)
        return jnp.concatenate(groups, axis=0)[:, :C]

    def chunk_body(ci, carry):
        c = (n_chunks - 1 - ci) if reverse else ci
        r0 = pl.multiple_of(c * C, C)
        ids = range(len(chains))
        qkvb = [gates(ci, r0) for ci in ids]
        o_sc = [state_and_offdiag(ci, *qkvb[ci]) for ci in ids]
        for ci in ids:
            diag_products(ci, qkvb[ci][0], qkvb[ci][3])
        rs = [jnp.dot(p_scr[ci].astype(BF16), ones, preferred_element_type=F32) for ci in ids]
        for ci in ids:
            bi, hh = chains[ci]
            o, sc_half, sc_quarter, sc_eighth = o_sc[ci]
            inner = jnp.where(same_quarter, diag_scores(rs[ci]) + sc_eighth, sc_quarter)
            scores = jnp.where(same_half, jnp.where(tri, inner, 0.0), sc_half)
            o = o + jnp.dot(scores.astype(BF16), qkvb[ci][2], preferred_element_type=F32)
            o_ref[bi, pl.ds(r0, C), hh * HG_DK:(hh + 1) * HG_DK] = o
        return carry

    lax.fori_loop(0, n_chunks, chunk_body, 0)


def _hgrn_diag_rows():
    return HG_CHUNK * SUBLANES


def _hgrn_scan(hg, lb_logits_dir, B, S, reverse, tb):
    T = hg.shape[0]
    W = HG_HEADS * HG_DK
    nblk = S // tb
    f_blk = 2 if reverse else 1
    v_blk = 3
    nbat = HG_BATCH_PER_STEP if B % HG_BATCH_PER_STEP == 0 else 1
    n_chains = nbat * HG_HEADS
    hg4 = hg.reshape(B // nbat, nbat, S, hg.shape[1])

    def spec(col_blk):
        return pl.BlockSpec((None, nbat, tb, W),
                            lambda g, i: (g, 0, (nblk - 1 - i) if reverse else i, col_blk))

    kern = functools.partial(_hgrn_kernel, reverse=reverse, n_chunks=tb // HG_CHUNK)
    out = pl.pallas_call(
        kern,
        grid=(B // nbat, nblk),
        in_specs=[spec(0), spec(f_blk), spec(v_blk),
                  pl.BlockSpec((lb_logits_dir.shape[0], W), lambda g, i: (0, 0))],
        out_specs=spec(0),
        out_shape=jax.ShapeDtypeStruct((B // nbat, nbat, S, W), F32),
        scratch_shapes=[
            pltpu.VMEM((n_chains, HG_DK, HG_DK), F32),
            pltpu.VMEM((n_chains, HG_CHUNK, HG_DK), F32),
            pltpu.VMEM((n_chains, HG_CHUNK, HG_DK), F32),
            pltpu.VMEM((n_chains, _hgrn_diag_rows(), HG_DK), F32),
        ],
        compiler_params=_cparams(("parallel", "arbitrary")),
        name="hgrn_rev" if reverse else "hgrn_fwd",
    )(hg4, hg4, hg4, lb_logits_dir)
    return out.reshape(T, W)


def _mla_prep_kernel(mla_ref, pos_ref, invf_ref, qa_ref, kva_ref, wq_ref, wkv_ref,
                     qgn_ref, qgr_ref, kgn_ref, kgr_ref, q_ref, k_ref, v_ref):
    tm = mla_ref.shape[0]
    lane = lax.broadcasted_iota(jnp.int32, (tm, LANES), 1)
    ang = pos_ref[...].astype(F32) * invf_ref[...]
    cos_t = jnp.where(lane < MLA_ROPE, jnp.cos(ang), 0.0)
    sin_a = jnp.sin(ang)
    sin_t = jnp.where(lane < MLA_ROPE // 2, -sin_a, jnp.where(lane < MLA_ROPE, sin_a, 0.0))

    def rope(t):
        swapped = jnp.where(lane < MLA_ROPE // 2,
                            pltpu.roll(t, LANES - MLA_ROPE // 2, 1),
                            pltpu.roll(t, MLA_ROPE // 2, 1))
        return t * cos_t + swapped * sin_t

    def norm_rope_part(t, gain):
        ms = jnp.sum(t * t, axis=-1, keepdims=True) * (1.0 / MLA_ROPE)
        return t * lax.rsqrt(ms + EPS) * gain

    cq = mla_ref[:, 0:MLA_Q_LORA]
    ckv = mla_ref[:, MLA_Q_LORA:MLA_Q_LORA + MLA_KV_LORA]
    kr = mla_ref[:, MLA_Q_LORA + MLA_KV_LORA:]
    q_up = jnp.dot(_rms(cq, qa_ref[...]).astype(BF16), wq_ref[...], preferred_element_type=F32)
    kv_up = jnp.dot(_rms(ckv, kva_ref[...]).astype(BF16), wkv_ref[...], preferred_element_type=F32)
    k_rope = rope(norm_rope_part(kr, kgr_ref[...])).astype(BF16)
    for h in range(MLA_HEADS):
        c0 = h * MLA_QPAD
        q_ref[:, c0:c0 + LANES] = _rms(q_up[:, c0:c0 + LANES], qgn_ref[...]).astype(BF16)
        q_ref[:, c0 + LANES:c0 + 2 * LANES] = rope(
            norm_rope_part(q_up[:, c0 + LANES:c0 + 2 * LANES], qgr_ref[...])).astype(BF16)
        k_ref[:, c0:c0 + LANES] = _rms(kv_up[:, c0:c0 + LANES], kgn_ref[...]).astype(BF16)
        k_ref[:, c0 + LANES:c0 + 2 * LANES] = k_rope
        v_ref[:, h * MLA_V:(h + 1) * MLA_V] = kv_up[:, c0 + LANES:c0 + 2 * LANES].astype(BF16)


def _mla_prep(mla, pos, invf, qa, kva, wq, wkv, qgn, qgr, kgn, kgr, tm):
    T, n_mla = mla.shape
    full = lambda a: pl.BlockSpec(a.shape, lambda i: (0, 0))
    return pl.pallas_call(
        _mla_prep_kernel,
        grid=(T // tm,),
        in_specs=[
            pl.BlockSpec((tm, n_mla), lambda i: (i, 0)),
            pl.BlockSpec((tm, 1), lambda i: (i, 0)),
            full(invf), full(qa), full(kva), full(wq), full(wkv),
            full(qgn), full(qgr), full(kgn), full(kgr),
        ],
        out_specs=[
            pl.BlockSpec((tm, MLA_HEADS * MLA_QPAD), lambda i: (i, 0)),
            pl.BlockSpec((tm, MLA_HEADS * MLA_QPAD), lambda i: (i, 0)),
            pl.BlockSpec((tm, MLA_HEADS * MLA_V), lambda i: (i, 0)),
        ],
        out_shape=[
            jax.ShapeDtypeStruct((T, MLA_HEADS * MLA_QPAD), BF16),
            jax.ShapeDtypeStruct((T, MLA_HEADS * MLA_QPAD), BF16),
            jax.ShapeDtypeStruct((T, MLA_HEADS * MLA_V), BF16),
        ],
        compiler_params=_cparams(("parallel",)),
        name="mla_prep",
    )(mla, pos, invf, qa, kva, wq, wkv, qgn, qgr, kgn, kgr)


def _attn_kernel(q_ref, k_ref, v_ref, g_ref, o_ref):
    c = MLA_QK ** -0.5 * math.log2(math.e)
    tq = q_ref.shape[0]
    rc = tq // ATTN_CHUNKS
    rows = [pl.ds(i * rc, rc) for i in range(ATTN_CHUNKS)]
    s = [_dot_nt(q_ref[r, :], k_ref[...]) for r in rows]
    p = [jnp.exp2((si - jnp.max(si, axis=-1, keepdims=True)) * c) for si in s]
    l = [jnp.sum(pi, axis=-1, keepdims=True) for pi in p]
    o = [jnp.dot(pi.astype(BF16), v_ref[...], preferred_element_type=F32) for pi in p]
    for r, oi, li in zip(rows, o, l):
        o_ref[r, :] = _rms(oi / li, g_ref[...]).astype(o_ref.dtype)


def _attention(q, k, v, o_gain, B, S, tq):
    T = q.shape[0]
    nq = S // tq
    return pl.pallas_call(
        _attn_kernel,
        grid=(B, MLA_HEADS, nq),
        in_specs=[
            pl.BlockSpec((tq, MLA_QPAD), lambda b, h, i: (b * nq + i, h)),
            pl.BlockSpec((S, MLA_QPAD), lambda b, h, i: (b, h)),
            pl.BlockSpec((S, MLA_V), lambda b, h, i: (b, h)),
            pl.BlockSpec((1, MLA_V), lambda b, h, i: (0, h)),
        ],
        out_specs=pl.BlockSpec((tq, MLA_V), lambda b, h, i: (b * nq + i, h)),
        out_shape=jax.ShapeDtypeStruct((T, MLA_HEADS * MLA_V), BF16),
        compiler_params=_cparams(("parallel", "parallel", "arbitrary")),
        name="mla_attention",
    )(q, k, v, o_gain)


def _out_proj_kernel(of_ref, ob_ref, g_ref, ym_ref, x_ref, hgg_ref, wo_ref, fg_ref, wq_ref, keys_ref,
                     x1_ref, h2_ref, st_ref):
    tm = x_ref.shape[0]
    rc = tm // OUT_CHUNKS
    rows = [pl.ds(i * rc, rc) for i in range(OUT_CHUNKS)]
    w_hg_rows = of_ref.shape[1]

    def mixer_out(r):
        o = of_ref[r, :] + ob_ref[r, :]
        gate_raw = g_ref[r, :]
        gate = gate_raw * _sigmoid(gate_raw)
        parts = []
        for h in range(HG_HEADS):
            sl = slice(h * HG_DK, (h + 1) * HG_DK)
            parts.append(_rms(o[:, sl], hgg_ref[:, sl]) * gate[:, sl])
        return jnp.concatenate(parts, axis=-1).astype(BF16)

    y_hg = [mixer_out(r) for r in rows]
    mix = [jnp.dot(y, wo_ref[0:w_hg_rows, :], preferred_element_type=F32)
           + jnp.dot(ym_ref[r, :], wo_ref[w_hg_rows:, :], preferred_element_type=F32)
           for y, r in zip(y_hg, rows)]
    x1 = [x_ref[r, :] + m for r, m in zip(rows, mix)]
    h2 = [_rms(x, fg_ref[...]).astype(BF16) for x in x1]
    for r, x, h in zip(rows, x1, h2):
        x1_ref[r, :] = x
        h2_ref[r, :] = h
    pq = [jnp.dot(h, wq_ref[...], preferred_element_type=F32).astype(BF16) for h in h2]
    pq = jnp.concatenate(pq, axis=0)
    for pc in range(keys_ref.shape[0]):
        st_ref[pc] = _dot_nt(keys_ref[pc], pq[:, pc * PEER_DHALF:(pc + 1) * PEER_DHALF])


def _out_proj(o_f, o_b, hg, y_mla, x2, hg_gain, w_out, ffn_gain, w_pq, keys, tm):
    T, D = x2.shape
    W = o_f.shape[1]
    npc = keys.shape[0]
    full2 = lambda a: pl.BlockSpec(a.shape, lambda i: (0, 0))
    return pl.pallas_call(
        _out_proj_kernel,
        grid=(T // tm,),
        in_specs=[
            pl.BlockSpec((tm, W), lambda i: (i, 0)),
            pl.BlockSpec((tm, W), lambda i: (i, 0)),
            pl.BlockSpec((tm, W), lambda i: (i, 4)),
            pl.BlockSpec((tm, y_mla.shape[1]), lambda i: (i, 0)),
            pl.BlockSpec((tm, D), lambda i: (i, 0)),
            full2(hg_gain), full2(w_out), full2(ffn_gain), full2(w_pq),
            pl.BlockSpec(keys.shape, lambda i: (0, 0, 0)),
        ],
        out_specs=[
            pl.BlockSpec((tm, D), lambda i: (i, 0)),
            pl.BlockSpec((tm, D), lambda i: (i, 0)),
            pl.BlockSpec((npc, PEER_NKEYS, tm), lambda i: (0, 0, i)),
        ],
        out_shape=[
            jax.ShapeDtypeStruct((T, D), F32),
            jax.ShapeDtypeStruct((T, D), BF16),
            jax.ShapeDtypeStruct((npc, PEER_NKEYS, T), F32),
        ],
        compiler_params=_cparams(("parallel",)),
        name="out_proj_peer_query",
    )(o_f, o_b, hg, y_mla, x2, hg_gain, w_out, ffn_gain, w_pq, keys)


def _sorting_network(n):
    pairs = []
    p = 1
    while p < n:
        k = p
        while k >= 1:
            for j in range(k % p, n - k, 2 * k):
                for i in range(min(k, n - j - k)):
                    if (i + j) // (2 * p) == (i + j + k) // (2 * p):
                        pairs.append((i + j, i + j + k))
            k //= 2
        p *= 2
    return pairs


def _bitonic_merge_network(n):
    pairs = []
    stride = n // 2
    while stride >= 1:
        pairs += [(i, i + stride) for i in range(n) if not i & stride]
        stride //= 2
    return pairs


def _compare_exchange(xs, pairs):
    xs = list(xs)
    for i, j in pairs:
        xs[i], xs[j] = jnp.maximum(xs[i], xs[j]), jnp.minimum(xs[i], xs[j])
    return xs


def _peer_select_kernel(s_ref, iz_ref, cnt_ref, atop_ref, btop_ref, top_scr):
    K = PEER_TOPK
    n_vregs = PEER_NKEYS // SUBLANES
    assert n_vregs == K
    sort_pairs = _sorting_network(K)
    merge_pairs = _bitonic_merge_network(K)
    for p in range(PEER_HEADS):
        for c in range(2):
            lists = _compare_exchange(
                [s_ref[2 * p + c, pl.ds(g * SUBLANES, SUBLANES), :] for g in range(n_vregs)], sort_pairs)
            shift = SUBLANES // 2
            while shift >= 1:
                other = [pltpu.roll(x, shift, 0) for x in lists]
                lists = _compare_exchange(
                    [jnp.maximum(lists[r], other[K - 1 - r]) for r in range(K)], merge_pairs)
                shift //= 2
            for r in range(K):
                top_scr[c, r, pl.ds(p, 1), :] = lists[r][0:1, :]
    a = [top_scr[0, r] for r in range(K)]
    b = [top_scr[1, r] for r in range(K)]
    pairs = [(i, j) for i in range(K) for j in range(K) if (i + 1) * (j + 1) <= K]
    cands = [a[i] + b[j] for i, j in pairs]
    top = a[0] + b[0]
    cands += [jnp.full_like(top, NEG_INF)] * (-len(cands) % K)
    groups = [_compare_exchange(cands[g:g + K], sort_pairs) for g in range(0, len(cands), K)]
    while len(groups) > 1:
        merged = [_compare_exchange([jnp.maximum(x[r], y[K - 1 - r]) for r in range(K)], merge_pairs)
                  for x, y in zip(groups[0::2], groups[1::2])]
        groups = merged + groups[len(groups) // 2 * 2:]
    best = groups[0]
    kth = best[K - 1]
    z = jnp.zeros_like(top)
    for r in range(K):
        z = z + jnp.exp(best[r] - top)
    iz_ref[...] = 1.0 / z
    for i in range(K):
        cnt = jnp.zeros_like(top)
        for j in range(K):
            if (i, j) in pairs:
                cnt = cnt + jnp.where(a[i] + b[j] >= kth, 1.0, 0.0)
        cnt_ref[i] = cnt
        atop_ref[i] = a[i]
        btop_ref[i] = b[i]


def _peer_select(scores_t, tt):
    npc, nk, T = scores_t.shape
    top = jax.ShapeDtypeStruct((PEER_TOPK, PEER_HEADS, T), F32)
    top_spec = pl.BlockSpec((PEER_TOPK, PEER_HEADS, tt), lambda i: (0, 0, i))
    return pl.pallas_call(
        _peer_select_kernel,
        grid=(T // tt,),
        in_specs=[pl.BlockSpec((npc, nk, tt), lambda i: (0, 0, i))],
        out_specs=[pl.BlockSpec((PEER_HEADS, tt), lambda i: (0, i))] + [top_spec] * 3,
        out_shape=[jax.ShapeDtypeStruct((PEER_HEADS, T), F32)] + [top] * 3,
        scratch_shapes=[pltpu.VMEM((2, PEER_TOPK, PEER_HEADS, tt), F32)],
        compiler_params=_cparams(("parallel",)),
        name="peer_select",
    )(scores_t)


def _gelu_exact_x2(x):
    return x * (1.0 + lax.erf(x * (1.0 / math.sqrt(2.0))))


def _peer_dense_kernel(h_ref, u_ref, vt_ref, s_ref, iz_ref, cnt_ref, atop_ref, btop_ref, x1_ref, o_ref,
                       n_scr, e1_scr, rank_scr, e2_scr, a_scr, p_scr, yt_scr):
    jt = pl.program_id(1)
    te, tt = a_scr.shape
    nb = te // PEER_NKEYS
    n_grp = nb // SUBLANES
    PK = SUBLANES * (4 // jnp.dtype(GATE_DTYPE).itemsize)
    IBQ = 32 * PK // PEER_NKEYS

    @pl.when(jt == 0)
    def _():
        yt_scr[...] = jnp.zeros_like(yt_scr)
        grouped = (PEER_NKEYS // SUBLANES, SUBLANES, LANES)
        for p in range(PEER_HEADS):
            for st in range(tt // LANES):
                sl = pl.ds(st * LANES, LANES)
                s1 = s_ref[2 * p, :, sl]
                s2 = s_ref[2 * p + 1, :, sl]
                n = jnp.zeros_like(s1)
                rank = jnp.zeros_like(s2)
                for r in range(PEER_TOPK):
                    n = jnp.where(s1 == atop_ref[r, pl.ds(p, 1), sl], cnt_ref[r, pl.ds(p, 1), sl], n)
                    rank = jnp.where(btop_ref[r, pl.ds(p, 1), sl] > s2, r + 1.0, rank)
                n_scr[p, :, :, sl] = n.reshape(grouped)
                e1_scr[p, :, :, sl] = jnp.exp(s1 - atop_ref[0, pl.ds(p, 1), sl]).reshape(grouped)
                rank_scr[p, :, sl] = rank.astype(GATE_DTYPE)
                m2 = btop_ref[0, pl.ds(p, 1), sl]
                e2_scr[p, :, sl] = (jnp.exp(s2 - m2) * (0.5 * iz_ref[pl.ds(p, 1), sl])).astype(GATE_DTYPE)

    zero = jnp.zeros((PK, LANES), GATE_DTYPE)
    for sub_tile in range(te // DENSE_SUBTILE):
        erows = pl.ds(sub_tile * DENSE_SUBTILE, DENSE_SUBTILE)
        a_scr[erows, :] = _dot_nt(u_ref[erows, :], h_ref[...])
        for ibg in range(sub_tile * DENSE_SUBTILE // (IBQ * PEER_NKEYS),
                         (sub_tile + 1) * DENSE_SUBTILE // (IBQ * PEER_NKEYS)):
            for st in range(tt // LANES):
                sl = pl.ds(st * LANES, LANES)
                gates = [[zero for _ in range(PEER_NKEYS // PK)] for _ in range(IBQ)]
                for p in range(PEER_HEADS):
                    rank = rank_scr[p, :, sl]
                    e2 = e2_scr[p, :, sl]
                    for q in range(IBQ):
                        ib = ibg * IBQ + q
                        grp = jt * n_grp + ib // SUBLANES
                        sub = pl.ds(ib % SUBLANES, 1)
                        n = jnp.broadcast_to(n_scr[p, grp, sub, sl], (PK, LANES)).astype(GATE_DTYPE)
                        e1 = jnp.broadcast_to(e1_scr[p, grp, sub, sl], (PK, LANES)).astype(GATE_DTYPE)
                        for r in range(PEER_NKEYS // PK):
                            rr = slice(r * PK, (r + 1) * PK)
                            gates[q][r] = gates[q][r] + jnp.where(rank[rr, :] < n, e2[rr, :], zero) * e1
                for q in range(IBQ):
                    rows = pl.ds((ibg * IBQ + q) * PEER_NKEYS, PEER_NKEYS)
                    gate = jnp.concatenate(gates[q], axis=0)
                    p_scr[rows, sl] = (_gelu_exact_x2(a_scr[rows, sl]).astype(GATE_DTYPE) * gate).astype(BF16)
        yt_scr[...] += jnp.dot(vt_ref[:, erows], p_scr[erows, :], preferred_element_type=F32)

    @pl.when(jt == pl.num_programs(1) - 1)
    def _():
        o_ref[...] = x1_ref[...] + yt_scr[...].T


def _peer_dense(h2, u_bf, vt_bf, scores_t, iz, cnt, atop, btop, x1, tt, te):
    T, D = h2.shape
    E = u_bf.shape[0]
    npc, nk, _ = scores_t.shape
    assert te % (PEER_NKEYS * SUBLANES) == 0 and E % te == 0
    top = pl.BlockSpec((PEER_TOPK, PEER_HEADS, tt), lambda t, j: (0, 0, t))
    return pl.pallas_call(
        _peer_dense_kernel,
        grid=(T // tt, E // te),
        in_specs=[
            pl.BlockSpec((tt, D), lambda t, j: (t, 0)),
            pl.BlockSpec((te, D), lambda t, j: (j, 0)),
            pl.BlockSpec((D, te), lambda t, j: (0, j)),
            pl.BlockSpec((npc, nk, tt), lambda t, j: (0, 0, t)),
            pl.BlockSpec((PEER_HEADS, tt), lambda t, j: (0, t)),
            top, top, top,
            pl.BlockSpec((tt, D), lambda t, j: (t, 0)),
        ],
        out_specs=pl.BlockSpec((tt, D), lambda t, j: (t, 0)),
        out_shape=jax.ShapeDtypeStruct((T, D), F32),
        scratch_shapes=[
            pltpu.VMEM((PEER_HEADS, nk // SUBLANES, SUBLANES, tt), F32),
            pltpu.VMEM((PEER_HEADS, nk // SUBLANES, SUBLANES, tt), F32),
            pltpu.VMEM((PEER_HEADS, nk, tt), GATE_DTYPE),
            pltpu.VMEM((PEER_HEADS, nk, tt), GATE_DTYPE),
            pltpu.VMEM((te, tt), F32),
            pltpu.VMEM((te, tt), BF16),
            pltpu.VMEM((D, tt), F32),
        ],
        compiler_params=_cparams(("parallel", "arbitrary")),
        name="peer_dense",
    )(h2, u_bf, vt_bf, scores_t, iz, cnt, atop, btop, x1)


def _transpose_cast_kernel(x_ref, o_ref):
    o_ref[...] = x_ref[...].T.astype(o_ref.dtype)


def _transpose_cast(x, dtype, rows):
    E, D = x.shape
    return pl.pallas_call(
        _transpose_cast_kernel,
        grid=(E // rows,),
        in_specs=[pl.BlockSpec((rows, D), lambda i: (i, 0))],
        out_specs=pl.BlockSpec((D, rows), lambda i: (0, i)),
        out_shape=jax.ShapeDtypeStruct((D, E), dtype),
        compiler_params=_cparams(("parallel",)),
        name="value_table_layout",
    )(x)


def _pick(n, pref):
    t = min(n, pref)
    assert n % t == 0, (n, t)
    return t


def _tiles(T, S):
    return {
        "in_proj_rows": _pick(T, 512),
        "scan_steps": _pick(S, 512),
        "mla_rows": _pick(T, 1024),
        "attn_queries": _pick(S, 1024),
        "out_proj_rows": _pick(T, 512),
        "select_tokens": _pick(T, 256),
        "dense_tokens": _pick(T, 512),
        "dense_experts": 2 * PEER_NKEYS * SUBLANES,
    }


def _layer(x2, pos, B, S, attn_norm, w_in, lb_logits, hg_o_norm, q_a_norm, w_q_up, kv_a_norm,
           w_kv_up, q_norm, k_norm, mla_o_norm, w_out, ffn_norm, peer_w_q, peer_sub_keys, peer_u, peer_v):
    T, D = x2.shape
    n_hg = 5 * HG_HEADS * HG_DK
    row = lambda a: a.reshape(1, -1).astype(F32)

    w_hg = w_in[:, :n_hg].astype(BF16)
    n_mla = w_in.shape[1] - n_hg
    w_mla = jnp.pad(w_in[:, n_hg:], ((0, 0), (0, -n_mla % LANES))).astype(BF16)
    pad_r = LANES - MLA_ROPE
    wq = w_q_up.reshape(MLA_Q_LORA, MLA_HEADS, MLA_QK)
    wq = jnp.pad(wq, ((0, 0), (0, 0), (0, MLA_QPAD - MLA_QK))).reshape(MLA_Q_LORA, MLA_HEADS * MLA_QPAD)
    wq = wq.astype(BF16)
    wkv = w_kv_up.astype(BF16)
    qgn, qgr = row(q_norm[:MLA_NOPE]), row(jnp.pad(q_norm[MLA_NOPE:], (0, pad_r)))
    kgn, kgr = row(k_norm[:MLA_NOPE]), row(jnp.pad(k_norm[MLA_NOPE:], (0, pad_r)))
    inv_freq = 1.0 / (ROPE_THETA ** (jnp.arange(0, MLA_ROPE, 2, dtype=F32) / MLA_ROPE))
    invf = row(jnp.concatenate([inv_freq, inv_freq, jnp.zeros((pad_r,), F32)]))
    keys = peer_sub_keys.reshape(2 * PEER_HEADS, PEER_NKEYS, PEER_DHALF).astype(BF16)
    u_bf = peer_u.astype(BF16)
    vt_bf = _transpose_cast(peer_v, BF16, _pick(peer_v.shape[0], 1024))

    tile = _tiles(T, S)
    hg, mla = _in_proj(x2, row(attn_norm), w_hg, w_mla, tile["in_proj_rows"])
    o_f = _hgrn_scan(hg, lb_logits[:, 0, :], B, S, False, tile["scan_steps"])
    o_b = _hgrn_scan(hg, lb_logits[:, 1, :], B, S, True, tile["scan_steps"])
    q, k, v = _mla_prep(mla, pos, invf, row(q_a_norm), row(kv_a_norm), wq, wkv, qgn, qgr, kgn, kgr,
                        tile["mla_rows"])
    y_mla = _attention(q, k, v, mla_o_norm.reshape(1, -1).astype(F32), B, S, tile["attn_queries"])
    x1, h2, scores_t = _out_proj(o_f, o_b, hg, y_mla, x2, row(hg_o_norm), w_out.astype(BF16),
                                 row(ffn_norm), peer_w_q.astype(BF16), keys, tile["out_proj_rows"])
    iz, cnt, atop, btop = _peer_select(scores_t, tile["select_tokens"])
    return _peer_dense(h2, u_bf, vt_bf, scores_t, iz, cnt, atop, btop, x1,
                       tile["dense_tokens"], tile["dense_experts"])


def kernel(x, positions, attn_norm, w_in, hg_lb_logits, hg_o_norm, q_a_norm, w_q_up, kv_a_norm, w_kv_up,
           q_norm, k_norm, mla_o_norm, w_out, ffn_norm, peer_w_q, peer_sub_keys, peer_u, peer_v):
    B, S, D = x.shape
    x2 = x.reshape(B * S, D)
    pos = positions.reshape(B * S, 1)
    for l in range(attn_norm.shape[0]):
        assert l == 0
        x2 = _layer(x2, pos, B, S, attn_norm[l], w_in[l], hg_lb_logits, hg_o_norm[l], q_a_norm[l],
                    w_q_up[l], kv_a_norm[l], w_kv_up[l], q_norm[l], k_norm[l], mla_o_norm[l], w_out[l],
                    ffn_norm[l], peer_w_q[l], peer_sub_keys[l], peer_u[l], peer_v[l])
    return x2.reshape(B, S, D)
```

```python
import functools
import math

import jax
import jax.numpy as jnp
from jax import lax
from jax.experimental import pallas as pl
from jax.experimental.pallas import tpu as pltpu

F32 = jnp.float32
BF16 = jnp.bfloat16
EPS = 1e-6
LANES = 128
SUBLANES = 8
NEG_INF = float("-inf")

HG_HEADS = 4
HG_DK = 128
HG_CHUNK = 64
HG_HALF = HG_CHUNK // 2
HG_QUARTER = HG_CHUNK // 4
HG_BATCH_PER_STEP = 4
MLA_HEADS = 4
MLA_Q_LORA = 384
MLA_KV_LORA = 256
MLA_NOPE = 128
MLA_ROPE = 64
MLA_V = 128
MLA_QK = MLA_NOPE + MLA_ROPE
MLA_QPAD = 2 * LANES
ROPE_THETA = 10000.0
ATTN_CHUNKS = 4
OUT_CHUNKS = 2
PEER_HEADS = 8
PEER_NKEYS = 128
PEER_TOPK = 16
PEER_DHALF = 128
GATE_DTYPE = BF16
DENSE_SUBTILE = 1024

VMEM_LIMIT = 56 * 1024 * 1024


def _cparams(sem):
    return pltpu.CompilerParams(dimension_semantics=sem, vmem_limit_bytes=VMEM_LIMIT)


def _rms(x, gain):
    ms = jnp.mean(x * x, axis=-1, keepdims=True)
    return x * lax.rsqrt(ms + EPS) * gain


def _sigmoid(x):
    return 1.0 / (1.0 + jnp.exp(-x))


def _dot_nt(a, b):
    return lax.dot_general(a, b, (((1,), (1,)), ((), ())), preferred_element_type=F32)


def _dot_tn(a, b):
    return lax.dot_general(a, b, (((0,), (0,)), ((), ())), preferred_element_type=F32)


def _in_proj_kernel(x_ref, g_ref, whg_ref, wmla_ref, hg_ref, mla_ref):
    h = _rms(x_ref[...], g_ref[...]).astype(BF16)
    hg_ref[...] = jnp.dot(h, whg_ref[...], preferred_element_type=F32)
    mla_ref[...] = jnp.dot(h, wmla_ref[...], preferred_element_type=F32)


def _in_proj(x2, gain, w_hg, w_mla, tm):
    T, D = x2.shape
    n_hg, n_mla = w_hg.shape[1], w_mla.shape[1]
    return pl.pallas_call(
        _in_proj_kernel,
        grid=(T // tm,),
        in_specs=[
            pl.BlockSpec((tm, D), lambda i: (i, 0)),
            pl.BlockSpec((1, D), lambda i: (0, 0)),
            pl.BlockSpec((D, n_hg), lambda i: (0, 0)),
            pl.BlockSpec((D, n_mla), lambda i: (0, 0)),
        ],
        out_specs=[
            pl.BlockSpec((tm, n_hg), lambda i: (i, 0)),
            pl.BlockSpec((tm, n_mla), lambda i: (i, 0)),
        ],
        out_shape=[jax.ShapeDtypeStruct((T, n_hg), F32), jax.ShapeDtypeStruct((T, n_mla), F32)],
        compiler_params=_cparams(("parallel",)),
        name="in_proj",
    )(x2, gain, w_hg, w_mla)


def _hgrn_kernel(q_ref, f_ref, v_ref, lb_ref, o_ref, st_ref, b_scr, k_scr, p_scr, *, reverse, n_chunks):
    C, Hf, Qt, G = HG_CHUNK, HG_HALF, HG_QUARTER, SUBLANES
    Et = G
    assert C == 8 * Et
    n_batch = q_ref.shape[0]
    n_heads = st_ref.shape[0] // n_batch
    chains = [(bi, hh) for bi in range(n_batch) for hh in range(n_heads)]

    @pl.when(pl.program_id(1) == 0)
    def _():
        st_ref[...] = jnp.zeros_like(st_ref)

    logits = lb_ref[...]
    ex = jnp.exp(logits - jnp.max(logits, axis=0, keepdims=True))
    lb_all = ex[0:1, :] / jnp.sum(ex, axis=0, keepdims=True)

    row = lax.broadcasted_iota(jnp.int32, (C, C), 0)
    col = lax.broadcasted_iota(jnp.int32, (C, C), 1)
    tri = (col >= row) if reverse else (col <= row)
    cum_mat = tri.astype(F32)
    same_half = (row >= Hf) == (col >= Hf)
    same_quarter = (row // Qt) == (col // Qt)
    row_c = lax.broadcasted_iota(jnp.int32, (C, LANES), 0)
    q_side = (row_c < Hf) if reverse else (row_c >= Hf)
    odd_quarter = (row_c // Qt) % 2 == 1
    q_side4 = jnp.logical_not(odd_quarter) if reverse else odd_quarter
    low_half = row_c < Hf
    odd_eighth = (row_c // Et) % 2 == 1
    q_side8 = jnp.logical_not(odd_eighth) if reverse else odd_eighth
    quarter_c = row_c // Qt
    lane_g = lax.broadcasted_iota(jnp.int32, (G, LANES), 1)
    ones = jnp.ones((LANES, LANES), BF16)
    mid_row = Hf if reverse else Hf - 1
    qmid_rows = (Qt, Hf + Qt) if reverse else (Qt - 1, Hf + Qt - 1)
    emid_rows = [qb * Qt + (Et if reverse else Et - 1) for qb in range(C // Qt)]
    end_row = 0 if reverse else C - 1

    def gates(ci, r0):
        bi, hh = chains[ci]
        cols = slice(hh * HG_DK, (hh + 1) * HG_DK)
        lb = lb_all[:, cols]
        q_raw = q_ref[bi, pl.ds(r0, C), cols]
        q = q_raw * _sigmoid(q_raw)
        f = lb + (1.0 - lb) * _sigmoid(f_ref[bi, pl.ds(r0, C), cols])
        k = 1.0 - f
        vb = v_ref[bi, pl.ds(r0, C), cols].astype(BF16)
        b = jnp.dot(cum_mat, jnp.log2(f), precision=lax.Precision.HIGHEST,
                    preferred_element_type=F32)
        b_scr[ci] = b
        k_scr[ci] = k
        return q, k, vb, b

    def state_and_offdiag(ci, q, k, vb, b):
        b_mid = b_scr[ci, pl.ds(mid_row, 1), :]
        b_end = b_scr[ci, pl.ds(end_row, 1), :]
        st = st_ref[ci]
        o = _dot_nt((q * jnp.exp2(b)).astype(BF16), st.astype(BF16))
        kd = (k * jnp.exp2(b_end - b)).astype(BF16)
        st_ref[ci] = st * jnp.exp2(b_end) + _dot_tn(vb, kd)
        qt = jnp.where(q_side, q * jnp.exp2(b - b_mid), 0.0).astype(BF16)
        kt = jnp.where(q_side, 0.0, k * jnp.exp2(b_mid - b)).astype(BF16)
        b_q = jnp.where(low_half, b_scr[ci, pl.ds(qmid_rows[0], 1), :], b_scr[ci, pl.ds(qmid_rows[1], 1), :])
        qt4 = jnp.where(q_side4, q * jnp.exp2(b - b_q), 0.0).astype(BF16)
        kt4 = jnp.where(q_side4, 0.0, k * jnp.exp2(b_q - b)).astype(BF16)
        b_e = b_scr[ci, pl.ds(emid_rows[-1], 1), :]
        for qb in reversed(range(C // Qt - 1)):
            b_e = jnp.where(quarter_c == qb, b_scr[ci, pl.ds(emid_rows[qb], 1), :], b_e)
        qt8 = jnp.where(q_side8, q * jnp.exp2(b - b_e), 0.0).astype(BF16)
        kt8 = jnp.where(q_side8, 0.0, k * jnp.exp2(b_e - b)).astype(BF16)
        return o, _dot_nt(qt, kt), _dot_nt(qt4, kt4), _dot_nt(qt8, kt8)

    def diag_products(ci, q, b):
        for s in range(C):
            rows = slice(s // G * G, s // G * G + G)
            bs = b_scr[ci, pl.ds(s, 1), :]
            ks = k_scr[ci, pl.ds(s, 1), :]
            p_scr[ci, pl.ds(s * G, G), :] = (q[rows, :] * ks) * jnp.exp2(b[rows, :] - bs)

    def diag_scores(rs):
        groups = []
        for g in range(C // G):
            acc = jnp.zeros((G, LANES), F32)
            for s in range(g * G, (g + 1) * G):
                acc = jnp.where(lane_g == s, rs[s * G:(s + 1) * G, :], acc)
            groups.append(acc)
        return jnp.concatenate(groups, axis=0)[:, :C]

    def chunk_body(ci, carry):
        c = (n_chunks - 1 - ci) if reverse else ci
        r0 = pl.multiple_of(c * C, C)
        ids = range(len(chains))
        qkvb = [gates(ci, r0) for ci in ids]
        o_sc = [state_and_offdiag(ci, *qkvb[ci]) for ci in ids]
        for ci in ids:
            diag_products(ci, qkvb[ci][0], qkvb[ci][3])
        rs = [jnp.dot(p_scr[ci].astype(BF16), ones, preferred_element_type=F32) for ci in ids]
        for ci in ids:
            bi, hh = chains[ci]
            o, sc_half, sc_quarter, sc_eighth = o_sc[ci]
            inner = jnp.where(same_quarter, diag_scores(rs[ci]) + sc_eighth, sc_quarter)
            scores = jnp.where(same_half, jnp.where(tri, inner, 0.0), sc_half)
            o = o + jnp.dot(scores.astype(BF16), qkvb[ci][2], preferred_element_type=F32)
            o_ref[bi, pl.ds(r0, C), hh * HG_DK:(hh + 1) * HG_DK] = o
        return carry

    lax.fori_loop(0, n_chunks, chunk_body, 0)


def _hgrn_diag_rows():
    return HG_CHUNK * SUBLANES


def _hgrn_scan(hg, lb_logits_dir, B, S, reverse, tb):
    T = hg.shape[0]
    W = HG_HEADS * HG_DK
    nblk = S // tb
    f_blk = 2 if reverse else 1
    v_blk = 3
    nbat = HG_BATCH_PER_STEP if B % HG_BATCH_PER_STEP == 0 else 1
    n_chains = nbat * HG_HEADS
    hg4 = hg.reshape(B // nbat, nbat, S, hg.shape[1])

    def spec(col_blk):
        return pl.BlockSpec((None, nbat, tb, W),
                            lambda g, i: (g, 0, (nblk - 1 - i) if reverse else i, col_blk))

    kern = functools.partial(_hgrn_kernel, reverse=reverse, n_chunks=tb // HG_CHUNK)
    out = pl.pallas_call(
        kern,
        grid=(B // nbat, nblk),
        in_specs=[spec(0), spec(f_blk), spec(v_blk),
                  pl.BlockSpec((lb_logits_dir.shape[0], W), lambda g, i: (0, 0))],
        out_specs=spec(0),
        out_shape=jax.ShapeDtypeStruct((B // nbat, nbat, S, W), F32),
        scratch_shapes=[
            pltpu.VMEM((n_chains, HG_DK, HG_DK), F32),
            pltpu.VMEM((n_chains, HG_CHUNK, HG_DK), F32),
            pltpu.VMEM((n_chains, HG_CHUNK, HG_DK), F32),
            pltpu.VMEM((n_chains, _hgrn_diag_rows(), HG_DK), F32),
        ],
        compiler_params=_cparams(("parallel", "arbitrary")),
        name="hgrn_rev" if reverse else "hgrn_fwd",
    )(hg4, hg4, hg4, lb_logits_dir)
    return out.reshape(T, W)


def _mla_prep_kernel(mla_ref, pos_ref, invf_ref, qa_ref, kva_ref, wq_ref, wkv_ref,
                     qgn_ref, qgr_ref, kgn_ref, kgr_ref, q_ref, k_ref, v_ref):
    tm = mla_ref.shape[0]
    lane = lax.broadcasted_iota(jnp.int32, (tm, LANES), 1)
    ang = pos_ref[...].astype(F32) * invf_ref[...]
    cos_t = jnp.where(lane < MLA_ROPE, jnp.cos(ang), 0.0)
    sin_a = jnp.sin(ang)
    sin_t = jnp.where(lane < MLA_ROPE // 2, -sin_a, jnp.where(lane < MLA_ROPE, sin_a, 0.0))

    def rope(t):
        swapped = jnp.where(lane < MLA_ROPE // 2,
                            pltpu.roll(t, LANES - MLA_ROPE // 2, 1),
                            pltpu.roll(t, MLA_ROPE // 2, 1))
        return t * cos_t + swapped * sin_t

    def norm_rope_part(t, gain):
        ms = jnp.sum(t * t, axis=-1, keepdims=True) * (1.0 / MLA_ROPE)
        return t * lax.rsqrt(ms + EPS) * gain

    cq = mla_ref[:, 0:MLA_Q_LORA]
    ckv = mla_ref[:, MLA_Q_LORA:MLA_Q_LORA + MLA_KV_LORA]
    kr = mla_ref[:, MLA_Q_LORA + MLA_KV_LORA:]
    q_up = jnp.dot(_rms(cq, qa_ref[...]).astype(BF16), wq_ref[...], preferred_element_type=F32)
    kv_up = jnp.dot(_rms(ckv, kva_ref[...]).astype(BF16), wkv_ref[...], preferred_element_type=F32)
    k_rope = rope(norm_rope_part(kr, kgr_ref[...])).astype(BF16)
    for h in range(MLA_HEADS):
        c0 = h * MLA_QPAD
        q_ref[:, c0:c0 + LANES] = _rms(q_up[:, c0:c0 + LANES], qgn_ref[...]).astype(BF16)
        q_ref[:, c0 + LANES:c0 + 2 * LANES] = rope(
            norm_rope_part(q_up[:, c0 + LANES:c0 + 2 * LANES], qgr_ref[...])).astype(BF16)
        k_ref[:, c0:c0 + LANES] = _rms(kv_up[:, c0:c0 + LANES], kgn_ref[...]).astype(BF16)
        k_ref[:, c0 + LANES:c0 + 2 * LANES] = k_rope
        v_ref[:, h * MLA_V:(h + 1) * MLA_V] = kv_up[:, c0 + LANES:c0 + 2 * LANES].astype(BF16)


def _mla_prep(mla, pos, invf, qa, kva, wq, wkv, qgn, qgr, kgn, kgr, tm):
    T, n_mla = mla.shape
    full = lambda a: pl.BlockSpec(a.shape, lambda i: (0, 0))
    return pl.pallas_call(
        _mla_prep_kernel,
        grid=(T // tm,),
        in_specs=[
            pl.BlockSpec((tm, n_mla), lambda i: (i, 0)),
            pl.BlockSpec((tm, 1), lambda i: (i, 0)),
            full(invf), full(qa), full(kva), full(wq), full(wkv),
            full(qgn), full(qgr), full(kgn), full(kgr),
        ],
        out_specs=[
            pl.BlockSpec((tm, MLA_HEADS * MLA_QPAD), lambda i: (i, 0)),
            pl.BlockSpec((tm, MLA_HEADS * MLA_QPAD), lambda i: (i, 0)),
            pl.BlockSpec((tm, MLA_HEADS * MLA_V), lambda i: (i, 0)),
        ],
        out_shape=[
            jax.ShapeDtypeStruct((T, MLA_HEADS * MLA_QPAD), BF16),
            jax.ShapeDtypeStruct((T, MLA_HEADS * MLA_QPAD), BF16),
            jax.ShapeDtypeStruct((T, MLA_HEADS * MLA_V), BF16),
        ],
        compiler_params=_cparams(("parallel",)),
        name="mla_prep",
    )(mla, pos, invf, qa, kva, wq, wkv, qgn, qgr, kgn, kgr)


def _attn_kernel(q_ref, k_ref, v_ref, g_ref, o_ref):
    c = MLA_QK ** -0.5 * math.log2(math.e)
    tq = q_ref.shape[0]
    rc = tq // ATTN_CHUNKS
    rows = [pl.ds(i * rc, rc) for i in range(ATTN_CHUNKS)]
    s = [_dot_nt(q_ref[r, :], k_ref[...]) for r in rows]
    p = [jnp.exp2((si - jnp.max(si, axis=-1, keepdims=True)) * c) for si in s]
    l = [jnp.sum(pi, axis=-1, keepdims=True) for pi in p]
    o = [jnp.dot(pi.astype(BF16), v_ref[...], preferred_element_type=F32) for pi in p]
    for r, oi, li in zip(rows, o, l):
        o_ref[r, :] = _rms(oi / li, g_ref[...]).astype(o_ref.dtype)


def _attention(q, k, v, o_gain, B, S, tq):
    T = q.shape[0]
    nq = S // tq
    return pl.pallas_call(
        _attn_kernel,
        grid=(B, MLA_HEADS, nq),
        in_specs=[
            pl.BlockSpec((tq, MLA_QPAD), lambda b, h, i: (b * nq + i, h)),
            pl.BlockSpec((S, MLA_QPAD), lambda b, h, i: (b, h)),
            pl.BlockSpec((S, MLA_V), lambda b, h, i: (b, h)),
            pl.BlockSpec((1, MLA_V), lambda b, h, i: (0, h)),
        ],
        out_specs=pl.BlockSpec((tq, MLA_V), lambda b, h, i: (b * nq + i, h)),
        out_shape=jax.ShapeDtypeStruct((T, MLA_HEADS * MLA_V), BF16),
        compiler_params=_cparams(("parallel", "parallel", "arbitrary")),
        name="mla_attention",
    )(q, k, v, o_gain)


def _out_proj_kernel(of_ref, ob_ref, g_ref, ym_ref, x_ref, hgg_ref, wo_ref, fg_ref, wq_ref, keys_ref,
                     x1_ref, h2_ref, st_ref):
    tm = x_ref.shape[0]
    rc = tm // OUT_CHUNKS
    rows = [pl.ds(i * rc, rc) for i in range(OUT_CHUNKS)]
    w_hg_rows = of_ref.shape[1]

    def mixer_out(r):
        o = of_ref[r, :] + ob_ref[r, :]
        gate_raw = g_ref[r, :]
        gate = gate_raw * _sigmoid(gate_raw)
        parts = []
        for h in range(HG_HEADS):
            sl = slice(h * HG_DK, (h + 1) * HG_DK)
            parts.append(_rms(o[:, sl], hgg_ref[:, sl]) * gate[:, sl])
        return jnp.concatenate(parts, axis=-1).astype(BF16)

    y_hg = [mixer_out(r) for r in rows]
    mix = [jnp.dot(y, wo_ref[0:w_hg_rows, :], preferred_element_type=F32)
           + jnp.dot(ym_ref[r, :], wo_ref[w_hg_rows:, :], preferred_element_type=F32)
           for y, r in zip(y_hg, rows)]
    x1 = [x_ref[r, :] + m for r, m in zip(rows, mix)]
    h2 = [_rms(x, fg_ref[...]).astype(BF16) for x in x1]
    for r, x, h in zip(rows, x1, h2):
        x1_ref[r, :] = x
        h2_ref[r, :] = h
    pq = [jnp.dot(h, wq_ref[...], preferred_element_type=F32).astype(BF16) for h in h2]
    pq = jnp.concatenate(pq, axis=0)
    for pc in range(keys_ref.shape[0]):
        st_ref[pc] = _dot_nt(keys_ref[pc], pq[:, pc * PEER_DHALF:(pc + 1) * PEER_DHALF])


def _out_proj(o_f, o_b, hg, y_mla, x2, hg_gain, w_out, ffn_gain, w_pq, keys, tm):
    T, D = x2.shape
    W = o_f.shape[1]
    npc = keys.shape[0]
    full2 = lambda a: pl.BlockSpec(a.shape, lambda i: (0, 0))
    return pl.pallas_call(
        _out_proj_kernel,
        grid=(T // tm,),
        in_specs=[
            pl.BlockSpec((tm, W), lambda i: (i, 0)),
            pl.BlockSpec((tm, W), lambda i: (i, 0)),
            pl.BlockSpec((tm, W), lambda i: (i, 4)),
            pl.BlockSpec((tm, y_mla.shape[1]), lambda i: (i, 0)),
            pl.BlockSpec((tm, D), lambda i: (i, 0)),
            full2(hg_gain), full2(w_out), full2(ffn_gain), full2(w_pq),
            pl.BlockSpec(keys.shape, lambda i: (0, 0, 0)),
        ],
        out_specs=[
            pl.BlockSpec((tm, D), lambda i: (i, 0)),
            pl.BlockSpec((tm, D), lambda i: (i, 0)),
            pl.BlockSpec((npc, PEER_NKEYS, tm), lambda i: (0, 0, i)),
        ],
        out_shape=[
            jax.ShapeDtypeStruct((T, D), F32),
            jax.ShapeDtypeStruct((T, D), BF16),
            jax.ShapeDtypeStruct((npc, PEER_NKEYS, T), F32),
        ],
        compiler_params=_cparams(("parallel",)),
        name="out_proj_peer_query",
    )(o_f, o_b, hg, y_mla, x2, hg_gain, w_out, ffn_gain, w_pq, keys)


def _sorting_network(n):
    pairs = []
    p = 1
    while p < n:
        k = p
        while k >= 1:
            for j in range(k % p, n - k, 2 * k):
                for i in range(min(k, n - j - k)):
                    if (i + j) // (2 * p) == (i + j + k) // (2 * p):
                        pairs.append((i + j, i + j + k))
            k //= 2
        p *= 2
    return pairs


def _bitonic_merge_network(n):
    pairs = []
    stride = n // 2
    while stride >= 1:
        pairs += [(i, i + stride) for i in range(n) if not i & stride]
        stride //= 2
    return pairs


def _compare_exchange(xs, pairs):
    xs = list(xs)
    for i, j in pairs:
        xs[i], xs[j] = jnp.maximum(xs[i], xs[j]), jnp.minimum(xs[i], xs[j])
    return xs


def _peer_select_kernel(s_ref, iz_ref, cnt_ref, atop_ref, btop_ref, top_scr):
    K = PEER_TOPK
    n_vregs = PEER_NKEYS // SUBLANES
    assert n_vregs == K
    sort_pairs = _sorting_network(K)
    merge_pairs = _bitonic_merge_network(K)
    for p in range(PEER_HEADS):
        for c in range(2):
            lists = _compare_exchange(
                [s_ref[2 * p + c, pl.ds(g * SUBLANES, SUBLANES), :] for g in range(n_vregs)], sort_pairs)
            shift = SUBLANES // 2
            while shift >= 1:
                other = [pltpu.roll(x, shift, 0) for x in lists]
                lists = _compare_exchange(
                    [jnp.maximum(lists[r], other[K - 1 - r]) for r in range(K)], merge_pairs)
                shift //= 2
            for r in range(K):
                top_scr[c, r, pl.ds(p, 1), :] = lists[r][0:1, :]
    a = [top_scr[0, r] for r in range(K)]
    b = [top_scr[1, r] for r in range(K)]
    pairs = [(i, j) for i in range(K) for j in range(K) if (i + 1) * (j + 1) <= K]
    cands = [a[i] + b[j] for i, j in pairs]
    top = a[0] + b[0]
    cands += [jnp.full_like(top, NEG_INF)] * (-len(cands) % K)
    groups = [_compare_exchange(cands[g:g + K], sort_pairs) for g in range(0, len(cands), K)]
    while len(groups) > 1:
        merged = [_compare_exchange([jnp.maximum(x[r], y[K - 1 - r]) for r in range(K)], merge_pairs)
                  for x, y in zip(groups[0::2], groups[1::2])]
        groups = merged + groups[len(groups) // 2 * 2:]
    best = groups[0]
    kth = best[K - 1]
    z = jnp.zeros_like(top)
    for r in range(K):
        z = z + jnp.exp(best[r] - top)
    iz_ref[...] = 1.0 / z
    for i in range(K):
        cnt = jnp.zeros_like(top)
        for j in range(K):
            if (i, j) in pairs:
                cnt = cnt + jnp.where(a[i] + b[j] >= kth, 1.0, 0.0)
        cnt_ref[i] = cnt
        atop_ref[i] = a[i]
        btop_ref[i] = b[i]


def _peer_select(scores_t, tt):
    npc, nk, T = scores_t.shape
    top = jax.ShapeDtypeStruct((PEER_TOPK, PEER_HEADS, T), F32)
    top_spec = pl.BlockSpec((PEER_TOPK, PEER_HEADS, tt), lambda i: (0, 0, i))
    return pl.pallas_call(
        _peer_select_kernel,
        grid=(T // tt,),
        in_specs=[pl.BlockSpec((npc, nk, tt), lambda i: (0, 0, i))],
        out_specs=[pl.BlockSpec((PEER_HEADS, tt), lambda i: (0, i))] + [top_spec] * 3,
        out_shape=[jax.ShapeDtypeStruct((PEER_HEADS, T), F32)] + [top] * 3,
        scratch_shapes=[pltpu.VMEM((2, PEER_TOPK, PEER_HEADS, tt), F32)],
        compiler_params=_cparams(("parallel",)),
        name="peer_select",
    )(scores_t)


def _twin_words(x):
    assert GATE_DTYPE == BF16
    bits = pltpu.bitcast(x.astype(BF16).astype(F32), jnp.uint32)
    return bits | (bits >> 16)


def _gelu_exact_x2(x):
    return x * (1.0 + lax.erf(x * (1.0 / math.sqrt(2.0))))


def _peer_dense_kernel(h_ref, u_ref, vt_ref, s_ref, iz_ref, cnt_ref, atop_ref, btop_ref, x1_ref, o_ref,
                       n_scr, e1_scr, rank_scr, e2_scr, a_scr, p_scr, yt_scr):
    jt = pl.program_id(1)
    te, tt = a_scr.shape
    nb = te // PEER_NKEYS
    n_grp = nb // SUBLANES
    PK = SUBLANES * (4 // jnp.dtype(GATE_DTYPE).itemsize)
    IBQ = 32 * PK // PEER_NKEYS

    @pl.when(jt == 0)
    def _():
        yt_scr[...] = jnp.zeros_like(yt_scr)
        grouped = (PEER_NKEYS // SUBLANES, SUBLANES, LANES)
        for p in range(PEER_HEADS):
            for st in range(tt // LANES):
                sl = pl.ds(st * LANES, LANES)
                s1 = s_ref[2 * p, :, sl]
                s2 = s_ref[2 * p + 1, :, sl]
                n = jnp.zeros_like(s1)
                rank = jnp.zeros_like(s2)
                for r in range(PEER_TOPK):
                    n = jnp.where(s1 == atop_ref[r, pl.ds(p, 1), sl], cnt_ref[r, pl.ds(p, 1), sl], n)
                    rank = jnp.where(btop_ref[r, pl.ds(p, 1), sl] > s2, r + 1.0, rank)
                e1 = jnp.exp(s1 - atop_ref[0, pl.ds(p, 1), sl])
                n_scr[p, :, :, sl] = _twin_words(n).reshape(grouped)
                e1_scr[p, :, :, sl] = _twin_words(e1).reshape(grouped)
                rank_scr[p, :, sl] = rank.astype(GATE_DTYPE)
                m2 = btop_ref[0, pl.ds(p, 1), sl]
                e2_scr[p, :, sl] = (jnp.exp(s2 - m2) * (0.5 * iz_ref[pl.ds(p, 1), sl])).astype(GATE_DTYPE)

    zero = jnp.zeros((PK, LANES), GATE_DTYPE)
    for sub_tile in range(te // DENSE_SUBTILE):
        erows = pl.ds(sub_tile * DENSE_SUBTILE, DENSE_SUBTILE)
        a_scr[erows, :] = _dot_nt(u_ref[erows, :], h_ref[...])
        for ibg in range(sub_tile * DENSE_SUBTILE // (IBQ * PEER_NKEYS),
                         (sub_tile + 1) * DENSE_SUBTILE // (IBQ * PEER_NKEYS)):
            for st in range(tt // LANES):
                sl = pl.ds(st * LANES, LANES)
                gates = [[zero for _ in range(PEER_NKEYS // PK)] for _ in range(IBQ)]
                for p in range(PEER_HEADS):
                    rank = rank_scr[p, :, sl]
                    e2 = e2_scr[p, :, sl]
                    for q in range(IBQ):
                        ib = ibg * IBQ + q
                        grp = jt * n_grp + ib // SUBLANES
                        sub = pl.ds(ib % SUBLANES, 1)
                        n = pltpu.bitcast(jnp.broadcast_to(n_scr[p, grp, sub, sl], (SUBLANES, LANES)), GATE_DTYPE)
                        e1 = pltpu.bitcast(jnp.broadcast_to(e1_scr[p, grp, sub, sl], (SUBLANES, LANES)), GATE_DTYPE)
                        for r in range(PEER_NKEYS // PK):
                            rr = slice(r * PK, (r + 1) * PK)
                            gates[q][r] = gates[q][r] + jnp.where(rank[rr, :] < n, e2[rr, :], zero) * e1
                for q in range(IBQ):
                    rows = pl.ds((ibg * IBQ + q) * PEER_NKEYS, PEER_NKEYS)
                    gate = jnp.concatenate(gates[q], axis=0)
                    p_scr[rows, sl] = (_gelu_exact_x2(a_scr[rows, sl]).astype(GATE_DTYPE) * gate).astype(BF16)
        yt_scr[...] += jnp.dot(vt_ref[:, erows], p_scr[erows, :], preferred_element_type=F32)

    @pl.when(jt == pl.num_programs(1) - 1)
    def _():
        o_ref[...] = x1_ref[...] + yt_scr[...].T


def _peer_dense(h2, u_bf, vt_bf, scores_t, iz, cnt, atop, btop, x1, tt, te):
    T, D = h2.shape
    E = u_bf.shape[0]
    npc, nk, _ = scores_t.shape
    assert te % (PEER_NKEYS * SUBLANES) == 0 and E % te == 0
    top = pl.BlockSpec((PEER_TOPK, PEER_HEADS, tt), lambda t, j: (0, 0, t))
    return pl.pallas_call(
        _peer_dense_kernel,
        grid=(T // tt, E // te),
        in_specs=[
            pl.BlockSpec((tt, D), lambda t, j: (t, 0)),
            pl.BlockSpec((te, D), lambda t, j: (j, 0)),
            pl.BlockSpec((D, te), lambda t, j: (0, j)),
            pl.BlockSpec((npc, nk, tt), lambda t, j: (0, 0, t)),
            pl.BlockSpec((PEER_HEADS, tt), lambda t, j: (0, t)),
            top, top, top,
            pl.BlockSpec((tt, D), lambda t, j: (t, 0)),
        ],
        out_specs=pl.BlockSpec((tt, D), lambda t, j: (t, 0)),
        out_shape=jax.ShapeDtypeStruct((T, D), F32),
        scratch_shapes=[
            pltpu.VMEM((PEER_HEADS, nk // SUBLANES, SUBLANES, tt), jnp.uint32),
            pltpu.VMEM((PEER_HEADS, nk // SUBLANES, SUBLANES, tt), jnp.uint32),
            pltpu.VMEM((PEER_HEADS, nk, tt), GATE_DTYPE),
            pltpu.VMEM((PEER_HEADS, nk, tt), GATE_DTYPE),
            pltpu.VMEM((te, tt), F32),
            pltpu.VMEM((te, tt), BF16),
            pltpu.VMEM((D, tt), F32),
        ],
        compiler_params=_cparams(("parallel", "arbitrary")),
        name="peer_dense",
    )(h2, u_bf, vt_bf, scores_t, iz, cnt, atop, btop, x1)


def _transpose_cast_kernel(x_ref, o_ref):
    o_ref[...] = x_ref[...].T.astype(o_ref.dtype)


def _transpose_cast(x, dtype, rows):
    E, D = x.shape
    return pl.pallas_call(
        _transpose_cast_kernel,
        grid=(E // rows,),
        in_specs=[pl.BlockSpec((rows, D), lambda i: (i, 0))],
        out_specs=pl.BlockSpec((D, rows), lambda i: (0, i)),
        out_shape=jax.ShapeDtypeStruct((D, E), dtype),
        compiler_params=_cparams(("parallel",)),
        name="value_table_layout",
    )(x)


def _pick(n, pref):
    t = min(n, pref)
    assert n % t == 0, (n, t)
    return t


def _tiles(T, S):
    return {
        "in_proj_rows": _pick(T, 512),
        "scan_steps": _pick(S, 512),
        "mla_rows": _pick(T, 1024),
        "attn_queries": _pick(S, 1024),
        "out_proj_rows": _pick(T, 512),
        "select_tokens": _pick(T, 256),
        "dense_tokens": _pick(T, 512),
        "dense_experts": 2 * PEER_NKEYS * SUBLANES,
    }


def _layer(x2, pos, B, S, attn_norm, w_in, lb_logits, hg_o_norm, q_a_norm, w_q_up, kv_a_norm,
           w_kv_up, q_norm, k_norm, mla_o_norm, w_out, ffn_norm, peer_w_q, peer_sub_keys, peer_u, peer_v):
    T, D = x2.shape
    n_hg = 5 * HG_HEADS * HG_DK
    row = lambda a: a.reshape(1, -1).astype(F32)

    w_hg = w_in[:, :n_hg].astype(BF16)
    n_mla = w_in.shape[1] - n_hg
    w_mla = jnp.pad(w_in[:, n_hg:], ((0, 0), (0, -n_mla % LANES))).astype(BF16)
    pad_r = LANES - MLA_ROPE
    wq = w_q_up.reshape(MLA_Q_LORA, MLA_HEADS, MLA_QK)
    wq = jnp.pad(wq, ((0, 0), (0, 0), (0, MLA_QPAD - MLA_QK))).reshape(MLA_Q_LORA, MLA_HEADS * MLA_QPAD)
    wq = wq.astype(BF16)
    wkv = w_kv_up.astype(BF16)
    qgn, qgr = row(q_norm[:MLA_NOPE]), row(jnp.pad(q_norm[MLA_NOPE:], (0, pad_r)))
    kgn, kgr = row(k_norm[:MLA_NOPE]), row(jnp.pad(k_norm[MLA_NOPE:], (0, pad_r)))
    inv_freq = 1.0 / (ROPE_THETA ** (jnp.arange(0, MLA_ROPE, 2, dtype=F32) / MLA_ROPE))
    invf = row(jnp.concatenate([inv_freq, inv_freq, jnp.zeros((pad_r,), F32)]))
    keys = peer_sub_keys.reshape(2 * PEER_HEADS, PEER_NKEYS, PEER_DHALF).astype(BF16)
    u_bf = peer_u.astype(BF16)
    vt_bf = _transpose_cast(peer_v, BF16, _pick(peer_v.shape[0], 1024))

    tile = _tiles(T, S)
    hg, mla = _in_proj(x2, row(attn_norm), w_hg, w_mla, tile["in_proj_rows"])
    o_f = _hgrn_scan(hg, lb_logits[:, 0, :], B, S, False, tile["scan_steps"])
    o_b = _hgrn_scan(hg, lb_logits[:, 1, :], B, S, True, tile["scan_steps"])
    q, k, v = _mla_prep(mla, pos, invf, row(q_a_norm), row(kv_a_norm), wq, wkv, qgn, qgr, kgn, kgr,
                        tile["mla_rows"])
    y_mla = _attention(q, k, v, mla_o_norm.reshape(1, -1).astype(F32), B, S, tile["attn_queries"])
    x1, h2, scores_t = _out_proj(o_f, o_b, hg, y_mla, x2, row(hg_o_norm), w_out.astype(BF16),
                                 row(ffn_norm), peer_w_q.astype(BF16), keys, tile["out_proj_rows"])
    iz, cnt, atop, btop = _peer_select(scores_t, tile["select_tokens"])
    return _peer_dense(h2, u_bf, vt_bf, scores_t, iz, cnt, atop, btop, x1,
                       tile["dense_tokens"], tile["dense_experts"])


def kernel(x, positions, attn_norm, w_in, hg_lb_logits, hg_o_norm, q_a_norm, w_q_up, kv_a_norm, w_kv_up,
           q_norm, k_norm, mla_o_norm, w_out, ffn_norm, peer_w_q, peer_sub_keys, peer_u, peer_v):
    B, S, D = x.shape
    x2 = x.reshape(B * S, D)
    pos = positions.reshape(B * S, 1)
    for l in range(attn_norm.shape[0]):
        assert l == 0
        x2 = _layer(x2, pos, B, S, attn_norm[l], w_in[l], hg_lb_logits, hg_o_norm[l], q_a_norm[l],
                    w_q_up[l], kv_a_norm[l], w_kv_up[l], q_norm[l], k_norm[l], mla_o_norm[l], w_out[l],
                    ffn_norm[l], peer_w_q[l], peer_sub_keys[l], peer_u[l], peer_v[l])
    return x2.reshape(B, S, D)
```

```python
import functools
import math

import jax
import jax.numpy as jnp
from jax import lax
from jax.experimental import pallas as pl
from jax.experimental.pallas import tpu as pltpu

F32 = jnp.float32
BF16 = jnp.bfloat16
EPS = 1e-6
LANES = 128
SUBLANES = 8
NEG_INF = float("-inf")

HG_HEADS = 4
HG_DK = 128
HG_CHUNK = 64
HG_HALF = HG_CHUNK // 2
HG_QUARTER = HG_CHUNK // 4
HG_BATCH_PER_STEP = 4
MLA_HEADS = 4
MLA_Q_LORA = 384
MLA_KV_LORA = 256
MLA_NOPE = 128
MLA_ROPE = 64
MLA_V = 128
MLA_QK = MLA_NOPE + MLA_ROPE
MLA_QPAD = 2 * LANES
ROPE_THETA = 10000.0
ATTN_CHUNKS = 4
OUT_CHUNKS = 2
PEER_HEADS = 8
PEER_NKEYS = 128
PEER_TOPK = 16
PEER_DHALF = 128
GATE_DTYPE = BF16
DENSE_SUBTILE = 1024

VMEM_LIMIT = 56 * 1024 * 1024


def _cparams(sem):
    return pltpu.CompilerParams(dimension_semantics=sem, vmem_limit_bytes=VMEM_LIMIT)


def _rms(x, gain):
    ms = jnp.mean(x * x, axis=-1, keepdims=True)
    return x * lax.rsqrt(ms + EPS) * gain


def _sigmoid(x):
    return 1.0 / (1.0 + jnp.exp(-x))


def _dot_nt(a, b):
    return lax.dot_general(a, b, (((1,), (1,)), ((), ())), preferred_element_type=F32)


def _dot_tn(a, b):
    return lax.dot_general(a, b, (((0,), (0,)), ((), ())), preferred_element_type=F32)


def _in_proj_kernel(x_ref, g_ref, whg_ref, wmla_ref, hg_ref, mla_ref):
    h = _rms(x_ref[...], g_ref[...]).astype(BF16)
    hg_ref[...] = jnp.dot(h, whg_ref[...], preferred_element_type=F32)
    mla_ref[...] = jnp.dot(h, wmla_ref[...], preferred_element_type=F32)


def _in_proj(x2, gain, w_hg, w_mla, tm):
    T, D = x2.shape
    n_hg, n_mla = w_hg.shape[1], w_mla.shape[1]
    return pl.pallas_call(
        _in_proj_kernel,
        grid=(T // tm,),
        in_specs=[
            pl.BlockSpec((tm, D), lambda i: (i, 0)),
            pl.BlockSpec((1, D), lambda i: (0, 0)),
            pl.BlockSpec((D, n_hg), lambda i: (0, 0)),
            pl.BlockSpec((D, n_mla), lambda i: (0, 0)),
        ],
        out_specs=[
            pl.BlockSpec((tm, n_hg), lambda i: (i, 0)),
            pl.BlockSpec((tm, n_mla), lambda i: (i, 0)),
        ],
        out_shape=[jax.ShapeDtypeStruct((T, n_hg), F32), jax.ShapeDtypeStruct((T, n_mla), F32)],
        compiler_params=_cparams(("parallel",)),
        name="in_proj",
    )(x2, gain, w_hg, w_mla)


def _hgrn_kernel(q_ref, f_ref, v_ref, lb_ref, o_ref, st_ref, b_scr, k_scr, p_scr, *, reverse, n_chunks):
    C, Hf, Qt, G = HG_CHUNK, HG_HALF, HG_QUARTER, SUBLANES
    Et = G
    assert C == 8 * Et
    n_batch = q_ref.shape[0]
    n_heads = st_ref.shape[0] // n_batch
    chains = [(bi, hh) for bi in range(n_batch) for hh in range(n_heads)]

    @pl.when(pl.program_id(1) == 0)
    def _():
        st_ref[...] = jnp.zeros_like(st_ref)

    logits = lb_ref[...]
    ex = jnp.exp(logits - jnp.max(logits, axis=0, keepdims=True))
    lb_all = ex[0:1, :] / jnp.sum(ex, axis=0, keepdims=True)

    row = lax.broadcasted_iota(jnp.int32, (C, C), 0)
    col = lax.broadcasted_iota(jnp.int32, (C, C), 1)
    tri = (col >= row) if reverse else (col <= row)
    cum_mat = tri.astype(F32)
    same_half = (row >= Hf) == (col >= Hf)
    same_quarter = (row // Qt) == (col // Qt)
    row_c = lax.broadcasted_iota(jnp.int32, (C, LANES), 0)
    q_side = (row_c < Hf) if reverse else (row_c >= Hf)
    odd_quarter = (row_c // Qt) % 2 == 1
    q_side4 = jnp.logical_not(odd_quarter) if reverse else odd_quarter
    low_half = row_c < Hf
    odd_eighth = (row_c // Et) % 2 == 1
    q_side8 = jnp.logical_not(odd_eighth) if reverse else odd_eighth
    quarter_c = row_c // Qt
    lane_g = lax.broadcasted_iota(jnp.int32, (G, LANES), 1)
    ones = jnp.ones((LANES, LANES), BF16)
    mid_row = Hf if reverse else Hf - 1
    qmid_rows = (Qt, Hf + Qt) if reverse else (Qt - 1, Hf + Qt - 1)
    emid_rows = [qb * Qt + (Et if reverse else Et - 1) for qb in range(C // Qt)]
    end_row = 0 if reverse else C - 1

    def gates(ci, r0):
        bi, hh = chains[ci]
        cols = slice(hh * HG_DK, (hh + 1) * HG_DK)
        lb = lb_all[:, cols]
        q_raw = q_ref[bi, pl.ds(r0, C), cols]
        q = q_raw * _sigmoid(q_raw)
        f = lb + (1.0 - lb) * _sigmoid(f_ref[bi, pl.ds(r0, C), cols])
        k = 1.0 - f
        vb = v_ref[bi, pl.ds(r0, C), cols].astype(BF16)
        b = jnp.dot(cum_mat, jnp.log2(f), precision=lax.Precision.HIGHEST,
                    preferred_element_type=F32)
        b_scr[ci] = b
        k_scr[ci] = k
        return q, k, vb, b

    def state_and_offdiag(ci, q, k, vb, b):
        b_mid = b_scr[ci, pl.ds(mid_row, 1), :]
        b_end = b_scr[ci, pl.ds(end_row, 1), :]
        st = st_ref[ci]
        o = _dot_nt((q * jnp.exp2(b)).astype(BF16), st.astype(BF16))
        kd = (k * jnp.exp2(b_end - b)).astype(BF16)
        st_ref[ci] = st * jnp.exp2(b_end) + _dot_tn(vb, kd)
        qt = jnp.where(q_side, q * jnp.exp2(b - b_mid), 0.0).astype(BF16)
        kt = jnp.where(q_side, 0.0, k * jnp.exp2(b_mid - b)).astype(BF16)
        b_q = jnp.where(low_half, b_scr[ci, pl.ds(qmid_rows[0], 1), :], b_scr[ci, pl.ds(qmid_rows[1], 1), :])
        qt4 = jnp.where(q_side4, q * jnp.exp2(b - b_q), 0.0).astype(BF16)
        kt4 = jnp.where(q_side4, 0.0, k * jnp.exp2(b_q - b)).astype(BF16)
        b_e = b_scr[ci, pl.ds(emid_rows[-1], 1), :]
        for qb in reversed(range(C // Qt - 1)):
            b_e = jnp.where(quarter_c == qb, b_scr[ci, pl.ds(emid_rows[qb], 1), :], b_e)
        qt8 = jnp.where(q_side8, q * jnp.exp2(b - b_e), 0.0).astype(BF16)
        kt8 = jnp.where(q_side8, 0.0, k * jnp.exp2(b_e - b)).astype(BF16)
        return o, _dot_nt(qt, kt), _dot_nt(qt4, kt4), _dot_nt(qt8, kt8)

    def diag_products(ci, q, b):
        for s in range(C):
            rows = slice(s // G * G, s // G * G + G)
            bs = b_scr[ci, pl.ds(s, 1), :]
            ks = k_scr[ci, pl.ds(s, 1), :]
            p_scr[ci, pl.ds(s * G, G), :] = (q[rows, :] * ks) * jnp.exp2(b[rows, :] - bs)

    def diag_scores(rs):
        groups = []
        for g in range(C // G):
            acc = jnp.zeros((G, LANES), F32)
            for s in range(g * G, (g + 1) * G):
                acc = jnp.where(lane_g == s, rs[s * G:(s + 1) * G, :], acc)
            groups.append(acc)
        return jnp.concatenate(groups, axis=0)[:, :C]

    def chunk_body(ci, carry):
        c = (n_chunks - 1 - ci) if reverse else ci
        r0 = pl.multiple_of(c * C, C)
        ids = range(len(chains))
        qkvb = [gates(ci, r0) for ci in ids]
        o_sc = [state_and_offdiag(ci, *qkvb[ci]) for ci in ids]
        for ci in ids:
            diag_products(ci, qkvb[ci][0], qkvb[ci][3])
        rs = [jnp.dot(p_scr[ci].astype(BF16), ones, preferred_element_type=F32) for ci in ids]
        for ci in ids:
            bi, hh = chains[ci]
            o, sc_half, sc_quarter, sc_eighth = o_sc[ci]
            inner = jnp.where(same_quarter, diag_scores(rs[ci]) + sc_eighth, sc_quarter)
            scores = jnp.where(same_half, jnp.where(tri, inner, 0.0), sc_half)
            o = o + jnp.dot(scores.astype(BF16), qkvb[ci][2], preferred_element_type=F32)
            o_ref[bi, pl.ds(r0, C), hh * HG_DK:(hh + 1) * HG_DK] = o
        return carry

    lax.fori_loop(0, n_chunks, chunk_body, 0)


def _hgrn_diag_rows():
    return HG_CHUNK * SUBLANES


def _hgrn_scan(hg, lb_logits_dir, B, S, reverse, tb):
    T = hg.shape[0]
    W = HG_HEADS * HG_DK
    nblk = S // tb
    f_blk = 2 if reverse else 1
    v_blk = 3
    nbat = HG_BATCH_PER_STEP if B % HG_BATCH_PER_STEP == 0 else 1
    n_chains = nbat * HG_HEADS
    hg4 = hg.reshape(B // nbat, nbat, S, hg.shape[1])

    def spec(col_blk):
        return pl.BlockSpec((None, nbat, tb, W),
                            lambda g, i: (g, 0, (nblk - 1 - i) if reverse else i, col_blk))

    kern = functools.partial(_hgrn_kernel, reverse=reverse, n_chunks=tb // HG_CHUNK)
    out = pl.pallas_call(
        kern,
        grid=(B // nbat, nblk),
        in_specs=[spec(0), spec(f_blk), spec(v_blk),
                  pl.BlockSpec((lb_logits_dir.shape[0], W), lambda g, i: (0, 0))],
        out_specs=spec(0),
        out_shape=jax.ShapeDtypeStruct((B // nbat, nbat, S, W), F32),
        scratch_shapes=[
            pltpu.VMEM((n_chains, HG_DK, HG_DK), F32),
            pltpu.VMEM((n_chains, HG_CHUNK, HG_DK), F32),
            pltpu.VMEM((n_chains, HG_CHUNK, HG_DK), F32),
            pltpu.VMEM((n_chains, _hgrn_diag_rows(), HG_DK), F32),
        ],
        compiler_params=_cparams(("parallel", "arbitrary")),
        name="hgrn_rev" if reverse else "hgrn_fwd",
    )(hg4, hg4, hg4, lb_logits_dir)
    return out.reshape(T, W)


def _mla_prep_kernel(mla_ref, pos_ref, invf_ref, qa_ref, kva_ref, wq_ref, wkv_ref,
                     qgn_ref, qgr_ref, kgn_ref, kgr_ref, q_ref, k_ref, v_ref):
    tm = mla_ref.shape[0]
    lane = lax.broadcasted_iota(jnp.int32, (tm, LANES), 1)
    ang = pos_ref[...].astype(F32) * invf_ref[...]
    cos_t = jnp.where(lane < MLA_ROPE, jnp.cos(ang), 0.0)
    sin_a = jnp.sin(ang)
    sin_t = jnp.where(lane < MLA_ROPE // 2, -sin_a, jnp.where(lane < MLA_ROPE, sin_a, 0.0))

    def rope(t):
        swapped = jnp.where(lane < MLA_ROPE // 2,
                            pltpu.roll(t, LANES - MLA_ROPE // 2, 1),
                            pltpu.roll(t, MLA_ROPE // 2, 1))
        return t * cos_t + swapped * sin_t

    def norm_rope_part(t, gain):
        ms = jnp.sum(t * t, axis=-1, keepdims=True) * (1.0 / MLA_ROPE)
        return t * lax.rsqrt(ms + EPS) * gain

    cq = mla_ref[:, 0:MLA_Q_LORA]
    ckv = mla_ref[:, MLA_Q_LORA:MLA_Q_LORA + MLA_KV_LORA]
    kr = mla_ref[:, MLA_Q_LORA + MLA_KV_LORA:]
    q_up = jnp.dot(_rms(cq, qa_ref[...]).astype(BF16), wq_ref[...], preferred_element_type=F32)
    kv_up = jnp.dot(_rms(ckv, kva_ref[...]).astype(BF16), wkv_ref[...], preferred_element_type=F32)
    k_rope = rope(norm_rope_part(kr, kgr_ref[...])).astype(BF16)
    for h in range(MLA_HEADS):
        c0 = h * MLA_QPAD
        q_ref[:, c0:c0 + LANES] = _rms(q_up[:, c0:c0 + LANES], qgn_ref[...]).astype(BF16)
        q_ref[:, c0 + LANES:c0 + 2 * LANES] = rope(
            norm_rope_part(q_up[:, c0 + LANES:c0 + 2 * LANES], qgr_ref[...])).astype(BF16)
        k_ref[:, c0:c0 + LANES] = _rms(kv_up[:, c0:c0 + LANES], kgn_ref[...]).astype(BF16)
        k_ref[:, c0 + LANES:c0 + 2 * LANES] = k_rope
        v_ref[:, h * MLA_V:(h + 1) * MLA_V] = kv_up[:, c0 + LANES:c0 + 2 * LANES].astype(BF16)


def _mla_prep(mla, pos, invf, qa, kva, wq, wkv, qgn, qgr, kgn, kgr, tm):
    T, n_mla = mla.shape
    full = lambda a: pl.BlockSpec(a.shape, lambda i: (0, 0))
    return pl.pallas_call(
        _mla_prep_kernel,
        grid=(T // tm,),
        in_specs=[
            pl.BlockSpec((tm, n_mla), lambda i: (i, 0)),
            pl.BlockSpec((tm, 1), lambda i: (i, 0)),
            full(invf), full(qa), full(kva), full(wq), full(wkv),
            full(qgn), full(qgr), full(kgn), full(kgr),
        ],
        out_specs=[
            pl.BlockSpec((tm, MLA_HEADS * MLA_QPAD), lambda i: (i, 0)),
            pl.BlockSpec((tm, MLA_HEADS * MLA_QPAD), lambda i: (i, 0)),
            pl.BlockSpec((tm, MLA_HEADS * MLA_V), lambda i: (i, 0)),
        ],
        out_shape=[
            jax.ShapeDtypeStruct((T, MLA_HEADS * MLA_QPAD), BF16),
            jax.ShapeDtypeStruct((T, MLA_HEADS * MLA_QPAD), BF16),
            jax.ShapeDtypeStruct((T, MLA_HEADS * MLA_V), BF16),
        ],
        compiler_params=_cparams(("parallel",)),
        name="mla_prep",
    )(mla, pos, invf, qa, kva, wq, wkv, qgn, qgr, kgn, kgr)


def _attn_kernel(q_ref, k_ref, v_ref, g_ref, o_ref):
    c = MLA_QK ** -0.5 * math.log2(math.e)
    tq = q_ref.shape[0]
    rc = tq // ATTN_CHUNKS
    rows = [pl.ds(i * rc, rc) for i in range(ATTN_CHUNKS)]
    s = [_dot_nt(q_ref[r, :], k_ref[...]) for r in rows]
    p = [jnp.exp2((si - jnp.max(si, axis=-1, keepdims=True)) * c) for si in s]
    l = [jnp.sum(pi, axis=-1, keepdims=True) for pi in p]
    o = [jnp.dot(pi.astype(BF16), v_ref[...], preferred_element_type=F32) for pi in p]
    for r, oi, li in zip(rows, o, l):
        o_ref[r, :] = _rms(oi / li, g_ref[...]).astype(o_ref.dtype)


def _attention(q, k, v, o_gain, B, S, tq):
    T = q.shape[0]
    nq = S // tq
    return pl.pallas_call(
        _attn_kernel,
        grid=(B, MLA_HEADS, nq),
        in_specs=[
            pl.BlockSpec((tq, MLA_QPAD), lambda b, h, i: (b * nq + i, h)),
            pl.BlockSpec((S, MLA_QPAD), lambda b, h, i: (b, h)),
            pl.BlockSpec((S, MLA_V), lambda b, h, i: (b, h)),
            pl.BlockSpec((1, MLA_V), lambda b, h, i: (0, h)),
        ],
        out_specs=pl.BlockSpec((tq, MLA_V), lambda b, h, i: (b * nq + i, h)),
        out_shape=jax.ShapeDtypeStruct((T, MLA_HEADS * MLA_V), BF16),
        compiler_params=_cparams(("parallel", "parallel", "arbitrary")),
        name="mla_attention",
    )(q, k, v, o_gain)


def _out_proj_kernel(of_ref, ob_ref, g_ref, ym_ref, x_ref, hgg_ref, wo_ref, fg_ref, wq_ref, keys_ref,
                     x1_ref, h2_ref, st_ref):
    tm = x_ref.shape[0]
    rc = tm // OUT_CHUNKS
    rows = [pl.ds(i * rc, rc) for i in range(OUT_CHUNKS)]
    w_hg_rows = of_ref.shape[1]

    def mixer_out(r):
        o = of_ref[r, :] + ob_ref[r, :]
        gate_raw = g_ref[r, :]
        gate = gate_raw * _sigmoid(gate_raw)
        parts = []
        for h in range(HG_HEADS):
            sl = slice(h * HG_DK, (h + 1) * HG_DK)
            parts.append(_rms(o[:, sl], hgg_ref[:, sl]) * gate[:, sl])
        return jnp.concatenate(parts, axis=-1).astype(BF16)

    y_hg = [mixer_out(r) for r in rows]
    mix = [jnp.dot(y, wo_ref[0:w_hg_rows, :], preferred_element_type=F32)
           + jnp.dot(ym_ref[r, :], wo_ref[w_hg_rows:, :], preferred_element_type=F32)
           for y, r in zip(y_hg, rows)]
    x1 = [x_ref[r, :] + m for r, m in zip(rows, mix)]
    h2 = [_rms(x, fg_ref[...]).astype(BF16) for x in x1]
    for r, x, h in zip(rows, x1, h2):
        x1_ref[r, :] = x
        h2_ref[r, :] = h
    pq = [jnp.dot(h, wq_ref[...], preferred_element_type=F32).astype(BF16) for h in h2]
    pq = jnp.concatenate(pq, axis=0)
    for pc in range(keys_ref.shape[0]):
        st_ref[pc] = _dot_nt(keys_ref[pc], pq[:, pc * PEER_DHALF:(pc + 1) * PEER_DHALF])


def _out_proj(o_f, o_b, hg, y_mla, x2, hg_gain, w_out, ffn_gain, w_pq, keys, tm):
    T, D = x2.shape
    W = o_f.shape[1]
    npc = keys.shape[0]
    full2 = lambda a: pl.BlockSpec(a.shape, lambda i: (0, 0))
    return pl.pallas_call(
        _out_proj_kernel,
        grid=(T // tm,),
        in_specs=[
            pl.BlockSpec((tm, W), lambda i: (i, 0)),
            pl.BlockSpec((tm, W), lambda i: (i, 0)),
            pl.BlockSpec((tm, W), lambda i: (i, 4)),
            pl.BlockSpec((tm, y_mla.shape[1]), lambda i: (i, 0)),
            pl.BlockSpec((tm, D), lambda i: (i, 0)),
            full2(hg_gain), full2(w_out), full2(ffn_gain), full2(w_pq),
            pl.BlockSpec(keys.shape, lambda i: (0, 0, 0)),
        ],
        out_specs=[
            pl.BlockSpec((tm, D), lambda i: (i, 0)),
            pl.BlockSpec((tm, D), lambda i: (i, 0)),
            pl.BlockSpec((npc, PEER_NKEYS, tm), lambda i: (0, 0, i)),
        ],
        out_shape=[
            jax.ShapeDtypeStruct((T, D), F32),
            jax.ShapeDtypeStruct((T, D), BF16),
            jax.ShapeDtypeStruct((npc, PEER_NKEYS, T), F32),
        ],
        compiler_params=_cparams(("parallel",)),
        name="out_proj_peer_query",
    )(o_f, o_b, hg, y_mla, x2, hg_gain, w_out, ffn_gain, w_pq, keys)


def _sorting_network(n):
    pairs = []
    p = 1
    while p < n:
        k = p
        while k >= 1:
            for j in range(k % p, n - k, 2 * k):
                for i in range(min(k, n - j - k)):
                    if (i + j) // (2 * p) == (i + j + k) // (2 * p):
                        pairs.append((i + j, i + j + k))
            k //= 2
        p *= 2
    return pairs


def _bitonic_merge_network(n):
    pairs = []
    stride = n // 2
    while stride >= 1:
        pairs += [(i, i + stride) for i in range(n) if not i & stride]
        stride //= 2
    return pairs


def _compare_exchange(xs, pairs):
    xs = list(xs)
    for i, j in pairs:
        xs[i], xs[j] = jnp.maximum(xs[i], xs[j]), jnp.minimum(xs[i], xs[j])
    return xs


def _peer_select_kernel(s_ref, iz_ref, cnt_ref, atop_ref, btop_ref, top_scr):
    K = PEER_TOPK
    n_vregs = PEER_NKEYS // SUBLANES
    assert n_vregs == K
    sort_pairs = _sorting_network(K)
    merge_pairs = _bitonic_merge_network(K)
    for p in range(PEER_HEADS):
        for c in range(2):
            lists = _compare_exchange(
                [s_ref[2 * p + c, pl.ds(g * SUBLANES, SUBLANES), :] for g in range(n_vregs)], sort_pairs)
            shift = SUBLANES // 2
            while shift >= 1:
                other = [pltpu.roll(x, shift, 0) for x in lists]
                lists = _compare_exchange(
                    [jnp.maximum(lists[r], other[K - 1 - r]) for r in range(K)], merge_pairs)
                shift //= 2
            for r in range(K):
                top_scr[c, r, pl.ds(p, 1), :] = lists[r][0:1, :]
    a = [top_scr[0, r] for r in range(K)]
    b = [top_scr[1, r] for r in range(K)]
    pairs = [(i, j) for i in range(K) for j in range(K) if (i + 1) * (j + 1) <= K]
    cands = [a[i] + b[j] for i, j in pairs]
    top = a[0] + b[0]
    cands += [jnp.full_like(top, NEG_INF)] * (-len(cands) % K)
    groups = [_compare_exchange(cands[g:g + K], sort_pairs) for g in range(0, len(cands), K)]
    while len(groups) > 1:
        merged = [_compare_exchange([jnp.maximum(x[r], y[K - 1 - r]) for r in range(K)], merge_pairs)
                  for x, y in zip(groups[0::2], groups[1::2])]
        groups = merged + groups[len(groups) // 2 * 2:]
    best = groups[0]
    kth = best[K - 1]
    z = jnp.zeros_like(top)
    for r in range(K):
        z = z + jnp.exp(best[r] - top)
    iz_ref[...] = 1.0 / z
    for i in range(K):
        cnt = jnp.zeros_like(top)
        for j in range(K):
            if (i, j) in pairs:
                cnt = cnt + jnp.where(a[i] + b[j] >= kth, 1.0, 0.0)
        cnt_ref[i] = cnt
        atop_ref[i] = a[i]
        btop_ref[i] = b[i]


def _peer_select(scores_t, tt):
    npc, nk, T = scores_t.shape
    top = jax.ShapeDtypeStruct((PEER_TOPK, PEER_HEADS, T), F32)
    top_spec = pl.BlockSpec((PEER_TOPK, PEER_HEADS, tt), lambda i: (0, 0, i))
    return pl.pallas_call(
        _peer_select_kernel,
        grid=(T // tt,),
        in_specs=[pl.BlockSpec((npc, nk, tt), lambda i: (0, 0, i))],
        out_specs=[pl.BlockSpec((PEER_HEADS, tt), lambda i: (0, i))] + [top_spec] * 3,
        out_shape=[jax.ShapeDtypeStruct((PEER_HEADS, T), F32)] + [top] * 3,
        scratch_shapes=[pltpu.VMEM((2, PEER_TOPK, PEER_HEADS, tt), F32)],
        compiler_params=_cparams(("parallel",)),
        name="peer_select",
    )(scores_t)


def _gelu_exact_x2(x):
    return x * (1.0 + lax.erf(x * (1.0 / math.sqrt(2.0))))


def _peer_dense_kernel(h_ref, u_ref, vt_ref, s_ref, iz_ref, cnt_ref, atop_ref, btop_ref, x1_ref, o_ref,
                       n_scr, e1_scr, rank_scr, e2_scr, a_scr, p_scr, yt_scr):
    jt = pl.program_id(1)
    te, tt = a_scr.shape
    nb = te // PEER_NKEYS
    n_grp = nb // SUBLANES
    PK = SUBLANES * (4 // jnp.dtype(GATE_DTYPE).itemsize)
    IBQ = 32 * PK // PEER_NKEYS

    @pl.when(jt == 0)
    def _():
        yt_scr[...] = jnp.zeros_like(yt_scr)
        grouped = (PEER_NKEYS // SUBLANES, SUBLANES, LANES)
        for p in range(PEER_HEADS):
            for st in range(tt // LANES):
                sl = pl.ds(st * LANES, LANES)
                s1 = s_ref[2 * p, :, sl]
                s2 = s_ref[2 * p + 1, :, sl]
                n = jnp.zeros_like(s1)
                rank = jnp.zeros_like(s2)
                for r in range(PEER_TOPK):
                    n = jnp.where(s1 == atop_ref[r, pl.ds(p, 1), sl], cnt_ref[r, pl.ds(p, 1), sl], n)
                    rank = jnp.where(btop_ref[r, pl.ds(p, 1), sl] > s2, r + 1.0, rank)
                n_scr[p, :, :, sl] = n.reshape(grouped)
                e1_scr[p, :, :, sl] = jnp.exp(s1 - atop_ref[0, pl.ds(p, 1), sl]).reshape(grouped)
                rank_scr[p, :, sl] = rank.astype(GATE_DTYPE)
                m2 = btop_ref[0, pl.ds(p, 1), sl]
                e2_scr[p, :, sl] = (jnp.exp(s2 - m2) * (0.5 * iz_ref[pl.ds(p, 1), sl])).astype(GATE_DTYPE)

    zero = jnp.zeros((PK, LANES), GATE_DTYPE)
    for sub_tile in range(te // DENSE_SUBTILE):
        erows = pl.ds(sub_tile * DENSE_SUBTILE, DENSE_SUBTILE)
        a_scr[erows, :] = _dot_nt(u_ref[erows, :], h_ref[...])
        for ibg in range(sub_tile * DENSE_SUBTILE // (IBQ * PEER_NKEYS),
                         (sub_tile + 1) * DENSE_SUBTILE // (IBQ * PEER_NKEYS)):
            for st in range(tt // LANES):
                sl = pl.ds(st * LANES, LANES)
                gates = [[zero for _ in range(PEER_NKEYS // PK)] for _ in range(IBQ)]
                for p in range(PEER_HEADS):
                    rank = rank_scr[p, :, sl]
                    e2 = e2_scr[p, :, sl]
                    for q in range(IBQ):
                        ib = ibg * IBQ + q
                        grp = jt * n_grp + ib // SUBLANES
                        sub = pl.ds(ib % SUBLANES, 1)
                        n = jnp.broadcast_to(n_scr[p, grp, sub, sl], (PK, LANES)).astype(GATE_DTYPE)
                        e1 = jnp.broadcast_to(e1_scr[p, grp, sub, sl], (PK, LANES)).astype(GATE_DTYPE)
                        for r in range(PEER_NKEYS // PK):
                            rr = slice(r * PK, (r + 1) * PK)
                            keep = jnp.minimum(jnp.maximum(n - rank[rr, :], zero), e1)
                            gates[q][r] = gates[q][r] + e2[rr, :] * keep
                for q in range(IBQ):
                    rows = pl.ds((ibg * IBQ + q) * PEER_NKEYS, PEER_NKEYS)
                    gate = jnp.concatenate(gates[q], axis=0)
                    p_scr[rows, sl] = (_gelu_exact_x2(a_scr[rows, sl]).astype(GATE_DTYPE) * gate).astype(BF16)
        yt_scr[...] += jnp.dot(vt_ref[:, erows], p_scr[erows, :], preferred_element_type=F32)

    @pl.when(jt == pl.num_programs(1) - 1)
    def _():
        o_ref[...] = x1_ref[...] + yt_scr[...].T


def _peer_dense(h2, u_bf, vt_bf, scores_t, iz, cnt, atop, btop, x1, tt, te):
    T, D = h2.shape
    E = u_bf.shape[0]
    npc, nk, _ = scores_t.shape
    assert te % (PEER_NKEYS * SUBLANES) == 0 and E % te == 0
    top = pl.BlockSpec((PEER_TOPK, PEER_HEADS, tt), lambda t, j: (0, 0, t))
    return pl.pallas_call(
        _peer_dense_kernel,
        grid=(T // tt, E // te),
        in_specs=[
            pl.BlockSpec((tt, D), lambda t, j: (t, 0)),
            pl.BlockSpec((te, D), lambda t, j: (j, 0)),
            pl.BlockSpec((D, te), lambda t, j: (0, j)),
            pl.BlockSpec((npc, nk, tt), lambda t, j: (0, 0, t)),
            pl.BlockSpec((PEER_HEADS, tt), lambda t, j: (0, t)),
            top, top, top,
            pl.BlockSpec((tt, D), lambda t, j: (t, 0)),
        ],
        out_specs=pl.BlockSpec((tt, D), lambda t, j: (t, 0)),
        out_shape=jax.ShapeDtypeStruct((T, D), F32),
        scratch_shapes=[
            pltpu.VMEM((PEER_HEADS, nk // SUBLANES, SUBLANES, tt), F32),
            pltpu.VMEM((PEER_HEADS, nk // SUBLANES, SUBLANES, tt), F32),
            pltpu.VMEM((PEER_HEADS, nk, tt), GATE_DTYPE),
            pltpu.VMEM((PEER_HEADS, nk, tt), GATE_DTYPE),
            pltpu.VMEM((te, tt), F32),
            pltpu.VMEM((te, tt), BF16),
            pltpu.VMEM((D, tt), F32),
        ],
        compiler_params=_cparams(("parallel", "arbitrary")),
        name="peer_dense",
    )(h2, u_bf, vt_bf, scores_t, iz, cnt, atop, btop, x1)


def _transpose_cast_kernel(x_ref, o_ref):
    o_ref[...] = x_ref[...].T.astype(o_ref.dtype)


def _transpose_cast(x, dtype, rows):
    E, D = x.shape
    return pl.pallas_call(
        _transpose_cast_kernel,
        grid=(E // rows,),
        in_specs=[pl.BlockSpec((rows, D), lambda i: (i, 0))],
        out_specs=pl.BlockSpec((D, rows), lambda i: (0, i)),
        out_shape=jax.ShapeDtypeStruct((D, E), dtype),
        compiler_params=_cparams(("parallel",)),
        name="value_table_layout",
    )(x)


def _pick(n, pref):
    t = min(n, pref)
    assert n % t == 0, (n, t)
    return t


def _tiles(T, S):
    return {
        "in_proj_rows": _pick(T, 512),
        "scan_steps": _pick(S, 512),
        "mla_rows": _pick(T, 1024),
        "attn_queries": _pick(S, 1024),
        "out_proj_rows": _pick(T, 512),
        "select_tokens": _pick(T, 256),
        "dense_tokens": _pick(T, 512),
        "dense_experts": 2 * PEER_NKEYS * SUBLANES,
    }


def _layer(x2, pos, B, S, attn_norm, w_in, lb_logits, hg_o_norm, q_a_norm, w_q_up, kv_a_norm,
           w_kv_up, q_norm, k_norm, mla_o_norm, w_out, ffn_norm, peer_w_q, peer_sub_keys, peer_u, peer_v):
    T, D = x2.shape
    n_hg = 5 * HG_HEADS * HG_DK
    row = lambda a: a.reshape(1, -1).astype(F32)

    w_hg = w_in[:, :n_hg].astype(BF16)
    n_mla = w_in.shape[1] - n_hg
    w_mla = jnp.pad(w_in[:, n_hg:], ((0, 0), (0, -n_mla % LANES))).astype(BF16)
    pad_r = LANES - MLA_ROPE
    wq = w_q_up.reshape(MLA_Q_LORA, MLA_HEADS, MLA_QK)
    wq = jnp.pad(wq, ((0, 0), (0, 0), (0, MLA_QPAD - MLA_QK))).reshape(MLA_Q_LORA, MLA_HEADS * MLA_QPAD)
    wq = wq.astype(BF16)
    wkv = w_kv_up.astype(BF16)
    qgn, qgr = row(q_norm[:MLA_NOPE]), row(jnp.pad(q_norm[MLA_NOPE:], (0, pad_r)))
    kgn, kgr = row(k_norm[:MLA_NOPE]), row(jnp.pad(k_norm[MLA_NOPE:], (0, pad_r)))
    inv_freq = 1.0 / (ROPE_THETA ** (jnp.arange(0, MLA_ROPE, 2, dtype=F32) / MLA_ROPE))
    invf = row(jnp.concatenate([inv_freq, inv_freq, jnp.zeros((pad_r,), F32)]))
    keys = peer_sub_keys.reshape(2 * PEER_HEADS, PEER_NKEYS, PEER_DHALF).astype(BF16)
    u_bf = peer_u.astype(BF16)
    vt_bf = _transpose_cast(peer_v, BF16, _pick(peer_v.shape[0], 1024))

    tile = _tiles(T, S)
    hg, mla = _in_proj(x2, row(attn_norm), w_hg, w_mla, tile["in_proj_rows"])
    o_f = _hgrn_scan(hg, lb_logits[:, 0, :], B, S, False, tile["scan_steps"])
    o_b = _hgrn_scan(hg, lb_logits[:, 1, :], B, S, True, tile["scan_steps"])
    q, k, v = _mla_prep(mla, pos, invf, row(q_a_norm), row(kv_a_norm), wq, wkv, qgn, qgr, kgn, kgr,
                        tile["mla_rows"])
    y_mla = _attention(q, k, v, mla_o_norm.reshape(1, -1).astype(F32), B, S, tile["attn_queries"])
    x1, h2, scores_t = _out_proj(o_f, o_b, hg, y_mla, x2, row(hg_o_norm), w_out.astype(BF16),
                                 row(ffn_norm), peer_w_q.astype(BF16), keys, tile["out_proj_rows"])
    iz, cnt, atop, btop = _peer_select(scores_t, tile["select_tokens"])
    return _peer_dense(h2, u_bf, vt_bf, scores_t, iz, cnt, atop, btop, x1,
                       tile["dense_tokens"], tile["dense_experts"])


def kernel(x, positions, attn_norm, w_in, hg_lb_logits, hg_o_norm, q_a_norm, w_q_up, kv_a_norm, w_kv_up,
           q_norm, k_norm, mla_o_norm, w_out, ffn_norm, peer_w_q, peer_sub_keys, peer_u, peer_v):
    B, S, D = x.shape
    x2 = x.reshape(B * S, D)
    pos = positions.reshape(B * S, 1)
    for l in range(attn_norm.shape[0]):
        assert l == 0
        x2 = _layer(x2, pos, B, S, attn_norm[l], w_in[l], hg_lb_logits, hg_o_norm[l], q_a_norm[l],
                    w_q_up[l], kv_a_norm[l], w_kv_up[l], q_norm[l], k_norm[l], mla_o_norm[l], w_out[l],
                    ffn_norm[l], peer_w_q[l], peer_sub_keys[l], peer_u[l], peer_v[l])
    return x2.reshape(B, S, D)
```

```python
import functools
import math

import jax
import jax.numpy as jnp
from jax import lax
from jax.experimental import pallas as pl
from jax.experimental.pallas import tpu as pltpu

F32 = jnp.float32
BF16 = jnp.bfloat16
EPS = 1e-6
LANES = 128
SUBLANES = 8
NEG_INF = float("-inf")

HG_HEADS = 4
HG_DK = 128
HG_CHUNK = 64
HG_HALF = HG_CHUNK // 2
HG_QUARTER = HG_CHUNK // 4
HG_BATCH_PER_STEP = 8
MLA_HEADS = 4
MLA_Q_LORA = 384
MLA_KV_LORA = 256
MLA_NOPE = 128
MLA_ROPE = 64
MLA_V = 128
MLA_QK = MLA_NOPE + MLA_ROPE
MLA_QPAD = 2 * LANES
ROPE_THETA = 10000.0
ATTN_CHUNKS = 8
OUT_CHUNKS = 2
PEER_HEADS = 8
PEER_NKEYS = 128
PEER_TOPK = 16
PEER_DHALF = 128
GATE_DTYPE = BF16
DENSE_SUBTILE = 1024

VMEM_LIMIT = 56 * 1024 * 1024


def _cparams(sem):
    return pltpu.CompilerParams(dimension_semantics=sem, vmem_limit_bytes=VMEM_LIMIT)


def _rms(x, gain):
    ms = jnp.mean(x * x, axis=-1, keepdims=True)
    return x * lax.rsqrt(ms + EPS) * gain


def _sigmoid(x):
    return 1.0 / (1.0 + jnp.exp(-x))


def _dot_nt(a, b):
    return lax.dot_general(a, b, (((1,), (1,)), ((), ())), preferred_element_type=F32)


def _dot_tn(a, b):
    return lax.dot_general(a, b, (((0,), (0,)), ((), ())), preferred_element_type=F32)


def _in_proj_kernel(x_ref, g_ref, whg_ref, wmla_ref, hg_ref, mla_ref):
    h = _rms(x_ref[...], g_ref[...]).astype(BF16)
    hg_ref[...] = jnp.dot(h, whg_ref[...], preferred_element_type=F32)
    mla_ref[...] = jnp.dot(h, wmla_ref[...], preferred_element_type=F32)


def _in_proj(x2, gain, w_hg, w_mla, tm):
    T, D = x2.shape
    n_hg, n_mla = w_hg.shape[1], w_mla.shape[1]
    return pl.pallas_call(
        _in_proj_kernel,
        grid=(T // tm,),
        in_specs=[
            pl.BlockSpec((tm, D), lambda i: (i, 0)),
            pl.BlockSpec((1, D), lambda i: (0, 0)),
            pl.BlockSpec((D, n_hg), lambda i: (0, 0)),
            pl.BlockSpec((D, n_mla), lambda i: (0, 0)),
        ],
        out_specs=[
            pl.BlockSpec((tm, n_hg), lambda i: (i, 0)),
            pl.BlockSpec((tm, n_mla), lambda i: (i, 0)),
        ],
        out_shape=[jax.ShapeDtypeStruct((T, n_hg), F32), jax.ShapeDtypeStruct((T, n_mla), F32)],
        compiler_params=_cparams(("parallel",)),
        name="in_proj",
    )(x2, gain, w_hg, w_mla)


def _hgrn_kernel(q_ref, f_ref, v_ref, lb_ref, o_ref, st_ref, b_scr, k_scr, p_scr, *, reverse, n_chunks):
    C, Hf, Qt, G = HG_CHUNK, HG_HALF, HG_QUARTER, SUBLANES
    Et = G
    assert C == 8 * Et
    n_batch = q_ref.shape[0]
    n_heads = st_ref.shape[0] // n_batch
    chains = [(bi, hh) for bi in range(n_batch) for hh in range(n_heads)]

    @pl.when(pl.program_id(1) == 0)
    def _():
        st_ref[...] = jnp.zeros_like(st_ref)

    logits = lb_ref[...]
    ex = jnp.exp(logits - jnp.max(logits, axis=0, keepdims=True))
    lb_all = ex[0:1, :] / jnp.sum(ex, axis=0, keepdims=True)

    row = lax.broadcasted_iota(jnp.int32, (C, C), 0)
    col = lax.broadcasted_iota(jnp.int32, (C, C), 1)
    tri = (col >= row) if reverse else (col <= row)
    cum_mat = tri.astype(F32)
    same_half = (row >= Hf) == (col >= Hf)
    same_quarter = (row // Qt) == (col // Qt)
    row_c = lax.broadcasted_iota(jnp.int32, (C, LANES), 0)
    q_side = (row_c < Hf) if reverse else (row_c >= Hf)
    odd_quarter = (row_c // Qt) % 2 == 1
    q_side4 = jnp.logical_not(odd_quarter) if reverse else odd_quarter
    low_half = row_c < Hf
    odd_eighth = (row_c // Et) % 2 == 1
    q_side8 = jnp.logical_not(odd_eighth) if reverse else odd_eighth
    quarter_c = row_c // Qt
    lane_g = lax.broadcasted_iota(jnp.int32, (G, LANES), 1)
    ones = jnp.ones((LANES, LANES), BF16)
    mid_row = Hf if reverse else Hf - 1
    qmid_rows = (Qt, Hf + Qt) if reverse else (Qt - 1, Hf + Qt - 1)
    emid_rows = [qb * Qt + (Et if reverse else Et - 1) for qb in range(C // Qt)]
    end_row = 0 if reverse else C - 1

    def gates(ci, r0):
        bi, hh = chains[ci]
        cols = slice(hh * HG_DK, (hh + 1) * HG_DK)
        lb = lb_all[:, cols]
        q_raw = q_ref[bi, pl.ds(r0, C), cols]
        q = q_raw * _sigmoid(q_raw)
        f = lb + (1.0 - lb) * _sigmoid(f_ref[bi, pl.ds(r0, C), cols])
        k = 1.0 - f
        vb = v_ref[bi, pl.ds(r0, C), cols].astype(BF16)
        b = jnp.dot(cum_mat, jnp.log2(f), precision=lax.Precision.HIGHEST,
                    preferred_element_type=F32)
        b_scr[ci] = b
        k_scr[ci] = k
        return q, k, vb, b

    def state_and_offdiag(ci, q, k, vb, b):
        b_mid = b_scr[ci, pl.ds(mid_row, 1), :]
        b_end = b_scr[ci, pl.ds(end_row, 1), :]
        st = st_ref[ci]
        o = _dot_nt((q * jnp.exp2(b)).astype(BF16), st.astype(BF16))
        kd = (k * jnp.exp2(b_end - b)).astype(BF16)
        st_ref[ci] = st * jnp.exp2(b_end) + _dot_tn(vb, kd)
        qt = jnp.where(q_side, q * jnp.exp2(b - b_mid), 0.0).astype(BF16)
        kt = jnp.where(q_side, 0.0, k * jnp.exp2(b_mid - b)).astype(BF16)
        b_q = jnp.where(low_half, b_scr[ci, pl.ds(qmid_rows[0], 1), :], b_scr[ci, pl.ds(qmid_rows[1], 1), :])
        qt4 = jnp.where(q_side4, q * jnp.exp2(b - b_q), 0.0).astype(BF16)
        kt4 = jnp.where(q_side4, 0.0, k * jnp.exp2(b_q - b)).astype(BF16)
        b_e = b_scr[ci, pl.ds(emid_rows[-1], 1), :]
        for qb in reversed(range(C // Qt - 1)):
            b_e = jnp.where(quarter_c == qb, b_scr[ci, pl.ds(emid_rows[qb], 1), :], b_e)
        qt8 = jnp.where(q_side8, q * jnp.exp2(b - b_e), 0.0).astype(BF16)
        kt8 = jnp.where(q_side8, 0.0, k * jnp.exp2(b_e - b)).astype(BF16)
        return o, _dot_nt(qt, kt), _dot_nt(qt4, kt4), _dot_nt(qt8, kt8)

    def diag_products(ci, q, b):
        for s in range(C):
            rows = slice(s // G * G, s // G * G + G)
            bs = b_scr[ci, pl.ds(s, 1), :]
            ks = k_scr[ci, pl.ds(s, 1), :]
            p_scr[ci, pl.ds(s * G, G), :] = (q[rows, :] * ks) * jnp.exp2(b[rows, :] - bs)

    def diag_scores(rs):
        groups = []
        for g in range(C // G):
            acc = jnp.zeros((G, LANES), F32)
            for s in range(g * G, (g + 1) * G):
                acc = jnp.where(lane_g == s, rs[s * G:(s + 1) * G, :], acc)
            groups.append(acc)
        return jnp.concatenate(groups, axis=0)[:, :C]

    def chunk_body(ci, carry):
        c = (n_chunks - 1 - ci) if reverse else ci
        r0 = pl.multiple_of(c * C, C)
        ids = range(len(chains))
        qkvb = [gates(ci, r0) for ci in ids]
        o_sc = [state_and_offdiag(ci, *qkvb[ci]) for ci in ids]
        for ci in ids:
            diag_products(ci, qkvb[ci][0], qkvb[ci][3])
        rs = [jnp.dot(p_scr[ci].astype(BF16), ones, preferred_element_type=F32) for ci in ids]
        for ci in ids:
            bi, hh = chains[ci]
            o, sc_half, sc_quarter, sc_eighth = o_sc[ci]
            inner = jnp.where(same_quarter, diag_scores(rs[ci]) + sc_eighth, sc_quarter)
            scores = jnp.where(same_half, jnp.where(tri, inner, 0.0), sc_half)
            o = o + jnp.dot(scores.astype(BF16), qkvb[ci][2], preferred_element_type=F32)
            o_ref[bi, pl.ds(r0, C), hh * HG_DK:(hh + 1) * HG_DK] = o
        return carry

    lax.fori_loop(0, n_chunks, chunk_body, 0)


def _hgrn_diag_rows():
    return HG_CHUNK * SUBLANES


def _hgrn_scan(hg, lb_logits_dir, B, S, reverse, tb):
    T = hg.shape[0]
    W = HG_HEADS * HG_DK
    nblk = S // tb
    f_blk = 2 if reverse else 1
    v_blk = 3
    nbat = HG_BATCH_PER_STEP if B % HG_BATCH_PER_STEP == 0 else 1
    n_chains = nbat * HG_HEADS
    hg4 = hg.reshape(B // nbat, nbat, S, hg.shape[1])

    def spec(col_blk):
        return pl.BlockSpec((None, nbat, tb, W),
                            lambda g, i: (g, 0, (nblk - 1 - i) if reverse else i, col_blk))

    kern = functools.partial(_hgrn_kernel, reverse=reverse, n_chunks=tb // HG_CHUNK)
    out = pl.pallas_call(
        kern,
        grid=(B // nbat, nblk),
        in_specs=[spec(0), spec(f_blk), spec(v_blk),
                  pl.BlockSpec((lb_logits_dir.shape[0], W), lambda g, i: (0, 0))],
        out_specs=spec(0),
        out_shape=jax.ShapeDtypeStruct((B // nbat, nbat, S, W), F32),
        scratch_shapes=[
            pltpu.VMEM((n_chains, HG_DK, HG_DK), F32),
            pltpu.VMEM((n_chains, HG_CHUNK, HG_DK), F32),
            pltpu.VMEM((n_chains, HG_CHUNK, HG_DK), F32),
            pltpu.VMEM((n_chains, _hgrn_diag_rows(), HG_DK), F32),
        ],
        compiler_params=_cparams(("parallel", "arbitrary")),
        name="hgrn_rev" if reverse else "hgrn_fwd",
    )(hg4, hg4, hg4, lb_logits_dir)
    return out.reshape(T, W)


def _mla_prep_kernel(mla_ref, pos_ref, invf_ref, qa_ref, kva_ref, wq_ref, wkv_ref,
                     qgn_ref, qgr_ref, kgn_ref, kgr_ref, q_ref, k_ref, v_ref):
    tm = mla_ref.shape[0]
    lane = lax.broadcasted_iota(jnp.int32, (tm, LANES), 1)
    ang = pos_ref[...].astype(F32) * invf_ref[...]
    cos_t = jnp.where(lane < MLA_ROPE, jnp.cos(ang), 0.0)
    sin_a = jnp.sin(ang)
    sin_t = jnp.where(lane < MLA_ROPE // 2, -sin_a, jnp.where(lane < MLA_ROPE, sin_a, 0.0))

    def rope(t):
        swapped = jnp.where(lane < MLA_ROPE // 2,
                            pltpu.roll(t, LANES - MLA_ROPE // 2, 1),
                            pltpu.roll(t, MLA_ROPE // 2, 1))
        return t * cos_t + swapped * sin_t

    def norm_rope_part(t, gain):
        ms = jnp.sum(t * t, axis=-1, keepdims=True) * (1.0 / MLA_ROPE)
        return t * lax.rsqrt(ms + EPS) * gain

    cq = mla_ref[:, 0:MLA_Q_LORA]
    ckv = mla_ref[:, MLA_Q_LORA:MLA_Q_LORA + MLA_KV_LORA]
    kr = mla_ref[:, MLA_Q_LORA + MLA_KV_LORA:]
    q_up = jnp.dot(_rms(cq, qa_ref[...]).astype(BF16), wq_ref[...], preferred_element_type=F32)
    kv_up = jnp.dot(_rms(ckv, kva_ref[...]).astype(BF16), wkv_ref[...], preferred_element_type=F32)
    k_rope = rope(norm_rope_part(kr, kgr_ref[...])).astype(BF16)
    for h in range(MLA_HEADS):
        c0 = h * MLA_QPAD
        q_ref[:, c0:c0 + LANES] = _rms(q_up[:, c0:c0 + LANES], qgn_ref[...]).astype(BF16)
        q_ref[:, c0 + LANES:c0 + 2 * LANES] = rope(
            norm_rope_part(q_up[:, c0 + LANES:c0 + 2 * LANES], qgr_ref[...])).astype(BF16)
        k_ref[:, c0:c0 + LANES] = _rms(kv_up[:, c0:c0 + LANES], kgn_ref[...]).astype(BF16)
        k_ref[:, c0 + LANES:c0 + 2 * LANES] = k_rope
        v_ref[:, h * MLA_V:(h + 1) * MLA_V] = kv_up[:, c0 + LANES:c0 + 2 * LANES].astype(BF16)


def _mla_prep(mla, pos, invf, qa, kva, wq, wkv, qgn, qgr, kgn, kgr, tm):
    T, n_mla = mla.shape
    full = lambda a: pl.BlockSpec(a.shape, lambda i: (0, 0))
    return pl.pallas_call(
        _mla_prep_kernel,
        grid=(T // tm,),
        in_specs=[
            pl.BlockSpec((tm, n_mla), lambda i: (i, 0)),
            pl.BlockSpec((tm, 1), lambda i: (i, 0)),
            full(invf), full(qa), full(kva), full(wq), full(wkv),
            full(qgn), full(qgr), full(kgn), full(kgr),
        ],
        out_specs=[
            pl.BlockSpec((tm, MLA_HEADS * MLA_QPAD), lambda i: (i, 0)),
            pl.BlockSpec((tm, MLA_HEADS * MLA_QPAD), lambda i: (i, 0)),
            pl.BlockSpec((tm, MLA_HEADS * MLA_V), lambda i: (i, 0)),
        ],
        out_shape=[
            jax.ShapeDtypeStruct((T, MLA_HEADS * MLA_QPAD), BF16),
            jax.ShapeDtypeStruct((T, MLA_HEADS * MLA_QPAD), BF16),
            jax.ShapeDtypeStruct((T, MLA_HEADS * MLA_V), BF16),
        ],
        compiler_params=_cparams(("parallel",)),
        name="mla_prep",
    )(mla, pos, invf, qa, kva, wq, wkv, qgn, qgr, kgn, kgr)


def _attn_kernel(q_ref, k_ref, v_ref, g_ref, o_ref):
    c = MLA_QK ** -0.5 * math.log2(math.e)
    tq = q_ref.shape[0]
    rc = tq // ATTN_CHUNKS
    rows = [pl.ds(i * rc, rc) for i in range(ATTN_CHUNKS)]
    s = [_dot_nt(q_ref[r, :], k_ref[...]) for r in rows]
    p = [jnp.exp2((si - jnp.max(si, axis=-1, keepdims=True)) * c) for si in s]
    l = [jnp.sum(pi, axis=-1, keepdims=True) for pi in p]
    o = [jnp.dot(pi.astype(BF16), v_ref[...], preferred_element_type=F32) for pi in p]
    for r, oi, li in zip(rows, o, l):
        o_ref[r, :] = _rms(oi / li, g_ref[...]).astype(o_ref.dtype)


def _attention(q, k, v, o_gain, B, S, tq):
    T = q.shape[0]
    nq = S // tq
    return pl.pallas_call(
        _attn_kernel,
        grid=(B, MLA_HEADS, nq),
        in_specs=[
            pl.BlockSpec((tq, MLA_QPAD), lambda b, h, i: (b * nq + i, h)),
            pl.BlockSpec((S, MLA_QPAD), lambda b, h, i: (b, h)),
            pl.BlockSpec((S, MLA_V), lambda b, h, i: (b, h)),
            pl.BlockSpec((1, MLA_V), lambda b, h, i: (0, h)),
        ],
        out_specs=pl.BlockSpec((tq, MLA_V), lambda b, h, i: (b * nq + i, h)),
        out_shape=jax.ShapeDtypeStruct((T, MLA_HEADS * MLA_V), BF16),
        compiler_params=_cparams(("parallel", "parallel", "arbitrary")),
        name="mla_attention",
    )(q, k, v, o_gain)


def _out_proj_kernel(of_ref, ob_ref, g_ref, ym_ref, x_ref, hgg_ref, wo_ref, fg_ref, wq_ref, keys_ref,
                     x1_ref, h2_ref, st_ref):
    tm = x_ref.shape[0]
    rc = tm // OUT_CHUNKS
    rows = [pl.ds(i * rc, rc) for i in range(OUT_CHUNKS)]
    w_hg_rows = of_ref.shape[1]

    def mixer_out(r):
        o = of_ref[r, :] + ob_ref[r, :]
        gate_raw = g_ref[r, :]
        gate = gate_raw * _sigmoid(gate_raw)
        parts = []
        for h in range(HG_HEADS):
            sl = slice(h * HG_DK, (h + 1) * HG_DK)
            parts.append(_rms(o[:, sl], hgg_ref[:, sl]) * gate[:, sl])
        return jnp.concatenate(parts, axis=-1).astype(BF16)

    y_hg = [mixer_out(r) for r in rows]
    mix = [jnp.dot(y, wo_ref[0:w_hg_rows, :], preferred_element_type=F32)
           + jnp.dot(ym_ref[r, :], wo_ref[w_hg_rows:, :], preferred_element_type=F32)
           for y, r in zip(y_hg, rows)]
    x1 = [x_ref[r, :] + m for r, m in zip(rows, mix)]
    h2 = [_rms(x, fg_ref[...]).astype(BF16) for x in x1]
    for r, x, h in zip(rows, x1, h2):
        x1_ref[r, :] = x
        h2_ref[r, :] = h
    pq = [jnp.dot(h, wq_ref[...], preferred_element_type=F32).astype(BF16) for h in h2]
    pq = jnp.concatenate(pq, axis=0)
    for pc in range(keys_ref.shape[0]):
        st_ref[pc] = _dot_nt(keys_ref[pc], pq[:, pc * PEER_DHALF:(pc + 1) * PEER_DHALF])


def _out_proj(o_f, o_b, hg, y_mla, x2, hg_gain, w_out, ffn_gain, w_pq, keys, tm):
    T, D = x2.shape
    W = o_f.shape[1]
    npc = keys.shape[0]
    full2 = lambda a: pl.BlockSpec(a.shape, lambda i: (0, 0))
    return pl.pallas_call(
        _out_proj_kernel,
        grid=(T // tm,),
        in_specs=[
            pl.BlockSpec((tm, W), lambda i: (i, 0)),
            pl.BlockSpec((tm, W), lambda i: (i, 0)),
            pl.BlockSpec((tm, W), lambda i: (i, 4)),
            pl.BlockSpec((tm, y_mla.shape[1]), lambda i: (i, 0)),
            pl.BlockSpec((tm, D), lambda i: (i, 0)),
            full2(hg_gain), full2(w_out), full2(ffn_gain), full2(w_pq),
            pl.BlockSpec(keys.shape, lambda i: (0, 0, 0)),
        ],
        out_specs=[
            pl.BlockSpec((tm, D), lambda i: (i, 0)),
            pl.BlockSpec((tm, D), lambda i: (i, 0)),
            pl.BlockSpec((npc, PEER_NKEYS, tm), lambda i: (0, 0, i)),
        ],
        out_shape=[
            jax.ShapeDtypeStruct((T, D), F32),
            jax.ShapeDtypeStruct((T, D), BF16),
            jax.ShapeDtypeStruct((npc, PEER_NKEYS, T), F32),
        ],
        compiler_params=_cparams(("parallel",)),
        name="out_proj_peer_query",
    )(o_f, o_b, hg, y_mla, x2, hg_gain, w_out, ffn_gain, w_pq, keys)


def _sorting_network(n):
    pairs = []
    p = 1
    while p < n:
        k = p
        while k >= 1:
            for j in range(k % p, n - k, 2 * k):
                for i in range(min(k, n - j - k)):
                    if (i + j) // (2 * p) == (i + j + k) // (2 * p):
                        pairs.append((i + j, i + j + k))
            k //= 2
        p *= 2
    return pairs


def _bitonic_merge_network(n):
    pairs = []
    stride = n // 2
    while stride >= 1:
        pairs += [(i, i + stride) for i in range(n) if not i & stride]
        stride //= 2
    return pairs


def _compare_exchange(xs, pairs):
    xs = list(xs)
    for i, j in pairs:
        xs[i], xs[j] = jnp.maximum(xs[i], xs[j]), jnp.minimum(xs[i], xs[j])
    return xs


def _peer_select_kernel(s_ref, iz_ref, cnt_ref, atop_ref, btop_ref, top_scr):
    K = PEER_TOPK
    n_vregs = PEER_NKEYS // SUBLANES
    assert n_vregs == K
    sort_pairs = _sorting_network(K)
    merge_pairs = _bitonic_merge_network(K)
    for p in range(PEER_HEADS):
        for c in range(2):
            lists = _compare_exchange(
                [s_ref[2 * p + c, pl.ds(g * SUBLANES, SUBLANES), :] for g in range(n_vregs)], sort_pairs)
            shift = SUBLANES // 2
            while shift >= 1:
                other = [pltpu.roll(x, shift, 0) for x in lists]
                lists = _compare_exchange(
                    [jnp.maximum(lists[r], other[K - 1 - r]) for r in range(K)], merge_pairs)
                shift //= 2
            for r in range(K):
                top_scr[c, r, pl.ds(p, 1), :] = lists[r][0:1, :]
    a = [top_scr[0, r] for r in range(K)]
    b = [top_scr[1, r] for r in range(K)]
    pairs = [(i, j) for i in range(K) for j in range(K) if (i + 1) * (j + 1) <= K]
    cands = [a[i] + b[j] for i, j in pairs]
    top = a[0] + b[0]
    cands += [jnp.full_like(top, NEG_INF)] * (-len(cands) % K)
    groups = [_compare_exchange(cands[g:g + K], sort_pairs) for g in range(0, len(cands), K)]
    while len(groups) > 1:
        merged = [_compare_exchange([jnp.maximum(x[r], y[K - 1 - r]) for r in range(K)], merge_pairs)
                  for x, y in zip(groups[0::2], groups[1::2])]
        groups = merged + groups[len(groups) // 2 * 2:]
    best = groups[0]
    kth = best[K - 1]
    z = jnp.zeros_like(top)
    for r in range(K):
        z = z + jnp.exp(best[r] - top)
    iz_ref[...] = 1.0 / z
    for i in range(K):
        cnt = jnp.zeros_like(top)
        for j in range(K):
            if (i, j) in pairs:
                cnt = cnt + jnp.where(a[i] + b[j] >= kth, 1.0, 0.0)
        cnt_ref[i] = cnt
        atop_ref[i] = a[i]
        btop_ref[i] = b[i]


def _peer_select(scores_t, tt):
    npc, nk, T = scores_t.shape
    top = jax.ShapeDtypeStruct((PEER_TOPK, PEER_HEADS, T), F32)
    top_spec = pl.BlockSpec((PEER_TOPK, PEER_HEADS, tt), lambda i: (0, 0, i))
    return pl.pallas_call(
        _peer_select_kernel,
        grid=(T // tt,),
        in_specs=[pl.BlockSpec((npc, nk, tt), lambda i: (0, 0, i))],
        out_specs=[pl.BlockSpec((PEER_HEADS, tt), lambda i: (0, i))] + [top_spec] * 3,
        out_shape=[jax.ShapeDtypeStruct((PEER_HEADS, T), F32)] + [top] * 3,
        scratch_shapes=[pltpu.VMEM((2, PEER_TOPK, PEER_HEADS, tt), F32)],
        compiler_params=_cparams(("parallel",)),
        name="peer_select",
    )(scores_t)


def _gelu_exact_x2(x):
    return x * (1.0 + lax.erf(x * (1.0 / math.sqrt(2.0))))


def _peer_dense_kernel(h_ref, u_ref, vt_ref, s_ref, iz_ref, cnt_ref, atop_ref, btop_ref, x1_ref, o_ref,
                       n_scr, e1_scr, rank_scr, e2_scr, a_scr, p_scr, yt_scr):
    jt = pl.program_id(1)
    te, tt = a_scr.shape
    nb = te // PEER_NKEYS
    n_grp = nb // SUBLANES
    PK = SUBLANES * (4 // jnp.dtype(GATE_DTYPE).itemsize)
    IBQ = 32 * PK // PEER_NKEYS

    @pl.when(jt == 0)
    def _():
        yt_scr[...] = jnp.zeros_like(yt_scr)
        grouped = (PEER_NKEYS // SUBLANES, SUBLANES, LANES)
        for p in range(PEER_HEADS):
            for st in range(tt // LANES):
                sl = pl.ds(st * LANES, LANES)
                s1 = s_ref[2 * p, :, sl]
                s2 = s_ref[2 * p + 1, :, sl]
                n = jnp.zeros_like(s1)
                rank = jnp.zeros_like(s2)
                for r in range(PEER_TOPK):
                    n = jnp.where(s1 == atop_ref[r, pl.ds(p, 1), sl], cnt_ref[r, pl.ds(p, 1), sl], n)
                    rank = jnp.where(btop_ref[r, pl.ds(p, 1), sl] > s2, r + 1.0, rank)
                n_scr[p, :, :, sl] = n.reshape(grouped)
                e1_scr[p, :, :, sl] = jnp.exp(s1 - atop_ref[0, pl.ds(p, 1), sl]).reshape(grouped)
                rank_scr[p, :, sl] = rank.astype(GATE_DTYPE)
                m2 = btop_ref[0, pl.ds(p, 1), sl]
                e2_scr[p, :, sl] = (jnp.exp(s2 - m2) * (0.5 * iz_ref[pl.ds(p, 1), sl])).astype(GATE_DTYPE)

    zero = jnp.zeros((PK, LANES), GATE_DTYPE)
    for sub_tile in range(te // DENSE_SUBTILE):
        erows = pl.ds(sub_tile * DENSE_SUBTILE, DENSE_SUBTILE)
        a_scr[erows, :] = _dot_nt(u_ref[erows, :], h_ref[...])
        for ibg in range(sub_tile * DENSE_SUBTILE // (IBQ * PEER_NKEYS),
                         (sub_tile + 1) * DENSE_SUBTILE // (IBQ * PEER_NKEYS)):
            for st in range(tt // LANES):
                sl = pl.ds(st * LANES, LANES)
                gates = [[zero for _ in range(PEER_NKEYS // PK)] for _ in range(IBQ)]
                for p in range(PEER_HEADS):
                    rank = rank_scr[p, :, sl]
                    e2 = e2_scr[p, :, sl]
                    for q in range(IBQ):
                        ib = ibg * IBQ + q
                        grp = jt * n_grp + ib // SUBLANES
                        sub = pl.ds(ib % SUBLANES, 1)
                        n = jnp.broadcast_to(n_scr[p, grp, sub, sl], (PK, LANES)).astype(GATE_DTYPE)
                        e1 = jnp.broadcast_to(e1_scr[p, grp, sub, sl], (PK, LANES)).astype(GATE_DTYPE)
                        for r in range(PEER_NKEYS // PK):
                            rr = slice(r * PK, (r + 1) * PK)
                            gates[q][r] = gates[q][r] + jnp.where(rank[rr, :] < n, e2[rr, :], zero) * e1
                for q in range(IBQ):
                    rows = pl.ds((ibg * IBQ + q) * PEER_NKEYS, PEER_NKEYS)
                    gate = jnp.concatenate(gates[q], axis=0)
                    p_scr[rows, sl] = (_gelu_exact_x2(a_scr[rows, sl]).astype(GATE_DTYPE) * gate).astype(BF16)
        yt_scr[...] += jnp.dot(vt_ref[:, erows], p_scr[erows, :], preferred_element_type=F32)

    @pl.when(jt == pl.num_programs(1) - 1)
    def _():
        o_ref[...] = x1_ref[...] + yt_scr[...].T


def _peer_dense(h2, u_bf, vt_bf, scores_t, iz, cnt, atop, btop, x1, tt, te):
    T, D = h2.shape
    E = u_bf.shape[0]
    npc, nk, _ = scores_t.shape
    assert te % (PEER_NKEYS * SUBLANES) == 0 and E % te == 0
    top = pl.BlockSpec((PEER_TOPK, PEER_HEADS, tt), lambda t, j: (0, 0, t))
    return pl.pallas_call(
        _peer_dense_kernel,
        grid=(T // tt, E // te),
        in_specs=[
            pl.BlockSpec((tt, D), lambda t, j: (t, 0)),
            pl.BlockSpec((te, D), lambda t, j: (j, 0)),
            pl.BlockSpec((D, te), lambda t, j: (0, j)),
            pl.BlockSpec((npc, nk, tt), lambda t, j: (0, 0, t)),
            pl.BlockSpec((PEER_HEADS, tt), lambda t, j: (0, t)),
            top, top, top,
            pl.BlockSpec((tt, D), lambda t, j: (t, 0)),
        ],
        out_specs=pl.BlockSpec((tt, D), lambda t, j: (t, 0)),
        out_shape=jax.ShapeDtypeStruct((T, D), F32),
        scratch_shapes=[
            pltpu.VMEM((PEER_HEADS, nk // SUBLANES, SUBLANES, tt), F32),
            pltpu.VMEM((PEER_HEADS, nk // SUBLANES, SUBLANES, tt), F32),
            pltpu.VMEM((PEER_HEADS, nk, tt), GATE_DTYPE),
            pltpu.VMEM((PEER_HEADS, nk, tt), GATE_DTYPE),
            pltpu.VMEM((te, tt), F32),
            pltpu.VMEM((te, tt), BF16),
            pltpu.VMEM((D, tt), F32),
        ],
        compiler_params=_cparams(("parallel", "arbitrary")),
        name="peer_dense",
    )(h2, u_bf, vt_bf, scores_t, iz, cnt, atop, btop, x1)


def _transpose_cast_kernel(x_ref, o_ref):
    o_ref[...] = x_ref[...].T.astype(o_ref.dtype)


def _transpose_cast(x, dtype, rows):
    E, D = x.shape
    return pl.pallas_call(
        _transpose_cast_kernel,
        grid=(E // rows,),
        in_specs=[pl.BlockSpec((rows, D), lambda i: (i, 0))],
        out_specs=pl.BlockSpec((D, rows), lambda i: (0, i)),
        out_shape=jax.ShapeDtypeStruct((D, E), dtype),
        compiler_params=_cparams(("parallel",)),
        name="value_table_layout",
    )(x)


def _pick(n, pref):
    t = min(n, pref)
    assert n % t == 0, (n, t)
    return t


def _tiles(T, S):
    return {
        "in_proj_rows": _pick(T, 512),
        "scan_steps": _pick(S, 256),
        "mla_rows": _pick(T, 1024),
        "attn_queries": _pick(S, 2048),
        "out_proj_rows": _pick(T, 512),
        "select_tokens": _pick(T, 256),
        "dense_tokens": _pick(T, 512),
        "dense_experts": 2 * PEER_NKEYS * SUBLANES,
    }


def _layer(x2, pos, B, S, attn_norm, w_in, lb_logits, hg_o_norm, q_a_norm, w_q_up, kv_a_norm,
           w_kv_up, q_norm, k_norm, mla_o_norm, w_out, ffn_norm, peer_w_q, peer_sub_keys, peer_u, peer_v):
    T, D = x2.shape
    n_hg = 5 * HG_HEADS * HG_DK
    row = lambda a: a.reshape(1, -1).astype(F32)

    w_hg = w_in[:, :n_hg].astype(BF16)
    n_mla = w_in.shape[1] - n_hg
    w_mla = jnp.pad(w_in[:, n_hg:], ((0, 0), (0, -n_mla % LANES))).astype(BF16)
    pad_r = LANES - MLA_ROPE
    wq = w_q_up.reshape(MLA_Q_LORA, MLA_HEADS, MLA_QK)
    wq = jnp.pad(wq, ((0, 0), (0, 0), (0, MLA_QPAD - MLA_QK))).reshape(MLA_Q_LORA, MLA_HEADS * MLA_QPAD)
    wq = wq.astype(BF16)
    wkv = w_kv_up.astype(BF16)
    qgn, qgr = row(q_norm[:MLA_NOPE]), row(jnp.pad(q_norm[MLA_NOPE:], (0, pad_r)))
    kgn, kgr = row(k_norm[:MLA_NOPE]), row(jnp.pad(k_norm[MLA_NOPE:], (0, pad_r)))
    inv_freq = 1.0 / (ROPE_THETA ** (jnp.arange(0, MLA_ROPE, 2, dtype=F32) / MLA_ROPE))
    invf = row(jnp.concatenate([inv_freq, inv_freq, jnp.zeros((pad_r,), F32)]))
    keys = peer_sub_keys.reshape(2 * PEER_HEADS, PEER_NKEYS, PEER_DHALF).astype(BF16)
    u_bf = peer_u.astype(BF16)
    vt_bf = _transpose_cast(peer_v, BF16, _pick(peer_v.shape[0], 1024))

    tile = _tiles(T, S)
    hg, mla = _in_proj(x2, row(attn_norm), w_hg, w_mla, tile["in_proj_rows"])
    o_f = _hgrn_scan(hg, lb_logits[:, 0, :], B, S, False, tile["scan_steps"])
    o_b = _hgrn_scan(hg, lb_logits[:, 1, :], B, S, True, tile["scan_steps"])
    q, k, v = _mla_prep(mla, pos, invf, row(q_a_norm), row(kv_a_norm), wq, wkv, qgn, qgr, kgn, kgr,
                        tile["mla_rows"])
    y_mla = _attention(q, k, v, mla_o_norm.reshape(1, -1).astype(F32), B, S, tile["attn_queries"])
    x1, h2, scores_t = _out_proj(o_f, o_b, hg, y_mla, x2, row(hg_o_norm), w_out.astype(BF16),
                                 row(ffn_norm), peer_w_q.astype(BF16), keys, tile["out_proj_rows"])
    iz, cnt, atop, btop = _peer_select(scores_t, tile["select_tokens"])
    return _peer_dense(h2, u_bf, vt_bf, scores_t, iz, cnt, atop, btop, x1,
                       tile["dense_tokens"], tile["dense_experts"])


def kernel(x, positions, attn_norm, w_in, hg_lb_logits, hg_o_norm, q_a_norm, w_q_up, kv_a_norm, w_kv_up,
           q_norm, k_norm, mla_o_norm, w_out, ffn_norm, peer_w_q, peer_sub_keys, peer_u, peer_v):
    B, S, D = x.shape
    x2 = x.reshape(B * S, D)
    pos = positions.reshape(B * S, 1)
    for l in range(attn_norm.shape[0]):
        assert l == 0
        x2 = _layer(x2, pos, B, S, attn_norm[l], w_in[l], hg_lb_logits, hg_o_norm[l], q_a_norm[l],
                    w_q_up[l], kv_a_norm[l], w_kv_up[l], q_norm[l], k_norm[l], mla_o_norm[l], w_out[l],
                    ffn_norm[l], peer_w_q[l], peer_sub_keys[l], peer_u[l], peer_v[l])
    return x2.reshape(B, S, D)
```

```python
import functools
import math

import jax
import jax.numpy as jnp
from jax import lax
from jax.experimental import pallas as pl
from jax.experimental.pallas import tpu as pltpu

F32 = jnp.float32
BF16 = jnp.bfloat16
EPS = 1e-6
LANES = 128
SUBLANES = 8
NEG_INF = float("-inf")

HG_HEADS = 4
HG_DK = 128
HG_CHUNK = 64
HG_HALF = HG_CHUNK // 2
HG_QUARTER = HG_CHUNK // 4
HG_BATCH_PER_STEP = 8
MLA_HEADS = 4
MLA_Q_LORA = 384
MLA_KV_LORA = 256
MLA_NOPE = 128
MLA_ROPE = 64
MLA_V = 128
MLA_QK = MLA_NOPE + MLA_ROPE
MLA_QPAD = 2 * LANES
ROPE_THETA = 10000.0
ATTN_CHUNKS = 8
OUT_CHUNKS = 2
PEER_HEADS = 8
PEER_NKEYS = 128
PEER_TOPK = 16
PEER_DHALF = 128
GATE_DTYPE = BF16
DENSE_SUBTILE = 1024

VMEM_LIMIT = 56 * 1024 * 1024


def _cparams(sem):
    return pltpu.CompilerParams(dimension_semantics=sem, vmem_limit_bytes=VMEM_LIMIT)


def _rms(x, gain):
    ms = jnp.mean(x * x, axis=-1, keepdims=True)
    return x * lax.rsqrt(ms + EPS) * gain


def _sigmoid(x):
    return 1.0 / (1.0 + jnp.exp(-x))


def _dot_nt(a, b):
    return lax.dot_general(a, b, (((1,), (1,)), ((), ())), preferred_element_type=F32)


def _dot_tn(a, b):
    return lax.dot_general(a, b, (((0,), (0,)), ((), ())), preferred_element_type=F32)


def _in_proj_kernel(x_ref, g_ref, whg_ref, wmla_ref, hg_ref, mla_ref):
    h = _rms(x_ref[...], g_ref[...]).astype(BF16)
    hg_ref[...] = jnp.dot(h, whg_ref[...], preferred_element_type=F32)
    mla_ref[...] = jnp.dot(h, wmla_ref[...], preferred_element_type=F32)


def _in_proj(x2, gain, w_hg, w_mla, tm):
    T, D = x2.shape
    n_hg, n_mla = w_hg.shape[1], w_mla.shape[1]
    return pl.pallas_call(
        _in_proj_kernel,
        grid=(T // tm,),
        in_specs=[
            pl.BlockSpec((tm, D), lambda i: (i, 0)),
            pl.BlockSpec((1, D), lambda i: (0, 0)),
            pl.BlockSpec((D, n_hg), lambda i: (0, 0)),
            pl.BlockSpec((D, n_mla), lambda i: (0, 0)),
        ],
        out_specs=[
            pl.BlockSpec((tm, n_hg), lambda i: (i, 0)),
            pl.BlockSpec((tm, n_mla), lambda i: (i, 0)),
        ],
        out_shape=[jax.ShapeDtypeStruct((T, n_hg), F32), jax.ShapeDtypeStruct((T, n_mla), F32)],
        compiler_params=_cparams(("parallel",)),
        name="in_proj",
    )(x2, gain, w_hg, w_mla)


def _hgrn_kernel(q_ref, f_ref, v_ref, lb_ref, o_ref, st_ref, b_scr, k_scr, p_scr, *, reverse, n_chunks):
    C, Hf, Qt, G = HG_CHUNK, HG_HALF, HG_QUARTER, SUBLANES
    Et = G
    assert C == 8 * Et
    n_batch = q_ref.shape[0]
    n_heads = st_ref.shape[0] // n_batch
    chains = [(bi, hh) for bi in range(n_batch) for hh in range(n_heads)]

    @pl.when(pl.program_id(1) == 0)
    def _():
        st_ref[...] = jnp.zeros_like(st_ref)

    logits = lb_ref[...]
    ex = jnp.exp(logits - jnp.max(logits, axis=0, keepdims=True))
    lb_all = ex[0:1, :] / jnp.sum(ex, axis=0, keepdims=True)

    row = lax.broadcasted_iota(jnp.int32, (C, C), 0)
    col = lax.broadcasted_iota(jnp.int32, (C, C), 1)
    tri = (col >= row) if reverse else (col <= row)
    cum_mat = tri.astype(F32)
    same_half = (row >= Hf) == (col >= Hf)
    same_quarter = (row // Qt) == (col // Qt)
    row_c = lax.broadcasted_iota(jnp.int32, (C, LANES), 0)
    q_side = (row_c < Hf) if reverse else (row_c >= Hf)
    odd_quarter = (row_c // Qt) % 2 == 1
    q_side4 = jnp.logical_not(odd_quarter) if reverse else odd_quarter
    low_half = row_c < Hf
    odd_eighth = (row_c // Et) % 2 == 1
    q_side8 = jnp.logical_not(odd_eighth) if reverse else odd_eighth
    quarter_c = row_c // Qt
    lane_g = lax.broadcasted_iota(jnp.int32, (G, LANES), 1)
    ones = jnp.ones((LANES, LANES), BF16)
    mid_row = Hf if reverse else Hf - 1
    qmid_rows = (Qt, Hf + Qt) if reverse else (Qt - 1, Hf + Qt - 1)
    emid_rows = [qb * Qt + (Et if reverse else Et - 1) for qb in range(C // Qt)]
    end_row = 0 if reverse else C - 1

    def gates(ci, r0):
        bi, hh = chains[ci]
        cols = slice(hh * HG_DK, (hh + 1) * HG_DK)
        lb = lb_all[:, cols]
        q_raw = q_ref[bi, pl.ds(r0, C), cols]
        q = q_raw * _sigmoid(q_raw)
        f = lb + (1.0 - lb) * _sigmoid(f_ref[bi, pl.ds(r0, C), cols])
        k = 1.0 - f
        vb = v_ref[bi, pl.ds(r0, C), cols].astype(BF16)
        b = jnp.dot(cum_mat, jnp.log2(f), precision=lax.Precision.HIGHEST,
                    preferred_element_type=F32)
        b_scr[ci] = b
        k_scr[ci] = k
        return q, k, vb, b

    def state_and_offdiag(ci, q, k, vb, b):
        b_mid = b_scr[ci, pl.ds(mid_row, 1), :]
        b_end = b_scr[ci, pl.ds(end_row, 1), :]
        st = st_ref[ci]
        o = _dot_nt((q * jnp.exp2(b)).astype(BF16), st.astype(BF16))
        kd = (k * jnp.exp2(b_end - b)).astype(BF16)
        st_ref[ci] = st * jnp.exp2(b_end) + _dot_tn(vb, kd)
        qt = jnp.where(q_side, q * jnp.exp2(b - b_mid), 0.0).astype(BF16)
        kt = jnp.where(q_side, 0.0, k * jnp.exp2(b_mid - b)).astype(BF16)
        b_q = jnp.where(low_half, b_scr[ci, pl.ds(qmid_rows[0], 1), :], b_scr[ci, pl.ds(qmid_rows[1], 1), :])
        qt4 = jnp.where(q_side4, q * jnp.exp2(b - b_q), 0.0).astype(BF16)
        kt4 = jnp.where(q_side4, 0.0, k * jnp.exp2(b_q - b)).astype(BF16)
        b_e = b_scr[ci, pl.ds(emid_rows[-1], 1), :]
        for qb in reversed(range(C // Qt - 1)):
            b_e = jnp.where(quarter_c == qb, b_scr[ci, pl.ds(emid_rows[qb], 1), :], b_e)
        qt8 = jnp.where(q_side8, q * jnp.exp2(b - b_e), 0.0).astype(BF16)
        kt8 = jnp.where(q_side8, 0.0, k * jnp.exp2(b_e - b)).astype(BF16)
        return o, _dot_nt(qt, kt), _dot_nt(qt4, kt4), _dot_nt(qt8, kt8)

    def diag_products(ci, q, b):
        for s in range(C):
            rows = slice(s // G * G, s // G * G + G)
            bs = b_scr[ci, pl.ds(s, 1), :]
            ks = k_scr[ci, pl.ds(s, 1), :]
            p_scr[ci, pl.ds(s * G, G), :] = (q[rows, :] * ks) * jnp.exp2(b[rows, :] - bs)

    def diag_scores(rs):
        groups = []
        for g in range(C // G):
            acc = jnp.zeros((G, LANES), F32)
            for s in range(g * G, (g + 1) * G):
                acc = jnp.where(lane_g == s, rs[s * G:(s + 1) * G, :], acc)
            groups.append(acc)
        return jnp.concatenate(groups, axis=0)[:, :C]

    def chunk_body(ci, carry):
        c = (n_chunks - 1 - ci) if reverse else ci
        r0 = pl.multiple_of(c * C, C)
        ids = range(len(chains))
        qkvb = [gates(ci, r0) for ci in ids]
        o_sc = [state_and_offdiag(ci, *qkvb[ci]) for ci in ids]
        for ci in ids:
            diag_products(ci, qkvb[ci][0], qkvb[ci][3])
        rs = [jnp.dot(p_scr[ci].astype(BF16), ones, preferred_element_type=F32) for ci in ids]
        for ci in ids:
            bi, hh = chains[ci]
            o, sc_half, sc_quarter, sc_eighth = o_sc[ci]
            inner = jnp.where(same_quarter, diag_scores(rs[ci]) + sc_eighth, sc_quarter)
            scores = jnp.where(same_half, jnp.where(tri, inner, 0.0), sc_half)
            o = o + jnp.dot(scores.astype(BF16), qkvb[ci][2], preferred_element_type=F32)
            o_ref[bi, pl.ds(r0, C), hh * HG_DK:(hh + 1) * HG_DK] = o
        return carry

    lax.fori_loop(0, n_chunks, chunk_body, 0)


def _hgrn_diag_rows():
    return HG_CHUNK * SUBLANES


def _hgrn_scan(hg, lb_logits_dir, B, S, reverse, tb):
    T = hg.shape[0]
    W = HG_HEADS * HG_DK
    nblk = S // tb
    f_blk = 2 if reverse else 1
    v_blk = 3
    nbat = HG_BATCH_PER_STEP if B % HG_BATCH_PER_STEP == 0 else 1
    n_chains = nbat * HG_HEADS
    hg4 = hg.reshape(B // nbat, nbat, S, hg.shape[1])

    def spec(col_blk):
        return pl.BlockSpec((None, nbat, tb, W),
                            lambda g, i: (g, 0, (nblk - 1 - i) if reverse else i, col_blk))

    kern = functools.partial(_hgrn_kernel, reverse=reverse, n_chunks=tb // HG_CHUNK)
    out = pl.pallas_call(
        kern,
        grid=(B // nbat, nblk),
        in_specs=[spec(0), spec(f_blk), spec(v_blk),
                  pl.BlockSpec((lb_logits_dir.shape[0], W), lambda g, i: (0, 0))],
        out_specs=spec(0),
        out_shape=jax.ShapeDtypeStruct((B // nbat, nbat, S, W), F32),
        scratch_shapes=[
            pltpu.VMEM((n_chains, HG_DK, HG_DK), F32),
            pltpu.VMEM((n_chains, HG_CHUNK, HG_DK), F32),
            pltpu.VMEM((n_chains, HG_CHUNK, HG_DK), F32),
            pltpu.VMEM((n_chains, _hgrn_diag_rows(), HG_DK), F32),
        ],
        compiler_params=_cparams(("parallel", "arbitrary")),
        name="hgrn_rev" if reverse else "hgrn_fwd",
    )(hg4, hg4, hg4, lb_logits_dir)
    return out.reshape(T, W)


def _mla_prep_kernel(mla_ref, pos_ref, invf_ref, qa_ref, kva_ref, wq_ref, wkv_ref,
                     qgn_ref, qgr_ref, kgn_ref, kgr_ref, q_ref, k_ref, v_ref):
    tm = mla_ref.shape[0]
    lane = lax.broadcasted_iota(jnp.int32, (tm, LANES), 1)
    ang = pos_ref[...].astype(F32) * invf_ref[...]
    cos_t = jnp.where(lane < MLA_ROPE, jnp.cos(ang), 0.0)
    sin_a = jnp.sin(ang)
    sin_t = jnp.where(lane < MLA_ROPE // 2, -sin_a, jnp.where(lane < MLA_ROPE, sin_a, 0.0))

    def rope(t):
        swapped = jnp.where(lane < MLA_ROPE // 2,
                            pltpu.roll(t, LANES - MLA_ROPE // 2, 1),
                            pltpu.roll(t, MLA_ROPE // 2, 1))
        return t * cos_t + swapped * sin_t

    def norm_rope_part(t, gain):
        ms = jnp.sum(t * t, axis=-1, keepdims=True) * (1.0 / MLA_ROPE)
        return t * lax.rsqrt(ms + EPS) * gain

    cq = mla_ref[:, 0:MLA_Q_LORA]
    ckv = mla_ref[:, MLA_Q_LORA:MLA_Q_LORA + MLA_KV_LORA]
    kr = mla_ref[:, MLA_Q_LORA + MLA_KV_LORA:]
    q_up = jnp.dot(_rms(cq, qa_ref[...]).astype(BF16), wq_ref[...], preferred_element_type=F32)
    kv_up = jnp.dot(_rms(ckv, kva_ref[...]).astype(BF16), wkv_ref[...], preferred_element_type=F32)
    k_rope = rope(norm_rope_part(kr, kgr_ref[...])).astype(BF16)
    for h in range(MLA_HEADS):
        c0 = h * MLA_QPAD
        q_ref[:, c0:c0 + LANES] = _rms(q_up[:, c0:c0 + LANES], qgn_ref[...]).astype(BF16)
        q_ref[:, c0 + LANES:c0 + 2 * LANES] = rope(
            norm_rope_part(q_up[:, c0 + LANES:c0 + 2 * LANES], qgr_ref[...])).astype(BF16)
        k_ref[:, c0:c0 + LANES] = _rms(kv_up[:, c0:c0 + LANES], kgn_ref[...]).astype(BF16)
        k_ref[:, c0 + LANES:c0 + 2 * LANES] = k_rope
        v_ref[:, c0:c0 + LANES] = kv_up[:, c0 + LANES:c0 + 2 * LANES].astype(BF16)
        v_ref[:, c0 + LANES:c0 + 2 * LANES] = jnp.where(lane == 0, 1.0, 0.0).astype(BF16)


def _mla_prep(mla, pos, invf, qa, kva, wq, wkv, qgn, qgr, kgn, kgr, tm):
    T, n_mla = mla.shape
    full = lambda a: pl.BlockSpec(a.shape, lambda i: (0, 0))
    return pl.pallas_call(
        _mla_prep_kernel,
        grid=(T // tm,),
        in_specs=[
            pl.BlockSpec((tm, n_mla), lambda i: (i, 0)),
            pl.BlockSpec((tm, 1), lambda i: (i, 0)),
            full(invf), full(qa), full(kva), full(wq), full(wkv),
            full(qgn), full(qgr), full(kgn), full(kgr),
        ],
        out_specs=[
            pl.BlockSpec((tm, MLA_HEADS * MLA_QPAD), lambda i: (i, 0)),
            pl.BlockSpec((tm, MLA_HEADS * MLA_QPAD), lambda i: (i, 0)),
            pl.BlockSpec((tm, MLA_HEADS * MLA_QPAD), lambda i: (i, 0)),
        ],
        out_shape=[
            jax.ShapeDtypeStruct((T, MLA_HEADS * MLA_QPAD), BF16),
            jax.ShapeDtypeStruct((T, MLA_HEADS * MLA_QPAD), BF16),
            jax.ShapeDtypeStruct((T, MLA_HEADS * MLA_QPAD), BF16),
        ],
        compiler_params=_cparams(("parallel",)),
        name="mla_prep",
    )(mla, pos, invf, qa, kva, wq, wkv, qgn, qgr, kgn, kgr)


def _attn_kernel(q_ref, k_ref, v_ref, g_ref, o_ref):
    c = MLA_QK ** -0.5 * math.log2(math.e)
    tq = q_ref.shape[0]
    rc = tq // ATTN_CHUNKS
    rows = [pl.ds(i * rc, rc) for i in range(ATTN_CHUNKS)]
    s = [_dot_nt(q_ref[r, :], k_ref[...]) for r in rows]
    p = [jnp.exp2((si - jnp.max(si, axis=-1, keepdims=True)) * c).astype(BF16) for si in s]
    o = [jnp.dot(pi, v_ref[...], preferred_element_type=F32) for pi in p]
    for r, oi in zip(rows, o):
        o_ref[r, :] = _rms(oi[:, :MLA_V] / oi[:, MLA_V:MLA_V + 1], g_ref[...]).astype(o_ref.dtype)


def _attention(q, k, v, o_gain, B, S, tq):
    T = q.shape[0]
    nq = S // tq
    return pl.pallas_call(
        _attn_kernel,
        grid=(B, MLA_HEADS, nq),
        in_specs=[
            pl.BlockSpec((tq, MLA_QPAD), lambda b, h, i: (b * nq + i, h)),
            pl.BlockSpec((S, MLA_QPAD), lambda b, h, i: (b, h)),
            pl.BlockSpec((S, MLA_QPAD), lambda b, h, i: (b, h)),
            pl.BlockSpec((1, MLA_V), lambda b, h, i: (0, h)),
        ],
        out_specs=pl.BlockSpec((tq, MLA_V), lambda b, h, i: (b * nq + i, h)),
        out_shape=jax.ShapeDtypeStruct((T, MLA_HEADS * MLA_V), BF16),
        compiler_params=_cparams(("parallel", "parallel", "arbitrary")),
        name="mla_attention",
    )(q, k, v, o_gain)


def _out_proj_kernel(of_ref, ob_ref, g_ref, ym_ref, x_ref, hgg_ref, wo_ref, fg_ref, wq_ref, keys_ref,
                     x1_ref, h2_ref, st_ref):
    tm = x_ref.shape[0]
    rc = tm // OUT_CHUNKS
    rows = [pl.ds(i * rc, rc) for i in range(OUT_CHUNKS)]
    w_hg_rows = of_ref.shape[1]

    def mixer_out(r):
        o = of_ref[r, :] + ob_ref[r, :]
        gate_raw = g_ref[r, :]
        gate = gate_raw * _sigmoid(gate_raw)
        parts = []
        for h in range(HG_HEADS):
            sl = slice(h * HG_DK, (h + 1) * HG_DK)
            parts.append(_rms(o[:, sl], hgg_ref[:, sl]) * gate[:, sl])
        return jnp.concatenate(parts, axis=-1).astype(BF16)

    y_hg = [mixer_out(r) for r in rows]
    mix = [jnp.dot(y, wo_ref[0:w_hg_rows, :], preferred_element_type=F32)
           + jnp.dot(ym_ref[r, :], wo_ref[w_hg_rows:, :], preferred_element_type=F32)
           for y, r in zip(y_hg, rows)]
    x1 = [x_ref[r, :] + m for r, m in zip(rows, mix)]
    h2 = [_rms(x, fg_ref[...]).astype(BF16) for x in x1]
    for r, x, h in zip(rows, x1, h2):
        x1_ref[r, :] = x
        h2_ref[r, :] = h
    pq = [jnp.dot(h, wq_ref[...], preferred_element_type=F32).astype(BF16) for h in h2]
    pq = jnp.concatenate(pq, axis=0)
    for pc in range(keys_ref.shape[0]):
        st_ref[pc] = _dot_nt(keys_ref[pc], pq[:, pc * PEER_DHALF:(pc + 1) * PEER_DHALF])


def _out_proj(o_f, o_b, hg, y_mla, x2, hg_gain, w_out, ffn_gain, w_pq, keys, tm):
    T, D = x2.shape
    W = o_f.shape[1]
    npc = keys.shape[0]
    full2 = lambda a: pl.BlockSpec(a.shape, lambda i: (0, 0))
    return pl.pallas_call(
        _out_proj_kernel,
        grid=(T // tm,),
        in_specs=[
            pl.BlockSpec((tm, W), lambda i: (i, 0)),
            pl.BlockSpec((tm, W), lambda i: (i, 0)),
            pl.BlockSpec((tm, W), lambda i: (i, 4)),
            pl.BlockSpec((tm, y_mla.shape[1]), lambda i: (i, 0)),
            pl.BlockSpec((tm, D), lambda i: (i, 0)),
            full2(hg_gain), full2(w_out), full2(ffn_gain), full2(w_pq),
            pl.BlockSpec(keys.shape, lambda i: (0, 0, 0)),
        ],
        out_specs=[
            pl.BlockSpec((tm, D), lambda i: (i, 0)),
            pl.BlockSpec((tm, D), lambda i: (i, 0)),
            pl.BlockSpec((npc, PEER_NKEYS, tm), lambda i: (0, 0, i)),
        ],
        out_shape=[
            jax.ShapeDtypeStruct((T, D), F32),
            jax.ShapeDtypeStruct((T, D), BF16),
            jax.ShapeDtypeStruct((npc, PEER_NKEYS, T), F32),
        ],
        compiler_params=_cparams(("parallel",)),
        name="out_proj_peer_query",
    )(o_f, o_b, hg, y_mla, x2, hg_gain, w_out, ffn_gain, w_pq, keys)


def _sorting_network(n):
    pairs = []
    p = 1
    while p < n:
        k = p
        while k >= 1:
            for j in range(k % p, n - k, 2 * k):
                for i in range(min(k, n - j - k)):
                    if (i + j) // (2 * p) == (i + j + k) // (2 * p):
                        pairs.append((i + j, i + j + k))
            k //= 2
        p *= 2
    return pairs


def _bitonic_merge_network(n):
    pairs = []
    stride = n // 2
    while stride >= 1:
        pairs += [(i, i + stride) for i in range(n) if not i & stride]
        stride //= 2
    return pairs


def _compare_exchange(xs, pairs):
    xs = list(xs)
    for i, j in pairs:
        xs[i], xs[j] = jnp.maximum(xs[i], xs[j]), jnp.minimum(xs[i], xs[j])
    return xs


def _peer_select_kernel(s_ref, iz_ref, cnt_ref, atop_ref, btop_ref, top_scr):
    K = PEER_TOPK
    n_vregs = PEER_NKEYS // SUBLANES
    assert n_vregs == K
    sort_pairs = _sorting_network(K)
    merge_pairs = _bitonic_merge_network(K)
    for p in range(PEER_HEADS):
        for c in range(2):
            lists = _compare_exchange(
                [s_ref[2 * p + c, pl.ds(g * SUBLANES, SUBLANES), :] for g in range(n_vregs)], sort_pairs)
            shift = SUBLANES // 2
            while shift >= 1:
                other = [pltpu.roll(x, shift, 0) for x in lists]
                lists = _compare_exchange(
                    [jnp.maximum(lists[r], other[K - 1 - r]) for r in range(K)], merge_pairs)
                shift //= 2
            for r in range(K):
                top_scr[c, r, pl.ds(p, 1), :] = lists[r][0:1, :]
    a = [top_scr[0, r] for r in range(K)]
    b = [top_scr[1, r] for r in range(K)]
    pairs = [(i, j) for i in range(K) for j in range(K) if (i + 1) * (j + 1) <= K]
    cands = [a[i] + b[j] for i, j in pairs]
    top = a[0] + b[0]
    cands += [jnp.full_like(top, NEG_INF)] * (-len(cands) % K)
    groups = [_compare_exchange(cands[g:g + K], sort_pairs) for g in range(0, len(cands), K)]
    while len(groups) > 1:
        merged = [_compare_exchange([jnp.maximum(x[r], y[K - 1 - r]) for r in range(K)], merge_pairs)
                  for x, y in zip(groups[0::2], groups[1::2])]
        groups = merged + groups[len(groups) // 2 * 2:]
    best = groups[0]
    kth = best[K - 1]
    z = jnp.zeros_like(top)
    for r in range(K):
        z = z + jnp.exp(best[r] - top)
    iz_ref[...] = 1.0 / z
    for i in range(K):
        cnt = jnp.zeros_like(top)
        for j in range(K):
            if (i, j) in pairs:
                cnt = cnt + jnp.where(a[i] + b[j] >= kth, 1.0, 0.0)
        cnt_ref[i] = cnt
        atop_ref[i] = a[i]
        btop_ref[i] = b[i]


def _peer_select(scores_t, tt):
    npc, nk, T = scores_t.shape
    top = jax.ShapeDtypeStruct((PEER_TOPK, PEER_HEADS, T), F32)
    top_spec = pl.BlockSpec((PEER_TOPK, PEER_HEADS, tt), lambda i: (0, 0, i))
    return pl.pallas_call(
        _peer_select_kernel,
        grid=(T // tt,),
        in_specs=[pl.BlockSpec((npc, nk, tt), lambda i: (0, 0, i))],
        out_specs=[pl.BlockSpec((PEER_HEADS, tt), lambda i: (0, i))] + [top_spec] * 3,
        out_shape=[jax.ShapeDtypeStruct((PEER_HEADS, T), F32)] + [top] * 3,
        scratch_shapes=[pltpu.VMEM((2, PEER_TOPK, PEER_HEADS, tt), F32)],
        compiler_params=_cparams(("parallel",)),
        name="peer_select",
    )(scores_t)


def _gelu_exact_x2(x):
    return x * (1.0 + lax.erf(x * (1.0 / math.sqrt(2.0))))


def _peer_dense_kernel(h_ref, u_ref, vt_ref, s_ref, iz_ref, cnt_ref, atop_ref, btop_ref, x1_ref, o_ref,
                       n_scr, e1_scr, rank_scr, e2_scr, a_scr, p_scr, yt_scr):
    jt = pl.program_id(1)
    te, tt = a_scr.shape
    nb = te // PEER_NKEYS
    n_grp = nb // SUBLANES
    PK = SUBLANES * (4 // jnp.dtype(GATE_DTYPE).itemsize)
    IBQ = 32 * PK // PEER_NKEYS

    @pl.when(jt == 0)
    def _():
        yt_scr[...] = jnp.zeros_like(yt_scr)
        grouped = (PEER_NKEYS // SUBLANES, SUBLANES, LANES)
        for p in range(PEER_HEADS):
            for st in range(tt // LANES):
                sl = pl.ds(st * LANES, LANES)
                s1 = s_ref[2 * p, :, sl]
                s2 = s_ref[2 * p + 1, :, sl]
                n = jnp.zeros_like(s1)
                rank = jnp.zeros_like(s2)
                for r in range(PEER_TOPK):
                    n = jnp.where(s1 == atop_ref[r, pl.ds(p, 1), sl], cnt_ref[r, pl.ds(p, 1), sl], n)
                    rank = jnp.where(btop_ref[r, pl.ds(p, 1), sl] > s2, r + 1.0, rank)
                n_scr[p, :, :, sl] = n.reshape(grouped)
                e1_scr[p, :, :, sl] = jnp.exp(s1 - atop_ref[0, pl.ds(p, 1), sl]).reshape(grouped)
                rank_scr[p, :, sl] = rank.astype(GATE_DTYPE)
                m2 = btop_ref[0, pl.ds(p, 1), sl]
                e2_scr[p, :, sl] = (jnp.exp(s2 - m2) * (0.5 * iz_ref[pl.ds(p, 1), sl])).astype(GATE_DTYPE)

    zero = jnp.zeros((PK, LANES), GATE_DTYPE)
    for sub_tile in range(te // DENSE_SUBTILE):
        erows = pl.ds(sub_tile * DENSE_SUBTILE, DENSE_SUBTILE)
        a_scr[erows, :] = _dot_nt(u_ref[erows, :], h_ref[...])
        for ibg in range(sub_tile * DENSE_SUBTILE // (IBQ * PEER_NKEYS),
                         (sub_tile + 1) * DENSE_SUBTILE // (IBQ * PEER_NKEYS)):
            for st in range(tt // LANES):
                sl = pl.ds(st * LANES, LANES)
                gates = [[zero for _ in range(PEER_NKEYS // PK)] for _ in range(IBQ)]
                for p in range(PEER_HEADS):
                    rank = rank_scr[p, :, sl]
                    e2 = e2_scr[p, :, sl]
                    for q in range(IBQ):
                        ib = ibg * IBQ + q
                        grp = jt * n_grp + ib // SUBLANES
                        sub = pl.ds(ib % SUBLANES, 1)
                        n = jnp.broadcast_to(n_scr[p, grp, sub, sl], (PK, LANES)).astype(GATE_DTYPE)
                        e1 = jnp.broadcast_to(e1_scr[p, grp, sub, sl], (PK, LANES)).astype(GATE_DTYPE)
                        for r in range(PEER_NKEYS // PK):
                            rr = slice(r * PK, (r + 1) * PK)
                            gates[q][r] = gates[q][r] + jnp.where(rank[rr, :] < n, e2[rr, :], zero) * e1
                for q in range(IBQ):
                    rows = pl.ds((ibg * IBQ + q) * PEER_NKEYS, PEER_NKEYS)
                    gate = jnp.concatenate(gates[q], axis=0)
                    p_scr[rows, sl] = (_gelu_exact_x2(a_scr[rows, sl]).astype(GATE_DTYPE) * gate).astype(BF16)
        yt_scr[...] += jnp.dot(vt_ref[:, erows], p_scr[erows, :], preferred_element_type=F32)

    @pl.when(jt == pl.num_programs(1) - 1)
    def _():
        o_ref[...] = x1_ref[...] + yt_scr[...].T


def _peer_dense(h2, u_bf, vt_bf, scores_t, iz, cnt, atop, btop, x1, tt, te):
    T, D = h2.shape
    E = u_bf.shape[0]
    npc, nk, _ = scores_t.shape
    assert te % (PEER_NKEYS * SUBLANES) == 0 and E % te == 0
    top = pl.BlockSpec((PEER_TOPK, PEER_HEADS, tt), lambda t, j: (0, 0, t))
    return pl.pallas_call(
        _peer_dense_kernel,
        grid=(T // tt, E // te),
        in_specs=[
            pl.BlockSpec((tt, D), lambda t, j: (t, 0)),
            pl.BlockSpec((te, D), lambda t, j: (j, 0)),
            pl.BlockSpec((D, te), lambda t, j: (0, j)),
            pl.BlockSpec((npc, nk, tt), lambda t, j: (0, 0, t)),
            pl.BlockSpec((PEER_HEADS, tt), lambda t, j: (0, t)),
            top, top, top,
            pl.BlockSpec((tt, D), lambda t, j: (t, 0)),
        ],
        out_specs=pl.BlockSpec((tt, D), lambda t, j: (t, 0)),
        out_shape=jax.ShapeDtypeStruct((T, D), F32),
        scratch_shapes=[
            pltpu.VMEM((PEER_HEADS, nk // SUBLANES, SUBLANES, tt), F32),
            pltpu.VMEM((PEER_HEADS, nk // SUBLANES, SUBLANES, tt), F32),
            pltpu.VMEM((PEER_HEADS, nk, tt), GATE_DTYPE),
            pltpu.VMEM((PEER_HEADS, nk, tt), GATE_DTYPE),
            pltpu.VMEM((te, tt), F32),
            pltpu.VMEM((te, tt), BF16),
            pltpu.VMEM((D, tt), F32),
        ],
        compiler_params=_cparams(("parallel", "arbitrary")),
        name="peer_dense",
    )(h2, u_bf, vt_bf, scores_t, iz, cnt, atop, btop, x1)


def _transpose_cast_kernel(x_ref, o_ref):
    o_ref[...] = x_ref[...].T.astype(o_ref.dtype)


def _transpose_cast(x, dtype, rows):
    E, D = x.shape
    return pl.pallas_call(
        _transpose_cast_kernel,
        grid=(E // rows,),
        in_specs=[pl.BlockSpec((rows, D), lambda i: (i, 0))],
        out_specs=pl.BlockSpec((D, rows), lambda i: (0, i)),
        out_shape=jax.ShapeDtypeStruct((D, E), dtype),
        compiler_params=_cparams(("parallel",)),
        name="value_table_layout",
    )(x)


def _pick(n, pref):
    t = min(n, pref)
    assert n % t == 0, (n, t)
    return t


def _tiles(T, S):
    return {
        "in_proj_rows": _pick(T, 512),
        "scan_steps": _pick(S, 256),
        "mla_rows": _pick(T, 1024),
        "attn_queries": _pick(S, 2048),
        "out_proj_rows": _pick(T, 512),
        "select_tokens": _pick(T, 256),
        "dense_tokens": _pick(T, 512),
        "dense_experts": 2 * PEER_NKEYS * SUBLANES,
    }


def _layer(x2, pos, B, S, attn_norm, w_in, lb_logits, hg_o_norm, q_a_norm, w_q_up, kv_a_norm,
           w_kv_up, q_norm, k_norm, mla_o_norm, w_out, ffn_norm, peer_w_q, peer_sub_keys, peer_u, peer_v):
    T, D = x2.shape
    n_hg = 5 * HG_HEADS * HG_DK
    row = lambda a: a.reshape(1, -1).astype(F32)

    w_hg = w_in[:, :n_hg].astype(BF16)
    n_mla = w_in.shape[1] - n_hg
    w_mla = jnp.pad(w_in[:, n_hg:], ((0, 0), (0, -n_mla % LANES))).astype(BF16)
    pad_r = LANES - MLA_ROPE
    wq = w_q_up.reshape(MLA_Q_LORA, MLA_HEADS, MLA_QK)
    wq = jnp.pad(wq, ((0, 0), (0, 0), (0, MLA_QPAD - MLA_QK))).reshape(MLA_Q_LORA, MLA_HEADS * MLA_QPAD)
    wq = wq.astype(BF16)
    wkv = w_kv_up.astype(BF16)
    qgn, qgr = row(q_norm[:MLA_NOPE]), row(jnp.pad(q_norm[MLA_NOPE:], (0, pad_r)))
    kgn, kgr = row(k_norm[:MLA_NOPE]), row(jnp.pad(k_norm[MLA_NOPE:], (0, pad_r)))
    inv_freq = 1.0 / (ROPE_THETA ** (jnp.arange(0, MLA_ROPE, 2, dtype=F32) / MLA_ROPE))
    invf = row(jnp.concatenate([inv_freq, inv_freq, jnp.zeros((pad_r,), F32)]))
    keys = peer_sub_keys.reshape(2 * PEER_HEADS, PEER_NKEYS, PEER_DHALF).astype(BF16)
    u_bf = peer_u.astype(BF16)
    vt_bf = _transpose_cast(peer_v, BF16, _pick(peer_v.shape[0], 1024))

    tile = _tiles(T, S)
    hg, mla = _in_proj(x2, row(attn_norm), w_hg, w_mla, tile["in_proj_rows"])
    o_f = _hgrn_scan(hg, lb_logits[:, 0, :], B, S, False, tile["scan_steps"])
    o_b = _hgrn_scan(hg, lb_logits[:, 1, :], B, S, True, tile["scan_steps"])
    q, k, v = _mla_prep(mla, pos, invf, row(q_a_norm), row(kv_a_norm), wq, wkv, qgn, qgr, kgn, kgr,
                        tile["mla_rows"])
    y_mla = _attention(q, k, v, mla_o_norm.reshape(1, -1).astype(F32), B, S, tile["attn_queries"])
    x1, h2, scores_t = _out_proj(o_f, o_b, hg, y_mla, x2, row(hg_o_norm), w_out.astype(BF16),
                                 row(ffn_norm), peer_w_q.astype(BF16), keys, tile["out_proj_rows"])
    iz, cnt, atop, btop = _peer_select(scores_t, tile["select_tokens"])
    return _peer_dense(h2, u_bf, vt_bf, scores_t, iz, cnt, atop, btop, x1,
                       tile["dense_tokens"], tile["dense_experts"])


def kernel(x, positions, attn_norm, w_in, hg_lb_logits, hg_o_norm, q_a_norm, w_q_up, kv_a_norm, w_kv_up,
           q_norm, k_norm, mla_o_norm, w_out, ffn_norm, peer_w_q, peer_sub_keys, peer_u, peer_v):
    B, S, D = x.shape
    x2 = x.reshape(B * S, D)
    pos = positions.reshape(B * S, 1)
    for l in range(attn_norm.shape[0]):
        assert l == 0
        x2 = _layer(x2, pos, B, S, attn_norm[l], w_in[l], hg_lb_logits, hg_o_norm[l], q_a_norm[l],
                    w_q_up[l], kv_a_norm[l], w_kv_up[l], q_norm[l], k_norm[l], mla_o_norm[l], w_out[l],
                    ffn_norm[l], peer_w_q[l], peer_sub_keys[l], peer_u[l], peer_v[l])
    return x2.reshape(B, S, D)
```

```python
import functools
import math

import jax
import jax.numpy as jnp
from jax import lax
from jax.experimental import pallas as pl
from jax.experimental.pallas import tpu as pltpu

F32 = jnp.float32
BF16 = jnp.bfloat16
EPS = 1e-6
LANES = 128
SUBLANES = 8
NEG_INF = float("-inf")

HG_HEADS = 4
HG_DK = 128
HG_CHUNK = 64
HG_HALF = HG_CHUNK // 2
HG_QUARTER = HG_CHUNK // 4
HG_BATCH_PER_STEP = 8
MLA_HEADS = 4
MLA_Q_LORA = 384
MLA_KV_LORA = 256
MLA_NOPE = 128
MLA_ROPE = 64
MLA_V = 128
MLA_QK = MLA_NOPE + MLA_ROPE
MLA_QPAD = 2 * LANES
ROPE_THETA = 10000.0
ATTN_CHUNKS = 8
OUT_CHUNKS = 2
PEER_HEADS = 8
PEER_NKEYS = 128
PEER_TOPK = 16
PEER_DHALF = 128
GATE_DTYPE = BF16
DENSE_SUBTILE = 1024

VMEM_LIMIT = 56 * 1024 * 1024


def _cparams(sem):
    return pltpu.CompilerParams(dimension_semantics=sem, vmem_limit_bytes=VMEM_LIMIT)


def _rms(x, gain):
    ms = jnp.mean(x * x, axis=-1, keepdims=True)
    return x * lax.rsqrt(ms + EPS) * gain


def _sigmoid(x):
    return 1.0 / (1.0 + jnp.exp(-x))


def _dot_nt(a, b):
    return lax.dot_general(a, b, (((1,), (1,)), ((), ())), preferred_element_type=F32)


def _dot_tn(a, b):
    return lax.dot_general(a, b, (((0,), (0,)), ((), ())), preferred_element_type=F32)


def _in_proj_kernel(x_ref, g_ref, whg_ref, wmla_ref, hg_ref, mla_ref):
    h = _rms(x_ref[...], g_ref[...]).astype(BF16)
    hg_ref[...] = jnp.dot(h, whg_ref[...], preferred_element_type=F32)
    mla_ref[...] = jnp.dot(h, wmla_ref[...], preferred_element_type=F32)


def _in_proj(x2, gain, w_hg, w_mla, tm):
    T, D = x2.shape
    n_hg, n_mla = w_hg.shape[1], w_mla.shape[1]
    return pl.pallas_call(
        _in_proj_kernel,
        grid=(T // tm,),
        in_specs=[
            pl.BlockSpec((tm, D), lambda i: (i, 0)),
            pl.BlockSpec((1, D), lambda i: (0, 0)),
            pl.BlockSpec((D, n_hg), lambda i: (0, 0)),
            pl.BlockSpec((D, n_mla), lambda i: (0, 0)),
        ],
        out_specs=[
            pl.BlockSpec((tm, n_hg), lambda i: (i, 0)),
            pl.BlockSpec((tm, n_mla), lambda i: (i, 0)),
        ],
        out_shape=[jax.ShapeDtypeStruct((T, n_hg), F32), jax.ShapeDtypeStruct((T, n_mla), F32)],
        compiler_params=_cparams(("parallel",)),
        name="in_proj",
    )(x2, gain, w_hg, w_mla)


def _hgrn_kernel(q_ref, f_ref, v_ref, lb_ref, o_ref, st_ref, b_scr, k_scr, p_scr, *, reverse, n_chunks):
    C, Hf, Qt, G = HG_CHUNK, HG_HALF, HG_QUARTER, SUBLANES
    Et = G
    assert C == 8 * Et
    n_batch = q_ref.shape[0]
    n_heads = st_ref.shape[0] // n_batch
    chains = [(bi, hh) for bi in range(n_batch) for hh in range(n_heads)]

    @pl.when(pl.program_id(1) == 0)
    def _():
        st_ref[...] = jnp.zeros_like(st_ref)

    logits = lb_ref[...]
    ex = jnp.exp(logits - jnp.max(logits, axis=0, keepdims=True))
    lb_all = ex[0:1, :] / jnp.sum(ex, axis=0, keepdims=True)

    row = lax.broadcasted_iota(jnp.int32, (C, C), 0)
    col = lax.broadcasted_iota(jnp.int32, (C, C), 1)
    tri = (col >= row) if reverse else (col <= row)
    cum_bf = tri.astype(F32).astype(BF16)
    same_half = (row >= Hf) == (col >= Hf)
    same_quarter = (row // Qt) == (col // Qt)
    row_c = lax.broadcasted_iota(jnp.int32, (C, LANES), 0)
    q_side = (row_c < Hf) if reverse else (row_c >= Hf)
    odd_quarter = (row_c // Qt) % 2 == 1
    q_side4 = jnp.logical_not(odd_quarter) if reverse else odd_quarter
    low_half = row_c < Hf
    odd_eighth = (row_c // Et) % 2 == 1
    q_side8 = jnp.logical_not(odd_eighth) if reverse else odd_eighth
    quarter_c = row_c // Qt
    lane_g = lax.broadcasted_iota(jnp.int32, (G, LANES), 1)
    ones = jnp.ones((LANES, LANES), BF16)
    mid_row = Hf if reverse else Hf - 1
    qmid_rows = (Qt, Hf + Qt) if reverse else (Qt - 1, Hf + Qt - 1)
    emid_rows = [qb * Qt + (Et if reverse else Et - 1) for qb in range(C // Qt)]
    end_row = 0 if reverse else C - 1

    def gates(ci, r0):
        bi, hh = chains[ci]
        cols = slice(hh * HG_DK, (hh + 1) * HG_DK)
        lb = lb_all[:, cols]
        q_raw = q_ref[bi, pl.ds(r0, C), cols]
        q = q_raw * _sigmoid(q_raw)
        f = lb + (1.0 - lb) * _sigmoid(f_ref[bi, pl.ds(r0, C), cols])
        k = 1.0 - f
        vb = v_ref[bi, pl.ds(r0, C), cols].astype(BF16)
        g = jnp.log2(f)
        g_hi = g.astype(BF16)
        g_r = g - g_hi.astype(F32)
        g_mid = g_r.astype(BF16)
        g_lo = (g_r - g_mid.astype(F32)).astype(BF16)
        b = (jnp.dot(cum_bf, g_hi, preferred_element_type=F32)
             + jnp.dot(cum_bf, g_mid, preferred_element_type=F32)
             + jnp.dot(cum_bf, g_lo, preferred_element_type=F32))
        b_scr[ci] = b
        k_scr[ci] = k
        return q, k, vb, b

    def state_and_offdiag(ci, q, k, vb, b):
        b_mid = b_scr[ci, pl.ds(mid_row, 1), :]
        b_end = b_scr[ci, pl.ds(end_row, 1), :]
        st = st_ref[ci]
        o = _dot_nt((q * jnp.exp2(b)).astype(BF16), st.astype(BF16))
        kd = (k * jnp.exp2(b_end - b)).astype(BF16)
        st_ref[ci] = st * jnp.exp2(b_end) + _dot_tn(vb, kd)
        qt = jnp.where(q_side, q * jnp.exp2(b - b_mid), 0.0).astype(BF16)
        kt = jnp.where(q_side, 0.0, k * jnp.exp2(b_mid - b)).astype(BF16)
        b_q = jnp.where(low_half, b_scr[ci, pl.ds(qmid_rows[0], 1), :], b_scr[ci, pl.ds(qmid_rows[1], 1), :])
        qt4 = jnp.where(q_side4, q * jnp.exp2(b - b_q), 0.0).astype(BF16)
        kt4 = jnp.where(q_side4, 0.0, k * jnp.exp2(b_q - b)).astype(BF16)
        b_e = b_scr[ci, pl.ds(emid_rows[-1], 1), :]
        for qb in reversed(range(C // Qt - 1)):
            b_e = jnp.where(quarter_c == qb, b_scr[ci, pl.ds(emid_rows[qb], 1), :], b_e)
        qt8 = jnp.where(q_side8, q * jnp.exp2(b - b_e), 0.0).astype(BF16)
        kt8 = jnp.where(q_side8, 0.0, k * jnp.exp2(b_e - b)).astype(BF16)
        return o, _dot_nt(qt, kt), _dot_nt(qt4, kt4), _dot_nt(qt8, kt8)

    def diag_products(ci, q, b):
        for s in range(C):
            rows = slice(s // G * G, s // G * G + G)
            bs = b_scr[ci, pl.ds(s, 1), :]
            ks = k_scr[ci, pl.ds(s, 1), :]
            p_scr[ci, pl.ds(s * G, G), :] = (q[rows, :] * ks) * jnp.exp2(b[rows, :] - bs)

    def diag_scores(rs):
        groups = []
        for g in range(C // G):
            acc = jnp.zeros((G, LANES), F32)
            for s in range(g * G, (g + 1) * G):
                acc = jnp.where(lane_g == s, rs[s * G:(s + 1) * G, :], acc)
            groups.append(acc)
        return jnp.concatenate(groups, axis=0)[:, :C]

    def chunk_body(ci, carry):
        c = (n_chunks - 1 - ci) if reverse else ci
        r0 = pl.multiple_of(c * C, C)
        ids = range(len(chains))
        qkvb = [gates(ci, r0) for ci in ids]
        o_sc = [state_and_offdiag(ci, *qkvb[ci]) for ci in ids]
        for ci in ids:
            diag_products(ci, qkvb[ci][0], qkvb[ci][3])
        rs = [jnp.dot(p_scr[ci].astype(BF16), ones, preferred_element_type=F32) for ci in ids]
        for ci in ids:
            bi, hh = chains[ci]
            o, sc_half, sc_quarter, sc_eighth = o_sc[ci]
            inner = jnp.where(same_quarter, diag_scores(rs[ci]) + sc_eighth, sc_quarter)
            scores = jnp.where(same_half, jnp.where(tri, inner, 0.0), sc_half)
            o = o + jnp.dot(scores.astype(BF16), qkvb[ci][2], preferred_element_type=F32)
            o_ref[bi, pl.ds(r0, C), hh * HG_DK:(hh + 1) * HG_DK] = o
        return carry

    lax.fori_loop(0, n_chunks, chunk_body, 0)


def _hgrn_diag_rows():
    return HG_CHUNK * SUBLANES


def _hgrn_scan(hg, lb_logits_dir, B, S, reverse, tb):
    T = hg.shape[0]
    W = HG_HEADS * HG_DK
    nblk = S // tb
    f_blk = 2 if reverse else 1
    v_blk = 3
    nbat = HG_BATCH_PER_STEP if B % HG_BATCH_PER_STEP == 0 else 1
    n_chains = nbat * HG_HEADS
    hg4 = hg.reshape(B // nbat, nbat, S, hg.shape[1])

    def spec(col_blk):
        return pl.BlockSpec((None, nbat, tb, W),
                            lambda g, i: (g, 0, (nblk - 1 - i) if reverse else i, col_blk))

    kern = functools.partial(_hgrn_kernel, reverse=reverse, n_chunks=tb // HG_CHUNK)
    out = pl.pallas_call(
        kern,
        grid=(B // nbat, nblk),
        in_specs=[spec(0), spec(f_blk), spec(v_blk),
                  pl.BlockSpec((lb_logits_dir.shape[0], W), lambda g, i: (0, 0))],
        out_specs=spec(0),
        out_shape=jax.ShapeDtypeStruct((B // nbat, nbat, S, W), F32),
        scratch_shapes=[
            pltpu.VMEM((n_chains, HG_DK, HG_DK), F32),
            pltpu.VMEM((n_chains, HG_CHUNK, HG_DK), F32),
            pltpu.VMEM((n_chains, HG_CHUNK, HG_DK), F32),
            pltpu.VMEM((n_chains, _hgrn_diag_rows(), HG_DK), F32),
        ],
        compiler_params=_cparams(("parallel", "arbitrary")),
        name="hgrn_rev" if reverse else "hgrn_fwd",
    )(hg4, hg4, hg4, lb_logits_dir)
    return out.reshape(T, W)


def _mla_prep_kernel(mla_ref, pos_ref, invf_ref, qa_ref, kva_ref, wq_ref, wkv_ref,
                     qgn_ref, qgr_ref, kgn_ref, kgr_ref, q_ref, k_ref, v_ref):
    tm = mla_ref.shape[0]
    lane = lax.broadcasted_iota(jnp.int32, (tm, LANES), 1)
    ang = pos_ref[...].astype(F32) * invf_ref[...]
    cos_t = jnp.where(lane < MLA_ROPE, jnp.cos(ang), 0.0)
    sin_a = jnp.sin(ang)
    sin_t = jnp.where(lane < MLA_ROPE // 2, -sin_a, jnp.where(lane < MLA_ROPE, sin_a, 0.0))

    def rope(t):
        swapped = jnp.where(lane < MLA_ROPE // 2,
                            pltpu.roll(t, LANES - MLA_ROPE // 2, 1),
                            pltpu.roll(t, MLA_ROPE // 2, 1))
        return t * cos_t + swapped * sin_t

    def norm_rope_part(t, gain):
        ms = jnp.sum(t * t, axis=-1, keepdims=True) * (1.0 / MLA_ROPE)
        return t * lax.rsqrt(ms + EPS) * gain

    cq = mla_ref[:, 0:MLA_Q_LORA]
    ckv = mla_ref[:, MLA_Q_LORA:MLA_Q_LORA + MLA_KV_LORA]
    kr = mla_ref[:, MLA_Q_LORA + MLA_KV_LORA:]
    q_up = jnp.dot(_rms(cq, qa_ref[...]).astype(BF16), wq_ref[...], preferred_element_type=F32)
    kv_up = jnp.dot(_rms(ckv, kva_ref[...]).astype(BF16), wkv_ref[...], preferred_element_type=F32)
    k_rope = rope(norm_rope_part(kr, kgr_ref[...])).astype(BF16)
    for h in range(MLA_HEADS):
        c0 = h * MLA_QPAD
        q_ref[:, c0:c0 + LANES] = _rms(q_up[:, c0:c0 + LANES], qgn_ref[...]).astype(BF16)
        q_ref[:, c0 + LANES:c0 + 2 * LANES] = rope(
            norm_rope_part(q_up[:, c0 + LANES:c0 + 2 * LANES], qgr_ref[...])).astype(BF16)
        k_ref[:, c0:c0 + LANES] = _rms(kv_up[:, c0:c0 + LANES], kgn_ref[...]).astype(BF16)
        k_ref[:, c0 + LANES:c0 + 2 * LANES] = k_rope
        v_ref[:, c0:c0 + LANES] = kv_up[:, c0 + LANES:c0 + 2 * LANES].astype(BF16)
        v_ref[:, c0 + LANES:c0 + 2 * LANES] = jnp.where(lane == 0, 1.0, 0.0).astype(BF16)


def _mla_prep(mla, pos, invf, qa, kva, wq, wkv, qgn, qgr, kgn, kgr, tm):
    T, n_mla = mla.shape
    full = lambda a: pl.BlockSpec(a.shape, lambda i: (0, 0))
    return pl.pallas_call(
        _mla_prep_kernel,
        grid=(T // tm,),
        in_specs=[
            pl.BlockSpec((tm, n_mla), lambda i: (i, 0)),
            pl.BlockSpec((tm, 1), lambda i: (i, 0)),
            full(invf), full(qa), full(kva), full(wq), full(wkv),
            full(qgn), full(qgr), full(kgn), full(kgr),
        ],
        out_specs=[
            pl.BlockSpec((tm, MLA_HEADS * MLA_QPAD), lambda i: (i, 0)),
            pl.BlockSpec((tm, MLA_HEADS * MLA_QPAD), lambda i: (i, 0)),
            pl.BlockSpec((tm, MLA_HEADS * MLA_QPAD), lambda i: (i, 0)),
        ],
        out_shape=[
            jax.ShapeDtypeStruct((T, MLA_HEADS * MLA_QPAD), BF16),
            jax.ShapeDtypeStruct((T, MLA_HEADS * MLA_QPAD), BF16),
            jax.ShapeDtypeStruct((T, MLA_HEADS * MLA_QPAD), BF16),
        ],
        compiler_params=_cparams(("parallel",)),
        name="mla_prep",
    )(mla, pos, invf, qa, kva, wq, wkv, qgn, qgr, kgn, kgr)


def _attn_kernel(q_ref, k_ref, v_ref, g_ref, o_ref):
    c = MLA_QK ** -0.5 * math.log2(math.e)
    tq = q_ref.shape[0]
    rc = tq // ATTN_CHUNKS
    rows = [pl.ds(i * rc, rc) for i in range(ATTN_CHUNKS)]
    s = [_dot_nt(q_ref[r, :], k_ref[...]) for r in rows]
    p = [jnp.exp2((si - jnp.max(si, axis=-1, keepdims=True)) * c).astype(BF16) for si in s]
    o = [jnp.dot(pi, v_ref[...], preferred_element_type=F32) for pi in p]
    for r, oi in zip(rows, o):
        o_ref[r, :] = _rms(oi[:, :MLA_V] / oi[:, MLA_V:MLA_V + 1], g_ref[...]).astype(o_ref.dtype)


def _attention(q, k, v, o_gain, B, S, tq):
    T = q.shape[0]
    nq = S // tq
    return pl.pallas_call(
        _attn_kernel,
        grid=(B, MLA_HEADS, nq),
        in_specs=[
            pl.BlockSpec((tq, MLA_QPAD), lambda b, h, i: (b * nq + i, h)),
            pl.BlockSpec((S, MLA_QPAD), lambda b, h, i: (b, h)),
            pl.BlockSpec((S, MLA_QPAD), lambda b, h, i: (b, h)),
            pl.BlockSpec((1, MLA_V), lambda b, h, i: (0, h)),
        ],
        out_specs=pl.BlockSpec((tq, MLA_V), lambda b, h, i: (b * nq + i, h)),
        out_shape=jax.ShapeDtypeStruct((T, MLA_HEADS * MLA_V), BF16),
        compiler_params=_cparams(("parallel", "parallel", "arbitrary")),
        name="mla_attention",
    )(q, k, v, o_gain)


def _out_proj_kernel(of_ref, ob_ref, g_ref, ym_ref, x_ref, hgg_ref, wo_ref, fg_ref, wq_ref, keys_ref,
                     x1_ref, h2_ref, st_ref):
    tm = x_ref.shape[0]
    rc = tm // OUT_CHUNKS
    rows = [pl.ds(i * rc, rc) for i in range(OUT_CHUNKS)]
    w_hg_rows = of_ref.shape[1]

    def mixer_out(r):
        o = of_ref[r, :] + ob_ref[r, :]
        gate_raw = g_ref[r, :]
        gate = gate_raw * _sigmoid(gate_raw)
        parts = []
        for h in range(HG_HEADS):
            sl = slice(h * HG_DK, (h + 1) * HG_DK)
            parts.append(_rms(o[:, sl], hgg_ref[:, sl]) * gate[:, sl])
        return jnp.concatenate(parts, axis=-1).astype(BF16)

    y_hg = [mixer_out(r) for r in rows]
    mix = [jnp.dot(y, wo_ref[0:w_hg_rows, :], preferred_element_type=F32)
           + jnp.dot(ym_ref[r, :], wo_ref[w_hg_rows:, :], preferred_element_type=F32)
           for y, r in zip(y_hg, rows)]
    x1 = [x_ref[r, :] + m for r, m in zip(rows, mix)]
    h2 = [_rms(x, fg_ref[...]).astype(BF16) for x in x1]
    for r, x, h in zip(rows, x1, h2):
        x1_ref[r, :] = x
        h2_ref[r, :] = h
    pq = [jnp.dot(h, wq_ref[...], preferred_element_type=F32).astype(BF16) for h in h2]
    pq = jnp.concatenate(pq, axis=0)
    for pc in range(keys_ref.shape[0]):
        st_ref[pc] = _dot_nt(keys_ref[pc], pq[:, pc * PEER_DHALF:(pc + 1) * PEER_DHALF])


def _out_proj(o_f, o_b, hg, y_mla, x2, hg_gain, w_out, ffn_gain, w_pq, keys, tm):
    T, D = x2.shape
    W = o_f.shape[1]
    npc = keys.shape[0]
    full2 = lambda a: pl.BlockSpec(a.shape, lambda i: (0, 0))
    return pl.pallas_call(
        _out_proj_kernel,
        grid=(T // tm,),
        in_specs=[
            pl.BlockSpec((tm, W), lambda i: (i, 0)),
            pl.BlockSpec((tm, W), lambda i: (i, 0)),
            pl.BlockSpec((tm, W), lambda i: (i, 4)),
            pl.BlockSpec((tm, y_mla.shape[1]), lambda i: (i, 0)),
            pl.BlockSpec((tm, D), lambda i: (i, 0)),
            full2(hg_gain), full2(w_out), full2(ffn_gain), full2(w_pq),
            pl.BlockSpec(keys.shape, lambda i: (0, 0, 0)),
        ],
        out_specs=[
            pl.BlockSpec((tm, D), lambda i: (i, 0)),
            pl.BlockSpec((tm, D), lambda i: (i, 0)),
            pl.BlockSpec((npc, PEER_NKEYS, tm), lambda i: (0, 0, i)),
        ],
        out_shape=[
            jax.ShapeDtypeStruct((T, D), F32),
            jax.ShapeDtypeStruct((T, D), BF16),
            jax.ShapeDtypeStruct((npc, PEER_NKEYS, T), F32),
        ],
        compiler_params=_cparams(("parallel",)),
        name="out_proj_peer_query",
    )(o_f, o_b, hg, y_mla, x2, hg_gain, w_out, ffn_gain, w_pq, keys)


def _sorting_network(n):
    pairs = []
    p = 1
    while p < n:
        k = p
        while k >= 1:
            for j in range(k % p, n - k, 2 * k):
                for i in range(min(k, n - j - k)):
                    if (i + j) // (2 * p) == (i + j + k) // (2 * p):
                        pairs.append((i + j, i + j + k))
            k //= 2
        p *= 2
    return pairs


def _bitonic_merge_network(n):
    pairs = []
    stride = n // 2
    while stride >= 1:
        pairs += [(i, i + stride) for i in range(n) if not i & stride]
        stride //= 2
    return pairs


def _compare_exchange(xs, pairs):
    xs = list(xs)
    for i, j in pairs:
        xs[i], xs[j] = jnp.maximum(xs[i], xs[j]), jnp.minimum(xs[i], xs[j])
    return xs


def _peer_select_kernel(s_ref, iz_ref, cnt_ref, atop_ref, btop_ref, top_scr):
    K = PEER_TOPK
    n_vregs = PEER_NKEYS // SUBLANES
    assert n_vregs == K
    sort_pairs = _sorting_network(K)
    merge_pairs = _bitonic_merge_network(K)
    for p in range(PEER_HEADS):
        for c in range(2):
            lists = _compare_exchange(
                [s_ref[2 * p + c, pl.ds(g * SUBLANES, SUBLANES), :] for g in range(n_vregs)], sort_pairs)
            shift = SUBLANES // 2
            while shift >= 1:
                other = [pltpu.roll(x, shift, 0) for x in lists]
                lists = _compare_exchange(
                    [jnp.maximum(lists[r], other[K - 1 - r]) for r in range(K)], merge_pairs)
                shift //= 2
            for r in range(K):
                top_scr[c, r, pl.ds(p, 1), :] = lists[r][0:1, :]
    a = [top_scr[0, r] for r in range(K)]
    b = [top_scr[1, r] for r in range(K)]
    pairs = [(i, j) for i in range(K) for j in range(K) if (i + 1) * (j + 1) <= K]
    cands = [a[i] + b[j] for i, j in pairs]
    top = a[0] + b[0]
    cands += [jnp.full_like(top, NEG_INF)] * (-len(cands) % K)
    groups = [_compare_exchange(cands[g:g + K], sort_pairs) for g in range(0, len(cands), K)]
    while len(groups) > 1:
        merged = [_compare_exchange([jnp.maximum(x[r], y[K - 1 - r]) for r in range(K)], merge_pairs)
                  for x, y in zip(groups[0::2], groups[1::2])]
        groups = merged + groups[len(groups) // 2 * 2:]
    best = groups[0]
    kth = best[K - 1]
    z = jnp.zeros_like(top)
    for r in range(K):
        z = z + jnp.exp(best[r] - top)
    iz_ref[...] = 1.0 / z
    for i in range(K):
        cnt = jnp.zeros_like(top)
        for j in range(K):
            if (i, j) in pairs:
                cnt = cnt + jnp.where(a[i] + b[j] >= kth, 1.0, 0.0)
        cnt_ref[i] = cnt
        atop_ref[i] = a[i]
        btop_ref[i] = b[i]


def _peer_select(scores_t, tt):
    npc, nk, T = scores_t.shape
    top = jax.ShapeDtypeStruct((PEER_TOPK, PEER_HEADS, T), F32)
    top_spec = pl.BlockSpec((PEER_TOPK, PEER_HEADS, tt), lambda i: (0, 0, i))
    return pl.pallas_call(
        _peer_select_kernel,
        grid=(T // tt,),
        in_specs=[pl.BlockSpec((npc, nk, tt), lambda i: (0, 0, i))],
        out_specs=[pl.BlockSpec((PEER_HEADS, tt), lambda i: (0, i))] + [top_spec] * 3,
        out_shape=[jax.ShapeDtypeStruct((PEER_HEADS, T), F32)] + [top] * 3,
        scratch_shapes=[pltpu.VMEM((2, PEER_TOPK, PEER_HEADS, tt), F32)],
        compiler_params=_cparams(("parallel",)),
        name="peer_select",
    )(scores_t)


def _gelu_exact_x2(x):
    return x * (1.0 + lax.erf(x * (1.0 / math.sqrt(2.0))))


def _peer_dense_kernel(h_ref, u_ref, vt_ref, s_ref, iz_ref, cnt_ref, atop_ref, btop_ref, x1_ref, o_ref,
                       n_scr, e1_scr, rank_scr, e2_scr, a_scr, p_scr, yt_scr):
    jt = pl.program_id(1)
    te, tt = a_scr.shape
    nb = te // PEER_NKEYS
    n_grp = nb // SUBLANES
    PK = SUBLANES * (4 // jnp.dtype(GATE_DTYPE).itemsize)
    IBQ = 32 * PK // PEER_NKEYS

    @pl.when(jt == 0)
    def _():
        yt_scr[...] = jnp.zeros_like(yt_scr)
        grouped = (PEER_NKEYS // SUBLANES, SUBLANES, LANES)
        for p in range(PEER_HEADS):
            for st in range(tt // LANES):
                sl = pl.ds(st * LANES, LANES)
                s1 = s_ref[2 * p, :, sl]
                s2 = s_ref[2 * p + 1, :, sl]
                n = jnp.zeros_like(s1)
                rank = jnp.zeros_like(s2)
                for r in range(PEER_TOPK):
                    n = jnp.where(s1 == atop_ref[r, pl.ds(p, 1), sl], cnt_ref[r, pl.ds(p, 1), sl], n)
                    rank = jnp.where(btop_ref[r, pl.ds(p, 1), sl] > s2, r + 1.0, rank)
                n_scr[p, :, :, sl] = n.reshape(grouped)
                e1_scr[p, :, :, sl] = jnp.exp(s1 - atop_ref[0, pl.ds(p, 1), sl]).reshape(grouped)
                rank_scr[p, :, sl] = rank.astype(GATE_DTYPE)
                m2 = btop_ref[0, pl.ds(p, 1), sl]
                e2_scr[p, :, sl] = (jnp.exp(s2 - m2) * (0.5 * iz_ref[pl.ds(p, 1), sl])).astype(GATE_DTYPE)

    zero = jnp.zeros((PK, LANES), GATE_DTYPE)
    for sub_tile in range(te // DENSE_SUBTILE):
        erows = pl.ds(sub_tile * DENSE_SUBTILE, DENSE_SUBTILE)
        a_scr[erows, :] = _dot_nt(u_ref[erows, :], h_ref[...])
        for ibg in range(sub_tile * DENSE_SUBTILE // (IBQ * PEER_NKEYS),
                         (sub_tile + 1) * DENSE_SUBTILE // (IBQ * PEER_NKEYS)):
            for st in range(tt // LANES):
                sl = pl.ds(st * LANES, LANES)
                gates = [[zero for _ in range(PEER_NKEYS // PK)] for _ in range(IBQ)]
                for p in range(PEER_HEADS):
                    rank = rank_scr[p, :, sl]
                    e2 = e2_scr[p, :, sl]
                    for q in range(IBQ):
                        ib = ibg * IBQ + q
                        grp = jt * n_grp + ib // SUBLANES
                        sub = pl.ds(ib % SUBLANES, 1)
                        n = jnp.broadcast_to(n_scr[p, grp, sub, sl], (PK, LANES)).astype(GATE_DTYPE)
                        e1 = jnp.broadcast_to(e1_scr[p, grp, sub, sl], (PK, LANES)).astype(GATE_DTYPE)
                        for r in range(PEER_NKEYS // PK):
                            rr = slice(r * PK, (r + 1) * PK)
                            gates[q][r] = gates[q][r] + jnp.where(rank[rr, :] < n, e2[rr, :], zero) * e1
                for q in range(IBQ):
                    rows = pl.ds((ibg * IBQ + q) * PEER_NKEYS, PEER_NKEYS)
                    gate = jnp.concatenate(gates[q], axis=0)
                    p_scr[rows, sl] = (_gelu_exact_x2(a_scr[rows, sl]).astype(GATE_DTYPE) * gate).astype(BF16)
        yt_scr[...] += jnp.dot(vt_ref[:, erows], p_scr[erows, :], preferred_element_type=F32)

    @pl.when(jt == pl.num_programs(1) - 1)
    def _():
        o_ref[...] = x1_ref[...] + yt_scr[...].T


def _peer_dense(h2, u_bf, vt_bf, scores_t, iz, cnt, atop, btop, x1, tt, te):
    T, D = h2.shape
    E = u_bf.shape[0]
    npc, nk, _ = scores_t.shape
    assert te % (PEER_NKEYS * SUBLANES) == 0 and E % te == 0
    top = pl.BlockSpec((PEER_TOPK, PEER_HEADS, tt), lambda t, j: (0, 0, t))
    return pl.pallas_call(
        _peer_dense_kernel,
        grid=(T // tt, E // te),
        in_specs=[
            pl.BlockSpec((tt, D), lambda t, j: (t, 0)),
            pl.BlockSpec((te, D), lambda t, j: (j, 0)),
            pl.BlockSpec((D, te), lambda t, j: (0, j)),
            pl.BlockSpec((npc, nk, tt), lambda t, j: (0, 0, t)),
            pl.BlockSpec((PEER_HEADS, tt), lambda t, j: (0, t)),
            top, top, top,
            pl.BlockSpec((tt, D), lambda t, j: (t, 0)),
        ],
        out_specs=pl.BlockSpec((tt, D), lambda t, j: (t, 0)),
        out_shape=jax.ShapeDtypeStruct((T, D), F32),
        scratch_shapes=[
            pltpu.VMEM((PEER_HEADS, nk // SUBLANES, SUBLANES, tt), F32),
            pltpu.VMEM((PEER_HEADS, nk // SUBLANES, SUBLANES, tt), F32),
            pltpu.VMEM((PEER_HEADS, nk, tt), GATE_DTYPE),
            pltpu.VMEM((PEER_HEADS, nk, tt), GATE_DTYPE),
            pltpu.VMEM((te, tt), F32),
            pltpu.VMEM((te, tt), BF16),
            pltpu.VMEM((D, tt), F32),
        ],
        compiler_params=_cparams(("parallel", "arbitrary")),
        name="peer_dense",
    )(h2, u_bf, vt_bf, scores_t, iz, cnt, atop, btop, x1)


def _transpose_cast_kernel(x_ref, o_ref):
    o_ref[...] = x_ref[...].T.astype(o_ref.dtype)


def _transpose_cast(x, dtype, rows):
    E, D = x.shape
    return pl.pallas_call(
        _transpose_cast_kernel,
        grid=(E // rows,),
        in_specs=[pl.BlockSpec((rows, D), lambda i: (i, 0))],
        out_specs=pl.BlockSpec((D, rows), lambda i: (0, i)),
        out_shape=jax.ShapeDtypeStruct((D, E), dtype),
        compiler_params=_cparams(("parallel",)),
        name="value_table_layout",
    )(x)


def _pick(n, pref):
    t = min(n, pref)
    assert n % t == 0, (n, t)
    return t


def _tiles(T, S):
    return {
        "in_proj_rows": _pick(T, 512),
        "scan_steps": _pick(S, 256),
        "mla_rows": _pick(T, 1024),
        "attn_queries": _pick(S, 2048),
        "out_proj_rows": _pick(T, 512),
        "select_tokens": _pick(T, 256),
        "dense_tokens": _pick(T, 512),
        "dense_experts": 2 * PEER_NKEYS * SUBLANES,
    }


def _layer(x2, pos, B, S, attn_norm, w_in, lb_logits, hg_o_norm, q_a_norm, w_q_up, kv_a_norm,
           w_kv_up, q_norm, k_norm, mla_o_norm, w_out, ffn_norm, peer_w_q, peer_sub_keys, peer_u, peer_v):
    T, D = x2.shape
    n_hg = 5 * HG_HEADS * HG_DK
    row = lambda a: a.reshape(1, -1).astype(F32)

    w_hg = w_in[:, :n_hg].astype(BF16)
    n_mla = w_in.shape[1] - n_hg
    w_mla = jnp.pad(w_in[:, n_hg:], ((0, 0), (0, -n_mla % LANES))).astype(BF16)
    pad_r = LANES - MLA_ROPE
    wq = w_q_up.reshape(MLA_Q_LORA, MLA_HEADS, MLA_QK)
    wq = jnp.pad(wq, ((0, 0), (0, 0), (0, MLA_QPAD - MLA_QK))).reshape(MLA_Q_LORA, MLA_HEADS * MLA_QPAD)
    wq = wq.astype(BF16)
    wkv = w_kv_up.astype(BF16)
    qgn, qgr = row(q_norm[:MLA_NOPE]), row(jnp.pad(q_norm[MLA_NOPE:], (0, pad_r)))
    kgn, kgr = row(k_norm[:MLA_NOPE]), row(jnp.pad(k_norm[MLA_NOPE:], (0, pad_r)))
    inv_freq = 1.0 / (ROPE_THETA ** (jnp.arange(0, MLA_ROPE, 2, dtype=F32) / MLA_ROPE))
    invf = row(jnp.concatenate([inv_freq, inv_freq, jnp.zeros((pad_r,), F32)]))
    keys = peer_sub_keys.reshape(2 * PEER_HEADS, PEER_NKEYS, PEER_DHALF).astype(BF16)
    u_bf = peer_u.astype(BF16)
    vt_bf = _transpose_cast(peer_v, BF16, _pick(peer_v.shape[0], 1024))

    tile = _tiles(T, S)
    hg, mla = _in_proj(x2, row(attn_norm), w_hg, w_mla, tile["in_proj_rows"])
    o_f = _hgrn_scan(hg, lb_logits[:, 0, :], B, S, False, tile["scan_steps"])
    o_b = _hgrn_scan(hg, lb_logits[:, 1, :], B, S, True, tile["scan_steps"])
    q, k, v = _mla_prep(mla, pos, invf, row(q_a_norm), row(kv_a_norm), wq, wkv, qgn, qgr, kgn, kgr,
                        tile["mla_rows"])
    y_mla = _attention(q, k, v, mla_o_norm.reshape(1, -1).astype(F32), B, S, tile["attn_queries"])
    x1, h2, scores_t = _out_proj(o_f, o_b, hg, y_mla, x2, row(hg_o_norm), w_out.astype(BF16),
                                 row(ffn_norm), peer_w_q.astype(BF16), keys, tile["out_proj_rows"])
    iz, cnt, atop, btop = _peer_select(scores_t, tile["select_tokens"])
    return _peer_dense(h2, u_bf, vt_bf, scores_t, iz, cnt, atop, btop, x1,
                       tile["dense_tokens"], tile["dense_experts"])


def kernel(x, positions, attn_norm, w_in, hg_lb_logits, hg_o_norm, q_a_norm, w_q_up, kv_a_norm, w_kv_up,
           q_norm, k_norm, mla_o_norm, w_out, ffn_norm, peer_w_q, peer_sub_keys, peer_u, peer_v):
    B, S, D = x.shape
    x2 = x.reshape(B * S, D)
    pos = positions.reshape(B * S, 1)
    for l in range(attn_norm.shape[0]):
        assert l == 0
        x2 = _layer(x2, pos, B, S, attn_norm[l], w_in[l], hg_lb_logits, hg_o_norm[l], q_a_norm[l],
                    w_q_up[l], kv_a_norm[l], w_kv_up[l], q_norm[l], k_norm[l], mla_o_norm[l], w_out[l],
                    ffn_norm[l], peer_w_q[l], peer_sub_keys[l], peer_u[l], peer_v[l])
    return x2.reshape(B, S, D)
```
